```python
import math
import numpy as np
import jax
import jax.numpy as jnp
from jax import lax

D_MODEL = 1024
BATCH = 4
SEQ = 4096
DEPTH = 4

F32 = jnp.float32
PLE_DIM = 256
ROPE_THETA = 500000.0
HEAD_DIM = 64
ROT_DIM = HEAD_DIM // 4
EPS = 1e-6
QBLK = 128
NEG = -1e30
BIG = 1e30

A_HEADS = 8
A_KV_GROUPS = 2
A_GQA = A_HEADS // A_KV_GROUPS
A_WIDTH = A_HEADS * HEAD_DIM
CMP_LEN = 32
CMP_STRIDE = 16
CMP_HID = 256
SLC_LEN = 64
SLC_TOPK = 16
WIN = 512

B_HEADS = 8
Q_LORA = 384
KV_LORA = 128
QK_NOPE = 64
QK_ROPE = 32
V_DIM = 64
B_WIDTH = B_HEADS * V_DIM

C_PAIRS = ((128, 1), (512, 4), (2048, 16))
C_GROUPS = len(C_PAIRS)
C_HEADS = 8
C_WIDTH = C_HEADS * HEAD_DIM

IN_SIZES = (
    A_WIDTH,
    6 * A_KV_GROUPS * HEAD_DIM,
    3 * A_HEADS,
    A_WIDTH,
    Q_LORA,
    KV_LORA,
    QK_ROPE,
    B_WIDTH,
    C_GROUPS * 3 * C_WIDTH,
    C_WIDTH,
    3 * D_MODEL,
)
D_IN = sum(IN_SIZES)
IN_OFFSETS = tuple(int(v) for v in np.cumsum(IN_SIZES)[:-1])

kernel_name = "hybrid_nsa_mla_dilated_trunk"


def rms_norm(x, g):
    xf = x.astype(F32)
    y = xf * lax.rsqrt(jnp.mean(xf * xf, axis=-1, keepdims=True) + EPS)
    return (y * g.astype(F32)).astype(x.dtype)


def rope_tables(positions, dim):
    inv = ROPE_THETA ** (-jnp.arange(0, dim, 2, dtype=F32) / dim)
    ang = positions.astype(F32)[..., None] * inv
    return jnp.cos(ang), jnp.sin(ang)


def apply_rope(x, cos, sin, dim):
    xf = x.astype(F32)
    half = dim // 2
    c = cos[:, :, None, :]
    s = sin[:, :, None, :]
    x1 = xf[..., :half]
    x2 = xf[..., half:dim]
    out = jnp.concatenate([x1 * c - x2 * s, x2 * c + x1 * s, xf[..., dim:]], axis=-1)
    return out.astype(x.dtype)


def banded_attention(q, k, v, max_dist):
    n, L, hkv, g, d = q.shape
    nblk = -(-L // QBLK)
    lp = nblk * QBLK
    nb = -(-max_dist // QBLK)
    pad = lp - L
    qp = jnp.pad(q.astype(F32), ((0, 0), (0, pad), (0, 0), (0, 0), (0, 0))).reshape(n, nblk, QBLK, hkv, g, d)
    kp = jnp.pad(k.astype(F32), ((0, 0), (nb * QBLK, pad), (0, 0), (0, 0)))
    vp = jnp.pad(v.astype(F32), ((0, 0), (nb * QBLK, pad), (0, 0), (0, 0)))
    kb = jnp.concatenate([kp[:, sft * QBLK: sft * QBLK + lp].reshape(n, nblk, QBLK, hkv, d) for sft in range(nb + 1)], axis=2)
    vb = jnp.concatenate([vp[:, sft * QBLK: sft * QBLK + lp].reshape(n, nblk, QBLK, hkv, d) for sft in range(nb + 1)], axis=2)
    qpos = jnp.arange(lp).reshape(nblk, QBLK)
    kpos = (jnp.arange(nblk)[:, None] - nb) * QBLK + jnp.arange((nb + 1) * QBLK)[None, :]
    rel = qpos[:, :, None] - kpos[:, None, :]
    valid = (rel >= 0) & (rel <= max_dist) & (kpos[:, None, :] >= 0)
    sc = jnp.einsum('nbqhgd,nbkhd->nbhgqk', qp, kb) * (d ** -0.5)
    sc = jnp.where(valid[None, :, None, None], sc, NEG)
    m = jnp.max(sc, axis=-1, keepdims=True)
    e = jnp.exp(sc - m)
    den = jnp.sum(e, axis=-1, keepdims=True)
    out = jnp.einsum('nbhgqk,nbkhd->nbqhgd', e / den, vb).reshape(n, lp, hkv, g, d)[:, :L]
    lse = (m + jnp.log(den))[..., 0]
    lse = lse.transpose(0, 1, 4, 2, 3).reshape(n, lp, hkv, g)[:, :L]
    return out, lse


def causal_block_attention(q, k, v):
    b, s, h, dqk = q.shape
    dv = v.shape[-1]
    nq = s // QBLK
    kf = k.astype(F32)
    vf = v.astype(F32)
    scale = dqk ** -0.5
    kpos = jnp.arange(s)
    qs = q.astype(F32).reshape(b, nq, QBLK, h, dqk).swapaxes(0, 1)
    ps = jnp.arange(s).reshape(nq, QBLK)

    def block(args):
        qi, pi = args
        sc = jnp.einsum('bqhd,bkhd->bhqk', qi, kf) * scale
        sc = jnp.where(kpos[None, :] <= pi[:, None], sc, NEG)
        return jnp.einsum('bhqk,bkhd->bqhd', jax.nn.softmax(sc, axis=-1), vf)

    o = lax.map(block, (qs, ps))
    return o.swapaxes(0, 1).reshape(b, s, h, dv)


def nsa_compress(x, pe, w1, w2):
    b, s, h, d = x.shape
    nch = s // CMP_STRIDE
    r = CMP_LEN // CMP_STRIDE
    nc = nch - r + 1
    ch = x.reshape(b, nch, CMP_STRIDE, h, d)
    blocks = jnp.concatenate([ch[:, i:i + nc] for i in range(r)], axis=2)
    blocks = blocks + pe[None, None, :, None, :].astype(x.dtype)
    flat = blocks.transpose(0, 1, 3, 2, 4).reshape(b, nc, h, CMP_LEN * d)
    return jax.nn.silu(flat @ w1) @ w2


def nsa_selected(q, k, v, idx, scale):
    b, s, hkv, g, d = q.shape
    ns = s // SLC_LEN
    nq = s // QBLK
    n = idx.shape[-1]
    kb = k.astype(F32).reshape(b, ns, SLC_LEN, hkv, d).transpose(0, 3, 1, 2, 4)
    vb = v.astype(F32).reshape(b, ns, SLC_LEN, hkv, d).transpose(0, 3, 1, 2, 4)
    bi = jnp.arange(b)[:, None, None, None]
    hi = jnp.arange(hkv)[None, None, :, None]
    off = jnp.arange(SLC_LEN)

    def block(args):
        qi, ii, ti = args
        kg = kb[bi, hi, ii]
        vg = vb[bi, hi, ii]
        sc = jnp.einsum('bqhgd,bqhnkd->bqhgnk', qi, kg) * scale
        kpos = ii[..., None] * SLC_LEN + off
        ok = (kpos <= ti[None, :, None, None, None])[:, :, :, None]
        sc = jnp.where(ok, sc, NEG).reshape(b, QBLK, hkv, g, n * SLC_LEN)
        pr = jax.nn.softmax(sc, axis=-1).reshape(b, QBLK, hkv, g, n, SLC_LEN)
        return jnp.einsum('bqhgnk,bqhnkd->bqhgd', pr, vg)

    qs = q.reshape(b, nq, QBLK, hkv, g, d).swapaxes(0, 1)
    iss = idx.reshape(b, nq, QBLK, hkv, n).swapaxes(0, 1)
    ts = jnp.arange(s).reshape(nq, QBLK)
    o = lax.map(block, (qs, iss, ts))
    return o.swapaxes(0, 1).reshape(b, s, hkv, g, d)


def nsa_mixer(a_q, a_kv, a_gate, cos, sin, cmp_pe, cmp_w1, cmp_w2):
    b, s, _ = a_q.shape
    d, hkv, g = HEAD_DIM, A_KV_GROUPS, A_GQA
    scale = d ** -0.5
    q = apply_rope(a_q.reshape(b, s, A_HEADS, d), cos, sin, ROT_DIM).reshape(b, s, hkv, g, d).astype(F32)
    kv = a_kv.reshape(b, s, 6, hkv, d)
    k_cmp = apply_rope(kv[:, :, 0], cos, sin, ROT_DIM)
    k_slc = apply_rope(kv[:, :, 2], cos, sin, ROT_DIM)
    k_win = apply_rope(kv[:, :, 4], cos, sin, ROT_DIM)
    v_cmp, v_slc, v_win = kv[:, :, 1], kv[:, :, 3], kv[:, :, 5]
    t = jnp.arange(s)
    kc = nsa_compress(k_cmp, cmp_pe[0], cmp_w1[0], cmp_w2[0]).astype(F32)
    vc = nsa_compress(v_cmp, cmp_pe[1], cmp_w1[1], cmp_w2[1]).astype(F32)
    nc = kc.shape[1]
    c_end = jnp.arange(nc) * CMP_STRIDE + CMP_LEN - 1
    cv = (c_end[None, :] <= t[:, None])[None, :, None, None, :]
    sc = jnp.einsum('bshgd,bchd->bshgc', q, kc) * scale
    pc = jax.nn.softmax(jnp.where(cv, sc, NEG), axis=-1) * cv
    o_cmp = jnp.einsum('bshgc,bchd->bshgd', pc, vc)
    ns = s // SLC_LEN
    cs = jnp.arange(nc)[:, None] * CMP_STRIDE
    js = jnp.arange(ns)[None, :] * SLC_LEN
    overlap = jnp.clip(jnp.minimum(cs + CMP_LEN, js + SLC_LEN) - jnp.maximum(cs, js), 0, None).astype(F32) / CMP_LEN
    imp = jnp.einsum('bshgc,cj->bshj', pc, overlap)
    cur = t // SLC_LEN
    jj = jnp.arange(ns)
    valid = (jj[None, :] <= cur[:, None])[None, :, None, :]
    forced = ((jj[None, :] == 0) | (jj[None, :] == cur[:, None]) | (jj[None, :] == cur[:, None] - 1))[None, :, None, :]
    score = jnp.where(forced, BIG, jnp.where(valid, imp, NEG))
    _, idx = lax.top_k(score, min(SLC_TOPK, ns))
    o_slc = nsa_selected(q, k_slc, v_slc, idx, scale)
    o_win, _ = banded_attention(q, k_win, v_win, WIN - 1)
    gt = jax.nn.sigmoid(a_gate.astype(F32)).reshape(b, s, hkv, g, 3)
    o = gt[..., 0:1] * o_cmp + gt[..., 1:2] * o_slc + gt[..., 2:3] * o_win
    return o.reshape(b, s, A_WIDTH)


def mla_mixer(b_cq, b_ckv, b_kr, cos, sin, g_q, w_uq, g_kv, w_ukv):
    b, s, _ = b_cq.shape
    q = (rms_norm(b_cq, g_q) @ w_uq).reshape(b, s, B_HEADS, QK_NOPE + QK_ROPE)
    q = jnp.concatenate([q[..., :QK_NOPE], apply_rope(q[..., QK_NOPE:], cos, sin, QK_ROPE)], axis=-1)
    kvu = (rms_norm(b_ckv, g_kv) @ w_ukv).reshape(b, s, B_HEADS, QK_NOPE + V_DIM)
    k_rope = apply_rope(b_kr[:, :, None, :], cos, sin, QK_ROPE)
    k = jnp.concatenate([kvu[..., :QK_NOPE], jnp.broadcast_to(k_rope, (b, s, B_HEADS, QK_ROPE))], axis=-1)
    v = kvu[..., QK_NOPE:]
    return causal_block_attention(q, k, v).reshape(b, s, B_WIDTH)


def dilated_mixer(c_qkv, cos, sin):
    b, s, _ = c_qkv.shape
    h, d = C_HEADS, HEAD_DIM
    qkv = c_qkv.reshape(b, s, C_GROUPS, 3, h, d)
    outs, lses = [], []
    for gi, (window, dil) in enumerate(C_PAIRS):
        L = s // dil
        q = apply_rope(qkv[:, :, gi, 0], cos, sin, ROT_DIM)
        k = apply_rope(qkv[:, :, gi, 1], cos, sin, ROT_DIM)
        v = qkv[:, :, gi, 2]

        def sub(z):
            return z.reshape(b, L, dil, h, d).transpose(0, 2, 1, 3, 4).reshape(b * dil, L, h, d)

        o, lse = banded_attention(sub(q)[:, :, :, None], sub(k), sub(v), window // dil)
        outs.append(o[:, :, :, 0].reshape(b, dil, L, h, d).transpose(0, 2, 1, 3, 4).reshape(b, s, h, d))
        lses.append(lse[:, :, :, 0].reshape(b, dil, L, h).transpose(0, 2, 1, 3).reshape(b, s, h))
    wts = jax.nn.softmax(jnp.stack(lses, axis=-1), axis=-1)
    o = jnp.einsum('bshg,gbshd->bshd', wts, jnp.stack(outs, axis=0))
    return o.reshape(b, s, C_WIDTH)


def setup_inputs(seed: int = 0) -> dict:
    key = jax.random.key(seed)
    ks = jax.random.split(key, 24)
    nrm = lambda k, shape, scale: jax.random.normal(k, shape, F32) * scale
    x = nrm(ks[0], (BATCH, SEQ, D_MODEL), 1.0)
    p = nrm(ks[1], (DEPTH, BATCH, SEQ, PLE_DIM), 1.0)
    start = jax.random.randint(ks[2], (BATCH, 1), 0, 8192, dtype=jnp.int32)
    positions = (start + jnp.arange(SEQ, dtype=jnp.int32)[None, :]).astype(jnp.int32)
    return {
        "x": x,
        "p": p,
        "positions": positions,
        "g_pre": 1.0 + nrm(ks[3], (DEPTH, D_MODEL), 0.01),
        "g_post": 1.0 + nrm(ks[4], (DEPTH, D_MODEL), 0.01),
        "w_in": nrm(ks[5], (DEPTH, D_MODEL, D_IN), D_MODEL ** -0.5),
        "nsa_cmp_pe": nrm(ks[6], (DEPTH, 2, CMP_LEN, HEAD_DIM), 0.02),
        "nsa_cmp_w1": nrm(ks[7], (DEPTH, 2, CMP_LEN * HEAD_DIM, CMP_HID), (CMP_LEN * HEAD_DIM) ** -0.5),
        "nsa_cmp_w2": nrm(ks[8], (DEPTH, 2, CMP_HID, HEAD_DIM), CMP_HID ** -0.5),
        "w_a": nrm(ks[9], (DEPTH, A_WIDTH, D_MODEL), A_WIDTH ** -0.5),
        "mla_g_q": 1.0 + nrm(ks[10], (DEPTH, Q_LORA), 0.01),
        "mla_w_uq": nrm(ks[11], (DEPTH, Q_LORA, B_HEADS * (QK_NOPE + QK_ROPE)), Q_LORA ** -0.5),
        "mla_g_kv": 1.0 + nrm(ks[12], (DEPTH, KV_LORA), 0.01),
        "mla_w_ukv": nrm(ks[13], (DEPTH, KV_LORA, B_HEADS * (QK_NOPE + V_DIM)), KV_LORA ** -0.5),
        "w_b": nrm(ks[14], (DEPTH, B_WIDTH, D_MODEL), B_WIDTH ** -0.5),
        "w_c": nrm(ks[15], (DEPTH, C_WIDTH, D_MODEL), C_WIDTH ** -0.5),
        "w_out": nrm(ks[16], (DEPTH, D_MODEL, D_MODEL), D_MODEL ** -0.5),
        "w_ple": nrm(ks[17], (DEPTH, PLE_DIM, D_MODEL), 0.5 * PLE_DIM ** -0.5),
        "w_plg": nrm(ks[18], (DEPTH, D_MODEL, D_MODEL), D_MODEL ** -0.5),
    }


def reference(x, p, positions, g_pre, g_post, w_in, nsa_cmp_pe, nsa_cmp_w1, nsa_cmp_w2, w_a,
              mla_g_q, mla_w_uq, mla_g_kv, mla_w_ukv, w_b, w_c, w_out, w_ple, w_plg):
    cos_p, sin_p = rope_tables(positions, ROT_DIM)
    cos_m, sin_m = rope_tables(positions, QK_ROPE)
    for i in range(DEPTH):
        h = rms_norm(x, g_pre[i])
        u = h @ w_in[i]
        (a_q, a_kv, a_g, a_z, b_cq, b_ckv, b_kr, b_z, c_qkv, c_z, m_g) = jnp.split(u, IN_OFFSETS, axis=-1)
        o_a = nsa_mixer(a_q, a_kv, a_g, cos_p, sin_p, nsa_cmp_pe[i], nsa_cmp_w1[i], nsa_cmp_w2[i]).astype(x.dtype)
        o_b = mla_mixer(b_cq, b_ckv, b_kr, cos_m, sin_m, mla_g_q[i], mla_w_uq[i], mla_g_kv[i], mla_w_ukv[i]).astype(x.dtype)
        o_c = dilated_mixer(c_qkv, cos_p, sin_p).astype(x.dtype)
        y_a = (o_a * jax.nn.silu(a_z)) @ w_a[i]
        y_b = (o_b * jax.nn.silu(b_z)) @ w_b[i]
        y_c = (o_c * jax.nn.silu(c_z)) @ w_c[i]
        gm = jax.nn.sigmoid(m_g).reshape(m_g.shape[:-1] + (3, D_MODEL))
        y = (gm[..., 0, :] * y_a + gm[..., 1, :] * y_b + gm[..., 2, :] * y_c) @ w_out[i]
        x = x + rms_norm(y, g_post[i])
        x = x + jax.nn.sigmoid(x @ w_plg[i]) * (p[i] @ w_ple[i])
    return x
```

```python
import functools

import numpy as np
import jax
import jax.numpy as jnp
from jax import lax
from jax.experimental import pallas as pl
from jax.experimental.pallas import tpu as pltpu

F32 = jnp.float32
BF16 = jnp.bfloat16

D_MODEL = 1024
PLE_DIM = 256
ROPE_THETA = 500000.0
HEAD_DIM = 64
ROT_DIM = HEAD_DIM // 4
EPS = 1e-6
NEG = -1e30
BIG = 1e30
LANES = 128

A_HEADS = 8
A_KV_GROUPS = 2
A_WIDTH = A_HEADS * HEAD_DIM
CMP_LEN = 32
CMP_STRIDE = 16
CMP_HID = 256
SLC_LEN = 64
SLC_TOPK = 16
WIN = 512

B_HEADS = 8
Q_LORA = 384
KV_LORA = 128
QK_NOPE = 64
QK_ROPE = 32
V_DIM = 64
B_WIDTH = B_HEADS * V_DIM

C_PAIRS = ((128, 1), (512, 4), (2048, 16))
C_GROUPS = len(C_PAIRS)
C_HEADS = 8
C_WIDTH = C_HEADS * HEAD_DIM

IN_SIZES = (A_WIDTH, 6 * A_KV_GROUPS * HEAD_DIM, 3 * A_HEADS, A_WIDTH, Q_LORA, KV_LORA, QK_ROPE,
            B_WIDTH, C_GROUPS * 3 * C_WIDTH, C_WIDTH, 3 * D_MODEL)
IN_OFF = tuple(int(v) for v in np.cumsum((0,) + IN_SIZES))
(OFF_AQ, OFF_AKV, OFF_AG, OFF_AZ, OFF_BCQ, OFF_BCKV, OFF_BKR, OFF_BZ, OFF_CQKV, OFF_CZ, OFF_MG) = IN_OFF[:-1]

U_AQ = 0
U_CQ = (512, 1536, 2560)
U_CK = (1024, 2048, 3072)
U_KCMP = 3584
U_KSLC = 3712
U_KWIN = 3840
ROPE_COLS = 3968
U_VCMP = 3968
U_AZ = 4096
U_BZ = 4608
U_CV = (5120, 5632, 6144)
U_CZ = 6656
U_MG = 7168
U_VSLC = 10240
U_VWIN = 10368
U_GATE = 10496
U_BCKV = 10624
U_BCQ = 10752
U_BKR = 11136
DP = 11264

MLA_SCALE = float((QK_NOPE + QK_ROPE) ** -0.5)
VMEM_LIMIT = 48 * 1024 * 1024


def _cparams(n_axes):
    return pltpu.CompilerParams(dimension_semantics=("arbitrary",) * n_axes, vmem_limit_bytes=VMEM_LIMIT)


def _lane_iota(shape):
    return lax.broadcasted_iota(jnp.int32, shape, len(shape) - 1)


def _pair_cols(w):
    lead = w.shape[:-1]
    return w.reshape(lead + (2, 4, HEAD_DIM)).swapaxes(-3, -2).reshape(lead + (A_WIDTH,))


def _regroup_w_in(w_in):
    def col(off, n):
        return w_in[..., off:off + n]
    z = lambda n: jnp.zeros(w_in.shape[:-1] + (n,), w_in.dtype)
    akv = lambda which: col(OFF_AKV + which * 128, 128)
    cq = lambda gi, t: col(OFF_CQKV + gi * 3 * C_WIDTH + t * C_WIDTH, C_WIDTH)
    pieces = [_pair_cols(col(OFF_AQ, A_WIDTH)) * 0.125]
    for gi in range(C_GROUPS):
        pieces += [cq(gi, 0) * 0.125, cq(gi, 1)]
    pieces += [akv(0), akv(2), akv(4)]
    pieces += [akv(1)]
    pieces += [_pair_cols(col(OFF_AZ, A_WIDTH)), col(OFF_BZ, B_WIDTH)]
    pieces += [cq(gi, 2) for gi in range(C_GROUPS)]
    pieces += [col(OFF_CZ, C_WIDTH), col(OFF_MG, 3 * D_MODEL)]
    pieces += [akv(3), akv(5)]
    pieces += [col(OFF_AG, 3 * A_HEADS), z(128 - 3 * A_HEADS)]
    pieces += [col(OFF_BCKV, KV_LORA), col(OFF_BCQ, Q_LORA)]
    pieces += [z(64), col(OFF_BKR, QK_ROPE), z(32)]
    out = jnp.concatenate(pieces, axis=-1).astype(BF16)
    assert out.shape[-1] == DP
    return out


def _rope_tables(positions):
    pos = positions.astype(F32).reshape(-1)

    def trig(dim):
        inv = ROPE_THETA ** (-jnp.arange(0, dim, 2, dtype=F32) / dim)
        ang = pos[:, None] * inv
        return jnp.cos(ang), jnp.sin(ang)

    t = pos.shape[0]
    ones = jnp.ones((t, 1), F32)
    zeros = jnp.zeros((t, 1), F32)
    cos, sin = trig(ROT_DIM)
    rest = HEAD_DIM - ROT_DIM
    c_head = jnp.concatenate([cos, cos, jnp.broadcast_to(ones, (t, rest))], axis=1)
    s1_head = jnp.concatenate([0 * sin, sin, jnp.broadcast_to(zeros, (t, rest))], axis=1)
    s2_head = jnp.concatenate([-sin, 0 * sin, jnp.broadcast_to(zeros, (t, rest))], axis=1)
    part = tuple(jnp.concatenate([a, a], axis=1) for a in (c_head, s1_head, s2_head))
    cos, sin = trig(QK_ROPE)
    lo = jnp.broadcast_to(ones, (t, 64))
    lz = jnp.broadcast_to(zeros, (t, 64))
    hi1 = jnp.broadcast_to(ones, (t, 32))
    hz = jnp.broadcast_to(zeros, (t, 32))
    mla = (jnp.concatenate([lo, cos, cos, hi1], axis=1),
           jnp.concatenate([lz, 0 * sin, sin, hz], axis=1),
           jnp.concatenate([lz, -sin, 0 * sin, hz], axis=1))
    return part, mla


IN_TM = 1024
IN_TN = 512


def _inproj_kernel(x_ref, g_ref, w_ref, c_ref, s1_ref, s2_ref, o_ref, h_ref):
    j = pl.program_id(1)

    @pl.when(j == 0)
    def _():
        x = x_ref[...]
        ms = jnp.mean(x * x, axis=-1, keepdims=True)
        h_ref[...] = (x * lax.rsqrt(ms + EPS) * g_ref[...]).astype(BF16)

    acc = jnp.dot(h_ref[...], w_ref[...], preferred_element_type=F32)
    nsub = IN_TN // LANES
    rope_blocks = ROPE_COLS // LANES
    full_tiles, rem = divmod(rope_blocks, nsub)

    def emit(n_rope):
        for c in range(nsub):
            xc = acc[:, c * LANES:(c + 1) * LANES]
            if c < n_rope:
                xc = (xc * c_ref[...] + pltpu.roll(xc, ROT_DIM // 2, 1) * s1_ref[...]
                      + pltpu.roll(xc, LANES - ROT_DIM // 2, 1) * s2_ref[...])
            o_ref[:, c * LANES:(c + 1) * LANES] = xc.astype(o_ref.dtype)

    pl.when(j < full_tiles)(lambda: emit(nsub))
    pl.when(j == full_tiles)(lambda: emit(rem))
    pl.when(j > full_tiles)(lambda: emit(0))


def _inproj(x2, g_pre, w_in_p, layer, tabs):
    t = x2.shape[0]
    tm = min(IN_TM, t)
    row = lambda i, j: (i, 0)
    return pl.pallas_call(
        _inproj_kernel,
        grid=(t // tm, DP // IN_TN),
        in_specs=[pl.BlockSpec((tm, D_MODEL), row),
                  pl.BlockSpec((None, 1, D_MODEL), lambda i, j: (layer, 0, 0)),
                  pl.BlockSpec((None, D_MODEL, IN_TN), lambda i, j: (layer, 0, j)),
                  pl.BlockSpec((tm, LANES), row), pl.BlockSpec((tm, LANES), row), pl.BlockSpec((tm, LANES), row)],
        out_specs=pl.BlockSpec((tm, IN_TN), lambda i, j: (i, j)),
        out_shape=jax.ShapeDtypeStruct((t, DP), BF16),
        scratch_shapes=[pltpu.VMEM((tm, D_MODEL), BF16)],
        compiler_params=_cparams(2),
        name="inproj",
    )(x2, g_pre, w_in_p, *tabs)


def _compress_kernel(x_ref, pe_ref, w1_ref, w2_ref, o_ref):
    half = CMP_STRIDE * HEAD_DIM
    x = x_ref[...].astype(F32)
    xa = (x + pe_ref[0:1, :]).astype(BF16)
    xb = (x + pe_ref[1:2, :]).astype(BF16)
    a = jnp.dot(xa, w1_ref[0:half, :], preferred_element_type=F32)
    b = jnp.dot(xb, w1_ref[half:2 * half, :], preferred_element_type=F32)
    nch = a.shape[0]
    hid = a + pltpu.roll(b, nch - 1, 0)
    act = hid * jax.nn.sigmoid(hid)
    o_ref[...] = jnp.dot(act.astype(BF16), w2_ref[...], preferred_element_type=F32).astype(o_ref.dtype)


def _compress(xc, pe, w1, w2, layer):
    b, _, _, nch, width = xc.shape
    return pl.pallas_call(
        _compress_kernel,
        grid=(b, 2, A_KV_GROUPS),
        in_specs=[pl.BlockSpec((None, None, None, nch, width), lambda i, w, g: (i, w, g, 0, 0)),
                  pl.BlockSpec((None, None, 2, width), lambda i, w, g: (layer, w, 0, 0)),
                  pl.BlockSpec((None, None, 2 * width, CMP_HID), lambda i, w, g: (layer, w, 0, 0)),
                  pl.BlockSpec((None, None, CMP_HID, HEAD_DIM), lambda i, w, g: (layer, w, 0, 0))],
        out_specs=pl.BlockSpec((None, None, None, nch, HEAD_DIM), lambda i, w, g: (i, w, g, 0, 0)),
        out_shape=jax.ShapeDtypeStruct((b, 2, A_KV_GROUPS, nch, HEAD_DIM), BF16),
        compiler_params=_cparams(3),
        name="nsa_compress",
    )(xc, pe, w1, w2)


CS_TQ = 128


def _cmp_select_kernel(q_ref, kc_ref, vct_ref, ovt_ref, o_ref, bias_ref, sc_ref):
    tq = CS_TQ
    qi = pl.program_id(1)
    nc = kc_ref.shape[0]
    ns = ovt_ref.shape[0]
    tpos = qi * tq + _lane_iota((1, tq))
    kc = kc_ref[...]
    lane = _lane_iota((1, LANES))
    kst = jnp.concatenate([jnp.where(lane < HEAD_DIM, kc, jnp.zeros_like(kc)),
                           jnp.where(lane >= HEAD_DIM, kc, jnp.zeros_like(kc))], axis=0)
    c_end = lax.broadcasted_iota(jnp.int32, (nc, 1), 0) * CMP_STRIDE + (CMP_LEN - 1)
    cv = c_end <= tpos
    vct = vct_ref[...]
    psum = [jnp.zeros((nc, tq), F32), jnp.zeros((nc, tq), F32)]
    sub = lax.broadcasted_iota(jnp.int32, (LANES, 1), 0)
    for i in range(A_HEADS // 2):
        qb = q_ref[:, i * LANES:(i + 1) * LANES]
        st = lax.dot_general(kst, qb, (((1,), (1,)), ((), ())), preferred_element_type=F32)
        ot = []
        for g in range(A_KV_GROUPS):
            s = jnp.where(cv, st[g * nc:(g + 1) * nc], NEG)
            m = jnp.max(s, axis=0, keepdims=True)
            e = jnp.where(cv, jnp.exp(s - m), 0.0)
            den = jnp.sum(e, axis=0, keepdims=True)
            p = e / jnp.where(den > 0.0, den, 1.0)
            psum[g] = psum[g] + p
            ot.append(jnp.dot(vct, p.astype(BF16), preferred_element_type=F32))
        o_pair = jnp.where(sub < HEAD_DIM, ot[0], ot[1])
        o_ref[:, i * LANES:(i + 1) * LANES] = o_pair.T.astype(o_ref.dtype)
    jdx = lax.broadcasted_iota(jnp.int32, (ns, 1), 0)
    cur = tpos // SLC_LEN
    valid = jdx <= cur
    forced = (jdx == 0) | (jdx == cur) | (jdx == cur - 1)
    biases = []
    for g in range(A_KV_GROUPS):
        imp = jnp.dot(ovt_ref[...], psum[g], preferred_element_type=F32, precision=lax.Precision.HIGHEST)
        score = jnp.where(forced, BIG, jnp.where(valid, imp, NEG))
        sc_ref[...] = score
        cnt = jnp.zeros((ns, tq), F32)
        for jp in range(ns):
            row = sc_ref[jp:jp + 1, :]
            tie = jnp.where(jdx > jp, 1.0, 0.0)
            cnt = cnt + jnp.where(row > score, 1.0, 0.0) + jnp.where(row == score, tie, 0.0)
        biases.append(jnp.where(cnt < float(min(SLC_TOPK, ns)), 0.0, NEG))
    bias_t = jnp.concatenate(biases, axis=0)
    bias_ref[...] = bias_t.T.astype(bias_ref.dtype)


def _cmp_select(u3, kc, vct, ovt):
    b, s, _ = u3.shape
    nc = kc.shape[1]
    ns = s // SLC_LEN
    return pl.pallas_call(
        _cmp_select_kernel,
        grid=(b, s // CS_TQ),
        in_specs=[pl.BlockSpec((None, CS_TQ, A_WIDTH), lambda i, q: (i, q, U_AQ // A_WIDTH)),
                  pl.BlockSpec((None, nc, LANES), lambda i, q: (i, 0, 0)),
                  pl.BlockSpec((None, LANES, nc), lambda i, q: (i, 0, 0)),
                  pl.BlockSpec((ns, nc), lambda i, q: (0, 0))],
        out_specs=[pl.BlockSpec((None, CS_TQ, A_WIDTH), lambda i, q: (i, q, 0)),
                   pl.BlockSpec((None, CS_TQ, LANES), lambda i, q: (i, q, 0))],
        out_shape=[jax.ShapeDtypeStruct((b, s, A_WIDTH), BF16), jax.ShapeDtypeStruct((b, s, LANES), BF16)],
        scratch_shapes=[pltpu.VMEM((ns, CS_TQ), F32)],
        compiler_params=_cparams(2),
        name="nsa_cmp_select",
    )(u3, kc, vct, ovt)


def _stack_heads(qb, extra=None):
    lane = _lane_iota((1, LANES))
    zero = jnp.zeros_like(qb)
    lo = jnp.where(lane < HEAD_DIM, qb, zero)
    hi = jnp.where(lane >= HEAD_DIM, qb, zero)
    if extra is not None:
        ez = jnp.zeros_like(extra)
        lo = jnp.concatenate([lo, jnp.where(lane < HEAD_DIM, extra, ez)], axis=1)
        hi = jnp.concatenate([hi, jnp.where(lane >= HEAD_DIM, extra, ez)], axis=1)
    return jnp.concatenate([lo, hi], axis=0)


def _flash_step(carry, q, k, v, mask, groups):
    m, l, acc = carry
    s = lax.dot_general(q, k, (((1,), (1,)), ((), ())), preferred_element_type=F32)
    if mask is not None:
        mm, tk = s.shape
        s = jnp.where(mask[None], s.reshape(groups, mm // groups, tk), NEG).reshape(mm, tk)
    m_new = jnp.maximum(m, jnp.max(s, axis=1, keepdims=True))
    alpha = jnp.exp(m - m_new)
    p = jnp.exp(s - m_new)
    l = alpha * l + jnp.sum(p, axis=1, keepdims=True)
    acc = alpha * acc + jnp.dot(p.astype(BF16), v, preferred_element_type=F32)
    return m_new, l, acc


def _flash_init(rows):
    return (jnp.full((rows, 1), NEG, F32), jnp.zeros((rows, 1), F32), jnp.zeros((rows, LANES), F32))


def _merge_pair(lo, hi):
    return jnp.where(_lane_iota((1, LANES)) < HEAD_DIM, lo, hi)


SLC_TQ = 128
SLC_TK = 256


def _slc_kernel(q_ref, bias_ref, k_ref, v_ref, oh_ref, o_ref):
    tq, tk = SLC_TQ, SLC_TK
    qi = pl.program_id(1)
    bias = bias_ref[...]
    nblk = A_HEADS // 2
    qa = jnp.concatenate([_stack_heads(q_ref[:, i * LANES:(i + 1) * LANES], bias) for i in range(nblk)], axis=0)
    rows = qa.shape[0]

    def tile(kt):
        start = pl.multiple_of(kt * tk, tk)
        k = jnp.concatenate([k_ref[pl.ds(start, tk), :], oh_ref[pl.ds(start, tk), :]], axis=1)
        return k, v_ref[pl.ds(start, tk), :]

    def body(kt, carry):
        k, v = tile(kt)
        return _flash_step(carry, qa, k, v, None, 1)

    n_full = (qi * tq) // tk
    carry = lax.fori_loop(0, n_full, body, _flash_init(rows))
    qpos = qi * tq + lax.broadcasted_iota(jnp.int32, (tq, 1), 0)
    kpos = n_full * tk + _lane_iota((1, tk))
    k, v = tile(n_full)
    _, l, acc = _flash_step(carry, qa, k, v, kpos <= qpos, 2 * nblk)
    o = acc / l
    for i in range(nblk):
        o_ref[:, i * LANES:(i + 1) * LANES] = _merge_pair(
            o[2 * i * tq:(2 * i + 1) * tq], o[(2 * i + 1) * tq:(2 * i + 2) * tq]).astype(o_ref.dtype)


def _slc_attention(u3, bias, onehot):
    b, s, _ = u3.shape
    return pl.pallas_call(
        _slc_kernel,
        grid=(b, s // SLC_TQ),
        in_specs=[pl.BlockSpec((None, SLC_TQ, A_WIDTH), lambda i, q: (i, q, U_AQ // A_WIDTH)),
                  pl.BlockSpec((None, SLC_TQ, LANES), lambda i, q: (i, q, 0)),
                  pl.BlockSpec((None, s, LANES), lambda i, q: (i, 0, U_KSLC // LANES)),
                  pl.BlockSpec((None, s, LANES), lambda i, q: (i, 0, U_VSLC // LANES)),
                  pl.BlockSpec((s, LANES), lambda i, q: (0, 0))],
        out_specs=pl.BlockSpec((None, SLC_TQ, A_WIDTH), lambda i, q: (i, q, 0)),
        out_shape=jax.ShapeDtypeStruct((b, s, A_WIDTH), BF16),
        compiler_params=_cparams(2),
        name="nsa_selected",
    )(u3, bias, u3, u3, onehot)


BAND_T = 128


def _banded_kernel(*refs, nprev, max_dist, shared_kv, with_lse):
    t = BAND_T
    ntile = nprev + 1
    q_ref = refs[0]
    k_refs = refs[1:1 + ntile]
    v_refs = refs[1 + ntile:1 + 2 * ntile]
    o_ref = refs[1 + 2 * ntile]
    lse_ref = refs[2 + 2 * ntile] if with_lse else None
    qi = pl.program_id(1)
    nblk = q_ref.shape[1] // LANES
    nk = ntile * t
    qrel = nprev * t + lax.broadcasted_iota(jnp.int32, (t, 1), 0)
    krel = _lane_iota((1, nk))
    rel = qrel - krel
    mask = (rel >= 0) & (rel <= max_dist) & (krel + (qi - nprev) * t >= 0)

    def attend(qs, k, v, groups):
        s = lax.dot_general(qs, k, (((1,), (1,)), ((), ())), preferred_element_type=F32)
        rows = s.shape[0]
        s = jnp.where(mask[None], s.reshape(groups, rows // groups, nk), NEG).reshape(rows, nk)
        m = jnp.max(s, axis=1, keepdims=True)
        p = jnp.exp(s - m)
        l = jnp.sum(p, axis=1, keepdims=True)
        o = jnp.dot(p.astype(BF16), v, preferred_element_type=F32) / l
        return o, m + jnp.log(l)

    def emit(i, o, lse):
        sl = slice(i * LANES, (i + 1) * LANES)
        o_ref[:, sl] = _merge_pair(o[0:t], o[t:2 * t]).astype(o_ref.dtype)
        if with_lse:
            lse_ref[:, sl] = _merge_pair(jnp.broadcast_to(lse[0:t], (t, LANES)),
                                         jnp.broadcast_to(lse[t:2 * t], (t, LANES)))

    if shared_kv:
        k = jnp.concatenate([r[...] for r in k_refs], axis=0)
        v = jnp.concatenate([r[...] for r in v_refs], axis=0)
        qs = jnp.concatenate([_stack_heads(q_ref[:, i * LANES:(i + 1) * LANES]) for i in range(nblk)], axis=0)
        o, lse = attend(qs, k, v, 2 * nblk)
        for i in range(nblk):
            emit(i, o[2 * i * t:(2 * i + 2) * t], lse[2 * i * t:(2 * i + 2) * t])
    else:
        for i in range(nblk):
            sl = slice(i * LANES, (i + 1) * LANES)
            k = jnp.concatenate([r[:, sl] for r in k_refs], axis=0)
            v = jnp.concatenate([r[:, sl] for r in v_refs], axis=0)
            o, lse = attend(_stack_heads(q_ref[:, sl]), k, v, 2)
            emit(i, o, lse)


def _banded(u, *, batch, seq, dil, q_off, k_off, v_off, kv_width, max_dist, with_lse):
    t = BAND_T
    ln = seq // dil
    nq = ln // t
    nprev = -(-max_dist // t)
    uv = u.reshape(batch * ln, dil * DP)
    shared = kv_width == LANES

    def qmap(n, q):
        return ((n // dil) * nq + q, ((n % dil) * DP + q_off) // A_WIDTH)

    def kvmap(off, d):
        return lambda n, q: ((n // dil) * nq + jnp.maximum(q - d, 0), ((n % dil) * DP + off) // kv_width)

    def omap(n, q):
        return ((n // dil) * nq + q, n % dil)

    in_specs = [pl.BlockSpec((t, A_WIDTH), qmap)]
    in_specs += [pl.BlockSpec((t, kv_width), kvmap(k_off, d)) for d in range(nprev, -1, -1)]
    in_specs += [pl.BlockSpec((t, kv_width), kvmap(v_off, d)) for d in range(nprev, -1, -1)]
    out_specs = [pl.BlockSpec((t, A_WIDTH), omap)]
    out_shape = [jax.ShapeDtypeStruct((batch * ln, dil * A_WIDTH), BF16)]
    if with_lse:
        out_specs.append(pl.BlockSpec((t, A_WIDTH), omap))
        out_shape.append(jax.ShapeDtypeStruct((batch * ln, dil * A_WIDTH), F32))
    outs = pl.pallas_call(
        functools.partial(_banded_kernel, nprev=nprev, max_dist=max_dist, shared_kv=shared, with_lse=with_lse),
        grid=(batch * dil, nq),
        in_specs=in_specs,
        out_specs=out_specs,
        out_shape=out_shape,
        compiler_params=_cparams(2),
        name="banded_attention",
    )(*([uv] * (1 + 2 * (nprev + 1))))
    return [o.reshape(batch * seq, A_WIDTH) for o in outs]


MLA_TM = 512


def _rms(x, g):
    return x * lax.rsqrt(jnp.mean(x * x, axis=-1, keepdims=True) + EPS) * g


def _mla_prep_kernel(cq_ref, ckv_ref, kr_ref, c_ref, s1_ref, s2_ref, gq_ref, gkv_ref, wq_ref, wk_ref, wv_ref,
                     q_out, k_out, v_out):
    half = QK_ROPE // 2
    cm, s1, s2 = c_ref[...], s1_ref[...], s2_ref[...]

    def rope(xc):
        return xc * cm + pltpu.roll(xc, half, 1) * s1 + pltpu.roll(xc, LANES - half, 1) * s2

    qn = _rms(cq_ref[...].astype(F32), gq_ref[...]).astype(BF16)
    q = jnp.dot(qn, wq_ref[...], preferred_element_type=F32)
    kvn = _rms(ckv_ref[...].astype(F32), gkv_ref[...]).astype(BF16)
    kk = jnp.dot(kvn, wk_ref[...], preferred_element_type=F32)
    kr = rope(kr_ref[...].astype(F32))
    for h in range(B_HEADS):
        sl = slice(h * LANES, (h + 1) * LANES)
        q_out[:, sl] = (rope(q[:, sl]) * MLA_SCALE).astype(q_out.dtype)
        k_out[:, sl] = (kk[:, sl] + kr).astype(k_out.dtype)
    v_out[...] = jnp.dot(kvn, wv_ref[...], preferred_element_type=F32).astype(v_out.dtype)


def _mla_prep(u, tabs, g_q, g_kv, wq, wk, wv, layer):
    t = u.shape[0]
    tm = min(MLA_TM, t)
    row = lambda i: (i, 0)
    wide = B_HEADS * LANES
    assert U_BCQ % Q_LORA == 0
    return pl.pallas_call(
        _mla_prep_kernel,
        grid=(t // tm,),
        in_specs=[pl.BlockSpec((tm, Q_LORA), lambda i: (i, U_BCQ // Q_LORA)),
                  pl.BlockSpec((tm, LANES), lambda i: (i, U_BCKV // LANES)),
                  pl.BlockSpec((tm, LANES), lambda i: (i, U_BKR // LANES)),
                  pl.BlockSpec((tm, LANES), row), pl.BlockSpec((tm, LANES), row), pl.BlockSpec((tm, LANES), row),
                  pl.BlockSpec((None, 1, Q_LORA), lambda i: (layer, 0, 0)),
                  pl.BlockSpec((None, 1, KV_LORA), lambda i: (layer, 0, 0)),
                  pl.BlockSpec((None, Q_LORA, wide), lambda i: (layer, 0, 0)),
                  pl.BlockSpec((None, KV_LORA, wide), lambda i: (layer, 0, 0)),
                  pl.BlockSpec((None, KV_LORA, B_WIDTH), lambda i: (layer, 0, 0))],
        out_specs=[pl.BlockSpec((tm, wide), row), pl.BlockSpec((tm, wide), row), pl.BlockSpec((tm, B_WIDTH), row)],
        out_shape=[jax.ShapeDtypeStruct((t, wide), BF16), jax.ShapeDtypeStruct((t, wide), BF16),
                   jax.ShapeDtypeStruct((t, B_WIDTH), BF16)],
        compiler_params=_cparams(1),
        name="mla_prep",
    )(u, u, u, *tabs, g_q, g_kv, wq, wk, wv)


MLA_TQ = 512
MLA_TK = 256


def _mla_flash_kernel(q_ref, k_ref, v_ref, o_ref):
    tq, tk = MLA_TQ, MLA_TK
    qi = pl.program_id(2)
    n_full = (qi * tq) // tk
    qpos = qi * tq + lax.broadcasted_iota(jnp.int32, (tq, 1), 0)
    outs = []
    for h in range(2):
        sl = slice(h * LANES, (h + 1) * LANES)
        q = q_ref[:, sl]

        def tile(kt, sl=sl):
            start = pl.multiple_of(kt * tk, tk)
            return k_ref[pl.ds(start, tk), sl], v_ref[pl.ds(start, tk), :]

        def body(kt, carry, q=q, tile=tile):
            k, v = tile(kt)
            return _flash_step(carry, q, k, v, None, 1)

        carry = lax.fori_loop(0, n_full, body, _flash_init(tq))
        for d in range(tq // tk):
            kt = n_full + d
            k, v = tile(kt)
            carry = _flash_step(carry, q, k, v, (kt * tk + _lane_iota((1, tk))) <= qpos, 1)
        outs.append(carry[2] / carry[1])
    o_ref[...] = _merge_pair(outs[0], outs[1]).astype(o_ref.dtype)


def _mla_flash(q, k, v, batch, seq):
    q3 = q.reshape(batch, seq, -1)
    k3 = k.reshape(batch, seq, -1)
    v3 = v.reshape(batch, seq, -1)
    o = pl.pallas_call(
        _mla_flash_kernel,
        grid=(batch, B_HEADS // 2, seq // MLA_TQ),
        in_specs=[pl.BlockSpec((None, MLA_TQ, 2 * LANES), lambda b, h, i: (b, i, h)),
                  pl.BlockSpec((None, seq, 2 * LANES), lambda b, h, i: (b, 0, h)),
                  pl.BlockSpec((None, seq, LANES), lambda b, h, i: (b, 0, h))],
        out_specs=pl.BlockSpec((None, MLA_TQ, LANES), lambda b, h, i: (b, i, h)),
        out_shape=jax.ShapeDtypeStruct((batch, seq, B_WIDTH), BF16),
        compiler_params=_cparams(3),
        name="mla_flash",
    )(q3, k3, v3)
    return o.reshape(batch * seq, B_WIDTH)


OUT_TM = 256


def _out_kernel(x_ref, p_ref, ocmp_ref, oslc_ref, owin_ref, gate_ref, e_ref, az_ref, ob_ref, bz_ref,
                oc0_ref, oc1_ref, oc2_ref, l0_ref, l1_ref, l2_ref, cz_ref, mg0_ref, mg1_ref, mg2_ref,
                wa_ref, wb_ref, wc_ref, wout_ref, wplg_ref, wple_ref, gpost_ref, o_ref):
    sig = jax.nn.sigmoid
    f = lambda r: r[...].astype(F32)

    def silu(z):
        return z * sig(z)

    def mm(a, w_ref):
        return jnp.dot(a.astype(BF16), w_ref[...], preferred_element_type=F32)

    gs = sig(jnp.dot(gate_ref[...], e_ref[...], preferred_element_type=F32))
    o_a = (gs[:, 0:A_WIDTH] * f(ocmp_ref) + gs[:, A_WIDTH:2 * A_WIDTH] * f(oslc_ref)
           + gs[:, 2 * A_WIDTH:3 * A_WIDTH] * f(owin_ref))
    y_a = mm(o_a * silu(f(az_ref)), wa_ref)
    y_b = mm(f(ob_ref) * silu(f(bz_ref)), wb_ref)
    l0, l1, l2 = l0_ref[...], l1_ref[...], l2_ref[...]
    mx = jnp.maximum(jnp.maximum(l0, l1), l2)
    e0, e1, e2 = jnp.exp(l0 - mx), jnp.exp(l1 - mx), jnp.exp(l2 - mx)
    o_c = (e0 * f(oc0_ref) + e1 * f(oc1_ref) + e2 * f(oc2_ref)) / (e0 + e1 + e2)
    y_c = mm(o_c * silu(f(cz_ref)), wc_ref)
    mix = sig(f(mg0_ref)) * y_a + sig(f(mg1_ref)) * y_b + sig(f(mg2_ref)) * y_c
    y = mm(mix, wout_ref)
    x1 = x_ref[...] + _rms(y, gpost_ref[...])
    o_ref[...] = x1 + sig(mm(x1, wplg_ref)) * mm(p_ref[...], wple_ref)


def _out_layer(x2, p, layer, o_cmp, o_slc, o_win, u, expand, o_b, o_c, lse_c, w):
    t = x2.shape[0]
    tm = min(OUT_TM, t)
    row = lambda i: (i, 0)
    ucol = lambda off, width: pl.BlockSpec((tm, width), lambda i: (i, off // width))
    half = pl.BlockSpec((tm, A_WIDTH), row)
    wspec = lambda r, c: pl.BlockSpec((None, r, c), lambda i: (layer, 0, 0))
    in_specs = [pl.BlockSpec((tm, D_MODEL), row),
                pl.BlockSpec((None, tm, PLE_DIM), lambda i: (layer, i, 0)),
                half, half, half,
                ucol(U_GATE, LANES), pl.BlockSpec((LANES, 3 * A_WIDTH), lambda i: (0, 0)),
                ucol(U_AZ, A_WIDTH), half, ucol(U_BZ, B_WIDTH),
                half, half, half, half, half, half, ucol(U_CZ, C_WIDTH),
                ucol(U_MG, D_MODEL), ucol(U_MG + D_MODEL, D_MODEL), ucol(U_MG + 2 * D_MODEL, D_MODEL),
                wspec(A_WIDTH, D_MODEL), wspec(B_WIDTH, D_MODEL), wspec(C_WIDTH, D_MODEL),
                wspec(D_MODEL, D_MODEL), wspec(D_MODEL, D_MODEL), wspec(PLE_DIM, D_MODEL),
                wspec(1, D_MODEL)]
    return pl.pallas_call(
        _out_kernel,
        grid=(t // tm,),
        in_specs=in_specs,
        out_specs=pl.BlockSpec((tm, D_MODEL), row),
        out_shape=jax.ShapeDtypeStruct((t, D_MODEL), F32),
        compiler_params=_cparams(1),
        name="out_layer",
    )(x2, p, o_cmp, o_slc, o_win, u, expand, u, o_b, u, *o_c, *lse_c, u, u, u, u,
      w["w_a"], w["w_b"], w["w_c"], w["w_out"], w["w_plg"], w["w_ple"], w["g_post"])


def _overlap_t(seq):
    nc = seq // CMP_STRIDE
    ns = seq // SLC_LEN
    cs = np.arange(nc)[:, None] * CMP_STRIDE
    js = np.arange(ns)[None, :] * SLC_LEN
    ov = np.clip(np.minimum(cs + CMP_LEN, js + SLC_LEN) - np.maximum(cs, js), 0, None).astype(np.float32) / CMP_LEN
    return jnp.asarray(ov.T)


def _block_onehot(seq):
    blk = np.arange(seq)[:, None] // SLC_LEN
    return jnp.asarray((np.arange(LANES)[None, :] % HEAD_DIM == blk).astype(np.float32), dtype=BF16)


def _gate_expand():
    e = np.zeros((LANES, 3 * A_WIDTH), np.float32)
    for br in range(3):
        for i in range(A_HEADS // 2):
            for g in range(A_KV_GROUPS):
                h = g * (A_HEADS // 2) + i
                c0 = br * A_WIDTH + i * LANES + g * HEAD_DIM
                e[h * 3 + br, c0:c0 + HEAD_DIM] = 1.0
    return jnp.asarray(e, dtype=BF16)


def kernel(x, p, positions, g_pre, g_post, w_in, nsa_cmp_pe, nsa_cmp_w1, nsa_cmp_w2, w_a, mla_g_q, mla_w_uq,
           mla_g_kv, mla_w_ukv, w_b, w_c, w_out, w_ple, w_plg):
    batch, seq, _ = x.shape
    depth = w_in.shape[0]
    t = batch * seq
    nch = seq // CMP_STRIDE

    part_tabs, mla_tabs = _rope_tables(positions)
    w_in_p = _regroup_w_in(w_in)
    g_pre3 = g_pre.reshape(depth, 1, D_MODEL)
    pe = nsa_cmp_pe.reshape(depth, 2, 2, CMP_STRIDE * HEAD_DIM)
    w1 = nsa_cmp_w1.astype(BF16)
    w2 = nsa_cmp_w2.astype(BF16)
    wq = mla_w_uq.reshape(depth, Q_LORA, B_HEADS, QK_NOPE + QK_ROPE)
    wq = jnp.pad(wq, ((0, 0), (0, 0), (0, 0), (0, LANES - QK_NOPE - QK_ROPE))).reshape(depth, Q_LORA, -1).astype(BF16)
    wkv = mla_w_ukv.reshape(depth, KV_LORA, B_HEADS, QK_NOPE + V_DIM)
    wk = jnp.pad(wkv[..., :QK_NOPE], ((0, 0), (0, 0), (0, 0), (0, LANES - QK_NOPE))).reshape(depth, KV_LORA, -1)
    wk = wk.astype(BF16)
    wv = wkv[..., QK_NOPE:].reshape(depth, KV_LORA, B_WIDTH).astype(BF16)
    gq3 = mla_g_q.reshape(depth, 1, Q_LORA)
    gkv3 = mla_g_kv.reshape(depth, 1, KV_LORA)
    w_a_p = w_a.reshape(depth, 2, 4, HEAD_DIM, D_MODEL).swapaxes(1, 2).reshape(depth, A_WIDTH, D_MODEL)
    wts = {"w_a": w_a_p.astype(BF16), "w_b": w_b.astype(BF16), "w_c": w_c.astype(BF16),
           "w_out": w_out.astype(BF16), "w_plg": w_plg.astype(BF16), "w_ple": w_ple.astype(BF16),
           "g_post": g_post.reshape(depth, 1, D_MODEL)}
    p3 = p.reshape(depth, t, PLE_DIM)
    ovt = _overlap_t(seq)
    onehot = _block_onehot(seq)
    expand = _gate_expand()

    x2 = x.reshape(t, D_MODEL)
    for layer in range(depth):
        u = _inproj(x2, g_pre3, w_in_p, layer, part_tabs)
        u3 = u.reshape(batch, seq, DP)
        kv_cmp = jnp.stack([u3[:, :, U_KCMP:U_KCMP + LANES], u3[:, :, U_VCMP:U_VCMP + LANES]], axis=1)
        xc = kv_cmp.reshape(batch, 2, nch, CMP_STRIDE, A_KV_GROUPS, HEAD_DIM).transpose(0, 1, 4, 2, 3, 5)
        xc = xc.reshape(batch, 2, A_KV_GROUPS, nch, CMP_STRIDE * HEAD_DIM)
        cmp_tok = _compress(xc, pe, w1, w2, layer)
        kc = cmp_tok[:, 0].transpose(0, 2, 1, 3).reshape(batch, nch, LANES)
        vct = cmp_tok[:, 1].transpose(0, 1, 3, 2).reshape(batch, LANES, nch)
        o_cmp, bias = _cmp_select(u3, kc, vct, ovt)
        o_slc = _slc_attention(u3, bias, onehot)
        (o_win,) = _banded(u, batch=batch, seq=seq, dil=1, q_off=U_AQ, k_off=U_KWIN, v_off=U_VWIN,
                           kv_width=LANES, max_dist=WIN - 1, with_lse=False)
        q_b, k_b, v_b = _mla_prep(u, mla_tabs, gq3, gkv3, wq, wk, wv, layer)
        o_b = _mla_flash(q_b, k_b, v_b, batch, seq)
        o_c, lse_c = [], []
        for gi, (window, dil) in enumerate(C_PAIRS):
            o_g, lse_g = _banded(u, batch=batch, seq=seq, dil=dil, q_off=U_CQ[gi], k_off=U_CK[gi], v_off=U_CV[gi],
                                 kv_width=C_WIDTH, max_dist=window // dil, with_lse=True)
            o_c.append(o_g)
            lse_c.append(lse_g)
        x2 = _out_layer(x2, p3, layer, o_cmp.reshape(t, A_WIDTH), o_slc.reshape(t, A_WIDTH), o_win, u, expand,
                        o_b, o_c, lse_c, wts)
    return x2.reshape(batch, seq, D_MODEL)
```

```python
import functools

import numpy as np
import jax
import jax.numpy as jnp
from jax import lax
from jax.experimental import pallas as pl
from jax.experimental.pallas import tpu as pltpu

F32 = jnp.float32
BF16 = jnp.bfloat16

D_MODEL = 1024
PLE_DIM = 256
ROPE_THETA = 500000.0
HEAD_DIM = 64
ROT_DIM = HEAD_DIM // 4
EPS = 1e-6
NEG = -1e30
BIG = 1e30
LANES = 128

A_HEADS = 8
A_KV_GROUPS = 2
A_WIDTH = A_HEADS * HEAD_DIM
CMP_LEN = 32
CMP_STRIDE = 16
CMP_HID = 256
SLC_LEN = 64
SLC_TOPK = 16
WIN = 512

B_HEADS = 8
Q_LORA = 384
KV_LORA = 128
QK_NOPE = 64
QK_ROPE = 32
V_DIM = 64
B_WIDTH = B_HEADS * V_DIM

C_PAIRS = ((128, 1), (512, 4), (2048, 16))
C_GROUPS = len(C_PAIRS)
C_HEADS = 8
C_WIDTH = C_HEADS * HEAD_DIM

IN_SIZES = (A_WIDTH, 6 * A_KV_GROUPS * HEAD_DIM, 3 * A_HEADS, A_WIDTH, Q_LORA, KV_LORA, QK_ROPE,
            B_WIDTH, C_GROUPS * 3 * C_WIDTH, C_WIDTH, 3 * D_MODEL)
IN_OFF = tuple(int(v) for v in np.cumsum((0,) + IN_SIZES))
(OFF_AQ, OFF_AKV, OFF_AG, OFF_AZ, OFF_BCQ, OFF_BCKV, OFF_BKR, OFF_BZ, OFF_CQKV, OFF_CZ, OFF_MG) = IN_OFF[:-1]

U_AQ = 0
U_CQ0 = 512
U_CK0 = 1024
U_KCMP = 1536
U_KSLC = 1664
U_KWIN = 1792
ROPE_COLS = 1920
U_VCMP = 1920
U_AZ = 2048
U_BZ = 2560
U_CV0 = 3072
U_CZ = 3584
U_MG = 4096
U_VSLC = 7168
U_VWIN = 7296
U_GATE = 7424
U_BCKV = 7552
U_BCQ = 7680
U_BKR = 8064
DP = 8192
DG_COLS = 3 * C_WIDTH
DG_ROPE_COLS = 2 * C_WIDTH

MLA_SCALE = float((QK_NOPE + QK_ROPE) ** -0.5)
VMEM_LIMIT = 48 * 1024 * 1024


def _cparams(n_axes):
    return pltpu.CompilerParams(dimension_semantics=("arbitrary",) * n_axes, vmem_limit_bytes=VMEM_LIMIT)


def _lane_iota(shape):
    return lax.broadcasted_iota(jnp.int32, shape, len(shape) - 1)


def _sub_iota(shape):
    return lax.broadcasted_iota(jnp.int32, shape, len(shape) - 2)


def _pair_cols(w):
    lead = w.shape[:-1]
    return w.reshape(lead + (2, 4, HEAD_DIM)).swapaxes(-3, -2).reshape(lead + (A_WIDTH,))


def _regroup_w_in(w_in):
    def col(off, n):
        return w_in[..., off:off + n]
    z = lambda n: jnp.zeros(w_in.shape[:-1] + (n,), w_in.dtype)
    akv = lambda which: col(OFF_AKV + which * 128, 128)
    cq = lambda gi, t: col(OFF_CQKV + gi * 3 * C_WIDTH + t * C_WIDTH, C_WIDTH)
    pieces = [_pair_cols(col(OFF_AQ, A_WIDTH)) * 0.125, cq(0, 0) * 0.125, cq(0, 1)]
    pieces += [akv(0), akv(2), akv(4)]
    pieces += [akv(1)]
    pieces += [_pair_cols(col(OFF_AZ, A_WIDTH)), col(OFF_BZ, B_WIDTH), cq(0, 2)]
    pieces += [col(OFF_CZ, C_WIDTH), col(OFF_MG, 3 * D_MODEL)]
    pieces += [akv(3), akv(5)]
    pieces += [col(OFF_AG, 3 * A_HEADS), z(128 - 3 * A_HEADS)]
    pieces += [col(OFF_BCKV, KV_LORA), col(OFF_BCQ, Q_LORA)]
    pieces += [z(64), col(OFF_BKR, QK_ROPE), z(32)]
    main = jnp.concatenate(pieces, axis=-1).astype(BF16)
    assert main.shape[-1] == DP
    groups = [jnp.concatenate([cq(gi, 0) * 0.125, cq(gi, 1), cq(gi, 2)], axis=-1).astype(BF16)
              for gi in range(1, C_GROUPS)]
    return main, groups


def _rope_tables(positions):
    pos = positions.astype(F32).reshape(-1)

    def trig(dim):
        inv = ROPE_THETA ** (-jnp.arange(0, dim, 2, dtype=F32) / dim)
        ang = pos[:, None] * inv
        return jnp.cos(ang), jnp.sin(ang)

    t = pos.shape[0]
    ones = jnp.ones((t, 1), F32)
    zeros = jnp.zeros((t, 1), F32)
    cos, sin = trig(ROT_DIM)
    rest = HEAD_DIM - ROT_DIM
    c_head = jnp.concatenate([cos, cos, jnp.broadcast_to(ones, (t, rest))], axis=1)
    s1_head = jnp.concatenate([0 * sin, sin, jnp.broadcast_to(zeros, (t, rest))], axis=1)
    s2_head = jnp.concatenate([-sin, 0 * sin, jnp.broadcast_to(zeros, (t, rest))], axis=1)
    part = tuple(jnp.concatenate([a, a], axis=1) for a in (c_head, s1_head, s2_head))
    cos, sin = trig(QK_ROPE)
    lo = jnp.broadcast_to(ones, (t, 64))
    lz = jnp.broadcast_to(zeros, (t, 64))
    hi1 = jnp.broadcast_to(ones, (t, 32))
    hz = jnp.broadcast_to(zeros, (t, 32))
    mla = (jnp.concatenate([lo, cos, cos, hi1], axis=1),
           jnp.concatenate([lz, 0 * sin, sin, hz], axis=1),
           jnp.concatenate([lz, -sin, 0 * sin, hz], axis=1))
    return part, mla


def _to_residue_major(a, batch, dil):
    t, c = a.shape
    ln = t // batch // dil
    return a.reshape(batch, ln, dil, c).swapaxes(1, 2).reshape(t, c)


def _from_residue_major(a, batch, dil):
    t, c = a.shape
    ln = t // batch // dil
    return a.reshape(batch, dil, ln, c).swapaxes(1, 2).reshape(t, c)


NORM_TM = 512
IN_TM = 1024
IN_TN = 512


def _norm_kernel(x_ref, g_ref, o_ref):
    x = x_ref[...]
    ms = jnp.mean(x * x, axis=-1, keepdims=True)
    o_ref[...] = (x * lax.rsqrt(ms + EPS) * g_ref[...]).astype(o_ref.dtype)


def _prenorm(x2, g_pre, layer):
    t = x2.shape[0]
    tm = min(NORM_TM, t)
    return pl.pallas_call(
        _norm_kernel,
        grid=(t // tm,),
        in_specs=[pl.BlockSpec((tm, D_MODEL), lambda i: (i, 0)),
                  pl.BlockSpec((None, 1, D_MODEL), lambda i: (layer, 0, 0))],
        out_specs=pl.BlockSpec((tm, D_MODEL), lambda i: (i, 0)),
        out_shape=jax.ShapeDtypeStruct((t, D_MODEL), BF16),
        compiler_params=_cparams(1),
        name="prenorm",
    )(x2, g_pre)


def _inproj_kernel(h_ref, w_ref, c_ref, s1_ref, s2_ref, o_ref, *, rope_cols):
    j = pl.program_id(1)
    acc = jnp.dot(h_ref[...], w_ref[...], preferred_element_type=F32)
    nsub = IN_TN // LANES
    full_tiles, rem = divmod(rope_cols // LANES, nsub)

    def emit(n_rope):
        for c in range(nsub):
            xc = acc[:, c * LANES:(c + 1) * LANES]
            if c < n_rope:
                xc = (xc * c_ref[...] + pltpu.roll(xc, ROT_DIM // 2, 1) * s1_ref[...]
                      + pltpu.roll(xc, LANES - ROT_DIM // 2, 1) * s2_ref[...])
            o_ref[:, c * LANES:(c + 1) * LANES] = xc.astype(o_ref.dtype)

    pl.when(j < full_tiles)(lambda: emit(nsub))
    pl.when(j == full_tiles)(lambda: emit(rem))
    pl.when(j > full_tiles)(lambda: emit(0))


def _inproj(h, w, layer, tabs, rope_cols):
    t = h.shape[0]
    ncols = w.shape[-1]
    tm = min(IN_TM, t)
    row = lambda i, j: (i, 0)
    return pl.pallas_call(
        functools.partial(_inproj_kernel, rope_cols=rope_cols),
        grid=(t // tm, ncols // IN_TN),
        in_specs=[pl.BlockSpec((tm, D_MODEL), row),
                  pl.BlockSpec((None, D_MODEL, IN_TN), lambda i, j: (layer, 0, j)),
                  pl.BlockSpec((tm, LANES), row), pl.BlockSpec((tm, LANES), row), pl.BlockSpec((tm, LANES), row)],
        out_specs=pl.BlockSpec((tm, IN_TN), lambda i, j: (i, j)),
        out_shape=jax.ShapeDtypeStruct((t, ncols), BF16),
        compiler_params=_cparams(2),
        name="inproj",
    )(h, w, *tabs)


def _compress_kernel(x_ref, pe_ref, w1_ref, w2_ref, o_ref):
    half = CMP_STRIDE * HEAD_DIM
    x = x_ref[...].astype(F32)
    xa = (x + pe_ref[0:1, :]).astype(BF16)
    xb = (x + pe_ref[1:2, :]).astype(BF16)
    a = jnp.dot(xa, w1_ref[0:half, :], preferred_element_type=F32)
    b = jnp.dot(xb, w1_ref[half:2 * half, :], preferred_element_type=F32)
    nch = a.shape[0]
    hid = a + pltpu.roll(b, nch - 1, 0)
    act = hid * jax.nn.sigmoid(hid)
    o_ref[...] = jnp.dot(act.astype(BF16), w2_ref[...], preferred_element_type=F32).astype(o_ref.dtype)


def _compress(xc, pe, w1, w2, layer):
    b, _, _, nch, width = xc.shape
    return pl.pallas_call(
        _compress_kernel,
        grid=(b, 2, A_KV_GROUPS),
        in_specs=[pl.BlockSpec((None, None, None, nch, width), lambda i, w, g: (i, w, g, 0, 0)),
                  pl.BlockSpec((None, None, 2, width), lambda i, w, g: (layer, w, 0, 0)),
                  pl.BlockSpec((None, None, 2 * width, CMP_HID), lambda i, w, g: (layer, w, 0, 0)),
                  pl.BlockSpec((None, None, CMP_HID, HEAD_DIM), lambda i, w, g: (layer, w, 0, 0))],
        out_specs=pl.BlockSpec((None, None, None, nch, HEAD_DIM), lambda i, w, g: (i, w, g, 0, 0)),
        out_shape=jax.ShapeDtypeStruct((b, 2, A_KV_GROUPS, nch, HEAD_DIM), BF16),
        compiler_params=_cparams(3),
        name="nsa_compress",
    )(xc, pe, w1, w2)


CS_TQ = 128


def _cmp_select_kernel(q_ref, kc_ref, vct_ref, ovt_ref, o_ref, bias_ref, sc_ref):
    tq = CS_TQ
    qi = pl.program_id(1)
    nc = kc_ref.shape[0]
    ns = ovt_ref.shape[0]
    tpos = qi * tq + _lane_iota((1, tq))
    kc = kc_ref[...]
    lane = _lane_iota((1, LANES))
    kst = jnp.concatenate([jnp.where(lane < HEAD_DIM, kc, jnp.zeros_like(kc)),
                           jnp.where(lane >= HEAD_DIM, kc, jnp.zeros_like(kc))], axis=0)
    c_end = _sub_iota((nc, 1)) * CMP_STRIDE + (CMP_LEN - 1)
    cv = c_end <= tpos
    vct = vct_ref[...]
    psum = [jnp.zeros((nc, tq), F32), jnp.zeros((nc, tq), F32)]
    sub = _sub_iota((LANES, 1))
    for i in range(A_HEADS // 2):
        qb = q_ref[:, i * LANES:(i + 1) * LANES]
        st = lax.dot_general(kst, qb, (((1,), (1,)), ((), ())), preferred_element_type=F32)
        ot = []
        for g in range(A_KV_GROUPS):
            s = jnp.where(cv, st[g * nc:(g + 1) * nc], NEG)
            m = jnp.max(s, axis=0, keepdims=True)
            e = jnp.where(cv, jnp.exp(s - m), 0.0)
            den = jnp.sum(e, axis=0, keepdims=True)
            p = e / jnp.where(den > 0.0, den, 1.0)
            psum[g] = psum[g] + p
            ot.append(jnp.dot(vct, p.astype(BF16), preferred_element_type=F32))
        o_pair = jnp.where(sub < HEAD_DIM, ot[0], ot[1])
        o_ref[:, i * LANES:(i + 1) * LANES] = o_pair.T.astype(o_ref.dtype)
    jdx = _sub_iota((ns, 1))
    cur = tpos // SLC_LEN
    valid = jdx <= cur
    forced = (jdx == 0) | (jdx == cur) | (jdx == cur - 1)
    for g in range(A_KV_GROUPS):
        imp = jnp.dot(ovt_ref[...], psum[g], preferred_element_type=F32, precision=lax.Precision.HIGHEST)
        score = jnp.where(forced, BIG, jnp.where(valid, imp, NEG))
        sc_ref[...] = score
        cnt = jnp.zeros((ns, tq), F32)
        for jp in range(ns):
            row = sc_ref[jp:jp + 1, :]
            tie = jnp.where(jdx > jp, 1.0, 0.0)
            cnt = cnt + jnp.where(row > score, 1.0, 0.0) + jnp.where(row == score, tie, 0.0)
        bias = jnp.where(cnt < float(min(SLC_TOPK, ns)), 0.0, NEG)
        bias_ref[g * ns:(g + 1) * ns, :] = bias.astype(bias_ref.dtype)


def _cmp_select(u3, kc, vct, ovt):
    b, s, _ = u3.shape
    nc = kc.shape[1]
    ns = s // SLC_LEN
    assert A_KV_GROUPS * ns == LANES
    return pl.pallas_call(
        _cmp_select_kernel,
        grid=(b, s // CS_TQ),
        in_specs=[pl.BlockSpec((None, CS_TQ, A_WIDTH), lambda i, q: (i, q, U_AQ // A_WIDTH)),
                  pl.BlockSpec((None, nc, LANES), lambda i, q: (i, 0, 0)),
                  pl.BlockSpec((None, LANES, nc), lambda i, q: (i, 0, 0)),
                  pl.BlockSpec((ns, nc), lambda i, q: (0, 0))],
        out_specs=[pl.BlockSpec((None, CS_TQ, A_WIDTH), lambda i, q: (i, q, 0)),
                   pl.BlockSpec((None, LANES, CS_TQ), lambda i, q: (i, 0, q))],
        out_shape=[jax.ShapeDtypeStruct((b, s, A_WIDTH), BF16), jax.ShapeDtypeStruct((b, LANES, s), BF16)],
        scratch_shapes=[pltpu.VMEM((ns, CS_TQ), F32)],
        compiler_params=_cparams(2),
        name="nsa_cmp_select",
    )(u3, kc, vct, ovt)


FLASH_TK = 256
FLASH_CHUNK = 256


def _flash_reset(m_ref, l_ref, acc_ref):
    m_ref[...] = jnp.full(m_ref.shape, NEG, F32)
    l_ref[...] = jnp.zeros(l_ref.shape, F32)
    acc_ref[...] = jnp.zeros(acc_ref.shape, F32)


def _flash_chunk(k, vt, qt, m_ref, l_ref, acc_ref, idx, mask):
    st = jnp.dot(k, qt, preferred_element_type=F32)
    if mask is not None:
        st = jnp.where(mask, st, NEG)
    m_old = m_ref[idx]
    m_new = jnp.maximum(m_old, jnp.max(st, axis=0, keepdims=True))
    alpha = jnp.exp(m_old - m_new)
    p = jnp.exp(st - m_new)
    l_ref[idx] = alpha * l_ref[idx] + jnp.sum(p, axis=0, keepdims=True)
    acc_ref[idx] = alpha * acc_ref[idx] + jnp.dot(vt, p.astype(BF16), preferred_element_type=F32)
    m_ref[idx] = m_new


def _transpose_bf16(x):
    return x.astype(F32).T.astype(BF16)


def _merge_rows(lo, hi):
    return jnp.where(_sub_iota((LANES, 1)) < HEAD_DIM, lo, hi)


SLC_TQ = 128


def _slc_kernel(q_ref, biast_ref, k_ref, vt_ref, oh_ref, o_ref, qa_ref, m_ref, l_ref, acc_ref):
    tq, tk = SLC_TQ, FLASH_TK
    qi = pl.program_id(1)
    nblk = A_HEADS // 2
    sub = _sub_iota((LANES, 1))
    biast = biast_ref[...]
    zero = jnp.zeros_like(biast)
    b_lo = jnp.where(sub < HEAD_DIM, biast, zero)
    b_hi = jnp.where(sub >= HEAD_DIM, biast, zero)
    for i in range(nblk):
        qt = _transpose_bf16(q_ref[:, i * LANES:(i + 1) * LANES])
        lo = jnp.concatenate([jnp.where(sub < HEAD_DIM, qt, zero), b_lo], axis=0)
        hi = jnp.concatenate([jnp.where(sub >= HEAD_DIM, qt, zero), b_hi], axis=0)
        qa_ref[i] = jnp.concatenate([lo, hi], axis=1)
    _flash_reset(m_ref, l_ref, acc_ref)

    def process(kt, mask):
        start = pl.multiple_of(kt * tk, tk)
        k = jnp.concatenate([k_ref[pl.ds(start, tk), :], oh_ref[pl.ds(start, tk), :]], axis=1)
        vt = vt_ref[kt]
        for i in range(nblk):
            _flash_chunk(k, vt, qa_ref[i], m_ref, l_ref, acc_ref, i, mask)

    n_full = (qi * tq) // tk

    def body(kt, c):
        process(kt, None)
        return c

    lax.fori_loop(0, n_full, body, 0)
    col = _lane_iota((1, 2 * tq))
    qpos = qi * tq + jnp.where(col >= tq, col - tq, col)
    kpos = n_full * tk + _sub_iota((tk, 1))
    process(n_full, kpos <= qpos)
    for i in range(nblk):
        ot = acc_ref[i] / l_ref[i]
        o_ref[:, i * LANES:(i + 1) * LANES] = _merge_rows(ot[:, 0:tq], ot[:, tq:2 * tq]).T.astype(o_ref.dtype)


def _slc_attention(u3, biast, vt, onehot):
    b, s, _ = u3.shape
    nblk = A_HEADS // 2
    assert 2 * SLC_TQ == FLASH_CHUNK
    return pl.pallas_call(
        _slc_kernel,
        grid=(b, s // SLC_TQ),
        in_specs=[pl.BlockSpec((None, SLC_TQ, A_WIDTH), lambda i, q: (i, q, U_AQ // A_WIDTH)),
                  pl.BlockSpec((None, LANES, SLC_TQ), lambda i, q: (i, 0, q)),
                  pl.BlockSpec((None, s, LANES), lambda i, q: (i, 0, U_KSLC // LANES)),
                  pl.BlockSpec((None, s // FLASH_TK, LANES, FLASH_TK), lambda i, q: (i, 0, 0, 0)),
                  pl.BlockSpec((s, LANES), lambda i, q: (0, 0))],
        out_specs=pl.BlockSpec((None, SLC_TQ, A_WIDTH), lambda i, q: (i, q, 0)),
        out_shape=jax.ShapeDtypeStruct((b, s, A_WIDTH), BF16),
        scratch_shapes=[pltpu.VMEM((nblk, 2 * LANES, FLASH_CHUNK), BF16),
                        pltpu.VMEM((nblk, 1, FLASH_CHUNK), F32), pltpu.VMEM((nblk, 1, FLASH_CHUNK), F32),
                        pltpu.VMEM((nblk, LANES, FLASH_CHUNK), F32)],
        compiler_params=_cparams(2),
        name="nsa_selected",
    )(u3, biast, u3, vt, onehot)


BAND_T = 128


def _stack_heads(qb):
    lane = _lane_iota((1, LANES))
    zero = jnp.zeros_like(qb)
    return jnp.concatenate([jnp.where(lane < HEAD_DIM, qb, zero), jnp.where(lane >= HEAD_DIM, qb, zero)], axis=0)


def _merge_pair(lo, hi):
    return jnp.where(_lane_iota((1, LANES)) < HEAD_DIM, lo, hi)


def _banded_kernel(*refs, nprev, max_dist, shared_kv, with_lse):
    t = BAND_T
    ntile = nprev + 1
    q_ref = refs[0]
    k_refs = refs[1:1 + ntile]
    v_refs = refs[1 + ntile:1 + 2 * ntile]
    o_ref = refs[1 + 2 * ntile]
    lse_ref = refs[2 + 2 * ntile] if with_lse else None
    qi = pl.program_id(1)
    nblk = q_ref.shape[1] // LANES
    nk = ntile * t
    qrel = nprev * t + _sub_iota((t, 1))
    krel = _lane_iota((1, nk))
    rel = qrel - krel
    mask = (rel >= 0) & (rel <= max_dist) & (krel + (qi - nprev) * t >= 0)

    def attend(qs, k, v, groups):
        s = lax.dot_general(qs, k, (((1,), (1,)), ((), ())), preferred_element_type=F32)
        rows = s.shape[0]
        s = jnp.where(mask[None], s.reshape(groups, rows // groups, nk), NEG).reshape(rows, nk)
        m = jnp.max(s, axis=1, keepdims=True)
        p = jnp.exp(s - m)
        l = jnp.sum(p, axis=1, keepdims=True)
        o = jnp.dot(p.astype(BF16), v, preferred_element_type=F32) / l
        return o, m + jnp.log(l)

    def emit(i, o, lse):
        sl = slice(i * LANES, (i + 1) * LANES)
        o_ref[:, sl] = _merge_pair(o[0:t], o[t:2 * t]).astype(o_ref.dtype)
        if with_lse:
            lse_ref[:, sl] = _merge_pair(jnp.broadcast_to(lse[0:t], (t, LANES)),
                                         jnp.broadcast_to(lse[t:2 * t], (t, LANES)))

    if shared_kv:
        k = jnp.concatenate([r[...] for r in k_refs], axis=0)
        v = jnp.concatenate([r[...] for r in v_refs], axis=0)
        qs = jnp.concatenate([_stack_heads(q_ref[:, i * LANES:(i + 1) * LANES]) for i in range(nblk)], axis=0)
        o, lse = attend(qs, k, v, 2 * nblk)
        for i in range(nblk):
            emit(i, o[2 * i * t:(2 * i + 2) * t], lse[2 * i * t:(2 * i + 2) * t])
    else:
        for i in range(nblk):
            sl = slice(i * LANES, (i + 1) * LANES)
            k = jnp.concatenate([r[:, sl] for r in k_refs], axis=0)
            v = jnp.concatenate([r[:, sl] for r in v_refs], axis=0)
            o, lse = attend(_stack_heads(q_ref[:, sl]), k, v, 2)
            emit(i, o, lse)


def _banded(src, *, nseq, seqlen, q_off, k_off, v_off, kv_width, max_dist, with_lse):
    t = BAND_T
    nq = seqlen // t
    nprev = -(-max_dist // t)
    shared = kv_width == LANES

    def kvmap(off, d):
        return lambda n, q: (n * nq + jnp.maximum(q - d, 0), off // kv_width)

    in_specs = [pl.BlockSpec((t, A_WIDTH), lambda n, q: (n * nq + q, q_off // A_WIDTH))]
    in_specs += [pl.BlockSpec((t, kv_width), kvmap(k_off, d)) for d in range(nprev, -1, -1)]
    in_specs += [pl.BlockSpec((t, kv_width), kvmap(v_off, d)) for d in range(nprev, -1, -1)]
    out_specs = [pl.BlockSpec((t, A_WIDTH), lambda n, q: (n * nq + q, 0))]
    out_shape = [jax.ShapeDtypeStruct((nseq * seqlen, A_WIDTH), BF16)]
    if with_lse:
        out_specs.append(pl.BlockSpec((t, A_WIDTH), lambda n, q: (n * nq + q, 0)))
        out_shape.append(jax.ShapeDtypeStruct((nseq * seqlen, A_WIDTH), F32))
    return pl.pallas_call(
        functools.partial(_banded_kernel, nprev=nprev, max_dist=max_dist, shared_kv=shared, with_lse=with_lse),
        grid=(nseq, nq),
        in_specs=in_specs,
        out_specs=out_specs,
        out_shape=out_shape,
        compiler_params=_cparams(2),
        name="banded_attention",
    )(*([src] * (1 + 2 * (nprev + 1))))


MLA_TM = 512


def _rms(x, g):
    return x * lax.rsqrt(jnp.mean(x * x, axis=-1, keepdims=True) + EPS) * g


def _mla_prep_kernel(cq_ref, ckv_ref, kr_ref, c_ref, s1_ref, s2_ref, gq_ref, gkv_ref, wq_ref, wk_ref, wv_ref,
                     q_out, k_out, vt_out):
    half = QK_ROPE // 2
    tk = FLASH_TK
    cm, s1, s2 = c_ref[...], s1_ref[...], s2_ref[...]

    def rope(xc):
        return xc * cm + pltpu.roll(xc, half, 1) * s1 + pltpu.roll(xc, LANES - half, 1) * s2

    qn = _rms(cq_ref[...].astype(F32), gq_ref[...]).astype(BF16)
    q = jnp.dot(qn, wq_ref[...], preferred_element_type=F32)
    kvn = _rms(ckv_ref[...].astype(F32), gkv_ref[...]).astype(BF16)
    kk = jnp.dot(kvn, wk_ref[...], preferred_element_type=F32)
    kr = rope(kr_ref[...].astype(F32))
    for h in range(B_HEADS):
        sl = slice(h * LANES, (h + 1) * LANES)
        q_out[:, sl] = (rope(q[:, sl]) * MLA_SCALE).astype(q_out.dtype)
        k_out[:, sl] = (kk[:, sl] + kr).astype(k_out.dtype)
    v = jnp.dot(kvn, wv_ref[...], preferred_element_type=F32)
    for j in range(v.shape[0] // tk):
        for hp in range(B_HEADS // 2):
            vt_out[j, hp] = v[j * tk:(j + 1) * tk, hp * LANES:(hp + 1) * LANES].T.astype(vt_out.dtype)


def _mla_prep(u, tabs, g_q, g_kv, wq, wk, wv, layer):
    t = u.shape[0]
    tm = min(MLA_TM, t)
    row = lambda i: (i, 0)
    wide = B_HEADS * LANES
    assert U_BCQ % Q_LORA == 0 and tm % FLASH_TK == 0
    return pl.pallas_call(
        _mla_prep_kernel,
        grid=(t // tm,),
        in_specs=[pl.BlockSpec((tm, Q_LORA), lambda i: (i, U_BCQ // Q_LORA)),
                  pl.BlockSpec((tm, LANES), lambda i: (i, U_BCKV // LANES)),
                  pl.BlockSpec((tm, LANES), lambda i: (i, U_BKR // LANES)),
                  pl.BlockSpec((tm, LANES), row), pl.BlockSpec((tm, LANES), row), pl.BlockSpec((tm, LANES), row),
                  pl.BlockSpec((None, 1, Q_LORA), lambda i: (layer, 0, 0)),
                  pl.BlockSpec((None, 1, KV_LORA), lambda i: (layer, 0, 0)),
                  pl.BlockSpec((None, Q_LORA, wide), lambda i: (layer, 0, 0)),
                  pl.BlockSpec((None, KV_LORA, wide), lambda i: (layer, 0, 0)),
                  pl.BlockSpec((None, KV_LORA, B_WIDTH), lambda i: (layer, 0, 0))],
        out_specs=[pl.BlockSpec((tm, wide), row), pl.BlockSpec((tm, wide), row),
                   pl.BlockSpec((tm // FLASH_TK, B_HEADS // 2, LANES, FLASH_TK), lambda i: (i, 0, 0, 0))],
        out_shape=[jax.ShapeDtypeStruct((t, wide), BF16), jax.ShapeDtypeStruct((t, wide), BF16),
                   jax.ShapeDtypeStruct((t // FLASH_TK, B_HEADS // 2, LANES, FLASH_TK), BF16)],
        compiler_params=_cparams(1),
        name="mla_prep",
    )(u, u, u, *tabs, g_q, g_kv, wq, wk, wv)


MLA_TQ = 512


def _mla_flash_kernel(q_ref, k_ref, vt_ref, o_ref, qt_ref, m_ref, l_ref, acc_ref):
    tq, tk, cw = MLA_TQ, FLASH_TK, FLASH_CHUNK
    nhalf = tq // cw
    qi = pl.program_id(2)
    for h in range(2):
        qt_ref[h] = _transpose_bf16(q_ref[:, h * LANES:(h + 1) * LANES])
    _flash_reset(m_ref, l_ref, acc_ref)

    def process(kt, first_half, masked):
        start = pl.multiple_of(kt * tk, tk)
        vt = vt_ref[kt]
        for h in range(2):
            k = k_ref[pl.ds(start, tk), h * LANES:(h + 1) * LANES]
            for half in range(first_half, nhalf):
                mask = None
                if masked and half == first_half:
                    qpos = qi * tq + half * cw + _lane_iota((1, cw))
                    mask = (kt * tk + _sub_iota((tk, 1))) <= qpos
                _flash_chunk(k, vt, qt_ref[h, :, half * cw:(half + 1) * cw], m_ref, l_ref, acc_ref,
                             h * nhalf + half, mask)

    n_full = qi * (tq // tk)

    def body(kt, c):
        process(kt, 0, False)
        return c

    lax.fori_loop(0, n_full, body, 0)
    for d in range(tq // tk):
        process(n_full + d, d, True)
    ots = []
    for h in range(2):
        ots.append(jnp.concatenate([acc_ref[h * nhalf + half] / l_ref[h * nhalf + half] for half in range(nhalf)],
                                   axis=1))
    o_ref[...] = _merge_rows(ots[0], ots[1]).T.astype(o_ref.dtype)


def _mla_flash(q, k, vt, batch, seq):
    assert FLASH_TK == FLASH_CHUNK
    q3 = q.reshape(batch, seq, -1)
    k3 = k.reshape(batch, seq, -1)
    vt5 = vt.reshape(batch, seq // FLASH_TK, B_HEADS // 2, LANES, FLASH_TK)
    nchain = 2 * (MLA_TQ // FLASH_CHUNK)
    o = pl.pallas_call(
        _mla_flash_kernel,
        grid=(batch, B_HEADS // 2, seq // MLA_TQ),
        in_specs=[pl.BlockSpec((None, MLA_TQ, 2 * LANES), lambda b, h, i: (b, i, h)),
                  pl.BlockSpec((None, seq, 2 * LANES), lambda b, h, i: (b, 0, h)),
                  pl.BlockSpec((None, seq // FLASH_TK, None, LANES, FLASH_TK), lambda b, h, i: (b, 0, h, 0, 0))],
        out_specs=pl.BlockSpec((None, MLA_TQ, LANES), lambda b, h, i: (b, i, h)),
        out_shape=jax.ShapeDtypeStruct((batch, seq, B_WIDTH), BF16),
        scratch_shapes=[pltpu.VMEM((2, LANES, MLA_TQ), BF16),
                        pltpu.VMEM((nchain, 1, FLASH_CHUNK), F32), pltpu.VMEM((nchain, 1, FLASH_CHUNK), F32),
                        pltpu.VMEM((nchain, LANES, FLASH_CHUNK), F32)],
        compiler_params=_cparams(3),
        name="mla_flash",
    )(q3, k3, vt5)
    return o.reshape(batch * seq, B_WIDTH)


OUT_TM = 256


def _out_kernel(x_ref, p_ref, ocmp_ref, oslc_ref, owin_ref, gate_ref, e_ref, az_ref, ob_ref, bz_ref,
                oc0_ref, oc1_ref, oc2_ref, l0_ref, l1_ref, l2_ref, cz_ref, mg0_ref, mg1_ref, mg2_ref,
                wa_ref, wb_ref, wc_ref, wout_ref, wplg_ref, wple_ref, gpost_ref, o_ref):
    sig = jax.nn.sigmoid
    f = lambda r: r[...].astype(F32)

    def silu(z):
        return z * sig(z)

    def mm(a, w_ref):
        return jnp.dot(a.astype(BF16), w_ref[...], preferred_element_type=F32)

    gs = sig(jnp.dot(gate_ref[...], e_ref[...], preferred_element_type=F32))
    o_a = (gs[:, 0:A_WIDTH] * f(ocmp_ref) + gs[:, A_WIDTH:2 * A_WIDTH] * f(oslc_ref)
           + gs[:, 2 * A_WIDTH:3 * A_WIDTH] * f(owin_ref))
    y_a = mm(o_a * silu(f(az_ref)), wa_ref)
    y_b = mm(f(ob_ref) * silu(f(bz_ref)), wb_ref)
    l0, l1, l2 = l0_ref[...], l1_ref[...], l2_ref[...]
    mx = jnp.maximum(jnp.maximum(l0, l1), l2)
    e0, e1, e2 = jnp.exp(l0 - mx), jnp.exp(l1 - mx), jnp.exp(l2 - mx)
    o_c = (e0 * f(oc0_ref) + e1 * f(oc1_ref) + e2 * f(oc2_ref)) / (e0 + e1 + e2)
    y_c = mm(o_c * silu(f(cz_ref)), wc_ref)
    mix = sig(f(mg0_ref)) * y_a + sig(f(mg1_ref)) * y_b + sig(f(mg2_ref)) * y_c
    y = mm(mix, wout_ref)
    x1 = x_ref[...] + _rms(y, gpost_ref[...])
    o_ref[...] = x1 + sig(mm(x1, wplg_ref)) * mm(p_ref[...], wple_ref)


def _out_layer(x2, p, layer, o_cmp, o_slc, o_win, u, expand, o_b, o_c, lse_c, w):
    t = x2.shape[0]
    tm = min(OUT_TM, t)
    row = lambda i: (i, 0)
    ucol = lambda off, width: pl.BlockSpec((tm, width), lambda i: (i, off // width))
    half = pl.BlockSpec((tm, A_WIDTH), row)
    wspec = lambda r, c: pl.BlockSpec((None, r, c), lambda i: (layer, 0, 0))
    in_specs = [pl.BlockSpec((tm, D_MODEL), row),
                pl.BlockSpec((None, tm, PLE_DIM), lambda i: (layer, i, 0)),
                half, half, half,
                ucol(U_GATE, LANES), pl.BlockSpec((LANES, 3 * A_WIDTH), lambda i: (0, 0)),
                ucol(U_AZ, A_WIDTH), half, ucol(U_BZ, B_WIDTH),
                half, half, half, half, half, half, ucol(U_CZ, C_WIDTH),
                ucol(U_MG, D_MODEL), ucol(U_MG + D_MODEL, D_MODEL), ucol(U_MG + 2 * D_MODEL, D_MODEL),
                wspec(A_WIDTH, D_MODEL), wspec(B_WIDTH, D_MODEL), wspec(C_WIDTH, D_MODEL),
                wspec(D_MODEL, D_MODEL), wspec(D_MODEL, D_MODEL), wspec(PLE_DIM, D_MODEL),
                wspec(1, D_MODEL)]
    return pl.pallas_call(
        _out_kernel,
        grid=(t // tm,),
        in_specs=in_specs,
        out_specs=pl.BlockSpec((tm, D_MODEL), row),
        out_shape=jax.ShapeDtypeStruct((t, D_MODEL), F32),
        compiler_params=_cparams(1),
        name="out_layer",
    )(x2, p, o_cmp, o_slc, o_win, u, expand, u, o_b, u, *o_c, *lse_c, u, u, u, u,
      w["w_a"], w["w_b"], w["w_c"], w["w_out"], w["w_plg"], w["w_ple"], w["g_post"])


def _overlap_t(seq):
    nc = seq // CMP_STRIDE
    ns = seq // SLC_LEN
    cs = np.arange(nc)[:, None] * CMP_STRIDE
    js = np.arange(ns)[None, :] * SLC_LEN
    ov = np.clip(np.minimum(cs + CMP_LEN, js + SLC_LEN) - np.maximum(cs, js), 0, None).astype(np.float32) / CMP_LEN
    return jnp.asarray(ov.T)


def _block_onehot(seq):
    blk = np.arange(seq)[:, None] // SLC_LEN
    return jnp.asarray((np.arange(LANES)[None, :] % HEAD_DIM == blk).astype(np.float32), dtype=BF16)


def _gate_expand():
    e = np.zeros((LANES, 3 * A_WIDTH), np.float32)
    for br in range(3):
        for i in range(A_HEADS // 2):
            for g in range(A_KV_GROUPS):
                h = g * (A_HEADS // 2) + i
                c0 = br * A_WIDTH + i * LANES + g * HEAD_DIM
                e[h * 3 + br, c0:c0 + HEAD_DIM] = 1.0
    return jnp.asarray(e, dtype=BF16)


def kernel(x, p, positions, g_pre, g_post, w_in, nsa_cmp_pe, nsa_cmp_w1, nsa_cmp_w2, w_a, mla_g_q, mla_w_uq,
           mla_g_kv, mla_w_ukv, w_b, w_c, w_out, w_ple, w_plg):
    batch, seq, _ = x.shape
    depth = w_in.shape[0]
    t = batch * seq
    nch = seq // CMP_STRIDE

    part_tabs, mla_tabs = _rope_tables(positions)
    dils = [dil for _, dil in C_PAIRS]
    assert dils[0] == 1
    perm_tabs = {dil: tuple(_to_residue_major(a, batch, dil) for a in part_tabs) for dil in dils[1:]}
    w_main, w_groups = _regroup_w_in(w_in)
    g_pre3 = g_pre.reshape(depth, 1, D_MODEL)
    pe = nsa_cmp_pe.reshape(depth, 2, 2, CMP_STRIDE * HEAD_DIM)
    w1 = nsa_cmp_w1.astype(BF16)
    w2 = nsa_cmp_w2.astype(BF16)
    wq = mla_w_uq.reshape(depth, Q_LORA, B_HEADS, QK_NOPE + QK_ROPE)
    wq = jnp.pad(wq, ((0, 0), (0, 0), (0, 0), (0, LANES - QK_NOPE - QK_ROPE))).reshape(depth, Q_LORA, -1).astype(BF16)
    wkv = mla_w_ukv.reshape(depth, KV_LORA, B_HEADS, QK_NOPE + V_DIM)
    wk = jnp.pad(wkv[..., :QK_NOPE], ((0, 0), (0, 0), (0, 0), (0, LANES - QK_NOPE))).reshape(depth, KV_LORA, -1)
    wk = wk.astype(BF16)
    wv = wkv[..., QK_NOPE:].reshape(depth, KV_LORA, B_WIDTH).astype(BF16)
    gq3 = mla_g_q.reshape(depth, 1, Q_LORA)
    gkv3 = mla_g_kv.reshape(depth, 1, KV_LORA)
    w_a_p = w_a.reshape(depth, 2, 4, HEAD_DIM, D_MODEL).swapaxes(1, 2).reshape(depth, A_WIDTH, D_MODEL)
    wts = {"w_a": w_a_p.astype(BF16), "w_b": w_b.astype(BF16), "w_c": w_c.astype(BF16),
           "w_out": w_out.astype(BF16), "w_plg": w_plg.astype(BF16), "w_ple": w_ple.astype(BF16),
           "g_post": g_post.reshape(depth, 1, D_MODEL)}
    p3 = p.reshape(depth, t, PLE_DIM)
    ovt = _overlap_t(seq)
    onehot = _block_onehot(seq)
    expand = _gate_expand()

    x2 = x.reshape(t, D_MODEL)
    for layer in range(depth):
        h = _prenorm(x2, g_pre3, layer)
        u = _inproj(h, w_main, layer, part_tabs, ROPE_COLS)
        u3 = u.reshape(batch, seq, DP)
        kv_cmp = jnp.stack([u3[:, :, U_KCMP:U_KCMP + LANES], u3[:, :, U_VCMP:U_VCMP + LANES]], axis=1)
        xc = kv_cmp.reshape(batch, 2, nch, CMP_STRIDE, A_KV_GROUPS, HEAD_DIM).transpose(0, 1, 4, 2, 3, 5)
        xc = xc.reshape(batch, 2, A_KV_GROUPS, nch, CMP_STRIDE * HEAD_DIM)
        cmp_tok = _compress(xc, pe, w1, w2, layer)
        kc = cmp_tok[:, 0].transpose(0, 2, 1, 3).reshape(batch, nch, LANES)
        vct = cmp_tok[:, 1].transpose(0, 1, 3, 2).reshape(batch, LANES, nch)
        o_cmp, biast = _cmp_select(u3, kc, vct, ovt)
        vt_slc = u3[:, :, U_VSLC:U_VSLC + LANES].reshape(batch, seq // FLASH_TK, FLASH_TK, LANES).swapaxes(2, 3)
        o_slc = _slc_attention(u3, biast, vt_slc, onehot)
        (o_win,) = _banded(u, nseq=batch, seqlen=seq, q_off=U_AQ, k_off=U_KWIN, v_off=U_VWIN,
                           kv_width=LANES, max_dist=WIN - 1, with_lse=False)
        q_b, k_b, vt_b = _mla_prep(u, mla_tabs, gq3, gkv3, wq, wk, wv, layer)
        o_b = _mla_flash(q_b, k_b, vt_b, batch, seq)
        o_c, lse_c = [], []
        for gi, (window, dil) in enumerate(C_PAIRS):
            if dil == 1:
                o_g, lse_g = _banded(u, nseq=batch, seqlen=seq, q_off=U_CQ0, k_off=U_CK0, v_off=U_CV0,
                                     kv_width=C_WIDTH, max_dist=window, with_lse=True)
            else:
                u_g = _inproj(_to_residue_major(h, batch, dil), w_groups[gi - 1], layer, perm_tabs[dil],
                              DG_ROPE_COLS)
                o_g, lse_g = _banded(u_g, nseq=batch * dil, seqlen=seq // dil, q_off=0, k_off=C_WIDTH,
                                     v_off=2 * C_WIDTH, kv_width=C_WIDTH, max_dist=window // dil, with_lse=True)
                o_g = _from_residue_major(o_g, batch, dil)
                lse_g = _from_residue_major(lse_g, batch, dil)
            o_c.append(o_g)
            lse_c.append(lse_g)
        x2 = _out_layer(x2, p3, layer, o_cmp.reshape(t, A_WIDTH), o_slc.reshape(t, A_WIDTH), o_win, u, expand,
                        o_b, o_c, lse_c, wts)
    return x2.reshape(batch, seq, D_MODEL)
```

```python
import functools

import numpy as np
import jax
import jax.numpy as jnp
from jax import lax
from jax.experimental import pallas as pl
from jax.experimental.pallas import tpu as pltpu

F32 = jnp.float32
BF16 = jnp.bfloat16

D_MODEL = 1024
PLE_DIM = 256
ROPE_THETA = 500000.0
HEAD_DIM = 64
ROT_DIM = HEAD_DIM // 4
EPS = 1e-6
NEG = -1e30
BIG = 1e30
LANES = 128

A_HEADS = 8
A_KV_GROUPS = 2
A_WIDTH = A_HEADS * HEAD_DIM
CMP_LEN = 32
CMP_STRIDE = 16
CMP_HID = 256
SLC_LEN = 64
SLC_TOPK = 16
WIN = 512

B_HEADS = 8
Q_LORA = 384
KV_LORA = 128
QK_NOPE = 64
QK_ROPE = 32
V_DIM = 64
B_WIDTH = B_HEADS * V_DIM

C_PAIRS = ((128, 1), (512, 4), (2048, 16))
C_GROUPS = len(C_PAIRS)
C_HEADS = 8
C_WIDTH = C_HEADS * HEAD_DIM

IN_SIZES = (A_WIDTH, 6 * A_KV_GROUPS * HEAD_DIM, 3 * A_HEADS, A_WIDTH, Q_LORA, KV_LORA, QK_ROPE,
            B_WIDTH, C_GROUPS * 3 * C_WIDTH, C_WIDTH, 3 * D_MODEL)
IN_OFF = tuple(int(v) for v in np.cumsum((0,) + IN_SIZES))
(OFF_AQ, OFF_AKV, OFF_AG, OFF_AZ, OFF_BCQ, OFF_BCKV, OFF_BKR, OFF_BZ, OFF_CQKV, OFF_CZ, OFF_MG) = IN_OFF[:-1]

U_AQ = 0
U_CQ0 = 512
U_CK0 = 1024
U_KCMP = 1536
U_KSLC = 1664
U_KWIN = 1792
ROPE_COLS = 1920
U_VCMP = 1920
U_AZ = 2048
U_BZ = 2560
U_CV0 = 3072
U_CZ = 3584
U_MG = 4096
U_VSLC = 7168
U_VWIN = 7296
U_GATE = 7424
U_BCKV = 7552
U_BCQ = 7680
U_BKR = 8064
DP = 8192
DG_COLS = 3 * C_WIDTH
DG_ROPE_COLS = 2 * C_WIDTH

LOG2E = float(np.log2(np.e))
Q_SCALE = float(HEAD_DIM ** -0.5) * LOG2E
MLA_SCALE = float((QK_NOPE + QK_ROPE) ** -0.5) * LOG2E
VMEM_LIMIT = 48 * 1024 * 1024


def _cparams(n_axes):
    return pltpu.CompilerParams(dimension_semantics=("arbitrary",) * n_axes, vmem_limit_bytes=VMEM_LIMIT)


def _lane_iota(shape):
    return lax.broadcasted_iota(jnp.int32, shape, len(shape) - 1)


def _sub_iota(shape):
    return lax.broadcasted_iota(jnp.int32, shape, len(shape) - 2)


def _pair_cols(w):
    lead = w.shape[:-1]
    return w.reshape(lead + (2, 4, HEAD_DIM)).swapaxes(-3, -2).reshape(lead + (A_WIDTH,))


def _regroup_w_in(w_in):
    def col(off, n):
        return w_in[..., off:off + n]
    z = lambda n: jnp.zeros(w_in.shape[:-1] + (n,), w_in.dtype)
    akv = lambda which: col(OFF_AKV + which * 128, 128)
    cq = lambda gi, t: col(OFF_CQKV + gi * 3 * C_WIDTH + t * C_WIDTH, C_WIDTH)
    pieces = [_pair_cols(col(OFF_AQ, A_WIDTH)) * Q_SCALE, cq(0, 0) * Q_SCALE, cq(0, 1)]
    pieces += [akv(0), akv(2), akv(4)]
    pieces += [akv(1)]
    pieces += [_pair_cols(col(OFF_AZ, A_WIDTH)), col(OFF_BZ, B_WIDTH), cq(0, 2)]
    pieces += [col(OFF_CZ, C_WIDTH), col(OFF_MG, 3 * D_MODEL)]
    pieces += [akv(3), akv(5)]
    pieces += [col(OFF_AG, 3 * A_HEADS), z(128 - 3 * A_HEADS)]
    pieces += [col(OFF_BCKV, KV_LORA), col(OFF_BCQ, Q_LORA)]
    pieces += [z(64), col(OFF_BKR, QK_ROPE), z(32)]
    main = jnp.concatenate(pieces, axis=-1).astype(BF16)
    assert main.shape[-1] == DP
    groups = [jnp.concatenate([cq(gi, 0) * Q_SCALE, cq(gi, 1), cq(gi, 2)], axis=-1).astype(BF16)
              for gi in range(1, C_GROUPS)]
    return main, groups


def _rope_tables(positions):
    pos = positions.astype(F32).reshape(-1)

    def trig(dim):
        inv = ROPE_THETA ** (-jnp.arange(0, dim, 2, dtype=F32) / dim)
        ang = pos[:, None] * inv
        return jnp.cos(ang), jnp.sin(ang)

    t = pos.shape[0]
    ones = jnp.ones((t, 1), F32)
    zeros = jnp.zeros((t, 1), F32)
    cos, sin = trig(ROT_DIM)
    rest = HEAD_DIM - ROT_DIM
    c_head = jnp.concatenate([cos, cos, jnp.broadcast_to(ones, (t, rest))], axis=1)
    s1_head = jnp.concatenate([0 * sin, sin, jnp.broadcast_to(zeros, (t, rest))], axis=1)
    s2_head = jnp.concatenate([-sin, 0 * sin, jnp.broadcast_to(zeros, (t, rest))], axis=1)
    part = tuple(jnp.concatenate([a, a], axis=1) for a in (c_head, s1_head, s2_head))
    cos, sin = trig(QK_ROPE)
    lo = jnp.broadcast_to(ones, (t, 64))
    lz = jnp.broadcast_to(zeros, (t, 64))
    hi1 = jnp.broadcast_to(ones, (t, 32))
    hz = jnp.broadcast_to(zeros, (t, 32))
    mla = (jnp.concatenate([lo, cos, cos, hi1], axis=1),
           jnp.concatenate([lz, 0 * sin, sin, hz], axis=1),
           jnp.concatenate([lz, -sin, 0 * sin, hz], axis=1))
    return part, mla


def _to_residue_major(a, batch, dil):
    t, c = a.shape
    ln = t // batch // dil
    return a.reshape(batch, ln, dil, c).swapaxes(1, 2).reshape(t, c)


NORM_TM = 512
IN_TM = 1024
IN_TN = 512


def _norm_kernel(x_ref, g_ref, o_ref, *rest, dils):
    o_dil, y_ref = rest[:len(dils)], rest[len(dils)]
    x = x_ref[...]
    ms = jnp.mean(x * x, axis=-1, keepdims=True)
    y = x * lax.rsqrt(ms + EPS) * g_ref[...]
    o_ref[...] = y.astype(o_ref.dtype)
    tm = x.shape[0]
    nlane = y_ref.shape[0]
    for c in range(nlane):
        y_ref[c] = y[:, c * LANES:(c + 1) * LANES]
    for o_d, dil in zip(o_dil, dils):
        for r in range(dil):
            for c in range(nlane):
                o_d[r, :, c * LANES:(c + 1) * LANES] = (
                    y_ref[c, pl.ds(r, tm // dil, stride=dil), :].astype(o_d.dtype))


def _prenorm(x2, g_pre, layer, batch, dils):
    t = x2.shape[0]
    seq = t // batch
    tm = min(NORM_TM, seq)
    nper = seq // tm
    out_specs = [pl.BlockSpec((tm, D_MODEL), lambda b, i: (b * nper + i, 0))]
    out_shape = [jax.ShapeDtypeStruct((t, D_MODEL), BF16)]
    for dil in dils:
        out_specs.append(pl.BlockSpec((None, dil, tm // dil, D_MODEL), lambda b, i: (b, 0, i, 0)))
        out_shape.append(jax.ShapeDtypeStruct((batch, dil, seq // dil, D_MODEL), BF16))
    outs = pl.pallas_call(
        functools.partial(_norm_kernel, dils=tuple(dils)),
        grid=(batch, nper),
        in_specs=[pl.BlockSpec((tm, D_MODEL), lambda b, i: (b * nper + i, 0)),
                  pl.BlockSpec((None, 1, D_MODEL), lambda b, i: (layer, 0, 0))],
        out_specs=out_specs,
        out_shape=out_shape,
        scratch_shapes=[pltpu.VMEM((D_MODEL // LANES, tm, LANES), F32)],
        compiler_params=_cparams(2),
        name="prenorm",
    )(x2, g_pre)
    return outs[0], [o.reshape(t, D_MODEL) for o in outs[1:]]


def _inproj_kernel(h_ref, w_ref, c_ref, s1_ref, s2_ref, o_ref, *, rope_cols):
    j = pl.program_id(1)
    acc = jnp.dot(h_ref[...], w_ref[...], preferred_element_type=F32)
    nsub = IN_TN // LANES
    full_tiles, rem = divmod(rope_cols // LANES, nsub)

    def emit(n_rope):
        for c in range(nsub):
            xc = acc[:, c * LANES:(c + 1) * LANES]
            if c < n_rope:
                xc = (xc * c_ref[...] + pltpu.roll(xc, ROT_DIM // 2, 1) * s1_ref[...]
                      + pltpu.roll(xc, LANES - ROT_DIM // 2, 1) * s2_ref[...])
            o_ref[:, c * LANES:(c + 1) * LANES] = xc.astype(o_ref.dtype)

    pl.when(j < full_tiles)(lambda: emit(nsub))
    pl.when(j == full_tiles)(lambda: emit(rem))
    pl.when(j > full_tiles)(lambda: emit(0))


def _inproj(h, w, layer, tabs, rope_cols):
    t = h.shape[0]
    ncols = w.shape[-1]
    tm = min(IN_TM, t)
    row = lambda i, j: (i, 0)
    return pl.pallas_call(
        functools.partial(_inproj_kernel, rope_cols=rope_cols),
        grid=(t // tm, ncols // IN_TN),
        in_specs=[pl.BlockSpec((tm, D_MODEL), row),
                  pl.BlockSpec((None, D_MODEL, IN_TN), lambda i, j: (layer, 0, j)),
                  pl.BlockSpec((tm, LANES), row), pl.BlockSpec((tm, LANES), row), pl.BlockSpec((tm, LANES), row)],
        out_specs=pl.BlockSpec((tm, IN_TN), lambda i, j: (i, j)),
        out_shape=jax.ShapeDtypeStruct((t, ncols), BF16),
        compiler_params=_cparams(2),
        name="inproj",
    )(h, w, *tabs)


def _compress_kernel(x_ref, pe_ref, w1_ref, w2_ref, o_ref):
    half = CMP_STRIDE * HEAD_DIM
    x = x_ref[...].astype(F32)
    xa = (x + pe_ref[0:1, :]).astype(BF16)
    xb = (x + pe_ref[1:2, :]).astype(BF16)
    a = jnp.dot(xa, w1_ref[0:half, :], preferred_element_type=F32)
    b = jnp.dot(xb, w1_ref[half:2 * half, :], preferred_element_type=F32)
    nch = a.shape[0]
    hid = a + pltpu.roll(b, nch - 1, 0)
    act = hid * jax.nn.sigmoid(hid)
    o_ref[...] = jnp.dot(act.astype(BF16), w2_ref[...], preferred_element_type=F32).astype(o_ref.dtype)


def _compress(xc, pe, w1, w2, layer):
    b, _, _, nch, width = xc.shape
    return pl.pallas_call(
        _compress_kernel,
        grid=(b, 2, A_KV_GROUPS),
        in_specs=[pl.BlockSpec((None, None, None, nch, width), lambda i, w, g: (i, w, g, 0, 0)),
                  pl.BlockSpec((None, None, 2, width), lambda i, w, g: (layer, w, 0, 0)),
                  pl.BlockSpec((None, None, 2 * width, CMP_HID), lambda i, w, g: (layer, w, 0, 0)),
                  pl.BlockSpec((None, None, CMP_HID, HEAD_DIM), lambda i, w, g: (layer, w, 0, 0))],
        out_specs=pl.BlockSpec((None, None, None, nch, HEAD_DIM), lambda i, w, g: (i, w, g, 0, 0)),
        out_shape=jax.ShapeDtypeStruct((b, 2, A_KV_GROUPS, nch, HEAD_DIM), BF16),
        compiler_params=_cparams(3),
        name="nsa_compress",
    )(xc, pe, w1, w2)


CS_TQ = 128


def _cmp_select_kernel(q_ref, kc_ref, vct_ref, ovt_ref, o_ref, bias_ref, sc_ref):
    tq = CS_TQ
    qi = pl.program_id(1)
    nc = kc_ref.shape[0]
    ns = ovt_ref.shape[0]
    tpos = qi * tq + _lane_iota((1, tq))
    kc = kc_ref[...]
    lane = _lane_iota((1, LANES))
    kst = jnp.concatenate([jnp.where(lane < HEAD_DIM, kc, jnp.zeros_like(kc)),
                           jnp.where(lane >= HEAD_DIM, kc, jnp.zeros_like(kc))], axis=0)
    c_end = _sub_iota((nc, 1)) * CMP_STRIDE + (CMP_LEN - 1)
    cv = c_end <= tpos
    vct = vct_ref[...]
    psum = [jnp.zeros((nc, tq), F32), jnp.zeros((nc, tq), F32)]
    sub = _sub_iota((LANES, 1))
    for i in range(A_HEADS // 2):
        qb = q_ref[:, i * LANES:(i + 1) * LANES]
        st = lax.dot_general(kst, qb, (((1,), (1,)), ((), ())), preferred_element_type=F32)
        ot = []
        for g in range(A_KV_GROUPS):
            s = jnp.where(cv, st[g * nc:(g + 1) * nc], NEG)
            m = jnp.max(s, axis=0, keepdims=True)
            e = jnp.where(cv, jnp.exp2(s - m), 0.0)
            den = jnp.sum(e, axis=0, keepdims=True)
            p = e / jnp.where(den > 0.0, den, 1.0)
            psum[g] = psum[g] + p
            ot.append(jnp.dot(vct, p.astype(BF16), preferred_element_type=F32))
        o_pair = jnp.where(sub < HEAD_DIM, ot[0], ot[1])
        o_ref[:, i * LANES:(i + 1) * LANES] = o_pair.T.astype(o_ref.dtype)
    jdx = _sub_iota((ns, 1))
    cur = tpos // SLC_LEN
    valid = jdx <= cur
    forced = (jdx == 0) | (jdx == cur) | (jdx == cur - 1)
    for g in range(A_KV_GROUPS):
        imp = jnp.dot(ovt_ref[...], psum[g], preferred_element_type=F32, precision=lax.Precision.HIGHEST)
        score = jnp.where(forced, BIG, jnp.where(valid, imp, NEG))
        sc_ref[...] = score
        cnt = jnp.zeros((ns, tq), F32)
        for jp in range(ns):
            row = sc_ref[jp:jp + 1, :]
            tie = jnp.where(jdx > jp, 1.0, 0.0)
            cnt = cnt + jnp.where(row > score, 1.0, 0.0) + jnp.where(row == score, tie, 0.0)
        bias = jnp.where(cnt < float(min(SLC_TOPK, ns)), 0.0, NEG)
        bias_ref[g * ns:(g + 1) * ns, :] = bias.astype(bias_ref.dtype)


def _cmp_select(u3, kc, vct, ovt):
    b, s, _ = u3.shape
    nc = kc.shape[1]
    ns = s // SLC_LEN
    assert A_KV_GROUPS * ns == LANES
    return pl.pallas_call(
        _cmp_select_kernel,
        grid=(b, s // CS_TQ),
        in_specs=[pl.BlockSpec((None, CS_TQ, A_WIDTH), lambda i, q: (i, q, U_AQ // A_WIDTH)),
                  pl.BlockSpec((None, nc, LANES), lambda i, q: (i, 0, 0)),
                  pl.BlockSpec((None, LANES, nc), lambda i, q: (i, 0, 0)),
                  pl.BlockSpec((ns, nc), lambda i, q: (0, 0))],
        out_specs=[pl.BlockSpec((None, CS_TQ, A_WIDTH), lambda i, q: (i, q, 0)),
                   pl.BlockSpec((None, LANES, CS_TQ), lambda i, q: (i, 0, q))],
        out_shape=[jax.ShapeDtypeStruct((b, s, A_WIDTH), BF16), jax.ShapeDtypeStruct((b, LANES, s), BF16)],
        scratch_shapes=[pltpu.VMEM((ns, CS_TQ), F32)],
        compiler_params=_cparams(2),
        name="nsa_cmp_select",
    )(u3, kc, vct, ovt)


FLASH_TK = 256
FLASH_CHUNK = 256


def _flash_reset(m_ref, l_ref, acc_ref):
    m_ref[...] = jnp.full(m_ref.shape, NEG, F32)
    l_ref[...] = jnp.zeros(l_ref.shape, F32)
    acc_ref[...] = jnp.zeros(acc_ref.shape, F32)


def _flash_update(st, vt, m_ref, l_ref, acc_ref, idx, mask):
    if mask is not None:
        st = jnp.where(mask, st, NEG)
    m_old = m_ref[idx]
    m_new = jnp.maximum(m_old, jnp.max(st, axis=0, keepdims=True))
    alpha = jnp.exp2(m_old - m_new)
    p = jnp.exp2(st - m_new)
    l_ref[idx] = alpha * l_ref[idx] + jnp.sum(p, axis=0, keepdims=True)
    acc_ref[idx] = alpha * acc_ref[idx] + jnp.dot(vt, p.astype(BF16), preferred_element_type=F32)
    m_ref[idx] = m_new


def _flash_pipeline(n_pairs, n_chain, put, use):
    for c in range(n_chain):
        put(0, c, 0)

    def pair(j, carry):
        t0 = 2 * j
        for c in range(n_chain):
            put(t0 + 1, c, 1)
            use(t0, c, 0)
        for c in range(n_chain):
            put(t0 + 2, c, 0)
            use(t0 + 1, c, 1)
        return carry

    lax.fori_loop(0, n_pairs, pair, 0)


def _transpose_bf16(x):
    return x.astype(F32).T.astype(BF16)


def _merge_rows(lo, hi):
    return jnp.where(_sub_iota((LANES, 1)) < HEAD_DIM, lo, hi)


SLC_TQ = 128


def _slc_kernel(q_ref, biast_ref, k_ref, vt_ref, oh_ref, o_ref, qa_ref, st_ref, m_ref, l_ref, acc_ref):
    tq, tk = SLC_TQ, FLASH_TK
    qi = pl.program_id(1)
    nblk = A_HEADS // 2
    sub = _sub_iota((LANES, 1))
    biast = biast_ref[...]
    zero = jnp.zeros_like(biast)
    b_lo = jnp.where(sub < HEAD_DIM, biast, zero)
    b_hi = jnp.where(sub >= HEAD_DIM, biast, zero)
    for i in range(nblk):
        qt = _transpose_bf16(q_ref[:, i * LANES:(i + 1) * LANES])
        lo = jnp.concatenate([jnp.where(sub < HEAD_DIM, qt, zero), b_lo], axis=0)
        hi = jnp.concatenate([jnp.where(sub >= HEAD_DIM, qt, zero), b_hi], axis=0)
        qa_ref[i] = jnp.concatenate([lo, hi], axis=1)
    _flash_reset(m_ref, l_ref, acc_ref)

    def put(kt, i, slot):
        start = pl.multiple_of(kt * tk, tk)
        k = jnp.concatenate([k_ref[pl.ds(start, tk), :], oh_ref[pl.ds(start, tk), :]], axis=1)
        st_ref[slot, i] = jnp.dot(k, qa_ref[i], preferred_element_type=F32)

    def use(kt, i, slot, mask=None):
        _flash_update(st_ref[slot, i], vt_ref[kt], m_ref, l_ref, acc_ref, i, mask)

    n_full = (qi * tq) // tk
    n_pairs = n_full // 2
    _flash_pipeline(n_pairs, nblk, put, use)
    col = _lane_iota((1, 2 * tq))
    qpos = qi * tq + jnp.where(col >= tq, col - tq, col)
    causal = (n_full * tk + _sub_iota((tk, 1))) <= qpos

    @pl.when(n_full == 2 * n_pairs)
    def _():
        for i in range(nblk):
            use(n_full, i, 0, causal)

    @pl.when(n_full != 2 * n_pairs)
    def _():
        for i in range(nblk):
            put(n_full, i, 1)
            use(n_full - 1, i, 0)
        for i in range(nblk):
            use(n_full, i, 1, causal)

    for i in range(nblk):
        ot = acc_ref[i] / l_ref[i]
        o_ref[:, i * LANES:(i + 1) * LANES] = _merge_rows(ot[:, 0:tq], ot[:, tq:2 * tq]).T.astype(o_ref.dtype)


def _slc_attention(u3, biast, vt, onehot):
    b, s, _ = u3.shape
    nblk = A_HEADS // 2
    assert 2 * SLC_TQ == FLASH_CHUNK
    return pl.pallas_call(
        _slc_kernel,
        grid=(b, s // SLC_TQ),
        in_specs=[pl.BlockSpec((None, SLC_TQ, A_WIDTH), lambda i, q: (i, q, U_AQ // A_WIDTH)),
                  pl.BlockSpec((None, LANES, SLC_TQ), lambda i, q: (i, 0, q)),
                  pl.BlockSpec((None, s, LANES), lambda i, q: (i, 0, U_KSLC // LANES)),
                  pl.BlockSpec((None, s // FLASH_TK, LANES, FLASH_TK), lambda i, q: (i, 0, 0, 0)),
                  pl.BlockSpec((s, LANES), lambda i, q: (0, 0))],
        out_specs=pl.BlockSpec((None, SLC_TQ, A_WIDTH), lambda i, q: (i, q, 0)),
        out_shape=jax.ShapeDtypeStruct((b, s, A_WIDTH), BF16),
        scratch_shapes=[pltpu.VMEM((nblk, 2 * LANES, FLASH_CHUNK), BF16),
                        pltpu.VMEM((2, nblk, FLASH_TK, FLASH_CHUNK), F32),
                        pltpu.VMEM((nblk, 1, FLASH_CHUNK), F32), pltpu.VMEM((nblk, 1, FLASH_CHUNK), F32),
                        pltpu.VMEM((nblk, LANES, FLASH_CHUNK), F32)],
        compiler_params=_cparams(2),
        name="nsa_selected",
    )(u3, biast, u3, vt, onehot)


BAND_T = 128


def _banded_kernel(*refs, nprev, max_dist, shared_kv, with_lse):
    t = BAND_T
    ntile = nprev + 1
    q_ref = refs[0]
    k_refs = refs[1:1 + ntile]
    v_refs = refs[1 + ntile:1 + 2 * ntile]
    o_ref = refs[1 + 2 * ntile]
    lse_ref = refs[2 + 2 * ntile] if with_lse else None
    qi = pl.program_id(1)
    nblk = q_ref.shape[1] // LANES
    nk = ntile * t
    col = _lane_iota((1, 2 * t))
    qrel = nprev * t + jnp.where(col >= t, col - t, col)
    krel = _sub_iota((nk, 1))
    rel = qrel - krel
    mask = (rel >= 0) & (rel <= max_dist) & (krel + (qi - nprev) * t >= 0)
    sub = _sub_iota((LANES, 1))

    def keys(i):
        sl = slice(None) if shared_kv else slice(i * LANES, (i + 1) * LANES)
        return jnp.concatenate([r[:, sl] for r in k_refs], axis=0)

    def values_t(i):
        sl = slice(None) if shared_kv else slice(i * LANES, (i + 1) * LANES)
        return jnp.concatenate([_transpose_bf16(r[:, sl]) for r in v_refs], axis=1)

    def scores(i, k):
        qt = _transpose_bf16(q_ref[:, i * LANES:(i + 1) * LANES])
        zero = jnp.zeros_like(qt)
        qs = jnp.concatenate([jnp.where(sub < HEAD_DIM, qt, zero), jnp.where(sub >= HEAD_DIM, qt, zero)], axis=1)
        return jnp.dot(k, qs, preferred_element_type=F32)

    ks = [keys(0)] * nblk if shared_kv else [keys(i) for i in range(nblk)]
    sts = [scores(i, ks[i]) for i in range(nblk)]
    vts = [values_t(0)] * nblk if shared_kv else [values_t(i) for i in range(nblk)]
    outs = []
    for i in range(nblk):
        st = jnp.where(mask, sts[i], NEG)
        m = jnp.max(st, axis=0, keepdims=True)
        p = jnp.exp2(st - m)
        l = jnp.sum(p, axis=0, keepdims=True)
        outs.append((jnp.dot(vts[i], p.astype(BF16), preferred_element_type=F32), l, m + jnp.log2(l)))
    for i, (ot, l, lse) in enumerate(outs):
        sl = slice(i * LANES, (i + 1) * LANES)
        ot = ot / l
        o_ref[:, sl] = _merge_rows(ot[:, 0:t], ot[:, t:2 * t]).T.astype(o_ref.dtype)
        if with_lse:
            lse_ref[:, sl] = _merge_rows(jnp.broadcast_to(lse[:, 0:t], (LANES, t)),
                                         jnp.broadcast_to(lse[:, t:2 * t], (LANES, t))).T


def _banded(src, *, nseq, seqlen, q_off, k_off, v_off, kv_width, max_dist, with_lse):
    t = BAND_T
    nq = seqlen // t
    nprev = -(-max_dist // t)
    shared = kv_width == LANES

    def kvmap(off, d):
        return lambda n, q: (n * nq + jnp.maximum(q - d, 0), off // kv_width)

    in_specs = [pl.BlockSpec((t, A_WIDTH), lambda n, q: (n * nq + q, q_off // A_WIDTH))]
    in_specs += [pl.BlockSpec((t, kv_width), kvmap(k_off, d)) for d in range(nprev, -1, -1)]
    in_specs += [pl.BlockSpec((t, kv_width), kvmap(v_off, d)) for d in range(nprev, -1, -1)]
    out_specs = [pl.BlockSpec((t, A_WIDTH), lambda n, q: (n * nq + q, 0))]
    out_shape = [jax.ShapeDtypeStruct((nseq * seqlen, A_WIDTH), BF16)]
    if with_lse:
        out_specs.append(pl.BlockSpec((t, A_WIDTH), lambda n, q: (n * nq + q, 0)))
        out_shape.append(jax.ShapeDtypeStruct((nseq * seqlen, A_WIDTH), F32))
    return pl.pallas_call(
        functools.partial(_banded_kernel, nprev=nprev, max_dist=max_dist, shared_kv=shared, with_lse=with_lse),
        grid=(nseq, nq),
        in_specs=in_specs,
        out_specs=out_specs,
        out_shape=out_shape,
        compiler_params=_cparams(2),
        name="banded_attention",
    )(*([src] * (1 + 2 * (nprev + 1))))


MLA_TM = 512


def _rms(x, g):
    return x * lax.rsqrt(jnp.mean(x * x, axis=-1, keepdims=True) + EPS) * g


def _mla_prep_kernel(cq_ref, ckv_ref, kr_ref, c_ref, s1_ref, s2_ref, gq_ref, gkv_ref, wq_ref, wk_ref, wv_ref,
                     q_out, k_out, vt_out):
    half = QK_ROPE // 2
    tk = FLASH_TK
    cm, s1, s2 = c_ref[...], s1_ref[...], s2_ref[...]

    def rope(xc):
        return xc * cm + pltpu.roll(xc, half, 1) * s1 + pltpu.roll(xc, LANES - half, 1) * s2

    qn = _rms(cq_ref[...].astype(F32), gq_ref[...]).astype(BF16)
    q = jnp.dot(qn, wq_ref[...], preferred_element_type=F32)
    kvn = _rms(ckv_ref[...].astype(F32), gkv_ref[...]).astype(BF16)
    kk = jnp.dot(kvn, wk_ref[...], preferred_element_type=F32)
    kr = rope(kr_ref[...].astype(F32))
    for h in range(B_HEADS):
        sl = slice(h * LANES, (h + 1) * LANES)
        q_out[:, sl] = (rope(q[:, sl]) * MLA_SCALE).astype(q_out.dtype)
        k_out[:, sl] = (kk[:, sl] + kr).astype(k_out.dtype)
    v = jnp.dot(kvn, wv_ref[...], preferred_element_type=F32)
    for j in range(v.shape[0] // tk):
        for hp in range(B_HEADS // 2):
            vt_out[j, hp] = v[j * tk:(j + 1) * tk, hp * LANES:(hp + 1) * LANES].T.astype(vt_out.dtype)


def _mla_prep(u, tabs, g_q, g_kv, wq, wk, wv, layer):
    t = u.shape[0]
    tm = min(MLA_TM, t)
    row = lambda i: (i, 0)
    wide = B_HEADS * LANES
    assert U_BCQ % Q_LORA == 0 and tm % FLASH_TK == 0
    return pl.pallas_call(
        _mla_prep_kernel,
        grid=(t // tm,),
        in_specs=[pl.BlockSpec((tm, Q_LORA), lambda i: (i, U_BCQ // Q_LORA)),
                  pl.BlockSpec((tm, LANES), lambda i: (i, U_BCKV // LANES)),
                  pl.BlockSpec((tm, LANES), lambda i: (i, U_BKR // LANES)),
                  pl.BlockSpec((tm, LANES), row), pl.BlockSpec((tm, LANES), row), pl.BlockSpec((tm, LANES), row),
                  pl.BlockSpec((None, 1, Q_LORA), lambda i: (layer, 0, 0)),
                  pl.BlockSpec((None, 1, KV_LORA), lambda i: (layer, 0, 0)),
                  pl.BlockSpec((None, Q_LORA, wide), lambda i: (layer, 0, 0)),
                  pl.BlockSpec((None, KV_LORA, wide), lambda i: (layer, 0, 0)),
                  pl.BlockSpec((None, KV_LORA, B_WIDTH), lambda i: (layer, 0, 0))],
        out_specs=[pl.BlockSpec((tm, wide), row), pl.BlockSpec((tm, wide), row),
                   pl.BlockSpec((tm // FLASH_TK, B_HEADS // 2, LANES, FLASH_TK), lambda i: (i, 0, 0, 0))],
        out_shape=[jax.ShapeDtypeStruct((t, wide), BF16), jax.ShapeDtypeStruct((t, wide), BF16),
                   jax.ShapeDtypeStruct((t // FLASH_TK, B_HEADS // 2, LANES, FLASH_TK), BF16)],
        compiler_params=_cparams(1),
        name="mla_prep",
    )(u, u, u, *tabs, g_q, g_kv, wq, wk, wv)


MLA_TQ = 512


def _mla_flash_kernel(q_ref, k_ref, vt_ref, o_ref, qt_ref, st_ref, m_ref, l_ref, acc_ref):
    tq, tk, cw = MLA_TQ, FLASH_TK, FLASH_CHUNK
    nhalf = tq // cw
    qi = pl.program_id(2)
    for h in range(2):
        qt_ref[h] = _transpose_bf16(q_ref[:, h * LANES:(h + 1) * LANES])
    _flash_reset(m_ref, l_ref, acc_ref)

    def put(kt, c, slot):
        h, half = divmod(c, nhalf)
        start = pl.multiple_of(kt * tk, tk)
        k = k_ref[pl.ds(start, tk), h * LANES:(h + 1) * LANES]
        st_ref[slot, c] = jnp.dot(k, qt_ref[h, :, half * cw:(half + 1) * cw], preferred_element_type=F32)

    def use(kt, c, slot, masked=False):
        mask = None
        if masked:
            qpos = qi * tq + (c % nhalf) * cw + _lane_iota((1, cw))
            mask = (kt * tk + _sub_iota((tk, 1))) <= qpos
        _flash_update(st_ref[slot, c], vt_ref[kt], m_ref, l_ref, acc_ref, c, mask)

    assert tq == 2 * tk and nhalf == 2
    n_full = 2 * qi
    _flash_pipeline(qi, 2 * nhalf, put, use)
    late = [c for c in range(2 * nhalf) if c % nhalf == 1]
    for c in late:
        put(n_full + 1, c, 1)
    for c in range(2 * nhalf):
        use(n_full, c, 0, masked=(c % nhalf == 0))
    for c in late:
        use(n_full + 1, c, 1, masked=True)
    ots = []
    for h in range(2):
        ots.append(jnp.concatenate([acc_ref[h * nhalf + half] / l_ref[h * nhalf + half] for half in range(nhalf)],
                                   axis=1))
    o_ref[...] = _merge_rows(ots[0], ots[1]).T.astype(o_ref.dtype)


def _mla_flash(q, k, vt, batch, seq):
    assert FLASH_TK == FLASH_CHUNK
    q3 = q.reshape(batch, seq, -1)
    k3 = k.reshape(batch, seq, -1)
    vt5 = vt.reshape(batch, seq // FLASH_TK, B_HEADS // 2, LANES, FLASH_TK)
    nchain = 2 * (MLA_TQ // FLASH_CHUNK)
    o = pl.pallas_call(
        _mla_flash_kernel,
        grid=(batch, B_HEADS // 2, seq // MLA_TQ),
        in_specs=[pl.BlockSpec((None, MLA_TQ, 2 * LANES), lambda b, h, i: (b, i, h)),
                  pl.BlockSpec((None, seq, 2 * LANES), lambda b, h, i: (b, 0, h)),
                  pl.BlockSpec((None, seq // FLASH_TK, None, LANES, FLASH_TK), lambda b, h, i: (b, 0, h, 0, 0))],
        out_specs=pl.BlockSpec((None, MLA_TQ, LANES), lambda b, h, i: (b, i, h)),
        out_shape=jax.ShapeDtypeStruct((batch, seq, B_WIDTH), BF16),
        scratch_shapes=[pltpu.VMEM((2, LANES, MLA_TQ), BF16),
                        pltpu.VMEM((2, nchain, FLASH_TK, FLASH_CHUNK), F32),
                        pltpu.VMEM((nchain, 1, FLASH_CHUNK), F32), pltpu.VMEM((nchain, 1, FLASH_CHUNK), F32),
                        pltpu.VMEM((nchain, LANES, FLASH_CHUNK), F32)],
        compiler_params=_cparams(3),
        name="mla_flash",
    )(q3, k3, vt5)
    return o.reshape(batch * seq, B_WIDTH)


OUT_TM = 256


def _out_kernel(x_ref, p_ref, ocmp_ref, oslc_ref, owin_ref, gate_ref, e_ref, az_ref, ob_ref, bz_ref,
                oc0_ref, oc1_ref, oc2_ref, l0_ref, l1_ref, l2_ref, cz_ref, mg0_ref, mg1_ref, mg2_ref,
                wa_ref, wb_ref, wc_ref, wout_ref, wplg_ref, wple_ref, gpost_ref, o_ref, *tok_refs):
    tok = iter(tok_refs)

    def sig(z):
        return 0.5 * jnp.tanh(0.5 * z) + 0.5

    def f(r):
        if len(r.shape) == 2:
            return r[...].astype(F32)
        dil, n, width = r.shape
        buf = next(tok)
        for res in range(dil):
            blk = r[res].astype(F32)
            for c in range(width // LANES):
                buf[c, pl.ds(res, n, stride=dil), :] = blk[:, c * LANES:(c + 1) * LANES]
        return jnp.concatenate([buf[c] for c in range(width // LANES)], axis=1)

    def silu(z):
        return z * sig(z)

    def mm(a, w_ref):
        return jnp.dot(a.astype(BF16), w_ref[...], preferred_element_type=F32)

    g = sig(gate_ref[...].astype(F32))
    g_hi = g.astype(BF16)
    g_lo = (g - g_hi.astype(F32)).astype(BF16)
    gs = (jnp.dot(g_hi, e_ref[...], preferred_element_type=F32)
          + jnp.dot(g_lo, e_ref[...], preferred_element_type=F32))
    o_a = (gs[:, 0:A_WIDTH] * f(ocmp_ref) + gs[:, A_WIDTH:2 * A_WIDTH] * f(oslc_ref)
           + gs[:, 2 * A_WIDTH:3 * A_WIDTH] * f(owin_ref))
    y_a = mm(o_a * silu(f(az_ref)), wa_ref)
    y_b = mm(f(ob_ref) * silu(f(bz_ref)), wb_ref)
    l0, l1, l2 = f(l0_ref), f(l1_ref), f(l2_ref)
    mx = jnp.maximum(jnp.maximum(l0, l1), l2)
    e0, e1, e2 = jnp.exp2(l0 - mx), jnp.exp2(l1 - mx), jnp.exp2(l2 - mx)
    o_c = (e0 * f(oc0_ref) + e1 * f(oc1_ref) + e2 * f(oc2_ref)) / (e0 + e1 + e2)
    y_c = mm(o_c * silu(f(cz_ref)), wc_ref)
    mix = sig(f(mg0_ref)) * y_a + sig(f(mg1_ref)) * y_b + sig(f(mg2_ref)) * y_c
    y = mm(mix, wout_ref)
    x1 = x_ref[...] + _rms(y, gpost_ref[...])
    o_ref[...] = x1 + sig(mm(x1, wplg_ref)) * mm(p_ref[...], wple_ref)


def _out_layer(x2, p, layer, o_cmp, o_slc, o_win, u, expand, o_b, o_c, lse_c, w, batch, dils):
    t = x2.shape[0]
    seq = t // batch
    tm = min(OUT_TM, seq)
    nper = seq // tm
    row = lambda i: (i, 0)
    ucol = lambda off, width: pl.BlockSpec((tm, width), lambda i: (i, off // width))
    half = pl.BlockSpec((tm, A_WIDTH), row)
    wspec = lambda r, c: pl.BlockSpec((None, r, c), lambda i: (layer, 0, 0))

    def group_spec(dil):
        if dil == 1:
            return half
        return pl.BlockSpec((None, dil, tm // dil, C_WIDTH), lambda i: (i // nper, 0, i % nper, 0))

    def group_view(a, dil):
        return a if dil == 1 else a.reshape(batch, dil, seq // dil, C_WIDTH)

    gspecs = [group_spec(d) for d in dils]
    o_c = [group_view(a, d) for a, d in zip(o_c, dils)]
    lse_c = [group_view(a, d) for a, d in zip(lse_c, dils)]
    n_tok = 2 * sum(1 for d in dils if d > 1)
    in_specs = [pl.BlockSpec((tm, D_MODEL), row),
                pl.BlockSpec((None, tm, PLE_DIM), lambda i: (layer, i, 0)),
                half, half, half,
                ucol(U_GATE, LANES), pl.BlockSpec((LANES, 3 * A_WIDTH), lambda i: (0, 0)),
                ucol(U_AZ, A_WIDTH), half, ucol(U_BZ, B_WIDTH),
                *gspecs, *gspecs, ucol(U_CZ, C_WIDTH),
                ucol(U_MG, D_MODEL), ucol(U_MG + D_MODEL, D_MODEL), ucol(U_MG + 2 * D_MODEL, D_MODEL),
                wspec(A_WIDTH, D_MODEL), wspec(B_WIDTH, D_MODEL), wspec(C_WIDTH, D_MODEL),
                wspec(D_MODEL, D_MODEL), wspec(D_MODEL, D_MODEL), wspec(PLE_DIM, D_MODEL),
                wspec(1, D_MODEL)]
    return pl.pallas_call(
        _out_kernel,
        grid=(t // tm,),
        in_specs=in_specs,
        out_specs=pl.BlockSpec((tm, D_MODEL), row),
        out_shape=jax.ShapeDtypeStruct((t, D_MODEL), F32),
        scratch_shapes=[pltpu.VMEM((C_WIDTH // LANES, tm, LANES), F32)] * n_tok,
        compiler_params=_cparams(1),
        name="out_layer",
    )(x2, p, o_cmp, o_slc, o_win, u, expand, u, o_b, u, *o_c, *lse_c, u, u, u, u,
      w["w_a"], w["w_b"], w["w_c"], w["w_out"], w["w_plg"], w["w_ple"], w["g_post"])


def _overlap_t(seq):
    nc = seq // CMP_STRIDE
    ns = seq // SLC_LEN
    cs = np.arange(nc)[:, None] * CMP_STRIDE
    js = np.arange(ns)[None, :] * SLC_LEN
    ov = np.clip(np.minimum(cs + CMP_LEN, js + SLC_LEN) - np.maximum(cs, js), 0, None).astype(np.float32) / CMP_LEN
    return jnp.asarray(ov.T)


def _block_onehot(seq):
    blk = np.arange(seq)[:, None] // SLC_LEN
    return jnp.asarray((np.arange(LANES)[None, :] % HEAD_DIM == blk).astype(np.float32), dtype=BF16)


def _gate_expand():
    e = np.zeros((LANES, 3 * A_WIDTH), np.float32)
    for br in range(3):
        for i in range(A_HEADS // 2):
            for g in range(A_KV_GROUPS):
                h = g * (A_HEADS // 2) + i
                c0 = br * A_WIDTH + i * LANES + g * HEAD_DIM
                e[h * 3 + br, c0:c0 + HEAD_DIM] = 1.0
    return jnp.asarray(e, dtype=BF16)


def kernel(x, p, positions, g_pre, g_post, w_in, nsa_cmp_pe, nsa_cmp_w1, nsa_cmp_w2, w_a, mla_g_q, mla_w_uq,
           mla_g_kv, mla_w_ukv, w_b, w_c, w_out, w_ple, w_plg):
    batch, seq, _ = x.shape
    depth = w_in.shape[0]
    t = batch * seq
    nch = seq // CMP_STRIDE

    part_tabs, mla_tabs = _rope_tables(positions)
    dils = [dil for _, dil in C_PAIRS]
    assert dils[0] == 1
    perm_tabs = {dil: tuple(_to_residue_major(a, batch, dil) for a in part_tabs) for dil in dils[1:]}
    w_main, w_groups = _regroup_w_in(w_in)
    g_pre3 = g_pre.reshape(depth, 1, D_MODEL)
    pe = nsa_cmp_pe.reshape(depth, 2, 2, CMP_STRIDE * HEAD_DIM)
    w1 = nsa_cmp_w1.astype(BF16)
    w2 = nsa_cmp_w2.astype(BF16)
    wq = mla_w_uq.reshape(depth, Q_LORA, B_HEADS, QK_NOPE + QK_ROPE)
    wq = jnp.pad(wq, ((0, 0), (0, 0), (0, 0), (0, LANES - QK_NOPE - QK_ROPE))).reshape(depth, Q_LORA, -1).astype(BF16)
    wkv = mla_w_ukv.reshape(depth, KV_LORA, B_HEADS, QK_NOPE + V_DIM)
    wk = jnp.pad(wkv[..., :QK_NOPE], ((0, 0), (0, 0), (0, 0), (0, LANES - QK_NOPE))).reshape(depth, KV_LORA, -1)
    wk = wk.astype(BF16)
    wv = wkv[..., QK_NOPE:].reshape(depth, KV_LORA, B_WIDTH).astype(BF16)
    gq3 = mla_g_q.reshape(depth, 1, Q_LORA)
    gkv3 = mla_g_kv.reshape(depth, 1, KV_LORA)
    w_a_p = w_a.reshape(depth, 2, 4, HEAD_DIM, D_MODEL).swapaxes(1, 2).reshape(depth, A_WIDTH, D_MODEL)
    wts = {"w_a": w_a_p.astype(BF16), "w_b": w_b.astype(BF16), "w_c": w_c.astype(BF16),
           "w_out": w_out.astype(BF16), "w_plg": w_plg.astype(BF16), "w_ple": w_ple.astype(BF16),
           "g_post": g_post.reshape(depth, 1, D_MODEL)}
    p3 = p.reshape(depth, t, PLE_DIM)
    ovt = _overlap_t(seq)
    onehot = _block_onehot(seq)
    expand = _gate_expand()

    x2 = x.reshape(t, D_MODEL)
    for layer in range(depth):
        h, h_res = _prenorm(x2, g_pre3, layer, batch, dils[1:])
        u = _inproj(h, w_main, layer, part_tabs, ROPE_COLS)
        u3 = u.reshape(batch, seq, DP)
        kv_cmp = jnp.stack([u3[:, :, U_KCMP:U_KCMP + LANES], u3[:, :, U_VCMP:U_VCMP + LANES]], axis=1)
        xc = kv_cmp.reshape(batch, 2, nch, CMP_STRIDE, A_KV_GROUPS, HEAD_DIM).transpose(0, 1, 4, 2, 3, 5)
        xc = xc.reshape(batch, 2, A_KV_GROUPS, nch, CMP_STRIDE * HEAD_DIM)
        cmp_tok = _compress(xc, pe, w1, w2, layer)
        kc = cmp_tok[:, 0].transpose(0, 2, 1, 3).reshape(batch, nch, LANES)
        vct = cmp_tok[:, 1].transpose(0, 1, 3, 2).reshape(batch, LANES, nch)
        o_cmp, biast = _cmp_select(u3, kc, vct, ovt)
        vt_slc = u3[:, :, U_VSLC:U_VSLC + LANES].reshape(batch, seq // FLASH_TK, FLASH_TK, LANES).swapaxes(2, 3)
        o_slc = _slc_attention(u3, biast, vt_slc, onehot)
        (o_win,) = _banded(u, nseq=batch, seqlen=seq, q_off=U_AQ, k_off=U_KWIN, v_off=U_VWIN,
                           kv_width=LANES, max_dist=WIN - 1, with_lse=False)
        q_b, k_b, vt_b = _mla_prep(u, mla_tabs, gq3, gkv3, wq, wk, wv, layer)
        o_b = _mla_flash(q_b, k_b, vt_b, batch, seq)
        o_c, lse_c = [], []
        for gi, (window, dil) in enumerate(C_PAIRS):
            if dil == 1:
                o_g, lse_g = _banded(u, nseq=batch, seqlen=seq, q_off=U_CQ0, k_off=U_CK0, v_off=U_CV0,
                                     kv_width=C_WIDTH, max_dist=window, with_lse=True)
            else:
                u_g = _inproj(h_res[gi - 1], w_groups[gi - 1], layer, perm_tabs[dil], DG_ROPE_COLS)
                o_g, lse_g = _banded(u_g, nseq=batch * dil, seqlen=seq // dil, q_off=0, k_off=C_WIDTH,
                                     v_off=2 * C_WIDTH, kv_width=C_WIDTH, max_dist=window // dil, with_lse=True)
            o_c.append(o_g)
            lse_c.append(lse_g)
        x2 = _out_layer(x2, p3, layer, o_cmp.reshape(t, A_WIDTH), o_slc.reshape(t, A_WIDTH), o_win, u, expand,
                        o_b, o_c, lse_c, wts, batch, dils)
    return x2.reshape(batch, seq, D_MODEL)
```

```python
import functools

import numpy as np
import jax
import jax.numpy as jnp
from jax import lax
from jax.experimental import pallas as pl
from jax.experimental.pallas import tpu as pltpu

F32 = jnp.float32
BF16 = jnp.bfloat16

D_MODEL = 1024
PLE_DIM = 256
ROPE_THETA = 500000.0
HEAD_DIM = 64
ROT_DIM = HEAD_DIM // 4
EPS = 1e-6
NEG = -1e30
BIG = 1e30
LANES = 128

A_HEADS = 8
A_KV_GROUPS = 2
A_WIDTH = A_HEADS * HEAD_DIM
CMP_LEN = 32
CMP_STRIDE = 16
CMP_HID = 256
SLC_LEN = 64
SLC_TOPK = 16
WIN = 512

B_HEADS = 8
Q_LORA = 384
KV_LORA = 128
QK_NOPE = 64
QK_ROPE = 32
V_DIM = 64
B_WIDTH = B_HEADS * V_DIM

C_PAIRS = ((128, 1), (512, 4), (2048, 16))
C_GROUPS = len(C_PAIRS)
C_HEADS = 8
C_WIDTH = C_HEADS * HEAD_DIM

IN_SIZES = (A_WIDTH, 6 * A_KV_GROUPS * HEAD_DIM, 3 * A_HEADS, A_WIDTH, Q_LORA, KV_LORA, QK_ROPE,
            B_WIDTH, C_GROUPS * 3 * C_WIDTH, C_WIDTH, 3 * D_MODEL)
IN_OFF = tuple(int(v) for v in np.cumsum((0,) + IN_SIZES))
(OFF_AQ, OFF_AKV, OFF_AG, OFF_AZ, OFF_BCQ, OFF_BCKV, OFF_BKR, OFF_BZ, OFF_CQKV, OFF_CZ, OFF_MG) = IN_OFF[:-1]

U_AQ = 0
U_CQ0 = 512
U_CK0 = 1024
U_KCMP = 1536
U_KSLC = 1664
U_KWIN = 1792
ROPE_COLS = 1920
U_VCMP = 1920
U_AZ = 2048
U_BZ = 2560
U_CV0 = 3072
U_CZ = 3584
U_MG = 4096
U_VSLC = 7168
U_VWIN = 7296
U_GATE = 7424
U_BCKV = 7552
U_BCQ = 7680
U_BKR = 8064
DP = 8192
DG_COLS = 3 * C_WIDTH
DG_ROPE_COLS = 2 * C_WIDTH

LOG2E = float(np.log2(np.e))
Q_SCALE = float(HEAD_DIM ** -0.5) * LOG2E
MLA_SCALE = float((QK_NOPE + QK_ROPE) ** -0.5) * LOG2E
VMEM_LIMIT = 48 * 1024 * 1024


def _cparams(n_axes):
    return pltpu.CompilerParams(dimension_semantics=("arbitrary",) * n_axes, vmem_limit_bytes=VMEM_LIMIT)


def _lane_iota(shape):
    return lax.broadcasted_iota(jnp.int32, shape, len(shape) - 1)


def _sub_iota(shape):
    return lax.broadcasted_iota(jnp.int32, shape, len(shape) - 2)


def _pair_cols(w):
    lead = w.shape[:-1]
    return w.reshape(lead + (2, 4, HEAD_DIM)).swapaxes(-3, -2).reshape(lead + (A_WIDTH,))


def _regroup_w_in(w_in):
    def col(off, n):
        return w_in[..., off:off + n]
    z = lambda n: jnp.zeros(w_in.shape[:-1] + (n,), w_in.dtype)
    akv = lambda which: col(OFF_AKV + which * 128, 128)
    cq = lambda gi, t: col(OFF_CQKV + gi * 3 * C_WIDTH + t * C_WIDTH, C_WIDTH)
    pieces = [_pair_cols(col(OFF_AQ, A_WIDTH)) * Q_SCALE, cq(0, 0) * Q_SCALE, cq(0, 1)]
    pieces += [akv(0), akv(2), akv(4)]
    pieces += [akv(1)]
    pieces += [_pair_cols(col(OFF_AZ, A_WIDTH)), col(OFF_BZ, B_WIDTH), cq(0, 2)]
    pieces += [col(OFF_CZ, C_WIDTH), col(OFF_MG, 3 * D_MODEL)]
    pieces += [akv(3), akv(5)]
    pieces += [col(OFF_AG, 3 * A_HEADS), z(128 - 3 * A_HEADS)]
    pieces += [col(OFF_BCKV, KV_LORA), col(OFF_BCQ, Q_LORA)]
    pieces += [z(64), col(OFF_BKR, QK_ROPE), z(32)]
    main = jnp.concatenate(pieces, axis=-1).astype(BF16)
    assert main.shape[-1] == DP
    groups = [jnp.concatenate([cq(gi, 0) * Q_SCALE, cq(gi, 1), cq(gi, 2)], axis=-1).astype(BF16)
              for gi in range(1, C_GROUPS)]
    return main, groups


def _rope_tables(positions):
    pos = positions.astype(F32).reshape(-1)

    def trig(dim):
        inv = ROPE_THETA ** (-jnp.arange(0, dim, 2, dtype=F32) / dim)
        ang = pos[:, None] * inv
        return jnp.cos(ang), jnp.sin(ang)

    t = pos.shape[0]
    ones = jnp.ones((t, 1), F32)
    zeros = jnp.zeros((t, 1), F32)
    cos, sin = trig(ROT_DIM)
    rest = HEAD_DIM - ROT_DIM
    c_head = jnp.concatenate([cos, cos, jnp.broadcast_to(ones, (t, rest))], axis=1)
    s1_head = jnp.concatenate([0 * sin, sin, jnp.broadcast_to(zeros, (t, rest))], axis=1)
    s2_head = jnp.concatenate([-sin, 0 * sin, jnp.broadcast_to(zeros, (t, rest))], axis=1)
    part = tuple(jnp.concatenate([a, a], axis=1) for a in (c_head, s1_head, s2_head))
    cos, sin = trig(QK_ROPE)
    lo = jnp.broadcast_to(ones, (t, 64))
    lz = jnp.broadcast_to(zeros, (t, 64))
    hi1 = jnp.broadcast_to(ones, (t, 32))
    hz = jnp.broadcast_to(zeros, (t, 32))
    mla = (jnp.concatenate([lo, cos, cos, hi1], axis=1),
           jnp.concatenate([lz, 0 * sin, sin, hz], axis=1),
           jnp.concatenate([lz, -sin, 0 * sin, hz], axis=1))
    return part, mla


def _to_residue_major(a, batch, dil):
    t, c = a.shape
    ln = t // batch // dil
    return a.reshape(batch, ln, dil, c).swapaxes(1, 2).reshape(t, c)


NORM_TM = 512
IN_TM = 512
IN_TN = 2048


def _norm_kernel(x_ref, g_ref, o_ref, *rest, dils):
    o_dil, y_ref = rest[:len(dils)], rest[len(dils)]
    x = x_ref[...]
    ms = jnp.mean(x * x, axis=-1, keepdims=True)
    y = x * lax.rsqrt(ms + EPS) * g_ref[...]
    o_ref[...] = y.astype(o_ref.dtype)
    tm = x.shape[0]
    nlane = y_ref.shape[0]
    for c in range(nlane):
        y_ref[c] = y[:, c * LANES:(c + 1) * LANES]
    for o_d, dil in zip(o_dil, dils):
        for r in range(dil):
            for c in range(nlane):
                o_d[r, :, c * LANES:(c + 1) * LANES] = (
                    y_ref[c, pl.ds(r, tm // dil, stride=dil), :].astype(o_d.dtype))


def _prenorm(x2, g_pre, layer, batch, dils):
    t = x2.shape[0]
    seq = t // batch
    tm = min(NORM_TM, seq)
    nper = seq // tm
    out_specs = [pl.BlockSpec((tm, D_MODEL), lambda b, i: (b * nper + i, 0))]
    out_shape = [jax.ShapeDtypeStruct((t, D_MODEL), BF16)]
    for dil in dils:
        out_specs.append(pl.BlockSpec((None, dil, tm // dil, D_MODEL), lambda b, i: (b, 0, i, 0)))
        out_shape.append(jax.ShapeDtypeStruct((batch, dil, seq // dil, D_MODEL), BF16))
    outs = pl.pallas_call(
        functools.partial(_norm_kernel, dils=tuple(dils)),
        grid=(batch, nper),
        in_specs=[pl.BlockSpec((tm, D_MODEL), lambda b, i: (b * nper + i, 0)),
                  pl.BlockSpec((None, 1, D_MODEL), lambda b, i: (layer, 0, 0))],
        out_specs=out_specs,
        out_shape=out_shape,
        scratch_shapes=[pltpu.VMEM((D_MODEL // LANES, tm, LANES), F32)],
        compiler_params=_cparams(2),
        name="prenorm",
    )(x2, g_pre)
    return outs[0], [o.reshape(t, D_MODEL) for o in outs[1:]]


def _inproj_kernel(h_ref, w_ref, c_ref, s1_ref, s2_ref, o_ref, *, rope_cols):
    j = pl.program_id(0)
    acc = jnp.dot(h_ref[...], w_ref[...], preferred_element_type=F32)
    nsub = w_ref.shape[1] // LANES
    full_tiles, rem = divmod(rope_cols // LANES, nsub)

    def emit(n_rope):
        for c in range(nsub):
            xc = acc[:, c * LANES:(c + 1) * LANES]
            if c < n_rope:
                xc = (xc * c_ref[...] + pltpu.roll(xc, ROT_DIM // 2, 1) * s1_ref[...]
                      + pltpu.roll(xc, LANES - ROT_DIM // 2, 1) * s2_ref[...])
            o_ref[:, c * LANES:(c + 1) * LANES] = xc.astype(o_ref.dtype)

    pl.when(j < full_tiles)(lambda: emit(nsub))
    pl.when(j == full_tiles)(lambda: emit(rem))
    pl.when(j > full_tiles)(lambda: emit(0))


def _tile_cols(w, tn):
    depth, d, n = w.shape
    return w.reshape(depth, d, n // tn, tn).swapaxes(1, 2)


def _inproj(h, w_tiled, layer, tabs, rope_cols):
    t = h.shape[0]
    _, ntile, _, tn = w_tiled.shape
    tm = min(IN_TM, t)
    rope_tiles = -(-rope_cols // tn)
    row = lambda j, i: (i, 0)
    tab = lambda j, i: (jnp.where(j < rope_tiles, i, 0), 0)
    return pl.pallas_call(
        functools.partial(_inproj_kernel, rope_cols=rope_cols),
        grid=(ntile, t // tm),
        in_specs=[pl.BlockSpec((tm, D_MODEL), row),
                  pl.BlockSpec((None, None, D_MODEL, tn), lambda j, i: (layer, j, 0, 0)),
                  pl.BlockSpec((tm, LANES), tab), pl.BlockSpec((tm, LANES), tab), pl.BlockSpec((tm, LANES), tab)],
        out_specs=pl.BlockSpec((tm, tn), lambda j, i: (i, j)),
        out_shape=jax.ShapeDtypeStruct((t, ntile * tn), BF16),
        compiler_params=_cparams(2),
        name="inproj",
    )(h, w_tiled, *tabs)


def _compress_kernel(x_ref, pe_ref, w1_ref, w2_ref, o_ref):
    half = CMP_STRIDE * HEAD_DIM
    x = x_ref[...].astype(F32)
    xa = (x + pe_ref[0:1, :]).astype(BF16)
    xb = (x + pe_ref[1:2, :]).astype(BF16)
    a = jnp.dot(xa, w1_ref[0:half, :], preferred_element_type=F32)
    b = jnp.dot(xb, w1_ref[half:2 * half, :], preferred_element_type=F32)
    nch = a.shape[0]
    hid = a + pltpu.roll(b, nch - 1, 0)
    act = hid * jax.nn.sigmoid(hid)
    o_ref[...] = jnp.dot(act.astype(BF16), w2_ref[...], preferred_element_type=F32).astype(o_ref.dtype)


def _compress(xc, pe, w1, w2, layer):
    b, _, _, nch, width = xc.shape
    return pl.pallas_call(
        _compress_kernel,
        grid=(b, 2, A_KV_GROUPS),
        in_specs=[pl.BlockSpec((None, None, None, nch, width), lambda i, w, g: (i, w, g, 0, 0)),
                  pl.BlockSpec((None, None, 2, width), lambda i, w, g: (layer, w, 0, 0)),
                  pl.BlockSpec((None, None, 2 * width, CMP_HID), lambda i, w, g: (layer, w, 0, 0)),
                  pl.BlockSpec((None, None, CMP_HID, HEAD_DIM), lambda i, w, g: (layer, w, 0, 0))],
        out_specs=pl.BlockSpec((None, None, None, nch, HEAD_DIM), lambda i, w, g: (i, w, g, 0, 0)),
        out_shape=jax.ShapeDtypeStruct((b, 2, A_KV_GROUPS, nch, HEAD_DIM), BF16),
        compiler_params=_cparams(3),
        name="nsa_compress",
    )(xc, pe, w1, w2)


CS_TQ = 128


def _cmp_select_kernel(q_ref, kc_ref, vct_ref, ovt_ref, o_ref, bias_ref, sc_ref):
    tq = CS_TQ
    qi = pl.program_id(1)
    nc = kc_ref.shape[0]
    ns = ovt_ref.shape[0]
    tpos = qi * tq + _lane_iota((1, tq))
    kc = kc_ref[...]
    lane = _lane_iota((1, LANES))
    kst = jnp.concatenate([jnp.where(lane < HEAD_DIM, kc, jnp.zeros_like(kc)),
                           jnp.where(lane >= HEAD_DIM, kc, jnp.zeros_like(kc))], axis=0)
    c_end = _sub_iota((nc, 1)) * CMP_STRIDE + (CMP_LEN - 1)
    cv = c_end <= tpos
    vct = vct_ref[...]
    psum = [jnp.zeros((nc, tq), F32), jnp.zeros((nc, tq), F32)]
    sub = _sub_iota((LANES, 1))
    for i in range(A_HEADS // 2):
        qb = q_ref[:, i * LANES:(i + 1) * LANES]
        st = lax.dot_general(kst, qb, (((1,), (1,)), ((), ())), preferred_element_type=F32)
        ot = []
        for g in range(A_KV_GROUPS):
            s = jnp.where(cv, st[g * nc:(g + 1) * nc], NEG)
            m = jnp.max(s, axis=0, keepdims=True)
            e = jnp.where(cv, jnp.exp2(s - m), 0.0)
            den = jnp.sum(e, axis=0, keepdims=True)
            p = e / jnp.where(den > 0.0, den, 1.0)
            psum[g] = psum[g] + p
            ot.append(jnp.dot(vct, p.astype(BF16), preferred_element_type=F32))
        o_pair = jnp.where(sub < HEAD_DIM, ot[0], ot[1])
        o_ref[:, i * LANES:(i + 1) * LANES] = o_pair.T.astype(o_ref.dtype)
    jdx = _sub_iota((ns, 1))
    cur = tpos // SLC_LEN
    valid = jdx <= cur
    forced = (jdx == 0) | (jdx == cur) | (jdx == cur - 1)
    for g in range(A_KV_GROUPS):
        imp = jnp.dot(ovt_ref[...], psum[g], preferred_element_type=F32, precision=lax.Precision.HIGHEST)
        score = jnp.where(forced, BIG, jnp.where(valid, imp, NEG))
        sc_ref[...] = score
        cnt = jnp.zeros((ns, tq), F32)
        for jp in range(ns):
            row = sc_ref[jp:jp + 1, :]
            tie = jnp.where(jdx > jp, 1.0, 0.0)
            cnt = cnt + jnp.where(row > score, 1.0, 0.0) + jnp.where(row == score, tie, 0.0)
        bias = jnp.where(cnt < float(min(SLC_TOPK, ns)), 0.0, NEG)
        bias_ref[g * ns:(g + 1) * ns, :] = bias.astype(bias_ref.dtype)


def _cmp_select(u3, kc, vct, ovt):
    b, s, _ = u3.shape
    nc = kc.shape[1]
    ns = s // SLC_LEN
    assert A_KV_GROUPS * ns == LANES
    return pl.pallas_call(
        _cmp_select_kernel,
        grid=(b, s // CS_TQ),
        in_specs=[pl.BlockSpec((None, CS_TQ, A_WIDTH), lambda i, q: (i, q, U_AQ // A_WIDTH)),
                  pl.BlockSpec((None, nc, LANES), lambda i, q: (i, 0, 0)),
                  pl.BlockSpec((None, LANES, nc), lambda i, q: (i, 0, 0)),
                  pl.BlockSpec((ns, nc), lambda i, q: (0, 0))],
        out_specs=[pl.BlockSpec((None, CS_TQ, A_WIDTH), lambda i, q: (i, q, 0)),
                   pl.BlockSpec((None, LANES, CS_TQ), lambda i, q: (i, 0, q))],
        out_shape=[jax.ShapeDtypeStruct((b, s, A_WIDTH), BF16), jax.ShapeDtypeStruct((b, LANES, s), BF16)],
        scratch_shapes=[pltpu.VMEM((ns, CS_TQ), F32)],
        compiler_params=_cparams(2),
        name="nsa_cmp_select",
    )(u3, kc, vct, ovt)


FLASH_TK = 256
FLASH_CHUNK = 256


def _flash_reset(m_ref, l_ref, acc_ref):
    m_ref[...] = jnp.full(m_ref.shape, NEG, F32)
    l_ref[...] = jnp.zeros(l_ref.shape, F32)
    acc_ref[...] = jnp.zeros(acc_ref.shape, F32)


def _flash_update(st, vt, m_ref, l_ref, acc_ref, idx, mask):
    if mask is not None:
        st = jnp.where(mask, st, NEG)
    m_old = m_ref[idx]
    m_new = jnp.maximum(m_old, jnp.max(st, axis=0, keepdims=True))
    alpha = jnp.exp2(m_old - m_new)
    p = jnp.exp2(st - m_new)
    l_ref[idx] = alpha * l_ref[idx] + jnp.sum(p, axis=0, keepdims=True)
    acc_ref[idx] = alpha * acc_ref[idx] + jnp.dot(vt, p.astype(BF16), preferred_element_type=F32)
    m_ref[idx] = m_new


def _flash_pipeline(n_pairs, n_chain, put, use):
    for c in range(n_chain):
        put(0, c, 0)

    def pair(j, carry):
        t0 = 2 * j
        for c in range(n_chain):
            put(t0 + 1, c, 1)
            use(t0, c, 0)
        for c in range(n_chain):
            put(t0 + 2, c, 0)
            use(t0 + 1, c, 1)
        return carry

    lax.fori_loop(0, n_pairs, pair, 0)


def _transpose_bf16(x):
    return x.astype(F32).T.astype(BF16)


def _merge_rows(lo, hi):
    return jnp.where(_sub_iota((LANES, 1)) < HEAD_DIM, lo, hi)


SLC_TQ = 128


def _slc_kernel(q_ref, biast_ref, k_ref, vt_ref, oh_ref, o_ref, qa_ref, st_ref, m_ref, l_ref, acc_ref):
    tq, tk = SLC_TQ, FLASH_TK
    qi = pl.program_id(1)
    nblk = A_HEADS // 2
    sub = _sub_iota((LANES, 1))
    biast = biast_ref[...]
    zero = jnp.zeros_like(biast)
    b_lo = jnp.where(sub < HEAD_DIM, biast, zero)
    b_hi = jnp.where(sub >= HEAD_DIM, biast, zero)
    for i in range(nblk):
        qt = _transpose_bf16(q_ref[:, i * LANES:(i + 1) * LANES])
        lo = jnp.concatenate([jnp.where(sub < HEAD_DIM, qt, zero), b_lo], axis=0)
        hi = jnp.concatenate([jnp.where(sub >= HEAD_DIM, qt, zero), b_hi], axis=0)
        qa_ref[i] = jnp.concatenate([lo, hi], axis=1)
    _flash_reset(m_ref, l_ref, acc_ref)

    def put(kt, i, slot):
        start = pl.multiple_of(kt * tk, tk)
        k = jnp.concatenate([k_ref[pl.ds(start, tk), :], oh_ref[pl.ds(start, tk), :]], axis=1)
        st_ref[slot, i] = jnp.dot(k, qa_ref[i], preferred_element_type=F32)

    def use(kt, i, slot, mask=None):
        _flash_update(st_ref[slot, i], vt_ref[kt], m_ref, l_ref, acc_ref, i, mask)

    n_full = (qi * tq) // tk
    n_pairs = n_full // 2
    _flash_pipeline(n_pairs, nblk, put, use)
    col = _lane_iota((1, 2 * tq))
    qpos = qi * tq + jnp.where(col >= tq, col - tq, col)
    causal = (n_full * tk + _sub_iota((tk, 1))) <= qpos

    @pl.when(n_full == 2 * n_pairs)
    def _():
        for i in range(nblk):
            use(n_full, i, 0, causal)

    @pl.when(n_full != 2 * n_pairs)
    def _():
        for i in range(nblk):
            put(n_full, i, 1)
            use(n_full - 1, i, 0)
        for i in range(nblk):
            use(n_full, i, 1, causal)

    for i in range(nblk):
        ot = acc_ref[i] / l_ref[i]
        o_ref[:, i * LANES:(i + 1) * LANES] = _merge_rows(ot[:, 0:tq], ot[:, tq:2 * tq]).T.astype(o_ref.dtype)


def _slc_attention(u3, biast, vt, onehot):
    b, s, _ = u3.shape
    nblk = A_HEADS // 2
    assert 2 * SLC_TQ == FLASH_CHUNK
    return pl.pallas_call(
        _slc_kernel,
        grid=(b, s // SLC_TQ),
        in_specs=[pl.BlockSpec((None, SLC_TQ, A_WIDTH), lambda i, q: (i, q, U_AQ // A_WIDTH)),
                  pl.BlockSpec((None, LANES, SLC_TQ), lambda i, q: (i, 0, q)),
                  pl.BlockSpec((None, s, LANES), lambda i, q: (i, 0, U_KSLC // LANES)),
                  pl.BlockSpec((None, s // FLASH_TK, LANES, FLASH_TK), lambda i, q: (i, 0, 0, 0)),
                  pl.BlockSpec((s, LANES), lambda i, q: (0, 0))],
        out_specs=pl.BlockSpec((None, SLC_TQ, A_WIDTH), lambda i, q: (i, q, 0)),
        out_shape=jax.ShapeDtypeStruct((b, s, A_WIDTH), BF16),
        scratch_shapes=[pltpu.VMEM((nblk, 2 * LANES, FLASH_CHUNK), BF16),
                        pltpu.VMEM((2, nblk, FLASH_TK, FLASH_CHUNK), F32),
                        pltpu.VMEM((nblk, 1, FLASH_CHUNK), F32), pltpu.VMEM((nblk, 1, FLASH_CHUNK), F32),
                        pltpu.VMEM((nblk, LANES, FLASH_CHUNK), F32)],
        compiler_params=_cparams(2),
        name="nsa_selected",
    )(u3, biast, u3, vt, onehot)


BAND_T = 128
BAND_ROWS = 512
BAND_LOOKAHEAD = 3


def _banded_kernel(*refs, nprev, max_dist, shared_kv, with_lse):
    t = BAND_T
    q_ref, k_ref, v_ref, o_ref = refs[:4]
    lse_ref = refs[4] if with_lse else None
    qi = pl.program_id(1)
    nsub = q_ref.shape[0] // t
    nblk = q_ref.shape[1] // LANES
    nk = (nprev + 1) * t
    col = _lane_iota((1, 2 * t))
    base = jnp.where(col >= t, col - t, col) - _sub_iota((nk, 1))
    sub = _sub_iota((LANES, 1))

    def window(s):
        qt = qi * nsub + s
        start = pl.multiple_of(jnp.maximum(qt - nprev, 0) * t, t)
        return start, qt * t - start

    def scores(s, i):
        start, _ = window(s)
        cols = slice(None) if shared_kv else slice(i * LANES, (i + 1) * LANES)
        k = k_ref[pl.ds(start, nk), cols]
        qt = _transpose_bf16(q_ref[s * t:(s + 1) * t, i * LANES:(i + 1) * LANES])
        zero = jnp.zeros_like(qt)
        qs = jnp.concatenate([jnp.where(sub < HEAD_DIM, qt, zero), jnp.where(sub >= HEAD_DIM, qt, zero)], axis=1)
        return jnp.dot(k, qs, preferred_element_type=F32)

    def finish(s, i, st):
        start, dist = window(s)
        cols = slice(None) if shared_kv else slice(i * LANES, (i + 1) * LANES)
        vt = jnp.concatenate([_transpose_bf16(v_ref[pl.ds(start + j * t, t), cols]) for j in range(nprev + 1)],
                             axis=1)
        rel = base + dist
        st = jnp.where((rel >= 0) & (rel <= max_dist), st, NEG)
        m = jnp.max(st, axis=0, keepdims=True)
        p = jnp.exp2(st - m)
        l = jnp.sum(p, axis=0, keepdims=True)
        ot = jnp.dot(vt, p.astype(BF16), preferred_element_type=F32) / l
        rows, sl = slice(s * t, (s + 1) * t), slice(i * LANES, (i + 1) * LANES)
        o_ref[rows, sl] = _merge_rows(ot[:, 0:t], ot[:, t:2 * t]).T.astype(o_ref.dtype)
        if with_lse:
            lse = m + jnp.log2(l)
            lse_ref[rows, sl] = _merge_rows(jnp.broadcast_to(lse[:, 0:t], (LANES, t)),
                                            jnp.broadcast_to(lse[:, t:2 * t], (LANES, t))).T

    items = [(s, i) for s in range(nsub) for i in range(nblk)]
    pending = [scores(*it) for it in items[:BAND_LOOKAHEAD]]
    for j, it in enumerate(items):
        if j + BAND_LOOKAHEAD < len(items):
            pending.append(scores(*items[j + BAND_LOOKAHEAD]))
        finish(*it, pending[j])


def _banded(src, *, nseq, seqlen, q_off, k_off, v_off, kv_width, max_dist, with_lse):
    t = BAND_T
    rows = min(BAND_ROWS, seqlen)
    nq = seqlen // rows
    nprev = -(-max_dist // t)
    shared = kv_width == LANES
    assert (nprev + 1) * t <= seqlen and seqlen % rows == 0 and rows % t == 0
    qo = pl.BlockSpec((rows, A_WIDTH), lambda n, q: (n * nq + q, q_off // A_WIDTH))
    out = pl.BlockSpec((rows, A_WIDTH), lambda n, q: (n * nq + q, 0))
    in_specs = [qo,
                pl.BlockSpec((seqlen, kv_width), lambda n, q: (n, k_off // kv_width)),
                pl.BlockSpec((seqlen, kv_width), lambda n, q: (n, v_off // kv_width))]
    out_specs = [out]
    out_shape = [jax.ShapeDtypeStruct((nseq * seqlen, A_WIDTH), BF16)]
    if with_lse:
        out_specs.append(out)
        out_shape.append(jax.ShapeDtypeStruct((nseq * seqlen, A_WIDTH), F32))
    return pl.pallas_call(
        functools.partial(_banded_kernel, nprev=nprev, max_dist=max_dist, shared_kv=shared, with_lse=with_lse),
        grid=(nseq, nq),
        in_specs=in_specs,
        out_specs=out_specs,
        out_shape=out_shape,
        compiler_params=_cparams(2),
        name="banded_attention",
    )(src, src, src)


MLA_TM = 512


def _rms(x, g):
    return x * lax.rsqrt(jnp.mean(x * x, axis=-1, keepdims=True) + EPS) * g


def _mla_prep_kernel(cq_ref, ckv_ref, kr_ref, c_ref, s1_ref, s2_ref, gq_ref, gkv_ref, wq_ref, wk_ref, wv_ref,
                     q_out, k_out, vt_out):
    half = QK_ROPE // 2
    tk = FLASH_TK
    cm, s1, s2 = c_ref[...], s1_ref[...], s2_ref[...]

    def rope(xc):
        return xc * cm + pltpu.roll(xc, half, 1) * s1 + pltpu.roll(xc, LANES - half, 1) * s2

    qn = _rms(cq_ref[...].astype(F32), gq_ref[...]).astype(BF16)
    q = jnp.dot(qn, wq_ref[...], preferred_element_type=F32)
    kvn = _rms(ckv_ref[...].astype(F32), gkv_ref[...]).astype(BF16)
    kk = jnp.dot(kvn, wk_ref[...], preferred_element_type=F32)
    kr = rope(kr_ref[...].astype(F32))
    for h in range(B_HEADS):
        sl = slice(h * LANES, (h + 1) * LANES)
        q_out[:, sl] = (rope(q[:, sl]) * MLA_SCALE).astype(q_out.dtype)
        k_out[:, sl] = (kk[:, sl] + kr).astype(k_out.dtype)
    v = jnp.dot(kvn, wv_ref[...], preferred_element_type=F32)
    for j in range(v.shape[0] // tk):
        for hp in range(B_HEADS // 2):
            vt_out[j, hp] = v[j * tk:(j + 1) * tk, hp * LANES:(hp + 1) * LANES].T.astype(vt_out.dtype)


def _mla_prep(u, tabs, g_q, g_kv, wq, wk, wv, layer):
    t = u.shape[0]
    tm = min(MLA_TM, t)
    row = lambda i: (i, 0)
    wide = B_HEADS * LANES
    assert U_BCQ % Q_LORA == 0 and tm % FLASH_TK == 0
    return pl.pallas_call(
        _mla_prep_kernel,
        grid=(t // tm,),
        in_specs=[pl.BlockSpec((tm, Q_LORA), lambda i: (i, U_BCQ // Q_LORA)),
                  pl.BlockSpec((tm, LANES), lambda i: (i, U_BCKV // LANES)),
                  pl.BlockSpec((tm, LANES), lambda i: (i, U_BKR // LANES)),
                  pl.BlockSpec((tm, LANES), row), pl.BlockSpec((tm, LANES), row), pl.BlockSpec((tm, LANES), row),
                  pl.BlockSpec((None, 1, Q_LORA), lambda i: (layer, 0, 0)),
                  pl.BlockSpec((None, 1, KV_LORA), lambda i: (layer, 0, 0)),
                  pl.BlockSpec((None, Q_LORA, wide), lambda i: (layer, 0, 0)),
                  pl.BlockSpec((None, KV_LORA, wide), lambda i: (layer, 0, 0)),
                  pl.BlockSpec((None, KV_LORA, B_WIDTH), lambda i: (layer, 0, 0))],
        out_specs=[pl.BlockSpec((tm, wide), row), pl.BlockSpec((tm, wide), row),
                   pl.BlockSpec((tm // FLASH_TK, B_HEADS // 2, LANES, FLASH_TK), lambda i: (i, 0, 0, 0))],
        out_shape=[jax.ShapeDtypeStruct((t, wide), BF16), jax.ShapeDtypeStruct((t, wide), BF16),
                   jax.ShapeDtypeStruct((t // FLASH_TK, B_HEADS // 2, LANES, FLASH_TK), BF16)],
        compiler_params=_cparams(1),
        name="mla_prep",
    )(u, u, u, *tabs, g_q, g_kv, wq, wk, wv)


MLA_TQ = 512


def _mla_flash_kernel(q_ref, k_ref, vt_ref, o_ref, qt_ref, st_ref, m_ref, l_ref, acc_ref):
    tq, tk, cw = MLA_TQ, FLASH_TK, FLASH_CHUNK
    nhalf = tq // cw
    qi = pl.program_id(2)
    for h in range(2):
        qt_ref[h] = _transpose_bf16(q_ref[:, h * LANES:(h + 1) * LANES])
    _flash_reset(m_ref, l_ref, acc_ref)

    def put(kt, c, slot):
        h, half = divmod(c, nhalf)
        start = pl.multiple_of(kt * tk, tk)
        k = k_ref[pl.ds(start, tk), h * LANES:(h + 1) * LANES]
        st_ref[slot, c] = jnp.dot(k, qt_ref[h, :, half * cw:(half + 1) * cw], preferred_element_type=F32)

    def use(kt, c, slot, masked=False):
        mask = None
        if masked:
            qpos = qi * tq + (c % nhalf) * cw + _lane_iota((1, cw))
            mask = (kt * tk + _sub_iota((tk, 1))) <= qpos
        _flash_update(st_ref[slot, c], vt_ref[kt], m_ref, l_ref, acc_ref, c, mask)

    assert tq == 2 * tk and nhalf == 2
    n_full = 2 * qi
    _flash_pipeline(qi, 2 * nhalf, put, use)
    late = [c for c in range(2 * nhalf) if c % nhalf == 1]
    for c in late:
        put(n_full + 1, c, 1)
    for c in range(2 * nhalf):
        use(n_full, c, 0, masked=(c % nhalf == 0))
    for c in late:
        use(n_full + 1, c, 1, masked=True)
    ots = []
    for h in range(2):
        ots.append(jnp.concatenate([acc_ref[h * nhalf + half] / l_ref[h * nhalf + half] for half in range(nhalf)],
                                   axis=1))
    o_ref[...] = _merge_rows(ots[0], ots[1]).T.astype(o_ref.dtype)


def _mla_flash(q, k, vt, batch, seq):
    assert FLASH_TK == FLASH_CHUNK
    q3 = q.reshape(batch, seq, -1)
    k3 = k.reshape(batch, seq, -1)
    vt5 = vt.reshape(batch, seq // FLASH_TK, B_HEADS // 2, LANES, FLASH_TK)
    nchain = 2 * (MLA_TQ // FLASH_CHUNK)
    o = pl.pallas_call(
        _mla_flash_kernel,
        grid=(batch, B_HEADS // 2, seq // MLA_TQ),
        in_specs=[pl.BlockSpec((None, MLA_TQ, 2 * LANES), lambda b, h, i: (b, i, h)),
                  pl.BlockSpec((None, seq, 2 * LANES), lambda b, h, i: (b, 0, h)),
                  pl.BlockSpec((None, seq // FLASH_TK, None, LANES, FLASH_TK), lambda b, h, i: (b, 0, h, 0, 0))],
        out_specs=pl.BlockSpec((None, MLA_TQ, LANES), lambda b, h, i: (b, i, h)),
        out_shape=jax.ShapeDtypeStruct((batch, seq, B_WIDTH), BF16),
        scratch_shapes=[pltpu.VMEM((2, LANES, MLA_TQ), BF16),
                        pltpu.VMEM((2, nchain, FLASH_TK, FLASH_CHUNK), F32),
                        pltpu.VMEM((nchain, 1, FLASH_CHUNK), F32), pltpu.VMEM((nchain, 1, FLASH_CHUNK), F32),
                        pltpu.VMEM((nchain, LANES, FLASH_CHUNK), F32)],
        compiler_params=_cparams(3),
        name="mla_flash",
    )(q3, k3, vt5)
    return o.reshape(batch * seq, B_WIDTH)


OUT_TM = 256


def _out_kernel(x_ref, p_ref, ocmp_ref, oslc_ref, owin_ref, gate_ref, e_ref, az_ref, ob_ref, bz_ref,
                oc0_ref, oc1_ref, oc2_ref, l0_ref, l1_ref, l2_ref, cz_ref, mg0_ref, mg1_ref, mg2_ref,
                wa_ref, wb_ref, wc_ref, wout_ref, wplg_ref, wple_ref, gpost_ref, o_ref, *tok_refs):
    tok = iter(tok_refs)

    def sig(z):
        return 0.5 * jnp.tanh(0.5 * z) + 0.5

    def f(r):
        if len(r.shape) == 2:
            return r[...].astype(F32)
        dil, n, width = r.shape
        buf = next(tok)
        for res in range(dil):
            blk = r[res].astype(F32)
            for c in range(width // LANES):
                buf[c, pl.ds(res, n, stride=dil), :] = blk[:, c * LANES:(c + 1) * LANES]
        return jnp.concatenate([buf[c] for c in range(width // LANES)], axis=1)

    def silu(z):
        return z * sig(z)

    def mm(a, w_ref):
        return jnp.dot(a.astype(BF16), w_ref[...], preferred_element_type=F32)

    g = sig(gate_ref[...].astype(F32))
    g_hi = g.astype(BF16)
    g_lo = (g - g_hi.astype(F32)).astype(BF16)
    gs = (jnp.dot(g_hi, e_ref[...], preferred_element_type=F32)
          + jnp.dot(g_lo, e_ref[...], preferred_element_type=F32))
    o_a = (gs[:, 0:A_WIDTH] * f(ocmp_ref) + gs[:, A_WIDTH:2 * A_WIDTH] * f(oslc_ref)
           + gs[:, 2 * A_WIDTH:3 * A_WIDTH] * f(owin_ref))
    y_a = mm(o_a * silu(f(az_ref)), wa_ref)
    y_b = mm(f(ob_ref) * silu(f(bz_ref)), wb_ref)
    l0, l1, l2 = f(l0_ref), f(l1_ref), f(l2_ref)
    mx = jnp.maximum(jnp.maximum(l0, l1), l2)
    e0, e1, e2 = jnp.exp2(l0 - mx), jnp.exp2(l1 - mx), jnp.exp2(l2 - mx)
    o_c = (e0 * f(oc0_ref) + e1 * f(oc1_ref) + e2 * f(oc2_ref)) / (e0 + e1 + e2)
    y_c = mm(o_c * silu(f(cz_ref)), wc_ref)
    mix = sig(f(mg0_ref)) * y_a + sig(f(mg1_ref)) * y_b + sig(f(mg2_ref)) * y_c
    y = mm(mix, wout_ref)
    x1 = x_ref[...] + _rms(y, gpost_ref[...])
    o_ref[...] = x1 + sig(mm(x1, wplg_ref)) * mm(p_ref[...], wple_ref)


def _out_layer(x2, p, layer, o_cmp, o_slc, o_win, u, expand, o_b, o_c, lse_c, w, batch, dils):
    t = x2.shape[0]
    seq = t // batch
    tm = min(OUT_TM, seq)
    nper = seq // tm
    row = lambda i: (i, 0)
    ucol = lambda off, width: pl.BlockSpec((tm, width), lambda i: (i, off // width))
    half = pl.BlockSpec((tm, A_WIDTH), row)
    wspec = lambda r, c: pl.BlockSpec((None, r, c), lambda i: (layer, 0, 0))

    def group_spec(dil):
        if dil == 1:
            return half
        return pl.BlockSpec((None, dil, tm // dil, C_WIDTH), lambda i: (i // nper, 0, i % nper, 0))

    def group_view(a, dil):
        return a if dil == 1 else a.reshape(batch, dil, seq // dil, C_WIDTH)

    gspecs = [group_spec(d) for d in dils]
    o_c = [group_view(a, d) for a, d in zip(o_c, dils)]
    lse_c = [group_view(a, d) for a, d in zip(lse_c, dils)]
    n_tok = 2 * sum(1 for d in dils if d > 1)
    in_specs = [pl.BlockSpec((tm, D_MODEL), row),
                pl.BlockSpec((None, tm, PLE_DIM), lambda i: (layer, i, 0)),
                half, half, half,
                ucol(U_GATE, LANES), pl.BlockSpec((LANES, 3 * A_WIDTH), lambda i: (0, 0)),
                ucol(U_AZ, A_WIDTH), half, ucol(U_BZ, B_WIDTH),
                *gspecs, *gspecs, ucol(U_CZ, C_WIDTH),
                ucol(U_MG, D_MODEL), ucol(U_MG + D_MODEL, D_MODEL), ucol(U_MG + 2 * D_MODEL, D_MODEL),
                wspec(A_WIDTH, D_MODEL), wspec(B_WIDTH, D_MODEL), wspec(C_WIDTH, D_MODEL),
                wspec(D_MODEL, D_MODEL), wspec(D_MODEL, D_MODEL), wspec(PLE_DIM, D_MODEL),
                wspec(1, D_MODEL)]
    return pl.pallas_call(
        _out_kernel,
        grid=(t // tm,),
        in_specs=in_specs,
        out_specs=pl.BlockSpec((tm, D_MODEL), row),
        out_shape=jax.ShapeDtypeStruct((t, D_MODEL), F32),
        scratch_shapes=[pltpu.VMEM((C_WIDTH // LANES, tm, LANES), F32)] * n_tok,
        compiler_params=_cparams(1),
        name="out_layer",
    )(x2, p, o_cmp, o_slc, o_win, u, expand, u, o_b, u, *o_c, *lse_c, u, u, u, u,
      w["w_a"], w["w_b"], w["w_c"], w["w_out"], w["w_plg"], w["w_ple"], w["g_post"])


def _overlap_t(seq):
    nc = seq // CMP_STRIDE
    ns = seq // SLC_LEN
    cs = np.arange(nc)[:, None] * CMP_STRIDE
    js = np.arange(ns)[None, :] * SLC_LEN
    ov = np.clip(np.minimum(cs + CMP_LEN, js + SLC_LEN) - np.maximum(cs, js), 0, None).astype(np.float32) / CMP_LEN
    return jnp.asarray(ov.T)


def _block_onehot(seq):
    blk = np.arange(seq)[:, None] // SLC_LEN
    return jnp.asarray((np.arange(LANES)[None, :] % HEAD_DIM == blk).astype(np.float32), dtype=BF16)


def _gate_expand():
    e = np.zeros((LANES, 3 * A_WIDTH), np.float32)
    for br in range(3):
        for i in range(A_HEADS // 2):
            for g in range(A_KV_GROUPS):
                h = g * (A_HEADS // 2) + i
                c0 = br * A_WIDTH + i * LANES + g * HEAD_DIM
                e[h * 3 + br, c0:c0 + HEAD_DIM] = 1.0
    return jnp.asarray(e, dtype=BF16)


def kernel(x, p, positions, g_pre, g_post, w_in, nsa_cmp_pe, nsa_cmp_w1, nsa_cmp_w2, w_a, mla_g_q, mla_w_uq,
           mla_g_kv, mla_w_ukv, w_b, w_c, w_out, w_ple, w_plg):
    batch, seq, _ = x.shape
    depth = w_in.shape[0]
    t = batch * seq
    nch = seq // CMP_STRIDE

    part_tabs, mla_tabs = _rope_tables(positions)
    dils = [dil for _, dil in C_PAIRS]
    assert dils[0] == 1
    perm_tabs = {dil: tuple(_to_residue_major(a, batch, dil) for a in part_tabs) for dil in dils[1:]}
    w_main, w_groups = _regroup_w_in(w_in)
    w_main = _tile_cols(w_main, IN_TN)
    w_groups = [_tile_cols(w, DG_COLS) for w in w_groups]
    g_pre3 = g_pre.reshape(depth, 1, D_MODEL)
    pe = nsa_cmp_pe.reshape(depth, 2, 2, CMP_STRIDE * HEAD_DIM)
    w1 = nsa_cmp_w1.astype(BF16)
    w2 = nsa_cmp_w2.astype(BF16)
    wq = mla_w_uq.reshape(depth, Q_LORA, B_HEADS, QK_NOPE + QK_ROPE)
    wq = jnp.pad(wq, ((0, 0), (0, 0), (0, 0), (0, LANES - QK_NOPE - QK_ROPE))).reshape(depth, Q_LORA, -1).astype(BF16)
    wkv = mla_w_ukv.reshape(depth, KV_LORA, B_HEADS, QK_NOPE + V_DIM)
    wk = jnp.pad(wkv[..., :QK_NOPE], ((0, 0), (0, 0), (0, 0), (0, LANES - QK_NOPE))).reshape(depth, KV_LORA, -1)
    wk = wk.astype(BF16)
    wv = wkv[..., QK_NOPE:].reshape(depth, KV_LORA, B_WIDTH).astype(BF16)
    gq3 = mla_g_q.reshape(depth, 1, Q_LORA)
    gkv3 = mla_g_kv.reshape(depth, 1, KV_LORA)
    w_a_p = w_a.reshape(depth, 2, 4, HEAD_DIM, D_MODEL).swapaxes(1, 2).reshape(depth, A_WIDTH, D_MODEL)
    wts = {"w_a": w_a_p.astype(BF16), "w_b": w_b.astype(BF16), "w_c": w_c.astype(BF16),
           "w_out": w_out.astype(BF16), "w_plg": w_plg.astype(BF16), "w_ple": w_ple.astype(BF16),
           "g_post": g_post.reshape(depth, 1, D_MODEL)}
    p3 = p.reshape(depth, t, PLE_DIM)
    ovt = _overlap_t(seq)
    onehot = _block_onehot(seq)
    expand = _gate_expand()

    x2 = x.reshape(t, D_MODEL)
    for layer in range(depth):
        h, h_res = _prenorm(x2, g_pre3, layer, batch, dils[1:])
        u = _inproj(h, w_main, layer, part_tabs, ROPE_COLS)
        u3 = u.reshape(batch, seq, DP)
        kv_cmp = jnp.stack([u3[:, :, U_KCMP:U_KCMP + LANES], u3[:, :, U_VCMP:U_VCMP + LANES]], axis=1)
        xc = kv_cmp.reshape(batch, 2, nch, CMP_STRIDE, A_KV_GROUPS, HEAD_DIM).transpose(0, 1, 4, 2, 3, 5)
        xc = xc.reshape(batch, 2, A_KV_GROUPS, nch, CMP_STRIDE * HEAD_DIM)
        cmp_tok = _compress(xc, pe, w1, w2, layer)
        kc = cmp_tok[:, 0].transpose(0, 2, 1, 3).reshape(batch, nch, LANES)
        vct = cmp_tok[:, 1].transpose(0, 1, 3, 2).reshape(batch, LANES, nch)
        o_cmp, biast = _cmp_select(u3, kc, vct, ovt)
        vt_slc = u3[:, :, U_VSLC:U_VSLC + LANES].reshape(batch, seq // FLASH_TK, FLASH_TK, LANES).swapaxes(2, 3)
        o_slc = _slc_attention(u3, biast, vt_slc, onehot)
        (o_win,) = _banded(u, nseq=batch, seqlen=seq, q_off=U_AQ, k_off=U_KWIN, v_off=U_VWIN,
                           kv_width=LANES, max_dist=WIN - 1, with_lse=False)
        q_b, k_b, vt_b = _mla_prep(u, mla_tabs, gq3, gkv3, wq, wk, wv, layer)
        o_b = _mla_flash(q_b, k_b, vt_b, batch, seq)
        o_c, lse_c = [], []
        for gi, (window, dil) in enumerate(C_PAIRS):
            if dil == 1:
                o_g, lse_g = _banded(u, nseq=batch, seqlen=seq, q_off=U_CQ0, k_off=U_CK0, v_off=U_CV0,
                                     kv_width=C_WIDTH, max_dist=window, with_lse=True)
            else:
                u_g = _inproj(h_res[gi - 1], w_groups[gi - 1], layer, perm_tabs[dil], DG_ROPE_COLS)
                o_g, lse_g = _banded(u_g, nseq=batch * dil, seqlen=seq // dil, q_off=0, k_off=C_WIDTH,
                                     v_off=2 * C_WIDTH, kv_width=C_WIDTH, max_dist=window // dil, with_lse=True)
            o_c.append(o_g)
            lse_c.append(lse_g)
        x2 = _out_layer(x2, p3, layer, o_cmp.reshape(t, A_WIDTH), o_slc.reshape(t, A_WIDTH), o_win, u, expand,
                        o_b, o_c, lse_c, wts, batch, dils)
    return x2.reshape(batch, seq, D_MODEL)
```

```python
import functools

import numpy as np
import jax
import jax.numpy as jnp
from jax import lax
from jax.experimental import pallas as pl
from jax.experimental.pallas import tpu as pltpu

F32 = jnp.float32
BF16 = jnp.bfloat16

D_MODEL = 1024
PLE_DIM = 256
ROPE_THETA = 500000.0
HEAD_DIM = 64
ROT_DIM = HEAD_DIM // 4
EPS = 1e-6
NEG = -1e30
BIG = 1e30
LANES = 128

A_HEADS = 8
A_KV_GROUPS = 2
A_WIDTH = A_HEADS * HEAD_DIM
CMP_LEN = 32
CMP_STRIDE = 16
CMP_HID = 256
SLC_LEN = 64
SLC_TOPK = 16
WIN = 512

B_HEADS = 8
Q_LORA = 384
KV_LORA = 128
QK_NOPE = 64
QK_ROPE = 32
V_DIM = 64
B_WIDTH = B_HEADS * V_DIM

C_PAIRS = ((128, 1), (512, 4), (2048, 16))
C_GROUPS = len(C_PAIRS)
C_HEADS = 8
C_WIDTH = C_HEADS * HEAD_DIM

IN_SIZES = (A_WIDTH, 6 * A_KV_GROUPS * HEAD_DIM, 3 * A_HEADS, A_WIDTH, Q_LORA, KV_LORA, QK_ROPE,
            B_WIDTH, C_GROUPS * 3 * C_WIDTH, C_WIDTH, 3 * D_MODEL)
IN_OFF = tuple(int(v) for v in np.cumsum((0,) + IN_SIZES))
(OFF_AQ, OFF_AKV, OFF_AG, OFF_AZ, OFF_BCQ, OFF_BCKV, OFF_BKR, OFF_BZ, OFF_CQKV, OFF_CZ, OFF_MG) = IN_OFF[:-1]

U_AQ = 0
U_CQ0 = 512
U_CK0 = 1024
U_KCMP = 1536
U_KSLC = 1664
U_KWIN = 1792
ROPE_COLS = 1920
U_VCMP = 1920
U_AZ = 2048
U_BZ = 2560
U_CV0 = 3072
U_CZ = 3584
U_MG = 4096
U_VSLC = 7168
U_VWIN = 7296
U_GATE = 7424
U_BCKV = 7552
U_BCQ = 7680
U_BKR = 8064
DP = 8192
DG_COLS = 3 * C_WIDTH
DG_ROPE_COLS = 2 * C_WIDTH

LOG2E = float(np.log2(np.e))
Q_SCALE = float(HEAD_DIM ** -0.5) * LOG2E
MLA_SCALE = float((QK_NOPE + QK_ROPE) ** -0.5) * LOG2E
VMEM_LIMIT = 48 * 1024 * 1024


def _cparams(n_axes):
    return pltpu.CompilerParams(dimension_semantics=("arbitrary",) * n_axes, vmem_limit_bytes=VMEM_LIMIT)


def _lane_iota(shape):
    return lax.broadcasted_iota(jnp.int32, shape, len(shape) - 1)


def _sub_iota(shape):
    return lax.broadcasted_iota(jnp.int32, shape, len(shape) - 2)


def _pair_cols(w):
    lead = w.shape[:-1]
    return w.reshape(lead + (2, 4, HEAD_DIM)).swapaxes(-3, -2).reshape(lead + (A_WIDTH,))


def _regroup_w_in(w_in):
    def col(off, n):
        return w_in[..., off:off + n]
    z = lambda n: jnp.zeros(w_in.shape[:-1] + (n,), w_in.dtype)
    akv = lambda which: col(OFF_AKV + which * 128, 128)
    cq = lambda gi, t: col(OFF_CQKV + gi * 3 * C_WIDTH + t * C_WIDTH, C_WIDTH)
    pieces = [_pair_cols(col(OFF_AQ, A_WIDTH)) * Q_SCALE, cq(0, 0) * Q_SCALE, cq(0, 1)]
    pieces += [akv(0), akv(2), akv(4)]
    pieces += [akv(1)]
    pieces += [_pair_cols(col(OFF_AZ, A_WIDTH)), col(OFF_BZ, B_WIDTH), cq(0, 2)]
    pieces += [col(OFF_CZ, C_WIDTH), col(OFF_MG, 3 * D_MODEL)]
    pieces += [akv(3), akv(5)]
    pieces += [col(OFF_AG, 3 * A_HEADS), z(128 - 3 * A_HEADS)]
    pieces += [col(OFF_BCKV, KV_LORA), col(OFF_BCQ, Q_LORA)]
    pieces += [z(64), col(OFF_BKR, QK_ROPE), z(32)]
    main = jnp.concatenate(pieces, axis=-1).astype(BF16)
    assert main.shape[-1] == DP
    groups = [jnp.concatenate([cq(gi, 0) * Q_SCALE, cq(gi, 1), cq(gi, 2)], axis=-1).astype(BF16)
              for gi in range(1, C_GROUPS)]
    return main, groups


def _rope_tables(positions):
    pos = positions.astype(F32).reshape(-1)

    def trig(dim):
        inv = ROPE_THETA ** (-jnp.arange(0, dim, 2, dtype=F32) / dim)
        ang = pos[:, None] * inv
        return jnp.cos(ang), jnp.sin(ang)

    t = pos.shape[0]
    ones = jnp.ones((t, 1), F32)
    zeros = jnp.zeros((t, 1), F32)
    cos, sin = trig(ROT_DIM)
    rest = HEAD_DIM - ROT_DIM
    c_head = jnp.concatenate([cos, cos, jnp.broadcast_to(ones, (t, rest))], axis=1)
    s1_head = jnp.concatenate([0 * sin, sin, jnp.broadcast_to(zeros, (t, rest))], axis=1)
    s2_head = jnp.concatenate([-sin, 0 * sin, jnp.broadcast_to(zeros, (t, rest))], axis=1)
    part = tuple(jnp.concatenate([a, a], axis=1) for a in (c_head, s1_head, s2_head))
    cos, sin = trig(QK_ROPE)
    lo = jnp.broadcast_to(ones, (t, 64))
    lz = jnp.broadcast_to(zeros, (t, 64))
    hi1 = jnp.broadcast_to(ones, (t, 32))
    hz = jnp.broadcast_to(zeros, (t, 32))
    mla = (jnp.concatenate([lo, cos, cos, hi1], axis=1),
           jnp.concatenate([lz, 0 * sin, sin, hz], axis=1),
           jnp.concatenate([lz, -sin, 0 * sin, hz], axis=1))
    return part, mla


def _to_residue_major(a, batch, dil):
    t, c = a.shape
    ln = t // batch // dil
    return a.reshape(batch, ln, dil, c).swapaxes(1, 2).reshape(t, c)


NORM_TM = 512
IN_TM = 512
IN_TN = 2048


def _norm_kernel(x_ref, g_ref, o_ref, *rest, dils):
    o_dil, y_ref = rest[:len(dils)], rest[len(dils)]
    x = x_ref[...]
    ms = jnp.mean(x * x, axis=-1, keepdims=True)
    y = x * lax.rsqrt(ms + EPS) * g_ref[...]
    o_ref[...] = y.astype(o_ref.dtype)
    tm = x.shape[0]
    nlane = y_ref.shape[0]
    for c in range(nlane):
        y_ref[c] = y[:, c * LANES:(c + 1) * LANES]
    for o_d, dil in zip(o_dil, dils):
        for r in range(dil):
            for c in range(nlane):
                o_d[r, :, c * LANES:(c + 1) * LANES] = (
                    y_ref[c, pl.ds(r, tm // dil, stride=dil), :].astype(o_d.dtype))


def _prenorm(x2, g_pre, layer, batch, dils):
    t = x2.shape[0]
    seq = t // batch
    tm = min(NORM_TM, seq)
    nper = seq // tm
    out_specs = [pl.BlockSpec((tm, D_MODEL), lambda b, i: (b * nper + i, 0))]
    out_shape = [jax.ShapeDtypeStruct((t, D_MODEL), BF16)]
    for dil in dils:
        out_specs.append(pl.BlockSpec((None, dil, tm // dil, D_MODEL), lambda b, i: (b, 0, i, 0)))
        out_shape.append(jax.ShapeDtypeStruct((batch, dil, seq // dil, D_MODEL), BF16))
    outs = pl.pallas_call(
        functools.partial(_norm_kernel, dils=tuple(dils)),
        grid=(batch, nper),
        in_specs=[pl.BlockSpec((tm, D_MODEL), lambda b, i: (b * nper + i, 0)),
                  pl.BlockSpec((None, 1, D_MODEL), lambda b, i: (layer, 0, 0))],
        out_specs=out_specs,
        out_shape=out_shape,
        scratch_shapes=[pltpu.VMEM((D_MODEL // LANES, tm, LANES), F32)],
        compiler_params=_cparams(2),
        name="prenorm",
    )(x2, g_pre)
    return outs[0], [o.reshape(t, D_MODEL) for o in outs[1:]]


def _inproj_kernel(h_ref, w_ref, c_ref, s1_ref, s2_ref, o_ref, *, rope_cols):
    j = pl.program_id(0)
    acc = jnp.dot(h_ref[...], w_ref[...], preferred_element_type=F32)
    nsub = w_ref.shape[1] // LANES
    full_tiles, rem = divmod(rope_cols // LANES, nsub)

    def emit(n_rope):
        for c in range(nsub):
            xc = acc[:, c * LANES:(c + 1) * LANES]
            if c < n_rope:
                xc = (xc * c_ref[...] + pltpu.roll(xc, ROT_DIM // 2, 1) * s1_ref[...]
                      + pltpu.roll(xc, LANES - ROT_DIM // 2, 1) * s2_ref[...])
            o_ref[:, c * LANES:(c + 1) * LANES] = xc.astype(o_ref.dtype)

    pl.when(j < full_tiles)(lambda: emit(nsub))
    pl.when(j == full_tiles)(lambda: emit(rem))
    pl.when(j > full_tiles)(lambda: emit(0))


def _tile_cols(w, tn):
    depth, d, n = w.shape
    return w.reshape(depth, d, n // tn, tn).swapaxes(1, 2)


def _inproj(h, w_tiled, layer, tabs, rope_cols):
    t = h.shape[0]
    _, ntile, _, tn = w_tiled.shape
    tm = min(IN_TM, t)
    rope_tiles = -(-rope_cols // tn)
    row = lambda j, i: (i, 0)
    tab = lambda j, i: (jnp.where(j < rope_tiles, i, 0), 0)
    return pl.pallas_call(
        functools.partial(_inproj_kernel, rope_cols=rope_cols),
        grid=(ntile, t // tm),
        in_specs=[pl.BlockSpec((tm, D_MODEL), row),
                  pl.BlockSpec((None, None, D_MODEL, tn), lambda j, i: (layer, j, 0, 0)),
                  pl.BlockSpec((tm, LANES), tab), pl.BlockSpec((tm, LANES), tab), pl.BlockSpec((tm, LANES), tab)],
        out_specs=pl.BlockSpec((tm, tn), lambda j, i: (i, j)),
        out_shape=jax.ShapeDtypeStruct((t, ntile * tn), BF16),
        compiler_params=_cparams(2),
        name="inproj",
    )(h, w_tiled, *tabs)


def _compress_kernel(k_ref, v_ref, pe_ref, w1_ref, w2_ref, kc_ref, vct_ref, x_ref):
    nch = x_ref.shape[0] // CMP_STRIDE
    for which, src in enumerate((k_ref, v_ref)):
        x_ref[...] = src[...].astype(F32)
        rows = [x_ref[pl.ds(l, nch, stride=CMP_STRIDE), :] for l in range(CMP_STRIDE)]
        halves = []
        for part in range(CMP_LEN // CMP_STRIDE):
            xs = [(rows[l] + pe_ref[which, part * CMP_STRIDE + l:part * CMP_STRIDE + l + 1, :]).astype(BF16)
                  for l in range(CMP_STRIDE)]
            halves.append(jnp.dot(jnp.concatenate(xs, axis=1), w1_ref[which, part], preferred_element_type=F32))
        hid = halves[0] + pltpu.roll(halves[1], nch - 1, 0)
        act = hid * jax.nn.sigmoid(hid)
        tok = jnp.dot(act.astype(BF16), w2_ref[which], preferred_element_type=F32)
        if which == 0:
            kc_ref[...] = tok.astype(kc_ref.dtype)
        else:
            vct_ref[...] = tok.T.astype(vct_ref.dtype)


def _compress_weights(pe, w1, w2):
    depth = w1.shape[0]
    g = A_KV_GROUPS
    assert CMP_LEN == 2 * CMP_STRIDE and g == 2
    w1r = w1.reshape(depth, 2, CMP_LEN, HEAD_DIM, CMP_HID)
    z1 = jnp.zeros_like(w1r)
    w1d = jnp.stack([jnp.concatenate([w1r, z1], axis=-1), jnp.concatenate([z1, w1r], axis=-1)], axis=3)
    w1d = w1d.reshape(depth, 2, 2, CMP_STRIDE * g * HEAD_DIM, g * CMP_HID).astype(BF16)
    z2 = jnp.zeros_like(w2)
    w2d = jnp.concatenate([jnp.concatenate([w2, z2], axis=-1), jnp.concatenate([z2, w2], axis=-1)], axis=2)
    pe2 = jnp.concatenate([pe, pe], axis=-1)
    return pe2, w1d, w2d.astype(BF16)


def _compress(u3, pe2, w1d, w2d, layer):
    b, s, _ = u3.shape
    nch = s // CMP_STRIDE
    slab = lambda off: pl.BlockSpec((None, s, LANES), lambda i: (i, 0, off // LANES))
    return pl.pallas_call(
        _compress_kernel,
        grid=(b,),
        in_specs=[slab(U_KCMP), slab(U_VCMP),
                  pl.BlockSpec((None,) + pe2.shape[1:], lambda i: (layer, 0, 0, 0)),
                  pl.BlockSpec((None,) + w1d.shape[1:], lambda i: (layer, 0, 0, 0, 0)),
                  pl.BlockSpec((None,) + w2d.shape[1:], lambda i: (layer, 0, 0, 0))],
        out_specs=[pl.BlockSpec((None, nch, LANES), lambda i: (i, 0, 0)),
                   pl.BlockSpec((None, LANES, nch), lambda i: (i, 0, 0))],
        out_shape=[jax.ShapeDtypeStruct((b, nch, LANES), BF16), jax.ShapeDtypeStruct((b, LANES, nch), BF16)],
        scratch_shapes=[pltpu.VMEM((s, LANES), F32)],
        compiler_params=_cparams(1),
        name="nsa_compress",
    )(u3, u3, pe2, w1d, w2d)


CS_TQ = 128


def _cmp_select_kernel(q_ref, kc_ref, vct_ref, ovt_ref, o_ref, bias_ref, sc_ref):
    tq = CS_TQ
    qi = pl.program_id(1)
    nc = kc_ref.shape[0]
    ns = ovt_ref.shape[0]
    tpos = qi * tq + _lane_iota((1, tq))
    kc = kc_ref[...]
    lane = _lane_iota((1, LANES))
    kst = jnp.concatenate([jnp.where(lane < HEAD_DIM, kc, jnp.zeros_like(kc)),
                           jnp.where(lane >= HEAD_DIM, kc, jnp.zeros_like(kc))], axis=0)
    c_end = _sub_iota((nc, 1)) * CMP_STRIDE + (CMP_LEN - 1)
    cv = c_end <= tpos
    vct = vct_ref[...]
    psum = [jnp.zeros((nc, tq), F32), jnp.zeros((nc, tq), F32)]
    sub = _sub_iota((LANES, 1))
    for i in range(A_HEADS // 2):
        qb = q_ref[:, i * LANES:(i + 1) * LANES]
        st = lax.dot_general(kst, qb, (((1,), (1,)), ((), ())), preferred_element_type=F32)
        ot = []
        for g in range(A_KV_GROUPS):
            s = jnp.where(cv, st[g * nc:(g + 1) * nc], NEG)
            m = jnp.max(s, axis=0, keepdims=True)
            e = jnp.where(cv, jnp.exp2(s - m), 0.0)
            den = jnp.sum(e, axis=0, keepdims=True)
            p = e / jnp.where(den > 0.0, den, 1.0)
            psum[g] = psum[g] + p
            ot.append(jnp.dot(vct, p.astype(BF16), preferred_element_type=F32))
        o_pair = jnp.where(sub < HEAD_DIM, ot[0], ot[1])
        o_ref[:, i * LANES:(i + 1) * LANES] = o_pair.T.astype(o_ref.dtype)
    jdx = _sub_iota((ns, 1))
    cur = tpos // SLC_LEN
    valid = jdx <= cur
    forced = (jdx == 0) | (jdx == cur) | (jdx == cur - 1)
    for g in range(A_KV_GROUPS):
        imp = jnp.dot(ovt_ref[...], psum[g], preferred_element_type=F32, precision=lax.Precision.HIGHEST)
        score = jnp.where(forced, BIG, jnp.where(valid, imp, NEG))
        sc_ref[...] = score
        slab = 8
        parts = [score[v * slab:(v + 1) * slab] for v in range(ns // slab)]
        cnts = [jnp.zeros((slab, tq), F32) for _ in parts]
        for jp in range(ns):
            row = sc_ref[jp:jp + 1, :]
            for v, sv in enumerate(parts):
                if v * slab > jp:
                    ahead = row >= sv
                elif (v + 1) * slab - 1 < jp:
                    ahead = row > sv
                else:
                    tie = jnp.where(jdx[v * slab:(v + 1) * slab] > jp, 1.0, 0.0)
                    cnts[v] = cnts[v] + jnp.where(row == sv, tie, 0.0)
                    ahead = row > sv
                cnts[v] = cnts[v] + jnp.where(ahead, 1.0, 0.0)
        cnt = jnp.concatenate(cnts, axis=0)
        bias = jnp.where(cnt < float(min(SLC_TOPK, ns)), 0.0, NEG)
        bias_ref[g * ns:(g + 1) * ns, :] = bias.astype(bias_ref.dtype)


def _cmp_select(u3, kc, vct, ovt):
    b, s, _ = u3.shape
    nc = kc.shape[1]
    ns = s // SLC_LEN
    assert A_KV_GROUPS * ns == LANES
    return pl.pallas_call(
        _cmp_select_kernel,
        grid=(b, s // CS_TQ),
        in_specs=[pl.BlockSpec((None, CS_TQ, A_WIDTH), lambda i, q: (i, q, U_AQ // A_WIDTH)),
                  pl.BlockSpec((None, nc, LANES), lambda i, q: (i, 0, 0)),
                  pl.BlockSpec((None, LANES, nc), lambda i, q: (i, 0, 0)),
                  pl.BlockSpec((ns, nc), lambda i, q: (0, 0))],
        out_specs=[pl.BlockSpec((None, CS_TQ, A_WIDTH), lambda i, q: (i, q, 0)),
                   pl.BlockSpec((None, LANES, CS_TQ), lambda i, q: (i, 0, q))],
        out_shape=[jax.ShapeDtypeStruct((b, s, A_WIDTH), BF16), jax.ShapeDtypeStruct((b, LANES, s), BF16)],
        scratch_shapes=[pltpu.VMEM((ns, CS_TQ), F32)],
        compiler_params=_cparams(2),
        name="nsa_cmp_select",
    )(u3, kc, vct, ovt)


FLASH_TK = 256
FLASH_CHUNK = 256


def _flash_reset(m_ref, l_ref, acc_ref):
    m_ref[...] = jnp.full(m_ref.shape, NEG, F32)
    l_ref[...] = jnp.zeros(l_ref.shape, F32)
    acc_ref[...] = jnp.zeros(acc_ref.shape, F32)


def _flash_update(st, vt, m_ref, l_ref, acc_ref, idx, mask):
    if mask is not None:
        st = jnp.where(mask, st, NEG)
    m_old = m_ref[idx]
    m_new = jnp.maximum(m_old, jnp.max(st, axis=0, keepdims=True))
    alpha = jnp.exp2(m_old - m_new)
    p = jnp.exp2(st - m_new)
    l_ref[idx] = alpha * l_ref[idx] + jnp.sum(p, axis=0, keepdims=True)
    acc_ref[idx] = alpha * acc_ref[idx] + jnp.dot(vt, p.astype(BF16), preferred_element_type=F32)
    m_ref[idx] = m_new


def _flash_pipeline(n_pairs, n_chain, put, use):
    for c in range(n_chain):
        put(0, c, 0)

    def pair(j, carry):
        t0 = 2 * j
        for c in range(n_chain):
            put(t0 + 1, c, 1)
            use(t0, c, 0)
        for c in range(n_chain):
            put(t0 + 2, c, 0)
            use(t0 + 1, c, 1)
        return carry

    lax.fori_loop(0, n_pairs, pair, 0)


def _transpose_bf16(x):
    return x.astype(F32).T.astype(BF16)


def _merge_rows(lo, hi):
    return jnp.where(_sub_iota((LANES, 1)) < HEAD_DIM, lo, hi)


SLC_TQ = 256


def _slc_kernel(q_ref, biast_ref, k_ref, v_ref, oh_ref, o_ref, vt_ref, qa_ref, st_ref, m_ref, l_ref, acc_ref):
    tq, tk = SLC_TQ, FLASH_TK
    qi = pl.program_id(1)
    nblk = A_HEADS // 2
    sub = _sub_iota((LANES, 1))

    @pl.when(qi == 0)
    def _():
        for kt in range(vt_ref.shape[0]):
            vt_ref[kt] = _transpose_bf16(v_ref[kt * tk:(kt + 1) * tk, :])

    nchain = 2 * nblk
    biast = biast_ref[...]
    zero = jnp.zeros_like(biast)
    mine = (sub < HEAD_DIM, sub >= HEAD_DIM)
    for i in range(nblk):
        qt = _transpose_bf16(q_ref[:, i * LANES:(i + 1) * LANES])
        for h in range(2):
            qa_ref[2 * i + h] = jnp.concatenate([jnp.where(mine[h], qt, zero), jnp.where(mine[h], biast, zero)],
                                                axis=0)
    _flash_reset(m_ref, l_ref, acc_ref)

    def put(kt, c, slot):
        start = pl.multiple_of(kt * tk, tk)
        k = jnp.concatenate([k_ref[pl.ds(start, tk), :], oh_ref[pl.ds(start, tk), :]], axis=1)
        st_ref[slot, c] = jnp.dot(k, qa_ref[c], preferred_element_type=F32)

    def use(kt, c, slot, mask=None):
        _flash_update(st_ref[slot, c], vt_ref[kt], m_ref, l_ref, acc_ref, c, mask)

    n_full = qi
    n_pairs = n_full // 2
    _flash_pipeline(n_pairs, nchain, put, use)
    causal = (n_full * tk + _sub_iota((tk, 1))) <= (qi * tq + _lane_iota((1, tq)))

    @pl.when(n_full == 2 * n_pairs)
    def _():
        for c in range(nchain):
            use(n_full, c, 0, causal)

    @pl.when(n_full != 2 * n_pairs)
    def _():
        for c in range(nchain):
            put(n_full, c, 1)
            use(n_full - 1, c, 0)
        for c in range(nchain):
            use(n_full, c, 1, causal)

    for i in range(nblk):
        ot = _merge_rows(acc_ref[2 * i] / l_ref[2 * i], acc_ref[2 * i + 1] / l_ref[2 * i + 1])
        o_ref[:, i * LANES:(i + 1) * LANES] = ot.T.astype(o_ref.dtype)


def _slc_attention(u3, biast, onehot):
    b, s, _ = u3.shape
    nblk = A_HEADS
    assert SLC_TQ == FLASH_CHUNK == FLASH_TK
    return pl.pallas_call(
        _slc_kernel,
        grid=(b, s // SLC_TQ),
        in_specs=[pl.BlockSpec((None, SLC_TQ, A_WIDTH), lambda i, q: (i, q, U_AQ // A_WIDTH)),
                  pl.BlockSpec((None, LANES, SLC_TQ), lambda i, q: (i, 0, q)),
                  pl.BlockSpec((None, s, LANES), lambda i, q: (i, 0, U_KSLC // LANES)),
                  pl.BlockSpec((None, s, LANES), lambda i, q: (i, 0, U_VSLC // LANES)),
                  pl.BlockSpec((s, LANES), lambda i, q: (0, 0))],
        out_specs=pl.BlockSpec((None, SLC_TQ, A_WIDTH), lambda i, q: (i, q, 0)),
        out_shape=jax.ShapeDtypeStruct((b, s, A_WIDTH), BF16),
        scratch_shapes=[pltpu.VMEM((s // FLASH_TK, LANES, FLASH_TK), BF16),
                        pltpu.VMEM((nblk, 2 * LANES, FLASH_CHUNK), BF16),
                        pltpu.VMEM((2, nblk, FLASH_TK, FLASH_CHUNK), F32),
                        pltpu.VMEM((nblk, 1, FLASH_CHUNK), F32), pltpu.VMEM((nblk, 1, FLASH_CHUNK), F32),
                        pltpu.VMEM((nblk, LANES, FLASH_CHUNK), F32)],
        compiler_params=_cparams(2),
        name="nsa_selected",
    )(u3, biast, u3, u3, onehot)


BAND_T = 128
BAND_ROWS = 512
BAND_LOOKAHEAD = 3


def _banded_kernel(*refs, nprev, max_dist, shared_kv, with_lse):
    t = BAND_T
    q_ref, k_ref, v_ref, o_ref = refs[:4]
    lse_ref = refs[4] if with_lse else None
    qi = pl.program_id(1)
    nsub = q_ref.shape[0] // t
    nblk = q_ref.shape[1] // LANES
    nk = (nprev + 1) * t
    col = _lane_iota((1, 2 * t))
    base = jnp.where(col >= t, col - t, col) - _sub_iota((nk, 1))
    sub = _sub_iota((LANES, 1))

    def window(s):
        qt = qi * nsub + s
        start = pl.multiple_of(jnp.maximum(qt - nprev, 0) * t, t)
        return start, qt * t - start

    def scores(s, i):
        start, _ = window(s)
        cols = slice(None) if shared_kv else slice(i * LANES, (i + 1) * LANES)
        k = k_ref[pl.ds(start, nk), cols]
        qt = _transpose_bf16(q_ref[s * t:(s + 1) * t, i * LANES:(i + 1) * LANES])
        zero = jnp.zeros_like(qt)
        qs = jnp.concatenate([jnp.where(sub < HEAD_DIM, qt, zero), jnp.where(sub >= HEAD_DIM, qt, zero)], axis=1)
        return jnp.dot(k, qs, preferred_element_type=F32)

    def finish(s, i, st):
        start, dist = window(s)
        cols = slice(None) if shared_kv else slice(i * LANES, (i + 1) * LANES)
        vt = jnp.concatenate([_transpose_bf16(v_ref[pl.ds(start + j * t, t), cols]) for j in range(nprev + 1)],
                             axis=1)
        rel = base + dist
        st = jnp.where((rel >= 0) & (rel <= max_dist), st, NEG)
        m = jnp.max(st, axis=0, keepdims=True)
        p = jnp.exp2(st - m)
        l = jnp.sum(p, axis=0, keepdims=True)
        ot = jnp.dot(vt, p.astype(BF16), preferred_element_type=F32) / l
        rows, sl = slice(s * t, (s + 1) * t), slice(i * LANES, (i + 1) * LANES)
        o_ref[rows, sl] = _merge_rows(ot[:, 0:t], ot[:, t:2 * t]).T.astype(o_ref.dtype)
        if with_lse:
            lse = m + jnp.log2(l)
            lse_ref[rows, sl] = _merge_rows(jnp.broadcast_to(lse[:, 0:t], (LANES, t)),
                                            jnp.broadcast_to(lse[:, t:2 * t], (LANES, t))).T

    items = [(s, i) for s in range(nsub) for i in range(nblk)]
    pending = [scores(*it) for it in items[:BAND_LOOKAHEAD]]
    for j, it in enumerate(items):
        if j + BAND_LOOKAHEAD < len(items):
            pending.append(scores(*items[j + BAND_LOOKAHEAD]))
        finish(*it, pending[j])


def _banded(src, *, nseq, seqlen, q_off, k_off, v_off, kv_width, max_dist, with_lse):
    t = BAND_T
    rows = min(BAND_ROWS, seqlen)
    nq = seqlen // rows
    nprev = -(-max_dist // t)
    shared = kv_width == LANES
    assert (nprev + 1) * t <= seqlen and seqlen % rows == 0 and rows % t == 0
    qo = pl.BlockSpec((rows, A_WIDTH), lambda n, q: (n * nq + q, q_off // A_WIDTH))
    out = pl.BlockSpec((rows, A_WIDTH), lambda n, q: (n * nq + q, 0))
    in_specs = [qo,
                pl.BlockSpec((seqlen, kv_width), lambda n, q: (n, k_off // kv_width)),
                pl.BlockSpec((seqlen, kv_width), lambda n, q: (n, v_off // kv_width))]
    out_specs = [out]
    out_shape = [jax.ShapeDtypeStruct((nseq * seqlen, A_WIDTH), BF16)]
    if with_lse:
        out_specs.append(out)
        out_shape.append(jax.ShapeDtypeStruct((nseq * seqlen, A_WIDTH), F32))
    return pl.pallas_call(
        functools.partial(_banded_kernel, nprev=nprev, max_dist=max_dist, shared_kv=shared, with_lse=with_lse),
        grid=(nseq, nq),
        in_specs=in_specs,
        out_specs=out_specs,
        out_shape=out_shape,
        compiler_params=_cparams(2),
        name="banded_attention",
    )(src, src, src)


MLA_TM = 512


def _rms(x, g):
    return x * lax.rsqrt(jnp.mean(x * x, axis=-1, keepdims=True) + EPS) * g


def _mla_prep_kernel(cq_ref, ckv_ref, kr_ref, c_ref, s1_ref, s2_ref, gq_ref, gkv_ref, wq_ref, wk_ref, wv_ref,
                     q_out, k_out, vt_out):
    half = QK_ROPE // 2
    tk = FLASH_TK
    cm, s1, s2 = c_ref[...], s1_ref[...], s2_ref[...]

    def rope(xc):
        return xc * cm + pltpu.roll(xc, half, 1) * s1 + pltpu.roll(xc, LANES - half, 1) * s2

    qn = _rms(cq_ref[...].astype(F32), gq_ref[...]).astype(BF16)
    q = jnp.dot(qn, wq_ref[...], preferred_element_type=F32)
    kvn = _rms(ckv_ref[...].astype(F32), gkv_ref[...]).astype(BF16)
    kk = jnp.dot(kvn, wk_ref[...], preferred_element_type=F32)
    kr = rope(kr_ref[...].astype(F32))
    for h in range(B_HEADS):
        sl = slice(h * LANES, (h + 1) * LANES)
        q_out[:, sl] = (rope(q[:, sl]) * MLA_SCALE).astype(q_out.dtype)
        k_out[:, sl] = (kk[:, sl] + kr).astype(k_out.dtype)
    v = jnp.dot(kvn, wv_ref[...], preferred_element_type=F32)
    for j in range(v.shape[0] // tk):
        for hp in range(B_HEADS // 2):
            vt_out[j, hp] = v[j * tk:(j + 1) * tk, hp * LANES:(hp + 1) * LANES].T.astype(vt_out.dtype)


def _mla_prep(u, tabs, g_q, g_kv, wq, wk, wv, layer):
    t = u.shape[0]
    tm = min(MLA_TM, t)
    row = lambda i: (i, 0)
    wide = B_HEADS * LANES
    assert U_BCQ % Q_LORA == 0 and tm % FLASH_TK == 0
    return pl.pallas_call(
        _mla_prep_kernel,
        grid=(t // tm,),
        in_specs=[pl.BlockSpec((tm, Q_LORA), lambda i: (i, U_BCQ // Q_LORA)),
                  pl.BlockSpec((tm, LANES), lambda i: (i, U_BCKV // LANES)),
                  pl.BlockSpec((tm, LANES), lambda i: (i, U_BKR // LANES)),
                  pl.BlockSpec((tm, LANES), row), pl.BlockSpec((tm, LANES), row), pl.BlockSpec((tm, LANES), row),
                  pl.BlockSpec((None, 1, Q_LORA), lambda i: (layer, 0, 0)),
                  pl.BlockSpec((None, 1, KV_LORA), lambda i: (layer, 0, 0)),
                  pl.BlockSpec((None, Q_LORA, wide), lambda i: (layer, 0, 0)),
                  pl.BlockSpec((None, KV_LORA, wide), lambda i: (layer, 0, 0)),
                  pl.BlockSpec((None, KV_LORA, B_WIDTH), lambda i: (layer, 0, 0))],
        out_specs=[pl.BlockSpec((tm, wide), row), pl.BlockSpec((tm, wide), row),
                   pl.BlockSpec((tm // FLASH_TK, B_HEADS // 2, LANES, FLASH_TK), lambda i: (i, 0, 0, 0))],
        out_shape=[jax.ShapeDtypeStruct((t, wide), BF16), jax.ShapeDtypeStruct((t, wide), BF16),
                   jax.ShapeDtypeStruct((t // FLASH_TK, B_HEADS // 2, LANES, FLASH_TK), BF16)],
        compiler_params=_cparams(1),
        name="mla_prep",
    )(u, u, u, *tabs, g_q, g_kv, wq, wk, wv)


MLA_TQ = 1024


def _mla_flash_kernel(q_ref, k_ref, vt_ref, o_ref, qt_ref, st_ref, m_ref, l_ref, acc_ref):
    tq, tk, cw = MLA_TQ, FLASH_TK, FLASH_CHUNK
    nhalf = tq // cw
    qi = pl.program_id(2)
    for h in range(2):
        qt_ref[h] = _transpose_bf16(q_ref[:, h * LANES:(h + 1) * LANES])
    _flash_reset(m_ref, l_ref, acc_ref)

    def put(kt, c, slot):
        h, half = divmod(c, nhalf)
        start = pl.multiple_of(kt * tk, tk)
        k = k_ref[pl.ds(start, tk), h * LANES:(h + 1) * LANES]
        st_ref[slot, c] = jnp.dot(k, qt_ref[h, :, half * cw:(half + 1) * cw], preferred_element_type=F32)

    def use(kt, c, slot, masked=False):
        mask = None
        if masked:
            qpos = qi * tq + (c % nhalf) * cw + _lane_iota((1, cw))
            mask = (kt * tk + _sub_iota((tk, 1))) <= qpos
        _flash_update(st_ref[slot, c], vt_ref[kt], m_ref, l_ref, acc_ref, c, mask)

    assert tk == cw and nhalf % 2 == 0
    n_full = nhalf * qi
    _flash_pipeline(n_full // 2, 2 * nhalf, put, use)
    live = lambda d: [c for c in range(2 * nhalf) if c % nhalf >= d]
    for d in range(nhalf):
        if d + 1 < nhalf:
            for c in live(d + 1):
                put(n_full + d + 1, c, (d + 1) % 2)
        for c in live(d):
            use(n_full + d, c, d % 2, masked=(c % nhalf == d))
    ots = []
    for h in range(2):
        ots.append(jnp.concatenate([acc_ref[h * nhalf + half] / l_ref[h * nhalf + half] for half in range(nhalf)],
                                   axis=1))
    o_ref[...] = _merge_rows(ots[0], ots[1]).T.astype(o_ref.dtype)


def _mla_flash(q, k, vt, batch, seq):
    assert FLASH_TK == FLASH_CHUNK
    q3 = q.reshape(batch, seq, -1)
    k3 = k.reshape(batch, seq, -1)
    vt5 = vt.reshape(batch, seq // FLASH_TK, B_HEADS // 2, LANES, FLASH_TK)
    nchain = 2 * (MLA_TQ // FLASH_CHUNK)
    o = pl.pallas_call(
        _mla_flash_kernel,
        grid=(batch, B_HEADS // 2, seq // MLA_TQ),
        in_specs=[pl.BlockSpec((None, MLA_TQ, 2 * LANES), lambda b, h, i: (b, i, h)),
                  pl.BlockSpec((None, seq, 2 * LANES), lambda b, h, i: (b, 0, h)),
                  pl.BlockSpec((None, seq // FLASH_TK, None, LANES, FLASH_TK), lambda b, h, i: (b, 0, h, 0, 0))],
        out_specs=pl.BlockSpec((None, MLA_TQ, LANES), lambda b, h, i: (b, i, h)),
        out_shape=jax.ShapeDtypeStruct((batch, seq, B_WIDTH), BF16),
        scratch_shapes=[pltpu.VMEM((2, LANES, MLA_TQ), BF16),
                        pltpu.VMEM((2, nchain, FLASH_TK, FLASH_CHUNK), F32),
                        pltpu.VMEM((nchain, 1, FLASH_CHUNK), F32), pltpu.VMEM((nchain, 1, FLASH_CHUNK), F32),
                        pltpu.VMEM((nchain, LANES, FLASH_CHUNK), F32)],
        compiler_params=_cparams(3),
        name="mla_flash",
    )(q3, k3, vt5)
    return o.reshape(batch * seq, B_WIDTH)


OUT_TM = 256


def _out_kernel(x_ref, p_ref, ocmp_ref, oslc_ref, owin_ref, gate_ref, e_ref, az_ref, ob_ref, bz_ref,
                oc0_ref, oc1_ref, oc2_ref, l0_ref, l1_ref, l2_ref, cz_ref, mg0_ref, mg1_ref, mg2_ref,
                wa_ref, wb_ref, wc_ref, wout_ref, wplg_ref, wple_ref, gpost_ref, o_ref, *tok_refs):
    tok = iter(tok_refs)

    def sig(z):
        return 0.5 * jnp.tanh(0.5 * z) + 0.5

    def f(r):
        if len(r.shape) == 2:
            return r[...].astype(F32)
        dil, n, width = r.shape
        buf = next(tok)
        for res in range(dil):
            blk = r[res].astype(F32)
            for c in range(width // LANES):
                buf[c, pl.ds(res, n, stride=dil), :] = blk[:, c * LANES:(c + 1) * LANES]
        return jnp.concatenate([buf[c] for c in range(width // LANES)], axis=1)

    def silu(z):
        return z * sig(z)

    def mm(a, w_ref):
        return jnp.dot(a.astype(BF16), w_ref[...], preferred_element_type=F32)

    g = sig(gate_ref[...].astype(F32))
    g_hi = g.astype(BF16)
    g_lo = (g - g_hi.astype(F32)).astype(BF16)
    gs = (jnp.dot(g_hi, e_ref[...], preferred_element_type=F32)
          + jnp.dot(g_lo, e_ref[...], preferred_element_type=F32))
    o_a = (gs[:, 0:A_WIDTH] * f(ocmp_ref) + gs[:, A_WIDTH:2 * A_WIDTH] * f(oslc_ref)
           + gs[:, 2 * A_WIDTH:3 * A_WIDTH] * f(owin_ref))
    y_a = mm(o_a * silu(f(az_ref)), wa_ref)
    y_b = mm(f(ob_ref) * silu(f(bz_ref)), wb_ref)
    l0, l1, l2 = f(l0_ref), f(l1_ref), f(l2_ref)
    mx = jnp.maximum(jnp.maximum(l0, l1), l2)
    e0, e1, e2 = jnp.exp2(l0 - mx), jnp.exp2(l1 - mx), jnp.exp2(l2 - mx)
    o_c = (e0 * f(oc0_ref) + e1 * f(oc1_ref) + e2 * f(oc2_ref)) / (e0 + e1 + e2)
    y_c = mm(o_c * silu(f(cz_ref)), wc_ref)
    mix = sig(f(mg0_ref)) * y_a + sig(f(mg1_ref)) * y_b + sig(f(mg2_ref)) * y_c
    y = mm(mix, wout_ref)
    x1 = x_ref[...] + _rms(y, gpost_ref[...])
    o_ref[...] = x1 + sig(mm(x1, wplg_ref)) * mm(p_ref[...], wple_ref)


def _out_layer(x2, p, layer, o_cmp, o_slc, o_win, u, expand, o_b, o_c, lse_c, w, batch, dils):
    t = x2.shape[0]
    seq = t // batch
    tm = min(OUT_TM, seq)
    nper = seq // tm
    row = lambda i: (i, 0)
    ucol = lambda off, width: pl.BlockSpec((tm, width), lambda i: (i, off // width))
    half = pl.BlockSpec((tm, A_WIDTH), row)
    wspec = lambda r, c: pl.BlockSpec((None, r, c), lambda i: (layer, 0, 0))

    def group_spec(dil):
        if dil == 1:
            return half
        return pl.BlockSpec((None, dil, tm // dil, C_WIDTH), lambda i: (i // nper, 0, i % nper, 0))

    def group_view(a, dil):
        return a if dil == 1 else a.reshape(batch, dil, seq // dil, C_WIDTH)

    gspecs = [group_spec(d) for d in dils]
    o_c = [group_view(a, d) for a, d in zip(o_c, dils)]
    lse_c = [group_view(a, d) for a, d in zip(lse_c, dils)]
    n_tok = 2 * sum(1 for d in dils if d > 1)
    in_specs = [pl.BlockSpec((tm, D_MODEL), row),
                pl.BlockSpec((None, tm, PLE_DIM), lambda i: (layer, i, 0)),
                half, half, half,
                ucol(U_GATE, LANES), pl.BlockSpec((LANES, 3 * A_WIDTH), lambda i: (0, 0)),
                ucol(U_AZ, A_WIDTH), half, ucol(U_BZ, B_WIDTH),
                *gspecs, *gspecs, ucol(U_CZ, C_WIDTH),
                ucol(U_MG, D_MODEL), ucol(U_MG + D_MODEL, D_MODEL), ucol(U_MG + 2 * D_MODEL, D_MODEL),
                wspec(A_WIDTH, D_MODEL), wspec(B_WIDTH, D_MODEL), wspec(C_WIDTH, D_MODEL),
                wspec(D_MODEL, D_MODEL), wspec(D_MODEL, D_MODEL), wspec(PLE_DIM, D_MODEL),
                wspec(1, D_MODEL)]
    return pl.pallas_call(
        _out_kernel,
        grid=(t // tm,),
        in_specs=in_specs,
        out_specs=pl.BlockSpec((tm, D_MODEL), row),
        out_shape=jax.ShapeDtypeStruct((t, D_MODEL), F32),
        scratch_shapes=[pltpu.VMEM((C_WIDTH // LANES, tm, LANES), F32)] * n_tok,
        compiler_params=_cparams(1),
        name="out_layer",
    )(x2, p, o_cmp, o_slc, o_win, u, expand, u, o_b, u, *o_c, *lse_c, u, u, u, u,
      w["w_a"], w["w_b"], w["w_c"], w["w_out"], w["w_plg"], w["w_ple"], w["g_post"])


def _overlap_t(seq):
    nc = seq // CMP_STRIDE
    ns = seq // SLC_LEN
    cs = np.arange(nc)[:, None] * CMP_STRIDE
    js = np.arange(ns)[None, :] * SLC_LEN
    ov = np.clip(np.minimum(cs + CMP_LEN, js + SLC_LEN) - np.maximum(cs, js), 0, None).astype(np.float32) / CMP_LEN
    return jnp.asarray(ov.T)


def _block_onehot(seq):
    blk = np.arange(seq)[:, None] // SLC_LEN
    return jnp.asarray((np.arange(LANES)[None, :] % HEAD_DIM == blk).astype(np.float32), dtype=BF16)


def _gate_expand():
    e = np.zeros((LANES, 3 * A_WIDTH), np.float32)
    for br in range(3):
        for i in range(A_HEADS // 2):
            for g in range(A_KV_GROUPS):
                h = g * (A_HEADS // 2) + i
                c0 = br * A_WIDTH + i * LANES + g * HEAD_DIM
                e[h * 3 + br, c0:c0 + HEAD_DIM] = 1.0
    return jnp.asarray(e, dtype=BF16)


def kernel(x, p, positions, g_pre, g_post, w_in, nsa_cmp_pe, nsa_cmp_w1, nsa_cmp_w2, w_a, mla_g_q, mla_w_uq,
           mla_g_kv, mla_w_ukv, w_b, w_c, w_out, w_ple, w_plg):
    batch, seq, _ = x.shape
    depth = w_in.shape[0]
    t = batch * seq
    nch = seq // CMP_STRIDE

    part_tabs, mla_tabs = _rope_tables(positions)
    dils = [dil for _, dil in C_PAIRS]
    assert dils[0] == 1
    perm_tabs = {dil: tuple(_to_residue_major(a, batch, dil) for a in part_tabs) for dil in dils[1:]}
    w_main, w_groups = _regroup_w_in(w_in)
    w_main = _tile_cols(w_main, IN_TN)
    w_groups = [_tile_cols(w, DG_COLS) for w in w_groups]
    g_pre3 = g_pre.reshape(depth, 1, D_MODEL)
    pe2, w1d, w2d = _compress_weights(nsa_cmp_pe, nsa_cmp_w1, nsa_cmp_w2)
    wq = mla_w_uq.reshape(depth, Q_LORA, B_HEADS, QK_NOPE + QK_ROPE)
    wq = jnp.pad(wq, ((0, 0), (0, 0), (0, 0), (0, LANES - QK_NOPE - QK_ROPE))).reshape(depth, Q_LORA, -1).astype(BF16)
    wkv = mla_w_ukv.reshape(depth, KV_LORA, B_HEADS, QK_NOPE + V_DIM)
    wk = jnp.pad(wkv[..., :QK_NOPE], ((0, 0), (0, 0), (0, 0), (0, LANES - QK_NOPE))).reshape(depth, KV_LORA, -1)
    wk = wk.astype(BF16)
    wv = wkv[..., QK_NOPE:].reshape(depth, KV_LORA, B_WIDTH).astype(BF16)
    gq3 = mla_g_q.reshape(depth, 1, Q_LORA)
    gkv3 = mla_g_kv.reshape(depth, 1, KV_LORA)
    w_a_p = w_a.reshape(depth, 2, 4, HEAD_DIM, D_MODEL).swapaxes(1, 2).reshape(depth, A_WIDTH, D_MODEL)
    wts = {"w_a": w_a_p.astype(BF16), "w_b": w_b.astype(BF16), "w_c": w_c.astype(BF16),
           "w_out": w_out.astype(BF16), "w_plg": w_plg.astype(BF16), "w_ple": w_ple.astype(BF16),
           "g_post": g_post.reshape(depth, 1, D_MODEL)}
    p3 = p.reshape(depth, t, PLE_DIM)
    ovt = _overlap_t(seq)
    onehot = _block_onehot(seq)
    expand = _gate_expand()

    x2 = x.reshape(t, D_MODEL)
    for layer in range(depth):
        h, h_res = _prenorm(x2, g_pre3, layer, batch, dils[1:])
        u = _inproj(h, w_main, layer, part_tabs, ROPE_COLS)
        u3 = u.reshape(batch, seq, DP)
        kc, vct = _compress(u3, pe2, w1d, w2d, layer)
        o_cmp, biast = _cmp_select(u3, kc, vct, ovt)
        o_slc = _slc_attention(u3, biast, onehot)
        (o_win,) = _banded(u, nseq=batch, seqlen=seq, q_off=U_AQ, k_off=U_KWIN, v_off=U_VWIN,
                           kv_width=LANES, max_dist=WIN - 1, with_lse=False)
        q_b, k_b, vt_b = _mla_prep(u, mla_tabs, gq3, gkv3, wq, wk, wv, layer)
        o_b = _mla_flash(q_b, k_b, vt_b, batch, seq)
        o_c, lse_c = [], []
        for gi, (window, dil) in enumerate(C_PAIRS):
            if dil == 1:
                o_g, lse_g = _banded(u, nseq=batch, seqlen=seq, q_off=U_CQ0, k_off=U_CK0, v_off=U_CV0,
                                     kv_width=C_WIDTH, max_dist=window, with_lse=True)
            else:
                u_g = _inproj(h_res[gi - 1], w_groups[gi - 1], layer, perm_tabs[dil], DG_ROPE_COLS)
                o_g, lse_g = _banded(u_g, nseq=batch * dil, seqlen=seq // dil, q_off=0, k_off=C_WIDTH,
                                     v_off=2 * C_WIDTH, kv_width=C_WIDTH, max_dist=window // dil, with_lse=True)
            o_c.append(o_g)
            lse_c.append(lse_g)
        x2 = _out_layer(x2, p3, layer, o_cmp.reshape(t, A_WIDTH), o_slc.reshape(t, A_WIDTH), o_win, u, expand,
                        o_b, o_c, lse_c, wts, batch, dils)
    return x2.reshape(batch, seq, D_MODEL)
```

```python
import functools

import numpy as np
import jax
import jax.numpy as jnp
from jax import lax
from jax.experimental import pallas as pl
from jax.experimental.pallas import tpu as pltpu

F32 = jnp.float32
BF16 = jnp.bfloat16

D_MODEL = 1024
PLE_DIM = 256
ROPE_THETA = 500000.0
HEAD_DIM = 64
ROT_DIM = HEAD_DIM // 4
EPS = 1e-6
NEG = -1e30
BIG = 1e30
LANES = 128

A_HEADS = 8
A_KV_GROUPS = 2
A_WIDTH = A_HEADS * HEAD_DIM
CMP_LEN = 32
CMP_STRIDE = 16
CMP_HID = 256
SLC_LEN = 64
SLC_TOPK = 16
WIN = 512

B_HEADS = 8
Q_LORA = 384
KV_LORA = 128
QK_NOPE = 64
QK_ROPE = 32
V_DIM = 64
B_WIDTH = B_HEADS * V_DIM

C_PAIRS = ((128, 1), (512, 4), (2048, 16))
C_GROUPS = len(C_PAIRS)
C_HEADS = 8
C_WIDTH = C_HEADS * HEAD_DIM

IN_SIZES = (A_WIDTH, 6 * A_KV_GROUPS * HEAD_DIM, 3 * A_HEADS, A_WIDTH, Q_LORA, KV_LORA, QK_ROPE,
            B_WIDTH, C_GROUPS * 3 * C_WIDTH, C_WIDTH, 3 * D_MODEL)
IN_OFF = tuple(int(v) for v in np.cumsum((0,) + IN_SIZES))
(OFF_AQ, OFF_AKV, OFF_AG, OFF_AZ, OFF_BCQ, OFF_BCKV, OFF_BKR, OFF_BZ, OFF_CQKV, OFF_CZ, OFF_MG) = IN_OFF[:-1]

U_AQ = 0
U_CQ0 = 512
U_CK0 = 1024
U_KCMP = 1536
U_KSLC = 1664
U_KWIN = 1792
ROPE_COLS = 1920
U_VCMP = 1920
U_AZ = 2048
U_BZ = 2560
U_CV0 = 3072
U_CZ = 3584
U_MG = 4096
U_VSLC = 7168
U_VWIN = 7296
U_GATE = 7424
U_BCKV = 7552
U_BCQ = 7680
U_BKR = 8064
DP = 8192
DG_COLS = 3 * C_WIDTH
DG_ROPE_COLS = 2 * C_WIDTH

LOG2E = float(np.log2(np.e))
Q_SCALE = float(HEAD_DIM ** -0.5) * LOG2E
MLA_SCALE = float((QK_NOPE + QK_ROPE) ** -0.5) * LOG2E
VMEM_LIMIT = 48 * 1024 * 1024


def _cparams(n_axes):
    return pltpu.CompilerParams(dimension_semantics=("arbitrary",) * n_axes, vmem_limit_bytes=VMEM_LIMIT)


def _lane_iota(shape):
    return lax.broadcasted_iota(jnp.int32, shape, len(shape) - 1)


def _sub_iota(shape):
    return lax.broadcasted_iota(jnp.int32, shape, len(shape) - 2)


def _pair_cols(w):
    lead = w.shape[:-1]
    return w.reshape(lead + (2, 4, HEAD_DIM)).swapaxes(-3, -2).reshape(lead + (A_WIDTH,))


def _regroup_w_in(w_in):
    def col(off, n):
        return w_in[..., off:off + n]
    z = lambda n: jnp.zeros(w_in.shape[:-1] + (n,), w_in.dtype)
    akv = lambda which: col(OFF_AKV + which * 128, 128)
    cq = lambda gi, t: col(OFF_CQKV + gi * 3 * C_WIDTH + t * C_WIDTH, C_WIDTH)
    pieces = [_pair_cols(col(OFF_AQ, A_WIDTH)) * Q_SCALE, cq(0, 0) * Q_SCALE, cq(0, 1)]
    pieces += [akv(0), akv(2), akv(4)]
    pieces += [akv(1)]
    pieces += [_pair_cols(col(OFF_AZ, A_WIDTH)), col(OFF_BZ, B_WIDTH), cq(0, 2)]
    pieces += [col(OFF_CZ, C_WIDTH), col(OFF_MG, 3 * D_MODEL)]
    pieces += [akv(3), akv(5)]
    pieces += [col(OFF_AG, 3 * A_HEADS), z(128 - 3 * A_HEADS)]
    pieces += [col(OFF_BCKV, KV_LORA), col(OFF_BCQ, Q_LORA)]
    pieces += [z(64), col(OFF_BKR, QK_ROPE), z(32)]
    main = jnp.concatenate(pieces, axis=-1).astype(BF16)
    assert main.shape[-1] == DP
    groups = [jnp.concatenate([cq(gi, 0) * Q_SCALE, cq(gi, 1), cq(gi, 2)], axis=-1).astype(BF16)
              for gi in range(1, C_GROUPS)]
    return main, groups


def _rope_tables(pos, dim, period, offset):
    half = dim // 2
    ch = np.arange(LANES) % period - offset
    in_span = (ch >= 0) & (ch < dim)
    freq = np.where(in_span, ch % half, 0).astype(np.float32)
    inv = jnp.where(jnp.asarray(in_span), ROPE_THETA ** (-2.0 * jnp.asarray(freq) / dim), 0.0)
    ang = pos[:, None] * inv[None, :]
    cos, sin = jnp.cos(ang), jnp.sin(ang)
    first = jnp.asarray(in_span & (ch < half))[None, :]
    second = jnp.asarray(in_span & (ch >= half))[None, :]
    return cos, jnp.where(second, sin, 0.0), jnp.where(first, -sin, 0.0)


def _residue_major_positions(positions, dil):
    b, s = positions.shape
    return positions.reshape(b, s // dil, dil).swapaxes(1, 2).reshape(-1)


NORM_TM = 512
IN_TM = 512
IN_TN = 2048


def _norm_kernel(x_ref, g_ref, o_ref, *rest, dils):
    o_dil, y_ref = rest[:len(dils)], rest[len(dils)]
    x = x_ref[...]
    ms = jnp.mean(x * x, axis=-1, keepdims=True)
    y = x * lax.rsqrt(ms + EPS) * g_ref[...]
    o_ref[...] = y.astype(o_ref.dtype)
    tm = x.shape[0]
    nlane = y_ref.shape[0]
    for c in range(nlane):
        y_ref[c] = y[:, c * LANES:(c + 1) * LANES]
    for o_d, dil in zip(o_dil, dils):
        for r in range(dil):
            for c in range(nlane):
                o_d[r, :, c * LANES:(c + 1) * LANES] = (
                    y_ref[c, pl.ds(r, tm // dil, stride=dil), :].astype(o_d.dtype))


def _prenorm(x2, g_pre, layer, batch, dils):
    t = x2.shape[0]
    seq = t // batch
    tm = min(NORM_TM, seq)
    nper = seq // tm
    out_specs = [pl.BlockSpec((tm, D_MODEL), lambda b, i: (b * nper + i, 0))]
    out_shape = [jax.ShapeDtypeStruct((t, D_MODEL), BF16)]
    for dil in dils:
        out_specs.append(pl.BlockSpec((None, dil, tm // dil, D_MODEL), lambda b, i: (b, 0, i, 0)))
        out_shape.append(jax.ShapeDtypeStruct((batch, dil, seq // dil, D_MODEL), BF16))
    outs = pl.pallas_call(
        functools.partial(_norm_kernel, dils=tuple(dils)),
        grid=(batch, nper),
        in_specs=[pl.BlockSpec((tm, D_MODEL), lambda b, i: (b * nper + i, 0)),
                  pl.BlockSpec((None, 1, D_MODEL), lambda b, i: (layer, 0, 0))],
        out_specs=out_specs,
        out_shape=out_shape,
        scratch_shapes=[pltpu.VMEM((D_MODEL // LANES, tm, LANES), F32)],
        compiler_params=_cparams(2),
        name="prenorm",
    )(x2, g_pre)
    return outs[0], [o.reshape(t, D_MODEL) for o in outs[1:]]


def _inproj_kernel(h_ref, w_ref, c_ref, s1_ref, s2_ref, o_ref, *, rope_cols):
    j = pl.program_id(0)
    acc = jnp.dot(h_ref[...], w_ref[...], preferred_element_type=F32)
    nsub = w_ref.shape[1] // LANES
    full_tiles, rem = divmod(rope_cols // LANES, nsub)

    def emit(n_rope):
        for c in range(nsub):
            xc = acc[:, c * LANES:(c + 1) * LANES]
            if c < n_rope:
                xc = (xc * c_ref[...] + pltpu.roll(xc, ROT_DIM // 2, 1) * s1_ref[...]
                      + pltpu.roll(xc, LANES - ROT_DIM // 2, 1) * s2_ref[...])
            o_ref[:, c * LANES:(c + 1) * LANES] = xc.astype(o_ref.dtype)

    pl.when(j < full_tiles)(lambda: emit(nsub))
    pl.when(j == full_tiles)(lambda: emit(rem))
    pl.when(j > full_tiles)(lambda: emit(0))


def _tile_cols(w, tn):
    depth, d, n = w.shape
    return w.reshape(depth, d, n // tn, tn).swapaxes(1, 2)


def _inproj(h, w_tiled, layer, tabs, rope_cols):
    t = h.shape[0]
    _, ntile, _, tn = w_tiled.shape
    tm = min(IN_TM, t)
    rope_tiles = -(-rope_cols // tn)
    row = lambda j, i: (i, 0)
    tab = lambda j, i: (jnp.where(j < rope_tiles, i, 0), 0)
    return pl.pallas_call(
        functools.partial(_inproj_kernel, rope_cols=rope_cols),
        grid=(ntile, t // tm),
        in_specs=[pl.BlockSpec((tm, D_MODEL), row),
                  pl.BlockSpec((None, None, D_MODEL, tn), lambda j, i: (layer, j, 0, 0)),
                  pl.BlockSpec((tm, LANES), tab), pl.BlockSpec((tm, LANES), tab), pl.BlockSpec((tm, LANES), tab)],
        out_specs=pl.BlockSpec((tm, tn), lambda j, i: (i, j)),
        out_shape=jax.ShapeDtypeStruct((t, ntile * tn), BF16),
        compiler_params=_cparams(2),
        name="inproj",
    )(h, w_tiled, *tabs)


def _compress_kernel(k_ref, v_ref, pe_ref, w1_ref, w2_ref, kc_ref, vct_ref, x_ref):
    nch = x_ref.shape[0] // CMP_STRIDE
    for which, src in enumerate((k_ref, v_ref)):
        x_ref[...] = src[...].astype(F32)
        rows = [x_ref[pl.ds(l, nch, stride=CMP_STRIDE), :] for l in range(CMP_STRIDE)]
        halves = []
        for part in range(CMP_LEN // CMP_STRIDE):
            xs = [(rows[l] + pe_ref[which, part * CMP_STRIDE + l:part * CMP_STRIDE + l + 1, :]).astype(BF16)
                  for l in range(CMP_STRIDE)]
            halves.append(jnp.dot(jnp.concatenate(xs, axis=1), w1_ref[which, part], preferred_element_type=F32))
        hid = halves[0] + pltpu.roll(halves[1], nch - 1, 0)
        act = hid * jax.nn.sigmoid(hid)
        tok = jnp.dot(act.astype(BF16), w2_ref[which], preferred_element_type=F32)
        if which == 0:
            kc_ref[...] = tok.astype(kc_ref.dtype)
        else:
            vct_ref[...] = tok.T.astype(vct_ref.dtype)


def _compress_weights(pe, w1, w2):
    depth = w1.shape[0]
    g = A_KV_GROUPS
    assert CMP_LEN == 2 * CMP_STRIDE and g == 2
    w1r = w1.reshape(depth, 2, CMP_LEN, HEAD_DIM, CMP_HID)
    z1 = jnp.zeros_like(w1r)
    w1d = jnp.stack([jnp.concatenate([w1r, z1], axis=-1), jnp.concatenate([z1, w1r], axis=-1)], axis=3)
    w1d = w1d.reshape(depth, 2, 2, CMP_STRIDE * g * HEAD_DIM, g * CMP_HID).astype(BF16)
    z2 = jnp.zeros_like(w2)
    w2d = jnp.concatenate([jnp.concatenate([w2, z2], axis=-1), jnp.concatenate([z2, w2], axis=-1)], axis=2)
    pe2 = jnp.concatenate([pe, pe], axis=-1)
    return pe2, w1d, w2d.astype(BF16)


def _compress(u3, pe2, w1d, w2d, layer):
    b, s, _ = u3.shape
    nch = s // CMP_STRIDE
    slab = lambda off: pl.BlockSpec((None, s, LANES), lambda i: (i, 0, off // LANES))
    return pl.pallas_call(
        _compress_kernel,
        grid=(b,),
        in_specs=[slab(U_KCMP), slab(U_VCMP),
                  pl.BlockSpec((None,) + pe2.shape[1:], lambda i: (layer, 0, 0, 0)),
                  pl.BlockSpec((None,) + w1d.shape[1:], lambda i: (layer, 0, 0, 0, 0)),
                  pl.BlockSpec((None,) + w2d.shape[1:], lambda i: (layer, 0, 0, 0))],
        out_specs=[pl.BlockSpec((None, nch, LANES), lambda i: (i, 0, 0)),
                   pl.BlockSpec((None, LANES, nch), lambda i: (i, 0, 0))],
        out_shape=[jax.ShapeDtypeStruct((b, nch, LANES), BF16), jax.ShapeDtypeStruct((b, LANES, nch), BF16)],
        scratch_shapes=[pltpu.VMEM((s, LANES), F32)],
        compiler_params=_cparams(1),
        name="nsa_compress",
    )(u3, u3, pe2, w1d, w2d)


CS_TQ = 128


def _cmp_select_kernel(q_ref, kc_ref, vct_ref, ovt_ref, o_ref, bias_ref, sc_ref):
    tq = CS_TQ
    qi = pl.program_id(1)
    nc = kc_ref.shape[0]
    ns = ovt_ref.shape[0]
    tpos = qi * tq + _lane_iota((1, tq))
    kc = kc_ref[...]
    lane = _lane_iota((1, LANES))
    kst = jnp.concatenate([jnp.where(lane < HEAD_DIM, kc, jnp.zeros_like(kc)),
                           jnp.where(lane >= HEAD_DIM, kc, jnp.zeros_like(kc))], axis=0)
    c_end = _sub_iota((nc, 1)) * CMP_STRIDE + (CMP_LEN - 1)
    cv = c_end <= tpos
    vct = vct_ref[...]
    psum = [jnp.zeros((nc, tq), F32), jnp.zeros((nc, tq), F32)]
    sub = _sub_iota((LANES, 1))
    for i in range(A_HEADS // 2):
        qb = q_ref[:, i * LANES:(i + 1) * LANES]
        st = lax.dot_general(kst, qb, (((1,), (1,)), ((), ())), preferred_element_type=F32)
        ot = []
        for g in range(A_KV_GROUPS):
            s = jnp.where(cv, st[g * nc:(g + 1) * nc], NEG)
            m = jnp.max(s, axis=0, keepdims=True)
            e = jnp.where(cv, jnp.exp2(s - m), 0.0)
            den = jnp.sum(e, axis=0, keepdims=True)
            p = e / jnp.where(den > 0.0, den, 1.0)
            psum[g] = psum[g] + p
            ot.append(jnp.dot(vct, p.astype(BF16), preferred_element_type=F32))
        o_pair = jnp.where(sub < HEAD_DIM, ot[0], ot[1])
        o_ref[:, i * LANES:(i + 1) * LANES] = o_pair.T.astype(o_ref.dtype)
    jdx = _sub_iota((ns, 1))
    cur = tpos // SLC_LEN
    valid = jdx <= cur
    forced = (jdx == 0) | (jdx == cur) | (jdx == cur - 1)
    for g in range(A_KV_GROUPS):
        imp = jnp.dot(ovt_ref[...], psum[g], preferred_element_type=F32, precision=lax.Precision.HIGHEST)
        score = jnp.where(forced, BIG, jnp.where(valid, imp, NEG))
        sc_ref[...] = score
        slab = 8
        parts = [score[v * slab:(v + 1) * slab] for v in range(ns // slab)]
        cnts = [jnp.zeros((slab, tq), F32) for _ in parts]
        for jp in range(ns):
            row = sc_ref[jp:jp + 1, :]
            for v, sv in enumerate(parts):
                if v * slab > jp:
                    ahead = row >= sv
                elif (v + 1) * slab - 1 < jp:
                    ahead = row > sv
                else:
                    tie = jnp.where(jdx[v * slab:(v + 1) * slab] > jp, 1.0, 0.0)
                    cnts[v] = cnts[v] + jnp.where(row == sv, tie, 0.0)
                    ahead = row > sv
                cnts[v] = cnts[v] + jnp.where(ahead, 1.0, 0.0)
        cnt = jnp.concatenate(cnts, axis=0)
        bias = jnp.where(cnt < float(min(SLC_TOPK, ns)), 0.0, NEG)
        bias_ref[g * ns:(g + 1) * ns, :] = bias.astype(bias_ref.dtype)


def _cmp_select(u3, kc, vct, ovt):
    b, s, _ = u3.shape
    nc = kc.shape[1]
    ns = s // SLC_LEN
    assert A_KV_GROUPS * ns == LANES
    return pl.pallas_call(
        _cmp_select_kernel,
        grid=(b, s // CS_TQ),
        in_specs=[pl.BlockSpec((None, CS_TQ, A_WIDTH), lambda i, q: (i, q, U_AQ // A_WIDTH)),
                  pl.BlockSpec((None, nc, LANES), lambda i, q: (i, 0, 0)),
                  pl.BlockSpec((None, LANES, nc), lambda i, q: (i, 0, 0)),
                  pl.BlockSpec((ns, nc), lambda i, q: (0, 0))],
        out_specs=[pl.BlockSpec((None, CS_TQ, A_WIDTH), lambda i, q: (i, q, 0)),
                   pl.BlockSpec((None, LANES, CS_TQ), lambda i, q: (i, 0, q))],
        out_shape=[jax.ShapeDtypeStruct((b, s, A_WIDTH), BF16), jax.ShapeDtypeStruct((b, LANES, s), BF16)],
        scratch_shapes=[pltpu.VMEM((ns, CS_TQ), F32)],
        compiler_params=_cparams(2),
        name="nsa_cmp_select",
    )(u3, kc, vct, ovt)


FLASH_TK = 256
FLASH_CHUNK = 256


def _flash_reset(m_ref, l_ref, acc_ref):
    m_ref[...] = jnp.full(m_ref.shape, NEG, F32)
    l_ref[...] = jnp.zeros(l_ref.shape, F32)
    acc_ref[...] = jnp.zeros(acc_ref.shape, F32)


def _flash_update(st, vt, m_ref, l_ref, acc_ref, idx, mask):
    if mask is not None:
        st = jnp.where(mask, st, NEG)
    m_old = m_ref[idx]
    m_new = jnp.maximum(m_old, jnp.max(st, axis=0, keepdims=True))
    alpha = jnp.exp2(m_old - m_new)
    p = jnp.exp2(st - m_new)
    l_ref[idx] = alpha * l_ref[idx] + jnp.sum(p, axis=0, keepdims=True)
    acc_ref[idx] = alpha * acc_ref[idx] + jnp.dot(vt, p.astype(BF16), preferred_element_type=F32)
    m_ref[idx] = m_new


def _flash_pipeline(n_pairs, n_chain, put, use):
    for c in range(n_chain):
        put(0, c, 0)

    def pair(j, carry):
        t0 = 2 * j
        for c in range(n_chain):
            put(t0 + 1, c, 1)
            use(t0, c, 0)
        for c in range(n_chain):
            put(t0 + 2, c, 0)
            use(t0 + 1, c, 1)
        return carry

    lax.fori_loop(0, n_pairs, pair, 0)


def _transpose_bf16(x):
    return x.astype(F32).T.astype(BF16)


def _merge_rows(lo, hi):
    return jnp.where(_sub_iota((LANES, 1)) < HEAD_DIM, lo, hi)


SLC_TQ = 256


def _slc_kernel(q_ref, biast_ref, k_ref, v_ref, oh_ref, o_ref, vt_ref, qa_ref, st_ref, m_ref, l_ref, acc_ref):
    tq, tk = SLC_TQ, FLASH_TK
    qi = pl.program_id(1)
    nblk = A_HEADS // 2
    sub = _sub_iota((LANES, 1))

    @pl.when(qi == 0)
    def _():
        for kt in range(vt_ref.shape[0]):
            vt_ref[kt] = _transpose_bf16(v_ref[kt * tk:(kt + 1) * tk, :])

    nchain = 2 * nblk
    biast = biast_ref[...]
    zero = jnp.zeros_like(biast)
    mine = (sub < HEAD_DIM, sub >= HEAD_DIM)
    for i in range(nblk):
        qt = _transpose_bf16(q_ref[:, i * LANES:(i + 1) * LANES])
        for h in range(2):
            qa_ref[2 * i + h] = jnp.concatenate([jnp.where(mine[h], qt, zero), jnp.where(mine[h], biast, zero)],
                                                axis=0)
    _flash_reset(m_ref, l_ref, acc_ref)

    def put(kt, c, slot):
        start = pl.multiple_of(kt * tk, tk)
        k = jnp.concatenate([k_ref[pl.ds(start, tk), :], oh_ref[pl.ds(start, tk), :]], axis=1)
        st_ref[slot, c] = jnp.dot(k, qa_ref[c], preferred_element_type=F32)

    def use(kt, c, slot, mask=None):
        _flash_update(st_ref[slot, c], vt_ref[kt], m_ref, l_ref, acc_ref, c, mask)

    n_full = qi
    n_pairs = n_full // 2
    _flash_pipeline(n_pairs, nchain, put, use)
    causal = (n_full * tk + _sub_iota((tk, 1))) <= (qi * tq + _lane_iota((1, tq)))

    @pl.when(n_full == 2 * n_pairs)
    def _():
        for c in range(nchain):
            use(n_full, c, 0, causal)

    @pl.when(n_full != 2 * n_pairs)
    def _():
        for c in range(nchain):
            put(n_full, c, 1)
            use(n_full - 1, c, 0)
        for c in range(nchain):
            use(n_full, c, 1, causal)

    for i in range(nblk):
        ot = _merge_rows(acc_ref[2 * i] / l_ref[2 * i], acc_ref[2 * i + 1] / l_ref[2 * i + 1])
        o_ref[:, i * LANES:(i + 1) * LANES] = ot.T.astype(o_ref.dtype)


def _slc_attention(u3, biast, onehot):
    b, s, _ = u3.shape
    nblk = A_HEADS
    assert SLC_TQ == FLASH_CHUNK == FLASH_TK
    return pl.pallas_call(
        _slc_kernel,
        grid=(b, s // SLC_TQ),
        in_specs=[pl.BlockSpec((None, SLC_TQ, A_WIDTH), lambda i, q: (i, q, U_AQ // A_WIDTH)),
                  pl.BlockSpec((None, LANES, SLC_TQ), lambda i, q: (i, 0, q)),
                  pl.BlockSpec((None, s, LANES), lambda i, q: (i, 0, U_KSLC // LANES)),
                  pl.BlockSpec((None, s, LANES), lambda i, q: (i, 0, U_VSLC // LANES)),
                  pl.BlockSpec((s, LANES), lambda i, q: (0, 0))],
        out_specs=pl.BlockSpec((None, SLC_TQ, A_WIDTH), lambda i, q: (i, q, 0)),
        out_shape=jax.ShapeDtypeStruct((b, s, A_WIDTH), BF16),
        scratch_shapes=[pltpu.VMEM((s // FLASH_TK, LANES, FLASH_TK), BF16),
                        pltpu.VMEM((nblk, 2 * LANES, FLASH_CHUNK), BF16),
                        pltpu.VMEM((2, nblk, FLASH_TK, FLASH_CHUNK), F32),
                        pltpu.VMEM((nblk, 1, FLASH_CHUNK), F32), pltpu.VMEM((nblk, 1, FLASH_CHUNK), F32),
                        pltpu.VMEM((nblk, LANES, FLASH_CHUNK), F32)],
        compiler_params=_cparams(2),
        name="nsa_selected",
    )(u3, biast, u3, u3, onehot)


BAND_T = 128
BAND_ROWS = 512
BAND_LOOKAHEAD = 2


def _banded_kernel(*refs, nprev, max_dist, shared_kv, with_lse):
    t = BAND_T
    q_ref, k_ref, v_ref, o_ref = refs[:4]
    lse_ref = refs[4] if with_lse else None
    qi = pl.program_id(1)
    nsub = q_ref.shape[0] // t
    nblk = q_ref.shape[1] // LANES
    nk = (nprev + 1) * t
    col = _lane_iota((1, 2 * t))
    base = jnp.where(col >= t, col - t, col) - _sub_iota((nk, 1))
    sub = _sub_iota((LANES, 1))

    def window(s):
        return pl.multiple_of(jnp.maximum(qi * nsub + s - nprev, 0) * t, t)

    def band_cap(dist):
        rel = base + dist
        return jnp.where((rel >= 0) & (rel <= max_dist), BIG, NEG)

    assert nsub >= nprev
    regular = band_cap(nprev * t)
    clamped = [band_cap(s * t) for s in range(nprev)]

    def cap(s):
        return jnp.where(qi == 0, clamped[s], regular) if s < nprev else regular

    def scores(s, i):
        start = window(s)
        cols = slice(None) if shared_kv else slice(i * LANES, (i + 1) * LANES)
        k = k_ref[pl.ds(start, nk), cols]
        qt = _transpose_bf16(q_ref[s * t:(s + 1) * t, i * LANES:(i + 1) * LANES])
        zero = jnp.zeros_like(qt)
        qs = jnp.concatenate([jnp.where(sub < HEAD_DIM, qt, zero), jnp.where(sub >= HEAD_DIM, qt, zero)], axis=1)
        return jnp.dot(k, qs, preferred_element_type=F32)

    vt_cache = {}

    def values_t(s, i):
        key = s if shared_kv else (s, i)
        if key not in vt_cache:
            start = window(s)
            cols = slice(None) if shared_kv else slice(i * LANES, (i + 1) * LANES)
            vt_cache[key] = jnp.concatenate(
                [_transpose_bf16(v_ref[pl.ds(start + j * t, t), cols]) for j in range(nprev + 1)], axis=1)
        return vt_cache[key]

    def finish(s, i, st):
        vt = values_t(s, i)
        st = jnp.minimum(st, cap(s))
        m = jnp.max(st, axis=0, keepdims=True)
        p = jnp.exp2(st - m)
        l = jnp.sum(p, axis=0, keepdims=True)
        ot = jnp.dot(vt, p.astype(BF16), preferred_element_type=F32) / l
        rows, sl = slice(s * t, (s + 1) * t), slice(i * LANES, (i + 1) * LANES)
        o_ref[rows, sl] = _merge_rows(ot[:, 0:t], ot[:, t:2 * t]).T.astype(o_ref.dtype)
        if with_lse:
            lse = m + jnp.log2(l)
            lse_ref[rows, sl] = _merge_rows(jnp.broadcast_to(lse[:, 0:t], (LANES, t)),
                                            jnp.broadcast_to(lse[:, t:2 * t], (LANES, t))).T

    items = [(s, i) for s in range(nsub) for i in range(nblk)]
    pending = [scores(*it) for it in items[:BAND_LOOKAHEAD]]
    for j, it in enumerate(items):
        if j + BAND_LOOKAHEAD < len(items):
            pending.append(scores(*items[j + BAND_LOOKAHEAD]))
        finish(*it, pending[j])


def _banded(src, *, nseq, seqlen, q_off, k_off, v_off, kv_width, max_dist, with_lse):
    t = BAND_T
    rows = min(BAND_ROWS, seqlen)
    nq = seqlen // rows
    nprev = -(-max_dist // t)
    shared = kv_width == LANES
    assert (nprev + 1) * t <= seqlen and seqlen % rows == 0 and rows % t == 0
    qo = pl.BlockSpec((rows, A_WIDTH), lambda n, q: (n * nq + q, q_off // A_WIDTH))
    out = pl.BlockSpec((rows, A_WIDTH), lambda n, q: (n * nq + q, 0))
    in_specs = [qo,
                pl.BlockSpec((seqlen, kv_width), lambda n, q: (n, k_off // kv_width)),
                pl.BlockSpec((seqlen, kv_width), lambda n, q: (n, v_off // kv_width))]
    out_specs = [out]
    out_shape = [jax.ShapeDtypeStruct((nseq * seqlen, A_WIDTH), BF16)]
    if with_lse:
        out_specs.append(out)
        out_shape.append(jax.ShapeDtypeStruct((nseq * seqlen, A_WIDTH), F32))
    return pl.pallas_call(
        functools.partial(_banded_kernel, nprev=nprev, max_dist=max_dist, shared_kv=shared, with_lse=with_lse),
        grid=(nseq, nq),
        in_specs=in_specs,
        out_specs=out_specs,
        out_shape=out_shape,
        compiler_params=_cparams(2),
        name="banded_attention",
    )(src, src, src)


MLA_TM = 512


def _rms(x, g):
    return x * lax.rsqrt(jnp.mean(x * x, axis=-1, keepdims=True) + EPS) * g


def _mla_prep_kernel(cq_ref, ckv_ref, kr_ref, c_ref, s1_ref, s2_ref, gq_ref, gkv_ref, wq_ref, wk_ref, wv_ref,
                     q_out, k_out, vt_out):
    half = QK_ROPE // 2
    tk = FLASH_TK
    cm, s1, s2 = c_ref[...], s1_ref[...], s2_ref[...]

    def rope(xc):
        return xc * cm + pltpu.roll(xc, half, 1) * s1 + pltpu.roll(xc, LANES - half, 1) * s2

    qn = _rms(cq_ref[...].astype(F32), gq_ref[...]).astype(BF16)
    q = jnp.dot(qn, wq_ref[...], preferred_element_type=F32)
    kvn = _rms(ckv_ref[...].astype(F32), gkv_ref[...]).astype(BF16)
    kk = jnp.dot(kvn, wk_ref[...], preferred_element_type=F32)
    kr = rope(kr_ref[...].astype(F32))
    for h in range(B_HEADS):
        sl = slice(h * LANES, (h + 1) * LANES)
        q_out[:, sl] = (rope(q[:, sl]) * MLA_SCALE).astype(q_out.dtype)
        k_out[:, sl] = (kk[:, sl] + kr).astype(k_out.dtype)
    v = jnp.dot(kvn, wv_ref[...], preferred_element_type=F32)
    for j in range(v.shape[0] // tk):
        for hp in range(B_HEADS // 2):
            vt_out[j, hp] = v[j * tk:(j + 1) * tk, hp * LANES:(hp + 1) * LANES].T.astype(vt_out.dtype)


def _mla_prep(u, tabs, g_q, g_kv, wq, wk, wv, layer):
    t = u.shape[0]
    tm = min(MLA_TM, t)
    row = lambda i: (i, 0)
    wide = B_HEADS * LANES
    assert U_BCQ % Q_LORA == 0 and tm % FLASH_TK == 0
    return pl.pallas_call(
        _mla_prep_kernel,
        grid=(t // tm,),
        in_specs=[pl.BlockSpec((tm, Q_LORA), lambda i: (i, U_BCQ // Q_LORA)),
                  pl.BlockSpec((tm, LANES), lambda i: (i, U_BCKV // LANES)),
                  pl.BlockSpec((tm, LANES), lambda i: (i, U_BKR // LANES)),
                  pl.BlockSpec((tm, LANES), row), pl.BlockSpec((tm, LANES), row), pl.BlockSpec((tm, LANES), row),
                  pl.BlockSpec((None, 1, Q_LORA), lambda i: (layer, 0, 0)),
                  pl.BlockSpec((None, 1, KV_LORA), lambda i: (layer, 0, 0)),
                  pl.BlockSpec((None, Q_LORA, wide), lambda i: (layer, 0, 0)),
                  pl.BlockSpec((None, KV_LORA, wide), lambda i: (layer, 0, 0)),
                  pl.BlockSpec((None, KV_LORA, B_WIDTH), lambda i: (layer, 0, 0))],
        out_specs=[pl.BlockSpec((tm, wide), row), pl.BlockSpec((tm, wide), row),
                   pl.BlockSpec((tm // FLASH_TK, B_HEADS // 2, LANES, FLASH_TK), lambda i: (i, 0, 0, 0))],
        out_shape=[jax.ShapeDtypeStruct((t, wide), BF16), jax.ShapeDtypeStruct((t, wide), BF16),
                   jax.ShapeDtypeStruct((t // FLASH_TK, B_HEADS // 2, LANES, FLASH_TK), BF16)],
        compiler_params=_cparams(1),
        name="mla_prep",
    )(u, u, u, *tabs, g_q, g_kv, wq, wk, wv)


MLA_TQ = 1024


def _mla_flash_kernel(q_ref, k_ref, vt_ref, o_ref, qt_ref, st_ref, m_ref, l_ref, acc_ref):
    tq, tk, cw = MLA_TQ, FLASH_TK, FLASH_CHUNK
    nhalf = tq // cw
    qi = pl.program_id(2)
    for h in range(2):
        qt_ref[h] = _transpose_bf16(q_ref[:, h * LANES:(h + 1) * LANES])
    _flash_reset(m_ref, l_ref, acc_ref)

    def put(kt, c, slot):
        h, half = divmod(c, nhalf)
        start = pl.multiple_of(kt * tk, tk)
        k = k_ref[pl.ds(start, tk), h * LANES:(h + 1) * LANES]
        st_ref[slot, c] = jnp.dot(k, qt_ref[h, :, half * cw:(half + 1) * cw], preferred_element_type=F32)

    def use(kt, c, slot, masked=False):
        mask = None
        if masked:
            qpos = qi * tq + (c % nhalf) * cw + _lane_iota((1, cw))
            mask = (kt * tk + _sub_iota((tk, 1))) <= qpos
        _flash_update(st_ref[slot, c], vt_ref[kt], m_ref, l_ref, acc_ref, c, mask)

    assert tk == cw and nhalf % 2 == 0
    n_full = nhalf * qi
    _flash_pipeline(n_full // 2, 2 * nhalf, put, use)
    live = lambda d: [c for c in range(2 * nhalf) if c % nhalf >= d]
    for d in range(nhalf):
        if d + 1 < nhalf:
            for c in live(d + 1):
                put(n_full + d + 1, c, (d + 1) % 2)
        for c in live(d):
            use(n_full + d, c, d % 2, masked=(c % nhalf == d))
    ots = []
    for h in range(2):
        ots.append(jnp.concatenate([acc_ref[h * nhalf + half] / l_ref[h * nhalf + half] for half in range(nhalf)],
                                   axis=1))
    o_ref[...] = _merge_rows(ots[0], ots[1]).T.astype(o_ref.dtype)


def _mla_flash(q, k, vt, batch, seq):
    assert FLASH_TK == FLASH_CHUNK
    q3 = q.reshape(batch, seq, -1)
    k3 = k.reshape(batch, seq, -1)
    vt5 = vt.reshape(batch, seq // FLASH_TK, B_HEADS // 2, LANES, FLASH_TK)
    nchain = 2 * (MLA_TQ // FLASH_CHUNK)
    o = pl.pallas_call(
        _mla_flash_kernel,
        grid=(batch, B_HEADS // 2, seq // MLA_TQ),
        in_specs=[pl.BlockSpec((None, MLA_TQ, 2 * LANES), lambda b, h, i: (b, i, h)),
                  pl.BlockSpec((None, seq, 2 * LANES), lambda b, h, i: (b, 0, h)),
                  pl.BlockSpec((None, seq // FLASH_TK, None, LANES, FLASH_TK), lambda b, h, i: (b, 0, h, 0, 0))],
        out_specs=pl.BlockSpec((None, MLA_TQ, LANES), lambda b, h, i: (b, i, h)),
        out_shape=jax.ShapeDtypeStruct((batch, seq, B_WIDTH), BF16),
        scratch_shapes=[pltpu.VMEM((2, LANES, MLA_TQ), BF16),
                        pltpu.VMEM((2, nchain, FLASH_TK, FLASH_CHUNK), F32),
                        pltpu.VMEM((nchain, 1, FLASH_CHUNK), F32), pltpu.VMEM((nchain, 1, FLASH_CHUNK), F32),
                        pltpu.VMEM((nchain, LANES, FLASH_CHUNK), F32)],
        compiler_params=_cparams(3),
        name="mla_flash",
    )(q3, k3, vt5)
    return o.reshape(batch * seq, B_WIDTH)


OUT_TM = 256


def _out_kernel(x_ref, p_ref, ocmp_ref, oslc_ref, owin_ref, gate_ref, e_ref, az_ref, ob_ref, bz_ref,
                oc0_ref, oc1_ref, oc2_ref, l0_ref, l1_ref, l2_ref, cz_ref, mg0_ref, mg1_ref, mg2_ref,
                wa_ref, wb_ref, wc_ref, wout_ref, wplg_ref, wple_ref, gpost_ref, o_ref, *tok_refs):
    tok = iter(tok_refs)

    def sig(z):
        return 0.5 * jnp.tanh(0.5 * z) + 0.5

    def f(r):
        if len(r.shape) == 2:
            return r[...].astype(F32)
        dil, n, width = r.shape
        buf = next(tok)
        for res in range(dil):
            blk = r[res].astype(F32)
            for c in range(width // LANES):
                buf[c, pl.ds(res, n, stride=dil), :] = blk[:, c * LANES:(c + 1) * LANES]
        return jnp.concatenate([buf[c] for c in range(width // LANES)], axis=1)

    def silu(z):
        return z * sig(z)

    def mm(a, w_ref):
        return jnp.dot(a.astype(BF16), w_ref[...], preferred_element_type=F32)

    g = sig(gate_ref[...].astype(F32))
    g_hi = g.astype(BF16)
    g_lo = (g - g_hi.astype(F32)).astype(BF16)
    gs = (jnp.dot(g_hi, e_ref[...], preferred_element_type=F32)
          + jnp.dot(g_lo, e_ref[...], preferred_element_type=F32))
    o_a = (gs[:, 0:A_WIDTH] * f(ocmp_ref) + gs[:, A_WIDTH:2 * A_WIDTH] * f(oslc_ref)
           + gs[:, 2 * A_WIDTH:3 * A_WIDTH] * f(owin_ref))
    y_a = mm(o_a * silu(f(az_ref)), wa_ref)
    y_b = mm(f(ob_ref) * silu(f(bz_ref)), wb_ref)
    l0, l1, l2 = f(l0_ref), f(l1_ref), f(l2_ref)
    mx = jnp.maximum(jnp.maximum(l0, l1), l2)
    e0, e1, e2 = jnp.exp2(l0 - mx), jnp.exp2(l1 - mx), jnp.exp2(l2 - mx)
    o_c = (e0 * f(oc0_ref) + e1 * f(oc1_ref) + e2 * f(oc2_ref)) / (e0 + e1 + e2)
    y_c = mm(o_c * silu(f(cz_ref)), wc_ref)
    mix = sig(f(mg0_ref)) * y_a + sig(f(mg1_ref)) * y_b + sig(f(mg2_ref)) * y_c
    y = mm(mix, wout_ref)
    x1 = x_ref[...] + _rms(y, gpost_ref[...])
    o_ref[...] = x1 + sig(mm(x1, wplg_ref)) * mm(p_ref[...], wple_ref)


def _out_layer(x2, p, layer, o_cmp, o_slc, o_win, u, expand, o_b, o_c, lse_c, w, batch, dils):
    t = x2.shape[0]
    seq = t // batch
    tm = min(OUT_TM, seq)
    nper = seq // tm
    row = lambda i: (i, 0)
    ucol = lambda off, width: pl.BlockSpec((tm, width), lambda i: (i, off // width))
    half = pl.BlockSpec((tm, A_WIDTH), row)
    wspec = lambda r, c: pl.BlockSpec((None, r, c), lambda i: (layer, 0, 0))

    def group_spec(dil):
        if dil == 1:
            return half
        return pl.BlockSpec((None, dil, tm // dil, C_WIDTH), lambda i: (i // nper, 0, i % nper, 0))

    def group_view(a, dil):
        return a if dil == 1 else a.reshape(batch, dil, seq // dil, C_WIDTH)

    gspecs = [group_spec(d) for d in dils]
    o_c = [group_view(a, d) for a, d in zip(o_c, dils)]
    lse_c = [group_view(a, d) for a, d in zip(lse_c, dils)]
    n_tok = 2 * sum(1 for d in dils if d > 1)
    in_specs = [pl.BlockSpec((tm, D_MODEL), row),
                pl.BlockSpec((None, tm, PLE_DIM), lambda i: (layer, i, 0)),
                half, half, half,
                ucol(U_GATE, LANES), pl.BlockSpec((LANES, 3 * A_WIDTH), lambda i: (0, 0)),
                ucol(U_AZ, A_WIDTH), half, ucol(U_BZ, B_WIDTH),
                *gspecs, *gspecs, ucol(U_CZ, C_WIDTH),
                ucol(U_MG, D_MODEL), ucol(U_MG + D_MODEL, D_MODEL), ucol(U_MG + 2 * D_MODEL, D_MODEL),
                wspec(A_WIDTH, D_MODEL), wspec(B_WIDTH, D_MODEL), wspec(C_WIDTH, D_MODEL),
                wspec(D_MODEL, D_MODEL), wspec(D_MODEL, D_MODEL), wspec(PLE_DIM, D_MODEL),
                wspec(1, D_MODEL)]
    return pl.pallas_call(
        _out_kernel,
        grid=(t // tm,),
        in_specs=in_specs,
        out_specs=pl.BlockSpec((tm, D_MODEL), row),
        out_shape=jax.ShapeDtypeStruct((t, D_MODEL), F32),
        scratch_shapes=[pltpu.VMEM((C_WIDTH // LANES, tm, LANES), F32)] * n_tok,
        compiler_params=_cparams(1),
        name="out_layer",
    )(x2, p, o_cmp, o_slc, o_win, u, expand, u, o_b, u, *o_c, *lse_c, u, u, u, u,
      w["w_a"], w["w_b"], w["w_c"], w["w_out"], w["w_plg"], w["w_ple"], w["g_post"])


def _overlap_t(seq):
    nc = seq // CMP_STRIDE
    ns = seq // SLC_LEN
    cs = np.arange(nc)[:, None] * CMP_STRIDE
    js = np.arange(ns)[None, :] * SLC_LEN
    ov = np.clip(np.minimum(cs + CMP_LEN, js + SLC_LEN) - np.maximum(cs, js), 0, None).astype(np.float32) / CMP_LEN
    return jnp.asarray(ov.T)


def _block_onehot(seq):
    blk = np.arange(seq)[:, None] // SLC_LEN
    return jnp.asarray((np.arange(LANES)[None, :] % HEAD_DIM == blk).astype(np.float32), dtype=BF16)


def _gate_expand():
    e = np.zeros((LANES, 3 * A_WIDTH), np.float32)
    for br in range(3):
        for i in range(A_HEADS // 2):
            for g in range(A_KV_GROUPS):
                h = g * (A_HEADS // 2) + i
                c0 = br * A_WIDTH + i * LANES + g * HEAD_DIM
                e[h * 3 + br, c0:c0 + HEAD_DIM] = 1.0
    return jnp.asarray(e, dtype=BF16)


def kernel(x, p, positions, g_pre, g_post, w_in, nsa_cmp_pe, nsa_cmp_w1, nsa_cmp_w2, w_a, mla_g_q, mla_w_uq,
           mla_g_kv, mla_w_ukv, w_b, w_c, w_out, w_ple, w_plg):
    batch, seq, _ = x.shape
    depth = w_in.shape[0]
    t = batch * seq
    nch = seq // CMP_STRIDE

    pos = positions.astype(F32)
    part_tabs = _rope_tables(pos.reshape(-1), ROT_DIM, HEAD_DIM, 0)
    mla_tabs = _rope_tables(pos.reshape(-1), QK_ROPE, LANES, QK_NOPE)
    dils = [dil for _, dil in C_PAIRS]
    assert dils[0] == 1
    perm_tabs = {dil: _rope_tables(_residue_major_positions(pos, dil), ROT_DIM, HEAD_DIM, 0) for dil in dils[1:]}
    w_main, w_groups = _regroup_w_in(w_in)
    w_main = _tile_cols(w_main, IN_TN)
    w_groups = [_tile_cols(w, DG_COLS) for w in w_groups]
    g_pre3 = g_pre.reshape(depth, 1, D_MODEL)
    pe2, w1d, w2d = _compress_weights(nsa_cmp_pe, nsa_cmp_w1, nsa_cmp_w2)
    wq = mla_w_uq.reshape(depth, Q_LORA, B_HEADS, QK_NOPE + QK_ROPE)
    wq = jnp.pad(wq, ((0, 0), (0, 0), (0, 0), (0, LANES - QK_NOPE - QK_ROPE))).reshape(depth, Q_LORA, -1).astype(BF16)
    wkv = mla_w_ukv.reshape(depth, KV_LORA, B_HEADS, QK_NOPE + V_DIM)
    wk = jnp.pad(wkv[..., :QK_NOPE], ((0, 0), (0, 0), (0, 0), (0, LANES - QK_NOPE))).reshape(depth, KV_LORA, -1)
    wk = wk.astype(BF16)
    wv = wkv[..., QK_NOPE:].reshape(depth, KV_LORA, B_WIDTH).astype(BF16)
    gq3 = mla_g_q.reshape(depth, 1, Q_LORA)
    gkv3 = mla_g_kv.reshape(depth, 1, KV_LORA)
    w_a_p = w_a.reshape(depth, 2, 4, HEAD_DIM, D_MODEL).swapaxes(1, 2).reshape(depth, A_WIDTH, D_MODEL)
    wts = {"w_a": w_a_p.astype(BF16), "w_b": w_b.astype(BF16), "w_c": w_c.astype(BF16),
           "w_out": w_out.astype(BF16), "w_plg": w_plg.astype(BF16), "w_ple": w_ple.astype(BF16),
           "g_post": g_post.reshape(depth, 1, D_MODEL)}
    p3 = p.reshape(depth, t, PLE_DIM)
    ovt = _overlap_t(seq)
    onehot = _block_onehot(seq)
    expand = _gate_expand()

    x2 = x.reshape(t, D_MODEL)
    for layer in range(depth):
        h, h_res = _prenorm(x2, g_pre3, layer, batch, dils[1:])
        u = _inproj(h, w_main, layer, part_tabs, ROPE_COLS)
        u3 = u.reshape(batch, seq, DP)
        kc, vct = _compress(u3, pe2, w1d, w2d, layer)
        o_cmp, biast = _cmp_select(u3, kc, vct, ovt)
        o_slc = _slc_attention(u3, biast, onehot)
        (o_win,) = _banded(u, nseq=batch, seqlen=seq, q_off=U_AQ, k_off=U_KWIN, v_off=U_VWIN,
                           kv_width=LANES, max_dist=WIN - 1, with_lse=False)
        q_b, k_b, vt_b = _mla_prep(u, mla_tabs, gq3, gkv3, wq, wk, wv, layer)
        o_b = _mla_flash(q_b, k_b, vt_b, batch, seq)
        o_c, lse_c = [], []
        for gi, (window, dil) in enumerate(C_PAIRS):
            if dil == 1:
                o_g, lse_g = _banded(u, nseq=batch, seqlen=seq, q_off=U_CQ0, k_off=U_CK0, v_off=U_CV0,
                                     kv_width=C_WIDTH, max_dist=window, with_lse=True)
            else:
                u_g = _inproj(h_res[gi - 1], w_groups[gi - 1], layer, perm_tabs[dil], DG_ROPE_COLS)
                o_g, lse_g = _banded(u_g, nseq=batch * dil, seqlen=seq // dil, q_off=0, k_off=C_WIDTH,
                                     v_off=2 * C_WIDTH, kv_width=C_WIDTH, max_dist=window // dil, with_lse=True)
            o_c.append(o_g)
            lse_c.append(lse_g)
        x2 = _out_layer(x2, p3, layer, o_cmp.reshape(t, A_WIDTH), o_slc.reshape(t, A_WIDTH), o_win, u, expand,
                        o_b, o_c, lse_c, wts, batch, dils)
    return x2.reshape(batch, seq, D_MODEL)
```

```python
import functools

import numpy as np
import jax
import jax.numpy as jnp
from jax import lax
from jax.experimental import pallas as pl
from jax.experimental.pallas import tpu as pltpu

F32 = jnp.float32
BF16 = jnp.bfloat16

D_MODEL = 1024
PLE_DIM = 256
ROPE_THETA = 500000.0
HEAD_DIM = 64
ROT_DIM = HEAD_DIM // 4
EPS = 1e-6
NEG = -1e30
BIG = 1e30
LANES = 128

A_HEADS = 8
A_KV_GROUPS = 2
A_WIDTH = A_HEADS * HEAD_DIM
CMP_LEN = 32
CMP_STRIDE = 16
CMP_HID = 256
SLC_LEN = 64
SLC_TOPK = 16
WIN = 512

B_HEADS = 8
Q_LORA = 384
KV_LORA = 128
QK_NOPE = 64
QK_ROPE = 32
V_DIM = 64
B_WIDTH = B_HEADS * V_DIM

C_PAIRS = ((128, 1), (512, 4), (2048, 16))
C_GROUPS = len(C_PAIRS)
C_HEADS = 8
C_WIDTH = C_HEADS * HEAD_DIM

IN_SIZES = (A_WIDTH, 6 * A_KV_GROUPS * HEAD_DIM, 3 * A_HEADS, A_WIDTH, Q_LORA, KV_LORA, QK_ROPE,
            B_WIDTH, C_GROUPS * 3 * C_WIDTH, C_WIDTH, 3 * D_MODEL)
IN_OFF = tuple(int(v) for v in np.cumsum((0,) + IN_SIZES))
(OFF_AQ, OFF_AKV, OFF_AG, OFF_AZ, OFF_BCQ, OFF_BCKV, OFF_BKR, OFF_BZ, OFF_CQKV, OFF_CZ, OFF_MG) = IN_OFF[:-1]

U_AQ = 0
U_CQ0 = 512
U_CK0 = 1024
U_KCMP = 1536
U_KSLC = 1664
U_KWIN = 1792
ROPE_COLS = 1920
U_VCMP = 1920
U_AZ = 2048
U_BZ = 2560
U_CV0 = 3072
U_CZ = 3584
U_MG = 4096
U_VSLC = 7168
U_VWIN = 7296
U_GATE = 7424
U_BCKV = 7552
U_BCQ = 7680
U_BKR = 8064
DP = 8192
DG_COLS = 3 * C_WIDTH
DG_ROPE_COLS = 2 * C_WIDTH

LOG2E = float(np.log2(np.e))
Q_SCALE = float(HEAD_DIM ** -0.5) * LOG2E
MLA_SCALE = float((QK_NOPE + QK_ROPE) ** -0.5) * LOG2E
VMEM_LIMIT = 48 * 1024 * 1024


def _cparams(n_axes):
    return pltpu.CompilerParams(dimension_semantics=("arbitrary",) * n_axes, vmem_limit_bytes=VMEM_LIMIT)


def _lane_iota(shape):
    return lax.broadcasted_iota(jnp.int32, shape, len(shape) - 1)


def _sub_iota(shape):
    return lax.broadcasted_iota(jnp.int32, shape, len(shape) - 2)


def _pair_cols(w):
    lead = w.shape[:-1]
    return w.reshape(lead + (2, 4, HEAD_DIM)).swapaxes(-3, -2).reshape(lead + (A_WIDTH,))


def _regroup_w_in(w_in):
    def col(off, n):
        return w_in[..., off:off + n]
    z = lambda n: jnp.zeros(w_in.shape[:-1] + (n,), w_in.dtype)
    akv = lambda which: col(OFF_AKV + which * 128, 128)
    cq = lambda gi, t: col(OFF_CQKV + gi * 3 * C_WIDTH + t * C_WIDTH, C_WIDTH)
    pieces = [_pair_cols(col(OFF_AQ, A_WIDTH)) * Q_SCALE, cq(0, 0) * Q_SCALE, cq(0, 1)]
    pieces += [akv(0), akv(2), akv(4)]
    pieces += [akv(1)]
    pieces += [_pair_cols(col(OFF_AZ, A_WIDTH)), col(OFF_BZ, B_WIDTH), cq(0, 2)]
    pieces += [col(OFF_CZ, C_WIDTH), col(OFF_MG, 3 * D_MODEL)]
    pieces += [akv(3), akv(5)]
    pieces += [col(OFF_AG, 3 * A_HEADS), z(128 - 3 * A_HEADS)]
    pieces += [col(OFF_BCKV, KV_LORA), col(OFF_BCQ, Q_LORA)]
    pieces += [z(64), col(OFF_BKR, QK_ROPE), z(32)]
    main = jnp.concatenate(pieces, axis=-1).astype(BF16)
    assert main.shape[-1] == DP
    groups = [jnp.concatenate([cq(gi, 0) * Q_SCALE, cq(gi, 1), cq(gi, 2)], axis=-1).astype(BF16)
              for gi in range(1, C_GROUPS)]
    return main, groups


def _rope_tables(pos, dim, period, offset):
    half = dim // 2
    ch = np.arange(LANES) % period - offset
    in_span = (ch >= 0) & (ch < dim)
    freq = np.where(in_span, ch % half, 0).astype(np.float32)
    inv = jnp.where(jnp.asarray(in_span), ROPE_THETA ** (-2.0 * jnp.asarray(freq) / dim), 0.0)
    ang = pos[:, None] * inv[None, :]
    cos, sin = jnp.cos(ang), jnp.sin(ang)
    first = jnp.asarray(in_span & (ch < half))[None, :]
    second = jnp.asarray(in_span & (ch >= half))[None, :]
    return cos, jnp.where(second, sin, 0.0), jnp.where(first, -sin, 0.0)


def _residue_major_positions(positions, dil):
    b, s = positions.shape
    return positions.reshape(b, s // dil, dil).swapaxes(1, 2).reshape(-1)


NORM_TM = 512
IN_TM = 512
IN_TN = 2048


def _norm_kernel(x_ref, g_ref, o_ref, *rest, dils):
    o_dil, y_ref = rest[:len(dils)], rest[len(dils)]
    x = x_ref[...]
    ms = jnp.mean(x * x, axis=-1, keepdims=True)
    y = x * lax.rsqrt(ms + EPS) * g_ref[...]
    o_ref[...] = y.astype(o_ref.dtype)
    tm = x.shape[0]
    nlane = y_ref.shape[0]
    for c in range(nlane):
        y_ref[c] = y[:, c * LANES:(c + 1) * LANES]
    for o_d, dil in zip(o_dil, dils):
        for r in range(dil):
            for c in range(nlane):
                o_d[r, :, c * LANES:(c + 1) * LANES] = (
                    y_ref[c, pl.ds(r, tm // dil, stride=dil), :].astype(o_d.dtype))


def _prenorm(x2, g_pre, layer, batch, dils):
    t = x2.shape[0]
    seq = t // batch
    tm = min(NORM_TM, seq)
    nper = seq // tm
    out_specs = [pl.BlockSpec((tm, D_MODEL), lambda b, i: (b * nper + i, 0))]
    out_shape = [jax.ShapeDtypeStruct((t, D_MODEL), BF16)]
    for dil in dils:
        out_specs.append(pl.BlockSpec((None, dil, tm // dil, D_MODEL), lambda b, i: (b, 0, i, 0)))
        out_shape.append(jax.ShapeDtypeStruct((batch, dil, seq // dil, D_MODEL), BF16))
    outs = pl.pallas_call(
        functools.partial(_norm_kernel, dils=tuple(dils)),
        grid=(batch, nper),
        in_specs=[pl.BlockSpec((tm, D_MODEL), lambda b, i: (b * nper + i, 0)),
                  pl.BlockSpec((None, 1, D_MODEL), lambda b, i: (layer, 0, 0))],
        out_specs=out_specs,
        out_shape=out_shape,
        scratch_shapes=[pltpu.VMEM((D_MODEL // LANES, tm, LANES), F32)],
        compiler_params=_cparams(2),
        name="prenorm",
    )(x2, g_pre)
    return outs[0], [o.reshape(t, D_MODEL) for o in outs[1:]]


def _inproj_kernel(h_ref, w_ref, c_ref, s1_ref, s2_ref, o_ref, *, rope_cols):
    j = pl.program_id(0)
    acc = jnp.dot(h_ref[...], w_ref[...], preferred_element_type=F32)
    nsub = w_ref.shape[1] // LANES
    full_tiles, rem = divmod(rope_cols // LANES, nsub)

    def emit(n_rope):
        for c in range(nsub):
            xc = acc[:, c * LANES:(c + 1) * LANES]
            if c < n_rope:
                xc = (xc * c_ref[...] + pltpu.roll(xc, ROT_DIM // 2, 1) * s1_ref[...]
                      + pltpu.roll(xc, LANES - ROT_DIM // 2, 1) * s2_ref[...])
            o_ref[:, c * LANES:(c + 1) * LANES] = xc.astype(o_ref.dtype)

    pl.when(j < full_tiles)(lambda: emit(nsub))
    pl.when(j == full_tiles)(lambda: emit(rem))
    pl.when(j > full_tiles)(lambda: emit(0))


def _tile_cols(w, tn):
    depth, d, n = w.shape
    return w.reshape(depth, d, n // tn, tn).swapaxes(1, 2)


def _inproj(h, w_tiled, layer, tabs, rope_cols):
    t = h.shape[0]
    _, ntile, _, tn = w_tiled.shape
    tm = min(IN_TM, t)
    rope_tiles = -(-rope_cols // tn)
    row = lambda j, i: (i, 0)
    tab = lambda j, i: (jnp.where(j < rope_tiles, i, 0), 0)
    return pl.pallas_call(
        functools.partial(_inproj_kernel, rope_cols=rope_cols),
        grid=(ntile, t // tm),
        in_specs=[pl.BlockSpec((tm, D_MODEL), row),
                  pl.BlockSpec((None, None, D_MODEL, tn), lambda j, i: (layer, j, 0, 0)),
                  pl.BlockSpec((tm, LANES), tab), pl.BlockSpec((tm, LANES), tab), pl.BlockSpec((tm, LANES), tab)],
        out_specs=pl.BlockSpec((tm, tn), lambda j, i: (i, j)),
        out_shape=jax.ShapeDtypeStruct((t, ntile * tn), BF16),
        compiler_params=_cparams(2),
        name="inproj",
    )(h, w_tiled, *tabs)


def _compress_kernel(k_ref, v_ref, pe_ref, w1_ref, w2_ref, kc_ref, vct_ref, x_ref):
    nch = x_ref.shape[0] // CMP_STRIDE
    for which, src in enumerate((k_ref, v_ref)):
        x_ref[...] = src[...].astype(F32)
        rows = [x_ref[pl.ds(l, nch, stride=CMP_STRIDE), :] for l in range(CMP_STRIDE)]
        halves = []
        for part in range(CMP_LEN // CMP_STRIDE):
            xs = [(rows[l] + pe_ref[which, part * CMP_STRIDE + l:part * CMP_STRIDE + l + 1, :]).astype(BF16)
                  for l in range(CMP_STRIDE)]
            halves.append(jnp.dot(jnp.concatenate(xs, axis=1), w1_ref[which, part], preferred_element_type=F32))
        hid = halves[0] + pltpu.roll(halves[1], nch - 1, 0)
        act = hid * jax.nn.sigmoid(hid)
        tok = jnp.dot(act.astype(BF16), w2_ref[which], preferred_element_type=F32)
        if which == 0:
            kc_ref[...] = tok.astype(kc_ref.dtype)
        else:
            vct_ref[...] = tok.T.astype(vct_ref.dtype)


def _compress_weights(pe, w1, w2):
    depth = w1.shape[0]
    g = A_KV_GROUPS
    assert CMP_LEN == 2 * CMP_STRIDE and g == 2
    w1r = w1.reshape(depth, 2, CMP_LEN, HEAD_DIM, CMP_HID)
    z1 = jnp.zeros_like(w1r)
    w1d = jnp.stack([jnp.concatenate([w1r, z1], axis=-1), jnp.concatenate([z1, w1r], axis=-1)], axis=3)
    w1d = w1d.reshape(depth, 2, 2, CMP_STRIDE * g * HEAD_DIM, g * CMP_HID).astype(BF16)
    z2 = jnp.zeros_like(w2)
    w2d = jnp.concatenate([jnp.concatenate([w2, z2], axis=-1), jnp.concatenate([z2, w2], axis=-1)], axis=2)
    pe2 = jnp.concatenate([pe, pe], axis=-1)
    return pe2, w1d, w2d.astype(BF16)


def _compress(u3, pe2, w1d, w2d, layer):
    b, s, _ = u3.shape
    nch = s // CMP_STRIDE
    slab = lambda off: pl.BlockSpec((None, s, LANES), lambda i: (i, 0, off // LANES))
    return pl.pallas_call(
        _compress_kernel,
        grid=(b,),
        in_specs=[slab(U_KCMP), slab(U_VCMP),
                  pl.BlockSpec((None,) + pe2.shape[1:], lambda i: (layer, 0, 0, 0)),
                  pl.BlockSpec((None,) + w1d.shape[1:], lambda i: (layer, 0, 0, 0, 0)),
                  pl.BlockSpec((None,) + w2d.shape[1:], lambda i: (layer, 0, 0, 0))],
        out_specs=[pl.BlockSpec((None, nch, LANES), lambda i: (i, 0, 0)),
                   pl.BlockSpec((None, LANES, nch), lambda i: (i, 0, 0))],
        out_shape=[jax.ShapeDtypeStruct((b, nch, LANES), BF16), jax.ShapeDtypeStruct((b, LANES, nch), BF16)],
        scratch_shapes=[pltpu.VMEM((s, LANES), F32)],
        compiler_params=_cparams(1),
        name="nsa_compress",
    )(u3, u3, pe2, w1d, w2d)


CS_TQ = 128


def _cmp_select_kernel(q_ref, kc_ref, vct_ref, ovt_ref, o_ref, bias_ref, sc_ref):
    tq = CS_TQ
    qi = pl.program_id(1)
    nc = kc_ref.shape[0]
    ns = ovt_ref.shape[0]
    tpos = qi * tq + _lane_iota((1, tq))
    kc = kc_ref[...]
    lane = _lane_iota((1, LANES))
    kst = jnp.concatenate([jnp.where(lane < HEAD_DIM, kc, jnp.zeros_like(kc)),
                           jnp.where(lane >= HEAD_DIM, kc, jnp.zeros_like(kc))], axis=0)
    c_end = _sub_iota((nc, 1)) * CMP_STRIDE + (CMP_LEN - 1)
    cv = c_end <= tpos
    vct = vct_ref[...]
    psum = [jnp.zeros((nc, tq), F32), jnp.zeros((nc, tq), F32)]
    sub = _sub_iota((LANES, 1))
    for i in range(A_HEADS // 2):
        qb = q_ref[:, i * LANES:(i + 1) * LANES]
        st = lax.dot_general(kst, qb, (((1,), (1,)), ((), ())), preferred_element_type=F32)
        ot = []
        for g in range(A_KV_GROUPS):
            s = jnp.where(cv, st[g * nc:(g + 1) * nc], NEG)
            m = jnp.max(s, axis=0, keepdims=True)
            e = jnp.where(cv, jnp.exp2(s - m), 0.0)
            den = jnp.sum(e, axis=0, keepdims=True)
            p = e / jnp.where(den > 0.0, den, 1.0)
            psum[g] = psum[g] + p
            ot.append(jnp.dot(vct, p.astype(BF16), preferred_element_type=F32))
        o_pair = jnp.where(sub < HEAD_DIM, ot[0], ot[1])
        o_ref[:, i * LANES:(i + 1) * LANES] = o_pair.T.astype(o_ref.dtype)
    jdx = _sub_iota((ns, 1))
    cur = tpos // SLC_LEN
    valid = jdx <= cur
    forced = (jdx == 0) | (jdx == cur) | (jdx == cur - 1)
    for g in range(A_KV_GROUPS):
        imp = jnp.dot(ovt_ref[...], psum[g], preferred_element_type=F32, precision=lax.Precision.HIGHEST)
        score = jnp.where(forced, BIG, jnp.where(valid, imp, NEG))
        sc_ref[...] = score
        slab = 8
        parts = [score[v * slab:(v + 1) * slab] for v in range(ns // slab)]
        cnts = [jnp.zeros((slab, tq), F32) for _ in parts]
        for jp in range(ns):
            row = sc_ref[jp:jp + 1, :]
            for v, sv in enumerate(parts):
                if v * slab > jp:
                    ahead = row >= sv
                elif (v + 1) * slab - 1 < jp:
                    ahead = row > sv
                else:
                    tie = jnp.where(jdx[v * slab:(v + 1) * slab] > jp, 1.0, 0.0)
                    cnts[v] = cnts[v] + jnp.where(row == sv, tie, 0.0)
                    ahead = row > sv
                cnts[v] = cnts[v] + jnp.where(ahead, 1.0, 0.0)
        cnt = jnp.concatenate(cnts, axis=0)
        bias = jnp.where(cnt < float(min(SLC_TOPK, ns)), 0.0, NEG)
        bias_ref[g * ns:(g + 1) * ns, :] = bias.astype(bias_ref.dtype)


def _cmp_select(u3, kc, vct, ovt):
    b, s, _ = u3.shape
    nc = kc.shape[1]
    ns = s // SLC_LEN
    assert A_KV_GROUPS * ns == LANES
    return pl.pallas_call(
        _cmp_select_kernel,
        grid=(b, s // CS_TQ),
        in_specs=[pl.BlockSpec((None, CS_TQ, A_WIDTH), lambda i, q: (i, q, U_AQ // A_WIDTH)),
                  pl.BlockSpec((None, nc, LANES), lambda i, q: (i, 0, 0)),
                  pl.BlockSpec((None, LANES, nc), lambda i, q: (i, 0, 0)),
                  pl.BlockSpec((ns, nc), lambda i, q: (0, 0))],
        out_specs=[pl.BlockSpec((None, CS_TQ, A_WIDTH), lambda i, q: (i, q, 0)),
                   pl.BlockSpec((None, LANES, CS_TQ), lambda i, q: (i, 0, q))],
        out_shape=[jax.ShapeDtypeStruct((b, s, A_WIDTH), BF16), jax.ShapeDtypeStruct((b, LANES, s), BF16)],
        scratch_shapes=[pltpu.VMEM((ns, CS_TQ), F32)],
        compiler_params=_cparams(2),
        name="nsa_cmp_select",
    )(u3, kc, vct, ovt)


FLASH_TK = 256
FLASH_CHUNK = 256
FLASH_UNROLL = 4
ONES_ROWS = 16
VT_ROWS = HEAD_DIM + ONES_ROWS


def _head_values_t(vt_pair):
    ones = jnp.ones((ONES_ROWS, vt_pair.shape[1]), vt_pair.dtype)
    return [jnp.concatenate([vt_pair[h * HEAD_DIM:(h + 1) * HEAD_DIM], ones], axis=0) for h in range(2)]


def _flash_reset(m_ref, acc_ref):
    m_ref[...] = jnp.full(m_ref.shape, NEG, F32)
    acc_ref[...] = jnp.zeros(acc_ref.shape, F32)


def _flash_update(st, vt, m_ref, acc_ref, idx, mask):
    if mask is not None:
        st = jnp.where(mask, st, NEG)
    m_old = m_ref[idx]
    m_new = jnp.maximum(m_old, jnp.max(st, axis=0, keepdims=True))
    alpha = jnp.exp2(m_old - m_new)
    p = jnp.exp2((st - m_new).astype(BF16))
    acc_ref[idx] = alpha * acc_ref[idx] + jnp.dot(vt, p, preferred_element_type=F32)
    m_ref[idx] = m_new


def _flash_result(acc_ref, idx):
    acc = acc_ref[idx]
    return acc[0:HEAD_DIM] / acc[HEAD_DIM:HEAD_DIM + 1]


def _flash_pipeline(n_pairs, n_chain, put, use):
    for c in range(n_chain):
        put(0, c, 0)

    def run(t0, count):
        for t in range(count):
            for c in range(n_chain):
                put(t0 + t + 1, c, (t + 1) % 2)
                use(t0 + t, c, t % 2)

    per = FLASH_UNROLL // 2
    n_long = n_pairs // per

    def long_body(j, carry):
        run(FLASH_UNROLL * j, FLASH_UNROLL)
        return carry

    def pair_body(j, carry):
        run(FLASH_UNROLL * n_long + 2 * j, 2)
        return carry

    lax.fori_loop(0, n_long, long_body, 0)
    lax.fori_loop(0, n_pairs - per * n_long, pair_body, 0)


def _transpose_bf16(x):
    return x.astype(F32).T.astype(BF16)


def _merge_rows(lo, hi):
    return jnp.where(_sub_iota((LANES, 1)) < HEAD_DIM, lo, hi)


SLC_TQ = 256


def _slc_kernel(q_ref, biast_ref, k_ref, v_ref, oh_ref, o_ref, vt_ref, qa_ref, st_ref, m_ref, acc_ref):
    tq, tk = SLC_TQ, FLASH_TK
    qi = pl.program_id(1)
    nblk = A_HEADS // 2
    sub = _sub_iota((LANES, 1))

    @pl.when(qi == 0)
    def _():
        for kt in range(vt_ref.shape[0]):
            for g, vt in enumerate(_head_values_t(_transpose_bf16(v_ref[kt * tk:(kt + 1) * tk, :]))):
                vt_ref[kt, g] = vt

    nchain = 2 * nblk
    biast = biast_ref[...]
    zero = jnp.zeros_like(biast)
    mine = (sub < HEAD_DIM, sub >= HEAD_DIM)
    for i in range(nblk):
        qt = _transpose_bf16(q_ref[:, i * LANES:(i + 1) * LANES])
        for h in range(2):
            qa_ref[2 * i + h] = jnp.concatenate([jnp.where(mine[h], qt, zero), jnp.where(mine[h], biast, zero)],
                                                axis=0)
    _flash_reset(m_ref, acc_ref)

    def put(kt, c, slot):
        start = pl.multiple_of(kt * tk, tk)
        k = jnp.concatenate([k_ref[pl.ds(start, tk), :], oh_ref[pl.ds(start, tk), :]], axis=1)
        st_ref[slot, c] = jnp.dot(k, qa_ref[c], preferred_element_type=F32)

    def use(kt, c, slot, mask=None):
        _flash_update(st_ref[slot, c], vt_ref[kt, c % 2], m_ref, acc_ref, c, mask)

    n_full = qi
    n_pairs = n_full // 2
    _flash_pipeline(n_pairs, nchain, put, use)
    causal = (n_full * tk + _sub_iota((tk, 1))) <= (qi * tq + _lane_iota((1, tq)))

    @pl.when(n_full == 2 * n_pairs)
    def _():
        for c in range(nchain):
            use(n_full, c, 0, causal)

    @pl.when(n_full != 2 * n_pairs)
    def _():
        for c in range(nchain):
            put(n_full, c, 1)
            use(n_full - 1, c, 0)
        for c in range(nchain):
            use(n_full, c, 1, causal)

    for i in range(nblk):
        ot = jnp.concatenate([_flash_result(acc_ref, 2 * i), _flash_result(acc_ref, 2 * i + 1)], axis=0)
        o_ref[:, i * LANES:(i + 1) * LANES] = ot.T.astype(o_ref.dtype)


def _slc_attention(u3, biast, onehot):
    b, s, _ = u3.shape
    nblk = A_HEADS
    assert SLC_TQ == FLASH_CHUNK == FLASH_TK
    return pl.pallas_call(
        _slc_kernel,
        grid=(b, s // SLC_TQ),
        in_specs=[pl.BlockSpec((None, SLC_TQ, A_WIDTH), lambda i, q: (i, q, U_AQ // A_WIDTH)),
                  pl.BlockSpec((None, LANES, SLC_TQ), lambda i, q: (i, 0, q)),
                  pl.BlockSpec((None, s, LANES), lambda i, q: (i, 0, U_KSLC // LANES)),
                  pl.BlockSpec((None, s, LANES), lambda i, q: (i, 0, U_VSLC // LANES)),
                  pl.BlockSpec((s, LANES), lambda i, q: (0, 0))],
        out_specs=pl.BlockSpec((None, SLC_TQ, A_WIDTH), lambda i, q: (i, q, 0)),
        out_shape=jax.ShapeDtypeStruct((b, s, A_WIDTH), BF16),
        scratch_shapes=[pltpu.VMEM((s // FLASH_TK, 2, VT_ROWS, FLASH_TK), BF16),
                        pltpu.VMEM((nblk, 2 * LANES, FLASH_CHUNK), BF16),
                        pltpu.VMEM((2, nblk, FLASH_TK, FLASH_CHUNK), F32),
                        pltpu.VMEM((nblk, 1, FLASH_CHUNK), F32),
                        pltpu.VMEM((nblk, VT_ROWS, FLASH_CHUNK), F32)],
        compiler_params=_cparams(2),
        name="nsa_selected",
    )(u3, biast, u3, u3, onehot)


BAND_T = 128
BAND_ROWS = 512
BAND_LOOKAHEAD = 2


def _banded_kernel(*refs, nprev, max_dist, shared_kv, with_lse):
    t = BAND_T
    q_ref, k_ref, v_ref, o_ref = refs[:4]
    lse_ref = refs[4] if with_lse else None
    qi = pl.program_id(1)
    nsub = q_ref.shape[0] // t
    nblk = q_ref.shape[1] // LANES
    nk = (nprev + 1) * t
    col = _lane_iota((1, 2 * t))
    base = jnp.where(col >= t, col - t, col) - _sub_iota((nk, 1))
    sub = _sub_iota((LANES, 1))

    def window(s):
        return pl.multiple_of(jnp.maximum(qi * nsub + s - nprev, 0) * t, t)

    def band_cap(dist):
        rel = base + dist
        return jnp.where((rel >= 0) & (rel <= max_dist), BIG, NEG)

    assert nsub >= nprev
    regular = band_cap(nprev * t)
    clamped = [band_cap(s * t) for s in range(nprev)]

    def cap(s):
        return jnp.where(qi == 0, clamped[s], regular) if s < nprev else regular

    def scores(s, i):
        start = window(s)
        cols = slice(None) if shared_kv else slice(i * LANES, (i + 1) * LANES)
        k = k_ref[pl.ds(start, nk), cols]
        qt = _transpose_bf16(q_ref[s * t:(s + 1) * t, i * LANES:(i + 1) * LANES])
        zero = jnp.zeros_like(qt)
        qs = jnp.concatenate([jnp.where(sub < HEAD_DIM, qt, zero), jnp.where(sub >= HEAD_DIM, qt, zero)], axis=1)
        return jnp.dot(k, qs, preferred_element_type=F32)

    vt_cache = {}

    def values_t(s, i):
        key = s if shared_kv else (s, i)
        if key not in vt_cache:
            start = window(s)
            cols = slice(None) if shared_kv else slice(i * LANES, (i + 1) * LANES)
            vt_cache[key] = jnp.concatenate(
                [_transpose_bf16(v_ref[pl.ds(start + j * t, t), cols]) for j in range(nprev + 1)], axis=1)
        return vt_cache[key]

    def finish(s, i, st):
        vt = values_t(s, i)
        st = jnp.minimum(st, cap(s))
        m = jnp.max(st, axis=0, keepdims=True)
        p = jnp.exp2(st - m)
        l = jnp.sum(p, axis=0, keepdims=True)
        ot = jnp.dot(vt, p.astype(BF16), preferred_element_type=F32) / l
        rows, sl = slice(s * t, (s + 1) * t), slice(i * LANES, (i + 1) * LANES)
        o_ref[rows, sl] = _merge_rows(ot[:, 0:t], ot[:, t:2 * t]).T.astype(o_ref.dtype)
        if with_lse:
            lse = m + jnp.log2(l)
            lse_ref[rows, sl] = _merge_rows(jnp.broadcast_to(lse[:, 0:t], (LANES, t)),
                                            jnp.broadcast_to(lse[:, t:2 * t], (LANES, t))).T

    items = [(s, i) for s in range(nsub) for i in range(nblk)]
    pending = [scores(*it) for it in items[:BAND_LOOKAHEAD]]
    for j, it in enumerate(items):
        if j + BAND_LOOKAHEAD < len(items):
            pending.append(scores(*items[j + BAND_LOOKAHEAD]))
        finish(*it, pending[j])


def _banded(src, *, nseq, seqlen, q_off, k_off, v_off, kv_width, max_dist, with_lse):
    t = BAND_T
    rows = min(BAND_ROWS, seqlen)
    nq = seqlen // rows
    nprev = -(-max_dist // t)
    shared = kv_width == LANES
    assert (nprev + 1) * t <= seqlen and seqlen % rows == 0 and rows % t == 0
    qo = pl.BlockSpec((rows, A_WIDTH), lambda n, q: (n * nq + q, q_off // A_WIDTH))
    out = pl.BlockSpec((rows, A_WIDTH), lambda n, q: (n * nq + q, 0))
    in_specs = [qo,
                pl.BlockSpec((seqlen, kv_width), lambda n, q: (n, k_off // kv_width)),
                pl.BlockSpec((seqlen, kv_width), lambda n, q: (n, v_off // kv_width))]
    out_specs = [out]
    out_shape = [jax.ShapeDtypeStruct((nseq * seqlen, A_WIDTH), BF16)]
    if with_lse:
        out_specs.append(out)
        out_shape.append(jax.ShapeDtypeStruct((nseq * seqlen, A_WIDTH), F32))
    return pl.pallas_call(
        functools.partial(_banded_kernel, nprev=nprev, max_dist=max_dist, shared_kv=shared, with_lse=with_lse),
        grid=(nseq, nq),
        in_specs=in_specs,
        out_specs=out_specs,
        out_shape=out_shape,
        compiler_params=_cparams(2),
        name="banded_attention",
    )(src, src, src)


MLA_TM = 512


def _rms(x, g):
    return x * lax.rsqrt(jnp.mean(x * x, axis=-1, keepdims=True) + EPS) * g


def _mla_prep_kernel(cq_ref, ckv_ref, kr_ref, c_ref, s1_ref, s2_ref, gq_ref, gkv_ref, wq_ref, wk_ref, wv_ref,
                     q_out, k_out, vt_out):
    half = QK_ROPE // 2
    tk = FLASH_TK
    cm, s1, s2 = c_ref[...], s1_ref[...], s2_ref[...]

    def rope(xc):
        return xc * cm + pltpu.roll(xc, half, 1) * s1 + pltpu.roll(xc, LANES - half, 1) * s2

    qn = _rms(cq_ref[...].astype(F32), gq_ref[...]).astype(BF16)
    q = jnp.dot(qn, wq_ref[...], preferred_element_type=F32)
    kvn = _rms(ckv_ref[...].astype(F32), gkv_ref[...]).astype(BF16)
    kk = jnp.dot(kvn, wk_ref[...], preferred_element_type=F32)
    kr = rope(kr_ref[...].astype(F32))
    for h in range(B_HEADS):
        sl = slice(h * LANES, (h + 1) * LANES)
        q_out[:, sl] = (rope(q[:, sl]) * MLA_SCALE).astype(q_out.dtype)
        k_out[:, sl] = (kk[:, sl] + kr).astype(k_out.dtype)
    v = jnp.dot(kvn, wv_ref[...], preferred_element_type=F32)
    for j in range(v.shape[0] // tk):
        for hp in range(B_HEADS // 2):
            pair_t = v[j * tk:(j + 1) * tk, hp * LANES:(hp + 1) * LANES].T.astype(vt_out.dtype)
            for h, vt in enumerate(_head_values_t(pair_t)):
                vt_out[j, hp, h] = vt


def _mla_prep(u, tabs, g_q, g_kv, wq, wk, wv, layer):
    t = u.shape[0]
    tm = min(MLA_TM, t)
    row = lambda i: (i, 0)
    wide = B_HEADS * LANES
    assert U_BCQ % Q_LORA == 0 and tm % FLASH_TK == 0
    return pl.pallas_call(
        _mla_prep_kernel,
        grid=(t // tm,),
        in_specs=[pl.BlockSpec((tm, Q_LORA), lambda i: (i, U_BCQ // Q_LORA)),
                  pl.BlockSpec((tm, LANES), lambda i: (i, U_BCKV // LANES)),
                  pl.BlockSpec((tm, LANES), lambda i: (i, U_BKR // LANES)),
                  pl.BlockSpec((tm, LANES), row), pl.BlockSpec((tm, LANES), row), pl.BlockSpec((tm, LANES), row),
                  pl.BlockSpec((None, 1, Q_LORA), lambda i: (layer, 0, 0)),
                  pl.BlockSpec((None, 1, KV_LORA), lambda i: (layer, 0, 0)),
                  pl.BlockSpec((None, Q_LORA, wide), lambda i: (layer, 0, 0)),
                  pl.BlockSpec((None, KV_LORA, wide), lambda i: (layer, 0, 0)),
                  pl.BlockSpec((None, KV_LORA, B_WIDTH), lambda i: (layer, 0, 0))],
        out_specs=[pl.BlockSpec((tm, wide), row), pl.BlockSpec((tm, wide), row),
                   pl.BlockSpec((tm // FLASH_TK, B_HEADS // 2, 2, VT_ROWS, FLASH_TK), lambda i: (i, 0, 0, 0, 0))],
        out_shape=[jax.ShapeDtypeStruct((t, wide), BF16), jax.ShapeDtypeStruct((t, wide), BF16),
                   jax.ShapeDtypeStruct((t // FLASH_TK, B_HEADS // 2, 2, VT_ROWS, FLASH_TK), BF16)],
        compiler_params=_cparams(1),
        name="mla_prep",
    )(u, u, u, *tabs, g_q, g_kv, wq, wk, wv)


MLA_TQ = 1024


def _mla_flash_kernel(q_ref, k_ref, vt_ref, o_ref, qt_ref, st_ref, m_ref, acc_ref):
    tq, tk, cw = MLA_TQ, FLASH_TK, FLASH_CHUNK
    nhalf = tq // cw
    qi = pl.program_id(2)
    for h in range(2):
        qt_ref[h] = _transpose_bf16(q_ref[:, h * LANES:(h + 1) * LANES])
    _flash_reset(m_ref, acc_ref)

    def put(kt, c, slot):
        h, half = divmod(c, nhalf)
        start = pl.multiple_of(kt * tk, tk)
        k = k_ref[pl.ds(start, tk), h * LANES:(h + 1) * LANES]
        st_ref[slot, c] = jnp.dot(k, qt_ref[h, :, half * cw:(half + 1) * cw], preferred_element_type=F32)

    def use(kt, c, slot, masked=False):
        mask = None
        if masked:
            qpos = qi * tq + (c % nhalf) * cw + _lane_iota((1, cw))
            mask = (kt * tk + _sub_iota((tk, 1))) <= qpos
        _flash_update(st_ref[slot, c], vt_ref[kt, c // nhalf], m_ref, acc_ref, c, mask)

    assert tk == cw and nhalf % 2 == 0
    n_full = nhalf * qi
    _flash_pipeline(n_full // 2, 2 * nhalf, put, use)
    live = lambda d: [c for c in range(2 * nhalf) if c % nhalf >= d]
    for d in range(nhalf):
        if d + 1 < nhalf:
            for c in live(d + 1):
                put(n_full + d + 1, c, (d + 1) % 2)
        for c in live(d):
            use(n_full + d, c, d % 2, masked=(c % nhalf == d))
    ots = []
    for h in range(2):
        ots.append(jnp.concatenate([_flash_result(acc_ref, h * nhalf + half) for half in range(nhalf)],
                                   axis=1))
    o_ref[...] = jnp.concatenate(ots, axis=0).T.astype(o_ref.dtype)


def _mla_flash(q, k, vt, batch, seq):
    assert FLASH_TK == FLASH_CHUNK
    q3 = q.reshape(batch, seq, -1)
    k3 = k.reshape(batch, seq, -1)
    vt6 = vt.reshape(batch, seq // FLASH_TK, B_HEADS // 2, 2, VT_ROWS, FLASH_TK)
    nchain = 2 * (MLA_TQ // FLASH_CHUNK)
    o = pl.pallas_call(
        _mla_flash_kernel,
        grid=(batch, B_HEADS // 2, seq // MLA_TQ),
        in_specs=[pl.BlockSpec((None, MLA_TQ, 2 * LANES), lambda b, h, i: (b, i, h)),
                  pl.BlockSpec((None, seq, 2 * LANES), lambda b, h, i: (b, 0, h)),
                  pl.BlockSpec((None, seq // FLASH_TK, None, 2, VT_ROWS, FLASH_TK),
                               lambda b, h, i: (b, 0, h, 0, 0, 0))],
        out_specs=pl.BlockSpec((None, MLA_TQ, LANES), lambda b, h, i: (b, i, h)),
        out_shape=jax.ShapeDtypeStruct((batch, seq, B_WIDTH), BF16),
        scratch_shapes=[pltpu.VMEM((2, LANES, MLA_TQ), BF16),
                        pltpu.VMEM((2, nchain, FLASH_TK, FLASH_CHUNK), F32),
                        pltpu.VMEM((nchain, 1, FLASH_CHUNK), F32),
                        pltpu.VMEM((nchain, VT_ROWS, FLASH_CHUNK), F32)],
        compiler_params=_cparams(3),
        name="mla_flash",
    )(q3, k3, vt6)
    return o.reshape(batch * seq, B_WIDTH)


OUT_TM = 256


def _out_kernel(x_ref, p_ref, ocmp_ref, oslc_ref, owin_ref, gate_ref, e_ref, az_ref, ob_ref, bz_ref,
                oc0_ref, oc1_ref, oc2_ref, l0_ref, l1_ref, l2_ref, cz_ref, mg0_ref, mg1_ref, mg2_ref,
                wa_ref, wb_ref, wc_ref, wout_ref, wplg_ref, wple_ref, gpost_ref, o_ref, *tok_refs):
    tok = iter(tok_refs)

    def sig(z):
        return 0.5 * jnp.tanh(0.5 * z) + 0.5

    def f(r):
        if len(r.shape) == 2:
            return r[...].astype(F32)
        dil, n, width = r.shape
        buf = next(tok)
        for res in range(dil):
            blk = r[res].astype(F32)
            for c in range(width // LANES):
                buf[c, pl.ds(res, n, stride=dil), :] = blk[:, c * LANES:(c + 1) * LANES]
        return jnp.concatenate([buf[c] for c in range(width // LANES)], axis=1)

    def silu(z):
        return z * sig(z)

    def mm(a, w_ref):
        return jnp.dot(a.astype(BF16), w_ref[...], preferred_element_type=F32)

    g = sig(gate_ref[...].astype(F32))
    g_hi = g.astype(BF16)
    g_lo = (g - g_hi.astype(F32)).astype(BF16)
    gs = (jnp.dot(g_hi, e_ref[...], preferred_element_type=F32)
          + jnp.dot(g_lo, e_ref[...], preferred_element_type=F32))
    o_a = (gs[:, 0:A_WIDTH] * f(ocmp_ref) + gs[:, A_WIDTH:2 * A_WIDTH] * f(oslc_ref)
           + gs[:, 2 * A_WIDTH:3 * A_WIDTH] * f(owin_ref))
    y_a = mm(o_a * silu(f(az_ref)), wa_ref)
    y_b = mm(f(ob_ref) * silu(f(bz_ref)), wb_ref)
    l0, l1, l2 = f(l0_ref), f(l1_ref), f(l2_ref)
    mx = jnp.maximum(jnp.maximum(l0, l1), l2)
    e0, e1, e2 = jnp.exp2(l0 - mx), jnp.exp2(l1 - mx), jnp.exp2(l2 - mx)
    o_c = (e0 * f(oc0_ref) + e1 * f(oc1_ref) + e2 * f(oc2_ref)) / (e0 + e1 + e2)
    y_c = mm(o_c * silu(f(cz_ref)), wc_ref)
    mix = sig(f(mg0_ref)) * y_a + sig(f(mg1_ref)) * y_b + sig(f(mg2_ref)) * y_c
    y = mm(mix, wout_ref)
    x1 = x_ref[...] + _rms(y, gpost_ref[...])
    o_ref[...] = x1 + sig(mm(x1, wplg_ref)) * mm(p_ref[...], wple_ref)


def _out_layer(x2, p, layer, o_cmp, o_slc, o_win, u, expand, o_b, o_c, lse_c, w, batch, dils):
    t = x2.shape[0]
    seq = t // batch
    tm = min(OUT_TM, seq)
    nper = seq // tm
    row = lambda i: (i, 0)
    ucol = lambda off, width: pl.BlockSpec((tm, width), lambda i: (i, off // width))
    half = pl.BlockSpec((tm, A_WIDTH), row)
    wspec = lambda r, c: pl.BlockSpec((None, r, c), lambda i: (layer, 0, 0))

    def group_spec(dil):
        if dil == 1:
            return half
        return pl.BlockSpec((None, dil, tm // dil, C_WIDTH), lambda i: (i // nper, 0, i % nper, 0))

    def group_view(a, dil):
        return a if dil == 1 else a.reshape(batch, dil, seq // dil, C_WIDTH)

    gspecs = [group_spec(d) for d in dils]
    o_c = [group_view(a, d) for a, d in zip(o_c, dils)]
    lse_c = [group_view(a, d) for a, d in zip(lse_c, dils)]
    n_tok = 2 * sum(1 for d in dils if d > 1)
    in_specs = [pl.BlockSpec((tm, D_MODEL), row),
                pl.BlockSpec((None, tm, PLE_DIM), lambda i: (layer, i, 0)),
                half, half, half,
                ucol(U_GATE, LANES), pl.BlockSpec((LANES, 3 * A_WIDTH), lambda i: (0, 0)),
                ucol(U_AZ, A_WIDTH), half, ucol(U_BZ, B_WIDTH),
                *gspecs, *gspecs, ucol(U_CZ, C_WIDTH),
                ucol(U_MG, D_MODEL), ucol(U_MG + D_MODEL, D_MODEL), ucol(U_MG + 2 * D_MODEL, D_MODEL),
                wspec(A_WIDTH, D_MODEL), wspec(B_WIDTH, D_MODEL), wspec(C_WIDTH, D_MODEL),
                wspec(D_MODEL, D_MODEL), wspec(D_MODEL, D_MODEL), wspec(PLE_DIM, D_MODEL),
                wspec(1, D_MODEL)]
    return pl.pallas_call(
        _out_kernel,
        grid=(t // tm,),
        in_specs=in_specs,
        out_specs=pl.BlockSpec((tm, D_MODEL), row),
        out_shape=jax.ShapeDtypeStruct((t, D_MODEL), F32),
        scratch_shapes=[pltpu.VMEM((C_WIDTH // LANES, tm, LANES), F32)] * n_tok,
        compiler_params=_cparams(1),
        name="out_layer",
    )(x2, p, o_cmp, o_slc, o_win, u, expand, u, o_b, u, *o_c, *lse_c, u, u, u, u,
      w["w_a"], w["w_b"], w["w_c"], w["w_out"], w["w_plg"], w["w_ple"], w["g_post"])


def _overlap_t(seq):
    nc = seq // CMP_STRIDE
    ns = seq // SLC_LEN
    cs = np.arange(nc)[:, None] * CMP_STRIDE
    js = np.arange(ns)[None, :] * SLC_LEN
    ov = np.clip(np.minimum(cs + CMP_LEN, js + SLC_LEN) - np.maximum(cs, js), 0, None).astype(np.float32) / CMP_LEN
    return jnp.asarray(ov.T)


def _block_onehot(seq):
    blk = np.arange(seq)[:, None] // SLC_LEN
    return jnp.asarray((np.arange(LANES)[None, :] % HEAD_DIM == blk).astype(np.float32), dtype=BF16)


def _gate_expand():
    e = np.zeros((LANES, 3 * A_WIDTH), np.float32)
    for br in range(3):
        for i in range(A_HEADS // 2):
            for g in range(A_KV_GROUPS):
                h = g * (A_HEADS // 2) + i
                c0 = br * A_WIDTH + i * LANES + g * HEAD_DIM
                e[h * 3 + br, c0:c0 + HEAD_DIM] = 1.0
    return jnp.asarray(e, dtype=BF16)


def kernel(x, p, positions, g_pre, g_post, w_in, nsa_cmp_pe, nsa_cmp_w1, nsa_cmp_w2, w_a, mla_g_q, mla_w_uq,
           mla_g_kv, mla_w_ukv, w_b, w_c, w_out, w_ple, w_plg):
    batch, seq, _ = x.shape
    depth = w_in.shape[0]
    t = batch * seq
    nch = seq // CMP_STRIDE

    pos = positions.astype(F32)
    part_tabs = _rope_tables(pos.reshape(-1), ROT_DIM, HEAD_DIM, 0)
    mla_tabs = _rope_tables(pos.reshape(-1), QK_ROPE, LANES, QK_NOPE)
    dils = [dil for _, dil in C_PAIRS]
    assert dils[0] == 1
    perm_tabs = {dil: _rope_tables(_residue_major_positions(pos, dil), ROT_DIM, HEAD_DIM, 0) for dil in dils[1:]}
    w_main, w_groups = _regroup_w_in(w_in)
    w_main = _tile_cols(w_main, IN_TN)
    w_groups = [_tile_cols(w, DG_COLS) for w in w_groups]
    g_pre3 = g_pre.reshape(depth, 1, D_MODEL)
    pe2, w1d, w2d = _compress_weights(nsa_cmp_pe, nsa_cmp_w1, nsa_cmp_w2)
    wq = mla_w_uq.reshape(depth, Q_LORA, B_HEADS, QK_NOPE + QK_ROPE)
    wq = jnp.pad(wq, ((0, 0), (0, 0), (0, 0), (0, LANES - QK_NOPE - QK_ROPE))).reshape(depth, Q_LORA, -1).astype(BF16)
    wkv = mla_w_ukv.reshape(depth, KV_LORA, B_HEADS, QK_NOPE + V_DIM)
    wk = jnp.pad(wkv[..., :QK_NOPE], ((0, 0), (0, 0), (0, 0), (0, LANES - QK_NOPE))).reshape(depth, KV_LORA, -1)
    wk = wk.astype(BF16)
    wv = wkv[..., QK_NOPE:].reshape(depth, KV_LORA, B_WIDTH).astype(BF16)
    gq3 = mla_g_q.reshape(depth, 1, Q_LORA)
    gkv3 = mla_g_kv.reshape(depth, 1, KV_LORA)
    w_a_p = w_a.reshape(depth, 2, 4, HEAD_DIM, D_MODEL).swapaxes(1, 2).reshape(depth, A_WIDTH, D_MODEL)
    wts = {"w_a": w_a_p.astype(BF16), "w_b": w_b.astype(BF16), "w_c": w_c.astype(BF16),
           "w_out": w_out.astype(BF16), "w_plg": w_plg.astype(BF16), "w_ple": w_ple.astype(BF16),
           "g_post": g_post.reshape(depth, 1, D_MODEL)}
    p3 = p.reshape(depth, t, PLE_DIM)
    ovt = _overlap_t(seq)
    onehot = _block_onehot(seq)
    expand = _gate_expand()

    x2 = x.reshape(t, D_MODEL)
    for layer in range(depth):
        h, h_res = _prenorm(x2, g_pre3, layer, batch, dils[1:])
        u = _inproj(h, w_main, layer, part_tabs, ROPE_COLS)
        u3 = u.reshape(batch, seq, DP)
        kc, vct = _compress(u3, pe2, w1d, w2d, layer)
        o_cmp, biast = _cmp_select(u3, kc, vct, ovt)
        o_slc = _slc_attention(u3, biast, onehot)
        (o_win,) = _banded(u, nseq=batch, seqlen=seq, q_off=U_AQ, k_off=U_KWIN, v_off=U_VWIN,
                           kv_width=LANES, max_dist=WIN - 1, with_lse=False)
        q_b, k_b, vt_b = _mla_prep(u, mla_tabs, gq3, gkv3, wq, wk, wv, layer)
        o_b = _mla_flash(q_b, k_b, vt_b, batch, seq)
        o_c, lse_c = [], []
        for gi, (window, dil) in enumerate(C_PAIRS):
            if dil == 1:
                o_g, lse_g = _banded(u, nseq=batch, seqlen=seq, q_off=U_CQ0, k_off=U_CK0, v_off=U_CV0,
                                     kv_width=C_WIDTH, max_dist=window, with_lse=True)
            else:
                u_g = _inproj(h_res[gi - 1], w_groups[gi - 1], layer, perm_tabs[dil], DG_ROPE_COLS)
                o_g, lse_g = _banded(u_g, nseq=batch * dil, seqlen=seq // dil, q_off=0, k_off=C_WIDTH,
                                     v_off=2 * C_WIDTH, kv_width=C_WIDTH, max_dist=window // dil, with_lse=True)
            o_c.append(o_g)
            lse_c.append(lse_g)
        x2 = _out_layer(x2, p3, layer, o_cmp.reshape(t, A_WIDTH), o_slc.reshape(t, A_WIDTH), o_win, u, expand,
                        o_b, o_c, lse_c, wts, batch, dils)
    return x2.reshape(batch, seq, D_MODEL)
```

```python
import functools

import numpy as np
import jax
import jax.numpy as jnp
from jax import lax
from jax.experimental import pallas as pl
from jax.experimental.pallas import tpu as pltpu

F32 = jnp.float32
BF16 = jnp.bfloat16

D_MODEL = 1024
PLE_DIM = 256
ROPE_THETA = 500000.0
HEAD_DIM = 64
ROT_DIM = HEAD_DIM // 4
EPS = 1e-6
NEG = -1e30
BIG = 1e30
LANES = 128

A_HEADS = 8
A_KV_GROUPS = 2
A_WIDTH = A_HEADS * HEAD_DIM
CMP_LEN = 32
CMP_STRIDE = 16
CMP_HID = 256
SLC_LEN = 64
SLC_TOPK = 16
WIN = 512

B_HEADS = 8
Q_LORA = 384
KV_LORA = 128
QK_NOPE = 64
QK_ROPE = 32
V_DIM = 64
B_WIDTH = B_HEADS * V_DIM

C_PAIRS = ((128, 1), (512, 4), (2048, 16))
C_GROUPS = len(C_PAIRS)
C_HEADS = 8
C_WIDTH = C_HEADS * HEAD_DIM

IN_SIZES = (A_WIDTH, 6 * A_KV_GROUPS * HEAD_DIM, 3 * A_HEADS, A_WIDTH, Q_LORA, KV_LORA, QK_ROPE,
            B_WIDTH, C_GROUPS * 3 * C_WIDTH, C_WIDTH, 3 * D_MODEL)
IN_OFF = tuple(int(v) for v in np.cumsum((0,) + IN_SIZES))
(OFF_AQ, OFF_AKV, OFF_AG, OFF_AZ, OFF_BCQ, OFF_BCKV, OFF_BKR, OFF_BZ, OFF_CQKV, OFF_CZ, OFF_MG) = IN_OFF[:-1]

U_AQ = 0
U_CQ0 = 512
U_CK0 = 1024
U_KCMP = 1536
U_KSLC = 1664
U_KWIN = 1792
ROPE_COLS = 1920
U_VCMP = 1920
U_AZ = 2048
U_BZ = 2560
U_CV0 = 3072
U_CZ = 3584
U_MG = 4096
U_VSLC = 7168
U_VWIN = 7296
U_GATE = 7424
U_BCKV = 7552
U_BCQ = 7680
U_BKR = 8064
DP = 8192
DG_COLS = 3 * C_WIDTH
DG_ROPE_COLS = 2 * C_WIDTH

LOG2E = float(np.log2(np.e))
Q_SCALE = float(HEAD_DIM ** -0.5) * LOG2E
MLA_SCALE = float((QK_NOPE + QK_ROPE) ** -0.5) * LOG2E
VMEM_LIMIT = 48 * 1024 * 1024


def _cparams(n_axes):
    return pltpu.CompilerParams(dimension_semantics=("arbitrary",) * n_axes, vmem_limit_bytes=VMEM_LIMIT)


def _lane_iota(shape):
    return lax.broadcasted_iota(jnp.int32, shape, len(shape) - 1)


def _sub_iota(shape):
    return lax.broadcasted_iota(jnp.int32, shape, len(shape) - 2)


def _pair_cols(w):
    lead = w.shape[:-1]
    return w.reshape(lead + (2, 4, HEAD_DIM)).swapaxes(-3, -2).reshape(lead + (A_WIDTH,))


def _regroup_w_in(w_in):
    def col(off, n):
        return w_in[..., off:off + n]
    z = lambda n: jnp.zeros(w_in.shape[:-1] + (n,), w_in.dtype)
    akv = lambda which: col(OFF_AKV + which * 128, 128)
    cq = lambda gi, t: col(OFF_CQKV + gi * 3 * C_WIDTH + t * C_WIDTH, C_WIDTH)
    pieces = [_pair_cols(col(OFF_AQ, A_WIDTH)) * Q_SCALE, cq(0, 0) * Q_SCALE, cq(0, 1)]
    pieces += [akv(0), akv(2), akv(4)]
    pieces += [akv(1)]
    pieces += [_pair_cols(col(OFF_AZ, A_WIDTH)), col(OFF_BZ, B_WIDTH), cq(0, 2)]
    pieces += [col(OFF_CZ, C_WIDTH), col(OFF_MG, 3 * D_MODEL)]
    pieces += [akv(3), akv(5)]
    pieces += [col(OFF_AG, 3 * A_HEADS), z(128 - 3 * A_HEADS)]
    pieces += [col(OFF_BCKV, KV_LORA), col(OFF_BCQ, Q_LORA)]
    pieces += [z(64), col(OFF_BKR, QK_ROPE), z(32)]
    main = jnp.concatenate(pieces, axis=-1).astype(BF16)
    assert main.shape[-1] == DP
    groups = [jnp.concatenate([cq(gi, 0) * Q_SCALE, cq(gi, 1), cq(gi, 2)], axis=-1).astype(BF16)
              for gi in range(1, C_GROUPS)]
    return main, groups


def _rope_tables(pos, dim, period, offset):
    half = dim // 2
    ch = np.arange(LANES) % period - offset
    in_span = (ch >= 0) & (ch < dim)
    freq = np.where(in_span, ch % half, 0).astype(np.float32)
    inv = jnp.where(jnp.asarray(in_span), ROPE_THETA ** (-2.0 * jnp.asarray(freq) / dim), 0.0)
    ang = pos[:, None] * inv[None, :]
    cos, sin = jnp.cos(ang), jnp.sin(ang)
    first = jnp.asarray(in_span & (ch < half))[None, :]
    second = jnp.asarray(in_span & (ch >= half))[None, :]
    return cos, jnp.where(second, sin, 0.0), jnp.where(first, -sin, 0.0)


def _residue_major_positions(positions, dil):
    b, s = positions.shape
    return positions.reshape(b, s // dil, dil).swapaxes(1, 2).reshape(-1)


NORM_TM = 512
IN_TM = 1024
IN_TN = 2048
IN_CHUNK = 512


def _norm_kernel(x_ref, g_ref, o_ref, *rest, dils):
    o_dil, y_ref = rest[:len(dils)], rest[len(dils)]
    x = x_ref[...]
    ms = jnp.mean(x * x, axis=-1, keepdims=True)
    y = x * lax.rsqrt(ms + EPS) * g_ref[...]
    o_ref[...] = y.astype(o_ref.dtype)
    tm = x.shape[0]
    nlane = y_ref.shape[0]
    for c in range(nlane):
        y_ref[c] = y[:, c * LANES:(c + 1) * LANES]
    for o_d, dil in zip(o_dil, dils):
        for r in range(dil):
            for c in range(nlane):
                o_d[r, :, c * LANES:(c + 1) * LANES] = (
                    y_ref[c, pl.ds(r, tm // dil, stride=dil), :].astype(o_d.dtype))


def _prenorm(x2, g_pre, layer, batch, dils):
    t = x2.shape[0]
    seq = t // batch
    tm = min(NORM_TM, seq)
    nper = seq // tm
    out_specs = [pl.BlockSpec((tm, D_MODEL), lambda b, i: (b * nper + i, 0))]
    out_shape = [jax.ShapeDtypeStruct((t, D_MODEL), BF16)]
    for dil in dils:
        out_specs.append(pl.BlockSpec((None, dil, tm // dil, D_MODEL), lambda b, i: (b, 0, i, 0)))
        out_shape.append(jax.ShapeDtypeStruct((batch, dil, seq // dil, D_MODEL), BF16))
    outs = pl.pallas_call(
        functools.partial(_norm_kernel, dils=tuple(dils)),
        grid=(batch, nper),
        in_specs=[pl.BlockSpec((tm, D_MODEL), lambda b, i: (b * nper + i, 0)),
                  pl.BlockSpec((None, 1, D_MODEL), lambda b, i: (layer, 0, 0))],
        out_specs=out_specs,
        out_shape=out_shape,
        scratch_shapes=[pltpu.VMEM((D_MODEL // LANES, tm, LANES), F32)],
        compiler_params=_cparams(2),
        name="prenorm",
    )(x2, g_pre)
    return outs[0], [o.reshape(t, D_MODEL) for o in outs[1:]]


def _inproj_kernel(h_ref, w_ref, c_ref, s1_ref, s2_ref, o_ref, *, rope_cols):
    j = pl.program_id(0)
    tn = w_ref.shape[1]
    nsub = tn // LANES
    full_tiles, rem = divmod(rope_cols // LANES, nsub)
    chunk = IN_CHUNK if tn % IN_CHUNK == 0 else tn
    per = chunk // LANES

    def emit(n_rope):
        for g in range(tn // chunk):
            acc = jnp.dot(h_ref[...], w_ref[:, g * chunk:(g + 1) * chunk], preferred_element_type=F32)
            for k in range(per):
                c = g * per + k
                xc = acc[:, k * LANES:(k + 1) * LANES]
                if c < n_rope:
                    xc = (xc * c_ref[...] + pltpu.roll(xc, ROT_DIM // 2, 1) * s1_ref[...]
                          + pltpu.roll(xc, LANES - ROT_DIM // 2, 1) * s2_ref[...])
                o_ref[:, c * LANES:(c + 1) * LANES] = xc.astype(o_ref.dtype)

    pl.when(j < full_tiles)(lambda: emit(nsub))
    pl.when(j == full_tiles)(lambda: emit(rem))
    pl.when(j > full_tiles)(lambda: emit(0))


def _tile_cols(w, tn):
    depth, d, n = w.shape
    return w.reshape(depth, d, n // tn, tn).swapaxes(1, 2)


def _inproj(h, w_tiled, layer, tabs, rope_cols):
    t = h.shape[0]
    _, ntile, _, tn = w_tiled.shape
    tm = min(IN_TM, t)
    rope_tiles = -(-rope_cols // tn)
    row = lambda j, i: (i, 0)
    tab = lambda j, i: (jnp.where(j < rope_tiles, i, 0), 0)
    return pl.pallas_call(
        functools.partial(_inproj_kernel, rope_cols=rope_cols),
        grid=(ntile, t // tm),
        in_specs=[pl.BlockSpec((tm, D_MODEL), row),
                  pl.BlockSpec((None, None, D_MODEL, tn), lambda j, i: (layer, j, 0, 0)),
                  pl.BlockSpec((tm, LANES), tab), pl.BlockSpec((tm, LANES), tab), pl.BlockSpec((tm, LANES), tab)],
        out_specs=pl.BlockSpec((tm, tn), lambda j, i: (i, j)),
        out_shape=jax.ShapeDtypeStruct((t, ntile * tn), BF16),
        compiler_params=_cparams(2),
        name="inproj",
    )(h, w_tiled, *tabs)


def _compress_kernel(k_ref, v_ref, pe_ref, w1_ref, w2_ref, kc_ref, vct_ref, x_ref):
    nch = x_ref.shape[0] // CMP_STRIDE
    for which, src in enumerate((k_ref, v_ref)):
        x_ref[...] = src[...].astype(F32)
        rows = [x_ref[pl.ds(l, nch, stride=CMP_STRIDE), :] for l in range(CMP_STRIDE)]
        halves = []
        for part in range(CMP_LEN // CMP_STRIDE):
            xs = [(rows[l] + pe_ref[which, part * CMP_STRIDE + l:part * CMP_STRIDE + l + 1, :]).astype(BF16)
                  for l in range(CMP_STRIDE)]
            halves.append(jnp.dot(jnp.concatenate(xs, axis=1), w1_ref[which, part], preferred_element_type=F32))
        hid = halves[0] + pltpu.roll(halves[1], nch - 1, 0)
        act = hid * jax.nn.sigmoid(hid)
        tok = jnp.dot(act.astype(BF16), w2_ref[which], preferred_element_type=F32)
        if which == 0:
            kc_ref[...] = tok.astype(kc_ref.dtype)
        else:
            vct_ref[...] = tok.T.astype(vct_ref.dtype)


def _compress_weights(pe, w1, w2):
    depth = w1.shape[0]
    g = A_KV_GROUPS
    assert CMP_LEN == 2 * CMP_STRIDE and g == 2
    w1r = w1.reshape(depth, 2, CMP_LEN, HEAD_DIM, CMP_HID)
    z1 = jnp.zeros_like(w1r)
    w1d = jnp.stack([jnp.concatenate([w1r, z1], axis=-1), jnp.concatenate([z1, w1r], axis=-1)], axis=3)
    w1d = w1d.reshape(depth, 2, 2, CMP_STRIDE * g * HEAD_DIM, g * CMP_HID).astype(BF16)
    z2 = jnp.zeros_like(w2)
    w2d = jnp.concatenate([jnp.concatenate([w2, z2], axis=-1), jnp.concatenate([z2, w2], axis=-1)], axis=2)
    pe2 = jnp.concatenate([pe, pe], axis=-1)
    return pe2, w1d, w2d.astype(BF16)


def _compress(u3, pe2, w1d, w2d, layer):
    b, s, _ = u3.shape
    nch = s // CMP_STRIDE
    slab = lambda off: pl.BlockSpec((None, s, LANES), lambda i: (i, 0, off // LANES))
    return pl.pallas_call(
        _compress_kernel,
        grid=(b,),
        in_specs=[slab(U_KCMP), slab(U_VCMP),
                  pl.BlockSpec((None,) + pe2.shape[1:], lambda i: (layer, 0, 0, 0)),
                  pl.BlockSpec((None,) + w1d.shape[1:], lambda i: (layer, 0, 0, 0, 0)),
                  pl.BlockSpec((None,) + w2d.shape[1:], lambda i: (layer, 0, 0, 0))],
        out_specs=[pl.BlockSpec((None, nch, LANES), lambda i: (i, 0, 0)),
                   pl.BlockSpec((None, LANES, nch), lambda i: (i, 0, 0))],
        out_shape=[jax.ShapeDtypeStruct((b, nch, LANES), BF16), jax.ShapeDtypeStruct((b, LANES, nch), BF16)],
        scratch_shapes=[pltpu.VMEM((s, LANES), F32)],
        compiler_params=_cparams(1),
        name="nsa_compress",
    )(u3, u3, pe2, w1d, w2d)


CS_TQ = 128


def _cmp_select_kernel(q_ref, kc_ref, vct_ref, ovt_ref, o_ref, bias_ref, sc_ref):
    tq = CS_TQ
    qi = pl.program_id(1)
    nc = kc_ref.shape[0]
    ns = ovt_ref.shape[0]
    tpos = qi * tq + _lane_iota((1, tq))
    kc = kc_ref[...]
    lane = _lane_iota((1, LANES))
    kst = jnp.concatenate([jnp.where(lane < HEAD_DIM, kc, jnp.zeros_like(kc)),
                           jnp.where(lane >= HEAD_DIM, kc, jnp.zeros_like(kc))], axis=0)
    c_end = _sub_iota((nc, 1)) * CMP_STRIDE + (CMP_LEN - 1)
    cv = c_end <= tpos
    vct = vct_ref[...]
    psum = [jnp.zeros((nc, tq), F32), jnp.zeros((nc, tq), F32)]
    sub = _sub_iota((LANES, 1))
    for i in range(A_HEADS // 2):
        qb = q_ref[:, i * LANES:(i + 1) * LANES]
        st = lax.dot_general(kst, qb, (((1,), (1,)), ((), ())), preferred_element_type=F32)
        ot = []
        for g in range(A_KV_GROUPS):
            s = jnp.where(cv, st[g * nc:(g + 1) * nc], NEG)
            m = jnp.max(s, axis=0, keepdims=True)
            e = jnp.where(cv, jnp.exp2(s - m), 0.0)
            den = jnp.sum(e, axis=0, keepdims=True)
            p = e / jnp.where(den > 0.0, den, 1.0)
            psum[g] = psum[g] + p
            ot.append(jnp.dot(vct, p.astype(BF16), preferred_element_type=F32))
        o_pair = jnp.where(sub < HEAD_DIM, ot[0], ot[1])
        o_ref[:, i * LANES:(i + 1) * LANES] = o_pair.T.astype(o_ref.dtype)
    jdx = _sub_iota((ns, 1))
    cur = tpos // SLC_LEN
    valid = jdx <= cur
    forced = (jdx == 0) | (jdx == cur) | (jdx == cur - 1)
    for g in range(A_KV_GROUPS):
        imp = jnp.dot(ovt_ref[...], psum[g], preferred_element_type=F32, precision=lax.Precision.HIGHEST)
        score = jnp.where(forced, BIG, jnp.where(valid, imp, NEG))
        sc_ref[...] = score
        slab = 8
        parts = [score[v * slab:(v + 1) * slab] for v in range(ns // slab)]
        cnts = [jnp.zeros((slab, tq), F32) for _ in parts]
        for jp in range(ns):
            row = sc_ref[jp:jp + 1, :]
            for v, sv in enumerate(parts):
                if v * slab > jp:
                    ahead = row >= sv
                elif (v + 1) * slab - 1 < jp:
                    ahead = row > sv
                else:
                    tie = jnp.where(jdx[v * slab:(v + 1) * slab] > jp, 1.0, 0.0)
                    cnts[v] = cnts[v] + jnp.where(row == sv, tie, 0.0)
                    ahead = row > sv
                cnts[v] = cnts[v] + jnp.where(ahead, 1.0, 0.0)
        cnt = jnp.concatenate(cnts, axis=0)
        bias = jnp.where(cnt < float(min(SLC_TOPK, ns)), 0.0, NEG)
        bias_ref[g * ns:(g + 1) * ns, :] = bias.astype(bias_ref.dtype)


def _cmp_select(u3, kc, vct, ovt):
    b, s, _ = u3.shape
    nc = kc.shape[1]
    ns = s // SLC_LEN
    assert A_KV_GROUPS * ns == LANES
    return pl.pallas_call(
        _cmp_select_kernel,
        grid=(b, s // CS_TQ),
        in_specs=[pl.BlockSpec((None, CS_TQ, A_WIDTH), lambda i, q: (i, q, U_AQ // A_WIDTH)),
                  pl.BlockSpec((None, nc, LANES), lambda i, q: (i, 0, 0)),
                  pl.BlockSpec((None, LANES, nc), lambda i, q: (i, 0, 0)),
                  pl.BlockSpec((ns, nc), lambda i, q: (0, 0))],
        out_specs=[pl.BlockSpec((None, CS_TQ, A_WIDTH), lambda i, q: (i, q, 0)),
                   pl.BlockSpec((None, LANES, CS_TQ), lambda i, q: (i, 0, q))],
        out_shape=[jax.ShapeDtypeStruct((b, s, A_WIDTH), BF16), jax.ShapeDtypeStruct((b, LANES, s), BF16)],
        scratch_shapes=[pltpu.VMEM((ns, CS_TQ), F32)],
        compiler_params=_cparams(2),
        name="nsa_cmp_select",
    )(u3, kc, vct, ovt)


FLASH_TK = 256
FLASH_CHUNK = 256
FLASH_UNROLL = 4
ONES_ROWS = 16
VT_ROWS = HEAD_DIM + ONES_ROWS


def _head_values_t(vt_pair):
    ones = jnp.ones((ONES_ROWS, vt_pair.shape[1]), vt_pair.dtype)
    return [jnp.concatenate([vt_pair[h * HEAD_DIM:(h + 1) * HEAD_DIM], ones], axis=0) for h in range(2)]


def _flash_reset(m_ref, acc_ref):
    m_ref[...] = jnp.full(m_ref.shape, NEG, F32)
    acc_ref[...] = jnp.zeros(acc_ref.shape, F32)


def _flash_update(st, vt, m_ref, acc_ref, idx, mask):
    if mask is not None:
        st = jnp.where(mask, st, NEG)
    m_old = m_ref[idx]
    m_new = jnp.maximum(m_old, jnp.max(st, axis=0, keepdims=True))
    alpha = jnp.exp2(m_old - m_new)
    p = jnp.exp2((st - m_new).astype(BF16))
    acc_ref[idx] = alpha * acc_ref[idx] + jnp.dot(vt, p, preferred_element_type=F32)
    m_ref[idx] = m_new


def _flash_result(acc_ref, idx):
    acc = acc_ref[idx]
    return acc[0:HEAD_DIM] / acc[HEAD_DIM:HEAD_DIM + 1]


def _flash_pipeline(n_pairs, n_chain, put, use):
    for c in range(n_chain):
        put(0, c, 0)

    def run(t0, count):
        for t in range(count):
            for c in range(n_chain):
                put(t0 + t + 1, c, (t + 1) % 2)
                use(t0 + t, c, t % 2)

    per = FLASH_UNROLL // 2
    n_long = n_pairs // per

    def long_body(j, carry):
        run(FLASH_UNROLL * j, FLASH_UNROLL)
        return carry

    def pair_body(j, carry):
        run(FLASH_UNROLL * n_long + 2 * j, 2)
        return carry

    lax.fori_loop(0, n_long, long_body, 0)
    lax.fori_loop(0, n_pairs - per * n_long, pair_body, 0)


def _transpose_bf16(x):
    return x.astype(F32).T.astype(BF16)


def _merge_rows(lo, hi):
    return jnp.where(_sub_iota((LANES, 1)) < HEAD_DIM, lo, hi)


SLC_TQ = 256


def _slc_kernel(q_ref, biast_ref, k_ref, v_ref, oh_ref, o_ref, vt_ref, qa_ref, st_ref, m_ref, acc_ref):
    tq, tk = SLC_TQ, FLASH_TK
    qi = pl.program_id(1)
    nblk = A_HEADS // 2
    sub = _sub_iota((LANES, 1))

    @pl.when(qi == 0)
    def _():
        for kt in range(vt_ref.shape[0]):
            for g, vt in enumerate(_head_values_t(_transpose_bf16(v_ref[kt * tk:(kt + 1) * tk, :]))):
                vt_ref[kt, g] = vt

    nchain = 2 * nblk
    biast = biast_ref[...]
    zero = jnp.zeros_like(biast)
    mine = (sub < HEAD_DIM, sub >= HEAD_DIM)
    for i in range(nblk):
        qt = _transpose_bf16(q_ref[:, i * LANES:(i + 1) * LANES])
        for h in range(2):
            qa_ref[2 * i + h] = jnp.concatenate([jnp.where(mine[h], qt, zero), jnp.where(mine[h], biast, zero)],
                                                axis=0)
    _flash_reset(m_ref, acc_ref)

    def put(kt, c, slot):
        start = pl.multiple_of(kt * tk, tk)
        k = jnp.concatenate([k_ref[pl.ds(start, tk), :], oh_ref[pl.ds(start, tk), :]], axis=1)
        st_ref[slot, c] = jnp.dot(k, qa_ref[c], preferred_element_type=F32)

    def use(kt, c, slot, mask=None):
        _flash_update(st_ref[slot, c], vt_ref[kt, c % 2], m_ref, acc_ref, c, mask)

    n_full = qi
    n_pairs = n_full // 2
    _flash_pipeline(n_pairs, nchain, put, use)
    causal = (n_full * tk + _sub_iota((tk, 1))) <= (qi * tq + _lane_iota((1, tq)))

    @pl.when(n_full == 2 * n_pairs)
    def _():
        for c in range(nchain):
            use(n_full, c, 0, causal)

    @pl.when(n_full != 2 * n_pairs)
    def _():
        for c in range(nchain):
            put(n_full, c, 1)
            use(n_full - 1, c, 0)
        for c in range(nchain):
            use(n_full, c, 1, causal)

    for i in range(nblk):
        ot = jnp.concatenate([_flash_result(acc_ref, 2 * i), _flash_result(acc_ref, 2 * i + 1)], axis=0)
        o_ref[:, i * LANES:(i + 1) * LANES] = ot.T.astype(o_ref.dtype)


def _slc_attention(u3, biast, onehot):
    b, s, _ = u3.shape
    nblk = A_HEADS
    assert SLC_TQ == FLASH_CHUNK == FLASH_TK
    return pl.pallas_call(
        _slc_kernel,
        grid=(b, s // SLC_TQ),
        in_specs=[pl.BlockSpec((None, SLC_TQ, A_WIDTH), lambda i, q: (i, q, U_AQ // A_WIDTH)),
                  pl.BlockSpec((None, LANES, SLC_TQ), lambda i, q: (i, 0, q)),
                  pl.BlockSpec((None, s, LANES), lambda i, q: (i, 0, U_KSLC // LANES)),
                  pl.BlockSpec((None, s, LANES), lambda i, q: (i, 0, U_VSLC // LANES)),
                  pl.BlockSpec((s, LANES), lambda i, q: (0, 0))],
        out_specs=pl.BlockSpec((None, SLC_TQ, A_WIDTH), lambda i, q: (i, q, 0)),
        out_shape=jax.ShapeDtypeStruct((b, s, A_WIDTH), BF16),
        scratch_shapes=[pltpu.VMEM((s // FLASH_TK, 2, VT_ROWS, FLASH_TK), BF16),
                        pltpu.VMEM((nblk, 2 * LANES, FLASH_CHUNK), BF16),
                        pltpu.VMEM((2, nblk, FLASH_TK, FLASH_CHUNK), F32),
                        pltpu.VMEM((nblk, 1, FLASH_CHUNK), F32),
                        pltpu.VMEM((nblk, VT_ROWS, FLASH_CHUNK), F32)],
        compiler_params=_cparams(2),
        name="nsa_selected",
    )(u3, biast, u3, u3, onehot)


BAND_T = 128
BAND_ROWS = 512
BAND_LOOKAHEAD = 2


def _banded_kernel(*refs, nprev, max_dist, shared_kv, with_lse):
    t = BAND_T
    q_ref, k_ref, v_ref, o_ref = refs[:4]
    lse_ref = refs[4] if with_lse else None
    qi = pl.program_id(1)
    nsub = q_ref.shape[0] // t
    nblk = q_ref.shape[1] // LANES
    nk = (nprev + 1) * t
    col = _lane_iota((1, 2 * t))
    base = jnp.where(col >= t, col - t, col) - _sub_iota((nk, 1))
    sub = _sub_iota((LANES, 1))

    def window(s):
        return pl.multiple_of(jnp.maximum(qi * nsub + s - nprev, 0) * t, t)

    def band_cap(dist):
        rel = base + dist
        return jnp.where((rel >= 0) & (rel <= max_dist), BIG, NEG)

    assert nsub >= nprev
    regular = band_cap(nprev * t)
    clamped = [band_cap(s * t) for s in range(nprev)]

    def cap(s):
        return jnp.where(qi == 0, clamped[s], regular) if s < nprev else regular

    def scores(s, i):
        start = window(s)
        cols = slice(None) if shared_kv else slice(i * LANES, (i + 1) * LANES)
        k = k_ref[pl.ds(start, nk), cols]
        qt = _transpose_bf16(q_ref[s * t:(s + 1) * t, i * LANES:(i + 1) * LANES])
        zero = jnp.zeros_like(qt)
        qs = jnp.concatenate([jnp.where(sub < HEAD_DIM, qt, zero), jnp.where(sub >= HEAD_DIM, qt, zero)], axis=1)
        return jnp.dot(k, qs, preferred_element_type=F32)

    vt_cache = {}

    def values_t(s, i):
        key = s if shared_kv else (s, i)
        if key not in vt_cache:
            start = window(s)
            cols = slice(None) if shared_kv else slice(i * LANES, (i + 1) * LANES)
            vt_cache[key] = jnp.concatenate(
                [_transpose_bf16(v_ref[pl.ds(start + j * t, t), cols]) for j in range(nprev + 1)], axis=1)
        return vt_cache[key]

    def finish(s, i, st):
        vt = values_t(s, i)
        st = jnp.minimum(st, cap(s))
        m = jnp.max(st, axis=0, keepdims=True)
        p = jnp.exp2(st - m)
        l = jnp.sum(p, axis=0, keepdims=True)
        ot = jnp.dot(vt, p.astype(BF16), preferred_element_type=F32) / l
        rows, sl = slice(s * t, (s + 1) * t), slice(i * LANES, (i + 1) * LANES)
        o_ref[rows, sl] = _merge_rows(ot[:, 0:t], ot[:, t:2 * t]).T.astype(o_ref.dtype)
        if with_lse:
            lse = m + jnp.log2(l)
            lse_ref[rows, sl] = _merge_rows(jnp.broadcast_to(lse[:, 0:t], (LANES, t)),
                                            jnp.broadcast_to(lse[:, t:2 * t], (LANES, t))).T

    items = [(s, i) for s in range(nsub) for i in range(nblk)]
    pending = [scores(*it) for it in items[:BAND_LOOKAHEAD]]
    for j, it in enumerate(items):
        if j + BAND_LOOKAHEAD < len(items):
            pending.append(scores(*items[j + BAND_LOOKAHEAD]))
        finish(*it, pending[j])


def _banded(src, *, nseq, seqlen, q_off, k_off, v_off, kv_width, max_dist, with_lse):
    t = BAND_T
    rows = min(BAND_ROWS, seqlen)
    nq = seqlen // rows
    nprev = -(-max_dist // t)
    shared = kv_width == LANES
    assert (nprev + 1) * t <= seqlen and seqlen % rows == 0 and rows % t == 0
    qo = pl.BlockSpec((rows, A_WIDTH), lambda n, q: (n * nq + q, q_off // A_WIDTH))
    out = pl.BlockSpec((rows, A_WIDTH), lambda n, q: (n * nq + q, 0))
    in_specs = [qo,
                pl.BlockSpec((seqlen, kv_width), lambda n, q: (n, k_off // kv_width)),
                pl.BlockSpec((seqlen, kv_width), lambda n, q: (n, v_off // kv_width))]
    out_specs = [out]
    out_shape = [jax.ShapeDtypeStruct((nseq * seqlen, A_WIDTH), BF16)]
    if with_lse:
        out_specs.append(out)
        out_shape.append(jax.ShapeDtypeStruct((nseq * seqlen, A_WIDTH), F32))
    return pl.pallas_call(
        functools.partial(_banded_kernel, nprev=nprev, max_dist=max_dist, shared_kv=shared, with_lse=with_lse),
        grid=(nseq, nq),
        in_specs=in_specs,
        out_specs=out_specs,
        out_shape=out_shape,
        compiler_params=_cparams(2),
        name="banded_attention",
    )(src, src, src)


MLA_TM = 512


def _rms(x, g):
    return x * lax.rsqrt(jnp.mean(x * x, axis=-1, keepdims=True) + EPS) * g


def _mla_prep_kernel(cq_ref, ckv_ref, kr_ref, c_ref, s1_ref, s2_ref, gq_ref, gkv_ref, wq_ref, wk_ref, wv_ref,
                     q_out, k_out, vt_out):
    half = QK_ROPE // 2
    tk = FLASH_TK
    cm, s1, s2 = c_ref[...], s1_ref[...], s2_ref[...]

    def rope(xc):
        return xc * cm + pltpu.roll(xc, half, 1) * s1 + pltpu.roll(xc, LANES - half, 1) * s2

    qn = _rms(cq_ref[...].astype(F32), gq_ref[...]).astype(BF16)
    q = jnp.dot(qn, wq_ref[...], preferred_element_type=F32)
    kvn = _rms(ckv_ref[...].astype(F32), gkv_ref[...]).astype(BF16)
    kk = jnp.dot(kvn, wk_ref[...], preferred_element_type=F32)
    kr = rope(kr_ref[...].astype(F32))
    for h in range(B_HEADS):
        sl = slice(h * LANES, (h + 1) * LANES)
        q_out[:, sl] = (rope(q[:, sl]) * MLA_SCALE).astype(q_out.dtype)
        k_out[:, sl] = (kk[:, sl] + kr).astype(k_out.dtype)
    v = jnp.dot(kvn, wv_ref[...], preferred_element_type=F32)
    for j in range(v.shape[0] // tk):
        for hp in range(B_HEADS // 2):
            pair_t = v[j * tk:(j + 1) * tk, hp * LANES:(hp + 1) * LANES].T.astype(vt_out.dtype)
            for h, vt in enumerate(_head_values_t(pair_t)):
                vt_out[j, hp, h] = vt


def _mla_prep(u, tabs, g_q, g_kv, wq, wk, wv, layer):
    t = u.shape[0]
    tm = min(MLA_TM, t)
    row = lambda i: (i, 0)
    wide = B_HEADS * LANES
    assert U_BCQ % Q_LORA == 0 and tm % FLASH_TK == 0
    return pl.pallas_call(
        _mla_prep_kernel,
        grid=(t // tm,),
        in_specs=[pl.BlockSpec((tm, Q_LORA), lambda i: (i, U_BCQ // Q_LORA)),
                  pl.BlockSpec((tm, LANES), lambda i: (i, U_BCKV // LANES)),
                  pl.BlockSpec((tm, LANES), lambda i: (i, U_BKR // LANES)),
                  pl.BlockSpec((tm, LANES), row), pl.BlockSpec((tm, LANES), row), pl.BlockSpec((tm, LANES), row),
                  pl.BlockSpec((None, 1, Q_LORA), lambda i: (layer, 0, 0)),
                  pl.BlockSpec((None, 1, KV_LORA), lambda i: (layer, 0, 0)),
                  pl.BlockSpec((None, Q_LORA, wide), lambda i: (layer, 0, 0)),
                  pl.BlockSpec((None, KV_LORA, wide), lambda i: (layer, 0, 0)),
                  pl.BlockSpec((None, KV_LORA, B_WIDTH), lambda i: (layer, 0, 0))],
        out_specs=[pl.BlockSpec((tm, wide), row), pl.BlockSpec((tm, wide), row),
                   pl.BlockSpec((tm // FLASH_TK, B_HEADS // 2, 2, VT_ROWS, FLASH_TK), lambda i: (i, 0, 0, 0, 0))],
        out_shape=[jax.ShapeDtypeStruct((t, wide), BF16), jax.ShapeDtypeStruct((t, wide), BF16),
                   jax.ShapeDtypeStruct((t // FLASH_TK, B_HEADS // 2, 2, VT_ROWS, FLASH_TK), BF16)],
        compiler_params=_cparams(1),
        name="mla_prep",
    )(u, u, u, *tabs, g_q, g_kv, wq, wk, wv)


MLA_TQ = 1024


def _mla_flash_kernel(q_ref, k_ref, vt_ref, o_ref, qt_ref, st_ref, m_ref, acc_ref):
    tq, tk, cw = MLA_TQ, FLASH_TK, FLASH_CHUNK
    nhalf = tq // cw
    qi = pl.program_id(2)
    for h in range(2):
        qt_ref[h] = _transpose_bf16(q_ref[:, h * LANES:(h + 1) * LANES])
    _flash_reset(m_ref, acc_ref)

    def put(kt, c, slot):
        h, half = divmod(c, nhalf)
        start = pl.multiple_of(kt * tk, tk)
        k = k_ref[pl.ds(start, tk), h * LANES:(h + 1) * LANES]
        st_ref[slot, c] = jnp.dot(k, qt_ref[h, :, half * cw:(half + 1) * cw], preferred_element_type=F32)

    def use(kt, c, slot, masked=False):
        mask = None
        if masked:
            qpos = qi * tq + (c % nhalf) * cw + _lane_iota((1, cw))
            mask = (kt * tk + _sub_iota((tk, 1))) <= qpos
        _flash_update(st_ref[slot, c], vt_ref[kt, c // nhalf], m_ref, acc_ref, c, mask)

    assert tk == cw and nhalf % 2 == 0
    n_full = nhalf * qi
    _flash_pipeline(n_full // 2, 2 * nhalf, put, use)
    live = lambda d: [c for c in range(2 * nhalf) if c % nhalf >= d]
    for d in range(nhalf):
        if d + 1 < nhalf:
            for c in live(d + 1):
                put(n_full + d + 1, c, (d + 1) % 2)
        for c in live(d):
            use(n_full + d, c, d % 2, masked=(c % nhalf == d))
    ots = []
    for h in range(2):
        ots.append(jnp.concatenate([_flash_result(acc_ref, h * nhalf + half) for half in range(nhalf)],
                                   axis=1))
    o_ref[...] = jnp.concatenate(ots, axis=0).T.astype(o_ref.dtype)


def _mla_flash(q, k, vt, batch, seq):
    assert FLASH_TK == FLASH_CHUNK
    q3 = q.reshape(batch, seq, -1)
    k3 = k.reshape(batch, seq, -1)
    vt6 = vt.reshape(batch, seq // FLASH_TK, B_HEADS // 2, 2, VT_ROWS, FLASH_TK)
    nchain = 2 * (MLA_TQ // FLASH_CHUNK)
    o = pl.pallas_call(
        _mla_flash_kernel,
        grid=(batch, B_HEADS // 2, seq // MLA_TQ),
        in_specs=[pl.BlockSpec((None, MLA_TQ, 2 * LANES), lambda b, h, i: (b, i, h)),
                  pl.BlockSpec((None, seq, 2 * LANES), lambda b, h, i: (b, 0, h)),
                  pl.BlockSpec((None, seq // FLASH_TK, None, 2, VT_ROWS, FLASH_TK),
                               lambda b, h, i: (b, 0, h, 0, 0, 0))],
        out_specs=pl.BlockSpec((None, MLA_TQ, LANES), lambda b, h, i: (b, i, h)),
        out_shape=jax.ShapeDtypeStruct((batch, seq, B_WIDTH), BF16),
        scratch_shapes=[pltpu.VMEM((2, LANES, MLA_TQ), BF16),
                        pltpu.VMEM((2, nchain, FLASH_TK, FLASH_CHUNK), F32),
                        pltpu.VMEM((nchain, 1, FLASH_CHUNK), F32),
                        pltpu.VMEM((nchain, VT_ROWS, FLASH_CHUNK), F32)],
        compiler_params=_cparams(3),
        name="mla_flash",
    )(q3, k3, vt6)
    return o.reshape(batch * seq, B_WIDTH)


OUT_TM = 256


def _out_kernel(x_ref, p_ref, ocmp_ref, oslc_ref, owin_ref, gate_ref, e_ref, az_ref, ob_ref, bz_ref,
                oc0_ref, oc1_ref, oc2_ref, l0_ref, l1_ref, l2_ref, cz_ref, mg0_ref, mg1_ref, mg2_ref,
                wa_ref, wb_ref, wc_ref, wout_ref, wplg_ref, wple_ref, gpost_ref, o_ref, *tok_refs):
    tok = iter(tok_refs)

    def sig(z):
        return 0.5 * jnp.tanh(0.5 * z) + 0.5

    def f(r):
        if len(r.shape) == 2:
            return r[...].astype(F32)
        dil, n, width = r.shape
        buf = next(tok)
        for res in range(dil):
            blk = r[res].astype(F32)
            for c in range(width // LANES):
                buf[c, pl.ds(res, n, stride=dil), :] = blk[:, c * LANES:(c + 1) * LANES]
        return jnp.concatenate([buf[c] for c in range(width // LANES)], axis=1)

    def silu(z):
        return z * sig(z)

    def mm(a, w_ref):
        return jnp.dot(a.astype(BF16), w_ref[...], preferred_element_type=F32)

    g = sig(gate_ref[...].astype(F32))
    gs = jnp.dot(g.astype(BF16), e_ref[...], preferred_element_type=F32)
    o_a = (gs[:, 0:A_WIDTH] * f(ocmp_ref) + gs[:, A_WIDTH:2 * A_WIDTH] * f(oslc_ref)
           + gs[:, 2 * A_WIDTH:3 * A_WIDTH] * f(owin_ref))
    y_a = mm(o_a * silu(f(az_ref)), wa_ref)
    y_b = mm(f(ob_ref) * silu(f(bz_ref)), wb_ref)
    l0, l1, l2 = f(l0_ref), f(l1_ref), f(l2_ref)
    mx = jnp.maximum(jnp.maximum(l0, l1), l2)
    e0, e1, e2 = jnp.exp2(l0 - mx), jnp.exp2(l1 - mx), jnp.exp2(l2 - mx)
    o_c = (e0 * f(oc0_ref) + e1 * f(oc1_ref) + e2 * f(oc2_ref)) / (e0 + e1 + e2)
    y_c = mm(o_c * silu(f(cz_ref)), wc_ref)
    mix = sig(f(mg0_ref)) * y_a + sig(f(mg1_ref)) * y_b + sig(f(mg2_ref)) * y_c
    y = mm(mix, wout_ref)
    x1 = x_ref[...] + _rms(y, gpost_ref[...])
    o_ref[...] = x1 + sig(mm(x1, wplg_ref)) * mm(p_ref[...], wple_ref)


def _out_layer(x2, p, layer, o_cmp, o_slc, o_win, u, expand, o_b, o_c, lse_c, w, batch, dils):
    t = x2.shape[0]
    seq = t // batch
    tm = min(OUT_TM, seq)
    nper = seq // tm
    row = lambda i: (i, 0)
    ucol = lambda off, width: pl.BlockSpec((tm, width), lambda i: (i, off // width))
    half = pl.BlockSpec((tm, A_WIDTH), row)
    wspec = lambda r, c: pl.BlockSpec((None, r, c), lambda i: (layer, 0, 0))

    def group_spec(dil):
        if dil == 1:
            return half
        return pl.BlockSpec((None, dil, tm // dil, C_WIDTH), lambda i: (i // nper, 0, i % nper, 0))

    def group_view(a, dil):
        return a if dil == 1 else a.reshape(batch, dil, seq // dil, C_WIDTH)

    gspecs = [group_spec(d) for d in dils]
    o_c = [group_view(a, d) for a, d in zip(o_c, dils)]
    lse_c = [group_view(a, d) for a, d in zip(lse_c, dils)]
    n_tok = 2 * sum(1 for d in dils if d > 1)
    in_specs = [pl.BlockSpec((tm, D_MODEL), row),
                pl.BlockSpec((None, tm, PLE_DIM), lambda i: (layer, i, 0)),
                half, half, half,
                ucol(U_GATE, LANES), pl.BlockSpec((LANES, 3 * A_WIDTH), lambda i: (0, 0)),
                ucol(U_AZ, A_WIDTH), half, ucol(U_BZ, B_WIDTH),
                *gspecs, *gspecs, ucol(U_CZ, C_WIDTH),
                ucol(U_MG, D_MODEL), ucol(U_MG + D_MODEL, D_MODEL), ucol(U_MG + 2 * D_MODEL, D_MODEL),
                wspec(A_WIDTH, D_MODEL), wspec(B_WIDTH, D_MODEL), wspec(C_WIDTH, D_MODEL),
                wspec(D_MODEL, D_MODEL), wspec(D_MODEL, D_MODEL), wspec(PLE_DIM, D_MODEL),
                wspec(1, D_MODEL)]
    return pl.pallas_call(
        _out_kernel,
        grid=(t // tm,),
        in_specs=in_specs,
        out_specs=pl.BlockSpec((tm, D_MODEL), row),
        out_shape=jax.ShapeDtypeStruct((t, D_MODEL), F32),
        scratch_shapes=[pltpu.VMEM((C_WIDTH // LANES, tm, LANES), F32)] * n_tok,
        compiler_params=_cparams(1),
        name="out_layer",
    )(x2, p, o_cmp, o_slc, o_win, u, expand, u, o_b, u, *o_c, *lse_c, u, u, u, u,
      w["w_a"], w["w_b"], w["w_c"], w["w_out"], w["w_plg"], w["w_ple"], w["g_post"])


def _overlap_t(seq):
    nc = seq // CMP_STRIDE
    ns = seq // SLC_LEN
    cs = np.arange(nc)[:, None] * CMP_STRIDE
    js = np.arange(ns)[None, :] * SLC_LEN
    ov = np.clip(np.minimum(cs + CMP_LEN, js + SLC_LEN) - np.maximum(cs, js), 0, None).astype(np.float32) / CMP_LEN
    return jnp.asarray(ov.T)


def _block_onehot(seq):
    blk = np.arange(seq)[:, None] // SLC_LEN
    return jnp.asarray((np.arange(LANES)[None, :] % HEAD_DIM == blk).astype(np.float32), dtype=BF16)


def _gate_expand():
    e = np.zeros((LANES, 3 * A_WIDTH), np.float32)
    for br in range(3):
        for i in range(A_HEADS // 2):
            for g in range(A_KV_GROUPS):
                h = g * (A_HEADS // 2) + i
                c0 = br * A_WIDTH + i * LANES + g * HEAD_DIM
                e[h * 3 + br, c0:c0 + HEAD_DIM] = 1.0
    return jnp.asarray(e, dtype=BF16)


def kernel(x, p, positions, g_pre, g_post, w_in, nsa_cmp_pe, nsa_cmp_w1, nsa_cmp_w2, w_a, mla_g_q, mla_w_uq,
           mla_g_kv, mla_w_ukv, w_b, w_c, w_out, w_ple, w_plg):
    batch, seq, _ = x.shape
    depth = w_in.shape[0]
    t = batch * seq
    nch = seq // CMP_STRIDE

    pos = positions.astype(F32)
    part_tabs = _rope_tables(pos.reshape(-1), ROT_DIM, HEAD_DIM, 0)
    mla_tabs = _rope_tables(pos.reshape(-1), QK_ROPE, LANES, QK_NOPE)
    dils = [dil for _, dil in C_PAIRS]
    assert dils[0] == 1
    perm_tabs = {dil: _rope_tables(_residue_major_positions(pos, dil), ROT_DIM, HEAD_DIM, 0) for dil in dils[1:]}
    w_main, w_groups = _regroup_w_in(w_in)
    w_main = _tile_cols(w_main, IN_TN)
    w_groups = [_tile_cols(w, DG_COLS) for w in w_groups]
    g_pre3 = g_pre.reshape(depth, 1, D_MODEL)
    pe2, w1d, w2d = _compress_weights(nsa_cmp_pe, nsa_cmp_w1, nsa_cmp_w2)
    wq = mla_w_uq.reshape(depth, Q_LORA, B_HEADS, QK_NOPE + QK_ROPE)
    wq = jnp.pad(wq, ((0, 0), (0, 0), (0, 0), (0, LANES - QK_NOPE - QK_ROPE))).reshape(depth, Q_LORA, -1).astype(BF16)
    wkv = mla_w_ukv.reshape(depth, KV_LORA, B_HEADS, QK_NOPE + V_DIM)
    wk = jnp.pad(wkv[..., :QK_NOPE], ((0, 0), (0, 0), (0, 0), (0, LANES - QK_NOPE))).reshape(depth, KV_LORA, -1)
    wk = wk.astype(BF16)
    wv = wkv[..., QK_NOPE:].reshape(depth, KV_LORA, B_WIDTH).astype(BF16)
    gq3 = mla_g_q.reshape(depth, 1, Q_LORA)
    gkv3 = mla_g_kv.reshape(depth, 1, KV_LORA)
    w_a_p = w_a.reshape(depth, 2, 4, HEAD_DIM, D_MODEL).swapaxes(1, 2).reshape(depth, A_WIDTH, D_MODEL)
    wts = {"w_a": w_a_p.astype(BF16), "w_b": w_b.astype(BF16), "w_c": w_c.astype(BF16),
           "w_out": w_out.astype(BF16), "w_plg": w_plg.astype(BF16), "w_ple": w_ple.astype(BF16),
           "g_post": g_post.reshape(depth, 1, D_MODEL)}
    p3 = p.reshape(depth, t, PLE_DIM)
    ovt = _overlap_t(seq)
    onehot = _block_onehot(seq)
    expand = _gate_expand()

    x2 = x.reshape(t, D_MODEL)
    for layer in range(depth):
        h, h_res = _prenorm(x2, g_pre3, layer, batch, dils[1:])
        u = _inproj(h, w_main, layer, part_tabs, ROPE_COLS)
        u3 = u.reshape(batch, seq, DP)
        kc, vct = _compress(u3, pe2, w1d, w2d, layer)
        o_cmp, biast = _cmp_select(u3, kc, vct, ovt)
        o_slc = _slc_attention(u3, biast, onehot)
        (o_win,) = _banded(u, nseq=batch, seqlen=seq, q_off=U_AQ, k_off=U_KWIN, v_off=U_VWIN,
                           kv_width=LANES, max_dist=WIN - 1, with_lse=False)
        q_b, k_b, vt_b = _mla_prep(u, mla_tabs, gq3, gkv3, wq, wk, wv, layer)
        o_b = _mla_flash(q_b, k_b, vt_b, batch, seq)
        o_c, lse_c = [], []
        for gi, (window, dil) in enumerate(C_PAIRS):
            if dil == 1:
                o_g, lse_g = _banded(u, nseq=batch, seqlen=seq, q_off=U_CQ0, k_off=U_CK0, v_off=U_CV0,
                                     kv_width=C_WIDTH, max_dist=window, with_lse=True)
            else:
                u_g = _inproj(h_res[gi - 1], w_groups[gi - 1], layer, perm_tabs[dil], DG_ROPE_COLS)
                o_g, lse_g = _banded(u_g, nseq=batch * dil, seqlen=seq // dil, q_off=0, k_off=C_WIDTH,
                                     v_off=2 * C_WIDTH, kv_width=C_WIDTH, max_dist=window // dil, with_lse=True)
            o_c.append(o_g)
            lse_c.append(lse_g)
        x2 = _out_layer(x2, p3, layer, o_cmp.reshape(t, A_WIDTH), o_slc.reshape(t, A_WIDTH), o_win, u, expand,
                        o_b, o_c, lse_c, wts, batch, dils)
    return x2.reshape(batch, seq, D_MODEL)
```

```python
import functools

import numpy as np
import jax
import jax.numpy as jnp
from jax import lax
from jax.experimental import pallas as pl
from jax.experimental.pallas import tpu as pltpu

F32 = jnp.float32
BF16 = jnp.bfloat16

D_MODEL = 1024
PLE_DIM = 256
ROPE_THETA = 500000.0
HEAD_DIM = 64
ROT_DIM = HEAD_DIM // 4
EPS = 1e-6
NEG = -1e30
BIG = 1e30
LANES = 128

A_HEADS = 8
A_KV_GROUPS = 2
A_WIDTH = A_HEADS * HEAD_DIM
CMP_LEN = 32
CMP_STRIDE = 16
CMP_HID = 256
SLC_LEN = 64
SLC_TOPK = 16
WIN = 512

B_HEADS = 8
Q_LORA = 384
KV_LORA = 128
QK_NOPE = 64
QK_ROPE = 32
V_DIM = 64
B_WIDTH = B_HEADS * V_DIM

C_PAIRS = ((128, 1), (512, 4), (2048, 16))
C_GROUPS = len(C_PAIRS)
C_HEADS = 8
C_WIDTH = C_HEADS * HEAD_DIM

IN_SIZES = (A_WIDTH, 6 * A_KV_GROUPS * HEAD_DIM, 3 * A_HEADS, A_WIDTH, Q_LORA, KV_LORA, QK_ROPE,
            B_WIDTH, C_GROUPS * 3 * C_WIDTH, C_WIDTH, 3 * D_MODEL)
IN_OFF = tuple(int(v) for v in np.cumsum((0,) + IN_SIZES))
(OFF_AQ, OFF_AKV, OFF_AG, OFF_AZ, OFF_BCQ, OFF_BCKV, OFF_BKR, OFF_BZ, OFF_CQKV, OFF_CZ, OFF_MG) = IN_OFF[:-1]

U_AQ = 0
U_CQ0 = 512
U_CK0 = 1024
U_KCMP = 1536
U_KSLC = 1664
U_KWIN = 1792
ROPE_COLS = 1920
U_VCMP = 1920
U_AZ = 2048
U_BZ = 2560
U_CV0 = 3072
U_CZ = 3584
U_MG = 4096
U_VSLC = 7168
U_VWIN = 7296
U_GATE = 7424
U_BCKV = 7552
U_BCQ = 7680
U_BKR = 8064
DP = 8192
DG_COLS = 3 * C_WIDTH
DG_ROPE_COLS = 2 * C_WIDTH

LOG2E = float(np.log2(np.e))
Q_SCALE = float(HEAD_DIM ** -0.5) * LOG2E
MLA_SCALE = float((QK_NOPE + QK_ROPE) ** -0.5) * LOG2E
VMEM_LIMIT = 48 * 1024 * 1024


def _cparams(n_axes):
    return pltpu.CompilerParams(dimension_semantics=("arbitrary",) * n_axes, vmem_limit_bytes=VMEM_LIMIT)


def _lane_iota(shape):
    return lax.broadcasted_iota(jnp.int32, shape, len(shape) - 1)


def _sub_iota(shape):
    return lax.broadcasted_iota(jnp.int32, shape, len(shape) - 2)


def _pair_cols(w):
    lead = w.shape[:-1]
    return w.reshape(lead + (2, 4, HEAD_DIM)).swapaxes(-3, -2).reshape(lead + (A_WIDTH,))


def _regroup_w_in(w_in):
    def col(off, n):
        return w_in[..., off:off + n]
    z = lambda n: jnp.zeros(w_in.shape[:-1] + (n,), w_in.dtype)
    akv = lambda which: col(OFF_AKV + which * 128, 128)
    cq = lambda gi, t: col(OFF_CQKV + gi * 3 * C_WIDTH + t * C_WIDTH, C_WIDTH)
    pieces = [_pair_cols(col(OFF_AQ, A_WIDTH)) * Q_SCALE, cq(0, 0) * Q_SCALE, cq(0, 1)]
    pieces += [akv(0), akv(2), akv(4)]
    pieces += [akv(1)]
    pieces += [_pair_cols(col(OFF_AZ, A_WIDTH)), col(OFF_BZ, B_WIDTH), cq(0, 2)]
    pieces += [col(OFF_CZ, C_WIDTH), col(OFF_MG, 3 * D_MODEL)]
    pieces += [akv(3), akv(5)]
    pieces += [col(OFF_AG, 3 * A_HEADS), z(128 - 3 * A_HEADS)]
    pieces += [col(OFF_BCKV, KV_LORA), col(OFF_BCQ, Q_LORA)]
    pieces += [z(64), col(OFF_BKR, QK_ROPE), z(32)]
    main = jnp.concatenate(pieces, axis=-1).astype(BF16)
    assert main.shape[-1] == DP
    groups = [jnp.concatenate([cq(gi, 0) * Q_SCALE, cq(gi, 1), cq(gi, 2)], axis=-1).astype(BF16)
              for gi in range(1, C_GROUPS)]
    return main, groups


def _rope_tables(pos, dim, period, offset):
    half = dim // 2
    ch = np.arange(LANES) % period - offset
    in_span = (ch >= 0) & (ch < dim)
    freq = np.where(in_span, ch % half, 0).astype(np.float32)
    inv = jnp.where(jnp.asarray(in_span), ROPE_THETA ** (-2.0 * jnp.asarray(freq) / dim), 0.0)
    ang = pos[:, None] * inv[None, :]
    cos, sin = jnp.cos(ang), jnp.sin(ang)
    first = jnp.asarray(in_span & (ch < half))[None, :]
    second = jnp.asarray(in_span & (ch >= half))[None, :]
    return cos, jnp.where(second, sin, 0.0), jnp.where(first, -sin, 0.0)


def _residue_major_positions(positions, dil):
    b, s = positions.shape
    return positions.reshape(b, s // dil, dil).swapaxes(1, 2).reshape(-1)


NORM_TM = 512
IN_TM = 1024
IN_TN = 2048
IN_CHUNK = 512


def _norm_kernel(x_ref, g_ref, o_ref, *rest, dils):
    o_dil, y_ref = rest[:len(dils)], rest[len(dils)]
    x = x_ref[...]
    ms = jnp.mean(x * x, axis=-1, keepdims=True)
    y = x * lax.rsqrt(ms + EPS) * g_ref[...]
    o_ref[...] = y.astype(o_ref.dtype)
    tm = x.shape[0]
    nlane = y_ref.shape[0]
    for c in range(nlane):
        y_ref[c] = y[:, c * LANES:(c + 1) * LANES]
    for o_d, dil in zip(o_dil, dils):
        for r in range(dil):
            for c in range(nlane):
                o_d[r, :, c * LANES:(c + 1) * LANES] = (
                    y_ref[c, pl.ds(r, tm // dil, stride=dil), :].astype(o_d.dtype))


def _prenorm(x2, g_pre, layer, batch, dils):
    t = x2.shape[0]
    seq = t // batch
    tm = min(NORM_TM, seq)
    nper = seq // tm
    out_specs = [pl.BlockSpec((tm, D_MODEL), lambda b, i: (b * nper + i, 0))]
    out_shape = [jax.ShapeDtypeStruct((t, D_MODEL), BF16)]
    for dil in dils:
        out_specs.append(pl.BlockSpec((None, dil, tm // dil, D_MODEL), lambda b, i: (b, 0, i, 0)))
        out_shape.append(jax.ShapeDtypeStruct((batch, dil, seq // dil, D_MODEL), BF16))
    outs = pl.pallas_call(
        functools.partial(_norm_kernel, dils=tuple(dils)),
        grid=(batch, nper),
        in_specs=[pl.BlockSpec((tm, D_MODEL), lambda b, i: (b * nper + i, 0)),
                  pl.BlockSpec((None, 1, D_MODEL), lambda b, i: (layer, 0, 0))],
        out_specs=out_specs,
        out_shape=out_shape,
        scratch_shapes=[pltpu.VMEM((D_MODEL // LANES, tm, LANES), F32)],
        compiler_params=_cparams(2),
        name="prenorm",
    )(x2, g_pre)
    return outs[0], [o.reshape(t, D_MODEL) for o in outs[1:]]


def _inproj_kernel(h_ref, w_ref, c_ref, s1_ref, s2_ref, o_ref, *, rope_cols):
    j = pl.program_id(0)
    tn = w_ref.shape[1]
    nsub = tn // LANES
    full_tiles, rem = divmod(rope_cols // LANES, nsub)
    chunk = IN_CHUNK if tn % IN_CHUNK == 0 else tn
    per = chunk // LANES

    def emit(n_rope):
        for g in range(tn // chunk):
            acc = jnp.dot(h_ref[...], w_ref[:, g * chunk:(g + 1) * chunk], preferred_element_type=F32)
            for k in range(per):
                c = g * per + k
                xc = acc[:, k * LANES:(k + 1) * LANES]
                if c < n_rope:
                    xc = (xc * c_ref[...] + pltpu.roll(xc, ROT_DIM // 2, 1) * s1_ref[...]
                          + pltpu.roll(xc, LANES - ROT_DIM // 2, 1) * s2_ref[...])
                o_ref[:, c * LANES:(c + 1) * LANES] = xc.astype(o_ref.dtype)

    pl.when(j < full_tiles)(lambda: emit(nsub))
    pl.when(j == full_tiles)(lambda: emit(rem))
    pl.when(j > full_tiles)(lambda: emit(0))


def _inproj(h, w, layer, tabs, rope_cols):
    t = h.shape[0]
    ncols = w.shape[-1]
    tn = min(IN_TN, ncols)
    ntile = ncols // tn
    tm = min(IN_TM, t)
    rope_tiles = -(-rope_cols // tn)
    row = lambda j, i: (i, 0)
    tab = lambda j, i: (jnp.where(j < rope_tiles, i, 0), 0)
    return pl.pallas_call(
        functools.partial(_inproj_kernel, rope_cols=rope_cols),
        grid=(ntile, t // tm),
        in_specs=[pl.BlockSpec((tm, D_MODEL), row),
                  pl.BlockSpec((None, D_MODEL, tn), lambda j, i: (layer, 0, j)),
                  pl.BlockSpec((tm, LANES), tab), pl.BlockSpec((tm, LANES), tab), pl.BlockSpec((tm, LANES), tab)],
        out_specs=pl.BlockSpec((tm, tn), lambda j, i: (i, j)),
        out_shape=jax.ShapeDtypeStruct((t, ntile * tn), BF16),
        compiler_params=_cparams(2),
        name="inproj",
    )(h, w, *tabs)


def _compress_kernel(k_ref, v_ref, pe_ref, w1_ref, w2_ref, kc_ref, vct_ref, x_ref):
    nch = x_ref.shape[0] // CMP_STRIDE
    for which, src in enumerate((k_ref, v_ref)):
        x_ref[...] = src[...].astype(F32)
        rows = [x_ref[pl.ds(l, nch, stride=CMP_STRIDE), :] for l in range(CMP_STRIDE)]
        halves = []
        for part in range(CMP_LEN // CMP_STRIDE):
            xs = [(rows[l] + pe_ref[which, part * CMP_STRIDE + l:part * CMP_STRIDE + l + 1, :]).astype(BF16)
                  for l in range(CMP_STRIDE)]
            halves.append(jnp.dot(jnp.concatenate(xs, axis=1), w1_ref[which, part], preferred_element_type=F32))
        hid = halves[0] + pltpu.roll(halves[1], nch - 1, 0)
        act = hid * jax.nn.sigmoid(hid)
        tok = jnp.dot(act.astype(BF16), w2_ref[which], preferred_element_type=F32)
        if which == 0:
            kc_ref[...] = tok.astype(kc_ref.dtype)
        else:
            vct_ref[...] = tok.T.astype(vct_ref.dtype)


def _compress_weights(pe, w1, w2):
    depth = w1.shape[0]
    g = A_KV_GROUPS
    assert CMP_LEN == 2 * CMP_STRIDE and g == 2
    w1r = w1.reshape(depth, 2, CMP_LEN, HEAD_DIM, CMP_HID)
    z1 = jnp.zeros_like(w1r)
    w1d = jnp.stack([jnp.concatenate([w1r, z1], axis=-1), jnp.concatenate([z1, w1r], axis=-1)], axis=3)
    w1d = w1d.reshape(depth, 2, 2, CMP_STRIDE * g * HEAD_DIM, g * CMP_HID).astype(BF16)
    z2 = jnp.zeros_like(w2)
    w2d = jnp.concatenate([jnp.concatenate([w2, z2], axis=-1), jnp.concatenate([z2, w2], axis=-1)], axis=2)
    pe2 = jnp.concatenate([pe, pe], axis=-1)
    return pe2, w1d, w2d.astype(BF16)


def _compress(u3, pe2, w1d, w2d, layer):
    b, s, _ = u3.shape
    nch = s // CMP_STRIDE
    slab = lambda off: pl.BlockSpec((None, s, LANES), lambda i: (i, 0, off // LANES))
    return pl.pallas_call(
        _compress_kernel,
        grid=(b,),
        in_specs=[slab(U_KCMP), slab(U_VCMP),
                  pl.BlockSpec((None,) + pe2.shape[1:], lambda i: (layer, 0, 0, 0)),
                  pl.BlockSpec((None,) + w1d.shape[1:], lambda i: (layer, 0, 0, 0, 0)),
                  pl.BlockSpec((None,) + w2d.shape[1:], lambda i: (layer, 0, 0, 0))],
        out_specs=[pl.BlockSpec((None, nch, LANES), lambda i: (i, 0, 0)),
                   pl.BlockSpec((None, LANES, nch), lambda i: (i, 0, 0))],
        out_shape=[jax.ShapeDtypeStruct((b, nch, LANES), BF16), jax.ShapeDtypeStruct((b, LANES, nch), BF16)],
        scratch_shapes=[pltpu.VMEM((s, LANES), F32)],
        compiler_params=_cparams(1),
        name="nsa_compress",
    )(u3, u3, pe2, w1d, w2d)


CS_TQ = 128


def _cmp_select_kernel(q_ref, kc_ref, vct_ref, ovt_ref, o_ref, bias_ref, sc_ref):
    tq = CS_TQ
    qi = pl.program_id(1)
    nc = kc_ref.shape[0]
    ns = ovt_ref.shape[0]
    tpos = qi * tq + _lane_iota((1, tq))
    kc = kc_ref[...]
    lane = _lane_iota((1, LANES))
    kst = jnp.concatenate([jnp.where(lane < HEAD_DIM, kc, jnp.zeros_like(kc)),
                           jnp.where(lane >= HEAD_DIM, kc, jnp.zeros_like(kc))], axis=0)
    c_end = _sub_iota((nc, 1)) * CMP_STRIDE + (CMP_LEN - 1)
    cv = c_end <= tpos
    vct = vct_ref[...]
    psum = [jnp.zeros((nc, tq), F32), jnp.zeros((nc, tq), F32)]
    sub = _sub_iota((LANES, 1))
    for i in range(A_HEADS // 2):
        qb = q_ref[:, i * LANES:(i + 1) * LANES]
        st = lax.dot_general(kst, qb, (((1,), (1,)), ((), ())), preferred_element_type=F32)
        ot = []
        for g in range(A_KV_GROUPS):
            s = jnp.where(cv, st[g * nc:(g + 1) * nc], NEG)
            m = jnp.max(s, axis=0, keepdims=True)
            e = jnp.where(cv, jnp.exp2(s - m), 0.0)
            den = jnp.sum(e, axis=0, keepdims=True)
            p = e / jnp.where(den > 0.0, den, 1.0)
            psum[g] = psum[g] + p
            ot.append(jnp.dot(vct, p.astype(BF16), preferred_element_type=F32))
        o_pair = jnp.where(sub < HEAD_DIM, ot[0], ot[1])
        o_ref[:, i * LANES:(i + 1) * LANES] = o_pair.T.astype(o_ref.dtype)
    jdx = _sub_iota((ns, 1))
    cur = tpos // SLC_LEN
    valid = jdx <= cur
    forced = (jdx == 0) | (jdx == cur) | (jdx == cur - 1)
    for g in range(A_KV_GROUPS):
        imp = jnp.dot(ovt_ref[...], psum[g], preferred_element_type=F32, precision=lax.Precision.HIGHEST)
        score = jnp.where(forced, BIG, jnp.where(valid, imp, NEG))
        sc_ref[...] = score
        slab = 8
        parts = [score[v * slab:(v + 1) * slab] for v in range(ns // slab)]
        cnts = [jnp.zeros((slab, tq), F32) for _ in parts]
        for jp in range(ns):
            row = sc_ref[jp:jp + 1, :]
            for v, sv in enumerate(parts):
                if v * slab > jp:
                    ahead = row >= sv
                elif (v + 1) * slab - 1 < jp:
                    ahead = row > sv
                else:
                    tie = jnp.where(jdx[v * slab:(v + 1) * slab] > jp, 1.0, 0.0)
                    cnts[v] = cnts[v] + jnp.where(row == sv, tie, 0.0)
                    ahead = row > sv
                cnts[v] = cnts[v] + jnp.where(ahead, 1.0, 0.0)
        cnt = jnp.concatenate(cnts, axis=0)
        bias = jnp.where(cnt < float(min(SLC_TOPK, ns)), 0.0, NEG)
        bias_ref[g * ns:(g + 1) * ns, :] = bias.astype(bias_ref.dtype)


def _cmp_select(u3, kc, vct, ovt):
    b, s, _ = u3.shape
    nc = kc.shape[1]
    ns = s // SLC_LEN
    assert A_KV_GROUPS * ns == LANES
    return pl.pallas_call(
        _cmp_select_kernel,
        grid=(b, s // CS_TQ),
        in_specs=[pl.BlockSpec((None, CS_TQ, A_WIDTH), lambda i, q: (i, q, U_AQ // A_WIDTH)),
                  pl.BlockSpec((None, nc, LANES), lambda i, q: (i, 0, 0)),
                  pl.BlockSpec((None, LANES, nc), lambda i, q: (i, 0, 0)),
                  pl.BlockSpec((ns, nc), lambda i, q: (0, 0))],
        out_specs=[pl.BlockSpec((None, CS_TQ, A_WIDTH), lambda i, q: (i, q, 0)),
                   pl.BlockSpec((None, LANES, CS_TQ), lambda i, q: (i, 0, q))],
        out_shape=[jax.ShapeDtypeStruct((b, s, A_WIDTH), BF16), jax.ShapeDtypeStruct((b, LANES, s), BF16)],
        scratch_shapes=[pltpu.VMEM((ns, CS_TQ), F32)],
        compiler_params=_cparams(2),
        name="nsa_cmp_select",
    )(u3, kc, vct, ovt)


FLASH_TK = 256
FLASH_CHUNK = 256
FLASH_UNROLL = 4
ONES_ROWS = 16
VT_ROWS = HEAD_DIM + ONES_ROWS


def _head_values_t(vt_pair):
    ones = jnp.ones((ONES_ROWS, vt_pair.shape[1]), vt_pair.dtype)
    return [jnp.concatenate([vt_pair[h * HEAD_DIM:(h + 1) * HEAD_DIM], ones], axis=0) for h in range(2)]


def _flash_reset(m_ref, acc_ref):
    m_ref[...] = jnp.full(m_ref.shape, NEG, F32)
    acc_ref[...] = jnp.zeros(acc_ref.shape, F32)


def _flash_update(st, vt, m_ref, acc_ref, idx, mask):
    if mask is not None:
        st = jnp.where(mask, st, NEG)
    m_old = m_ref[idx]
    m_new = jnp.maximum(m_old, jnp.max(st, axis=0, keepdims=True))
    alpha = jnp.exp2(m_old - m_new)
    p = jnp.exp2((st - m_new).astype(BF16))
    acc_ref[idx] = alpha * acc_ref[idx] + jnp.dot(vt, p, preferred_element_type=F32)
    m_ref[idx] = m_new


def _flash_result(acc_ref, idx):
    acc = acc_ref[idx]
    return acc[0:HEAD_DIM] / acc[HEAD_DIM:HEAD_DIM + 1]


def _flash_pipeline(n_pairs, n_chain, put, use):
    for c in range(n_chain):
        put(0, c, 0)

    def run(t0, count):
        for t in range(count):
            for c in range(n_chain):
                put(t0 + t + 1, c, (t + 1) % 2)
                use(t0 + t, c, t % 2)

    per = FLASH_UNROLL // 2
    n_long = n_pairs // per

    def long_body(j, carry):
        run(FLASH_UNROLL * j, FLASH_UNROLL)
        return carry

    def pair_body(j, carry):
        run(FLASH_UNROLL * n_long + 2 * j, 2)
        return carry

    lax.fori_loop(0, n_long, long_body, 0)
    lax.fori_loop(0, n_pairs - per * n_long, pair_body, 0)


def _transpose_bf16(x):
    return x.astype(F32).T.astype(BF16)


def _merge_rows(lo, hi):
    return jnp.where(_sub_iota((LANES, 1)) < HEAD_DIM, lo, hi)


SLC_TQ = 256


def _slc_kernel(q_ref, biast_ref, k_ref, v_ref, oh_ref, o_ref, vt_ref, qa_ref, st_ref, m_ref, acc_ref):
    tq, tk = SLC_TQ, FLASH_TK
    qi = pl.program_id(1)
    nblk = A_HEADS // 2
    sub = _sub_iota((LANES, 1))

    @pl.when(qi == 0)
    def _():
        for kt in range(vt_ref.shape[0]):
            for g, vt in enumerate(_head_values_t(_transpose_bf16(v_ref[kt * tk:(kt + 1) * tk, :]))):
                vt_ref[kt, g] = vt

    nchain = 2 * nblk
    biast = biast_ref[...]
    zero = jnp.zeros_like(biast)
    mine = (sub < HEAD_DIM, sub >= HEAD_DIM)
    for i in range(nblk):
        qt = _transpose_bf16(q_ref[:, i * LANES:(i + 1) * LANES])
        for h in range(2):
            qa_ref[2 * i + h] = jnp.concatenate([jnp.where(mine[h], qt, zero), jnp.where(mine[h], biast, zero)],
                                                axis=0)
    _flash_reset(m_ref, acc_ref)

    def put(kt, c, slot):
        start = pl.multiple_of(kt * tk, tk)
        k = jnp.concatenate([k_ref[pl.ds(start, tk), :], oh_ref[pl.ds(start, tk), :]], axis=1)
        st_ref[slot, c] = jnp.dot(k, qa_ref[c], preferred_element_type=F32)

    def use(kt, c, slot, mask=None):
        _flash_update(st_ref[slot, c], vt_ref[kt, c % 2], m_ref, acc_ref, c, mask)

    n_full = qi
    n_pairs = n_full // 2
    _flash_pipeline(n_pairs, nchain, put, use)
    causal = (n_full * tk + _sub_iota((tk, 1))) <= (qi * tq + _lane_iota((1, tq)))

    @pl.when(n_full == 2 * n_pairs)
    def _():
        for c in range(nchain):
            use(n_full, c, 0, causal)

    @pl.when(n_full != 2 * n_pairs)
    def _():
        for c in range(nchain):
            put(n_full, c, 1)
            use(n_full - 1, c, 0)
        for c in range(nchain):
            use(n_full, c, 1, causal)

    for i in range(nblk):
        ot = jnp.concatenate([_flash_result(acc_ref, 2 * i), _flash_result(acc_ref, 2 * i + 1)], axis=0)
        o_ref[:, i * LANES:(i + 1) * LANES] = ot.T.astype(o_ref.dtype)


def _slc_attention(u3, biast, onehot):
    b, s, _ = u3.shape
    nblk = A_HEADS
    assert SLC_TQ == FLASH_CHUNK == FLASH_TK
    return pl.pallas_call(
        _slc_kernel,
        grid=(b, s // SLC_TQ),
        in_specs=[pl.BlockSpec((None, SLC_TQ, A_WIDTH), lambda i, q: (i, q, U_AQ // A_WIDTH)),
                  pl.BlockSpec((None, LANES, SLC_TQ), lambda i, q: (i, 0, q)),
                  pl.BlockSpec((None, s, LANES), lambda i, q: (i, 0, U_KSLC // LANES)),
                  pl.BlockSpec((None, s, LANES), lambda i, q: (i, 0, U_VSLC // LANES)),
                  pl.BlockSpec((s, LANES), lambda i, q: (0, 0))],
        out_specs=pl.BlockSpec((None, SLC_TQ, A_WIDTH), lambda i, q: (i, q, 0)),
        out_shape=jax.ShapeDtypeStruct((b, s, A_WIDTH), BF16),
        scratch_shapes=[pltpu.VMEM((s // FLASH_TK, 2, VT_ROWS, FLASH_TK), BF16),
                        pltpu.VMEM((nblk, 2 * LANES, FLASH_CHUNK), BF16),
                        pltpu.VMEM((2, nblk, FLASH_TK, FLASH_CHUNK), F32),
                        pltpu.VMEM((nblk, 1, FLASH_CHUNK), F32),
                        pltpu.VMEM((nblk, VT_ROWS, FLASH_CHUNK), F32)],
        compiler_params=_cparams(2),
        name="nsa_selected",
    )(u3, biast, u3, u3, onehot)


BAND_T = 128
BAND_ROWS = 1024
BAND_LOOKAHEAD = 2


def _banded_kernel(*refs, nprev, max_dist, shared_kv, with_lse):
    t = BAND_T
    q_ref, k_ref, v_ref, o_ref = refs[:4]
    lse_ref = refs[4] if with_lse else None
    qi = pl.program_id(1)
    nsub = q_ref.shape[0] // t
    nblk = q_ref.shape[1] // LANES
    nk = (nprev + 1) * t
    col = _lane_iota((1, 2 * t))
    base = jnp.where(col >= t, col - t, col) - _sub_iota((nk, 1))
    sub = _sub_iota((LANES, 1))

    def window(s):
        return pl.multiple_of(jnp.maximum(qi * nsub + s - nprev, 0) * t, t)

    def band_cap(dist):
        rel = base + dist
        return jnp.where((rel >= 0) & (rel <= max_dist), BIG, NEG)

    assert nsub >= nprev
    regular = band_cap(nprev * t)
    clamped = [band_cap(s * t) for s in range(nprev)]

    def cap(s):
        return jnp.where(qi == 0, clamped[s], regular) if s < nprev else regular

    def scores(s, i):
        start = window(s)
        cols = slice(None) if shared_kv else slice(i * LANES, (i + 1) * LANES)
        k = k_ref[pl.ds(start, nk), cols]
        qb = q_ref[s * t:(s + 1) * t, i * LANES:(i + 1) * LANES]
        lane = _lane_iota((1, LANES))
        zero = jnp.zeros_like(qb)
        qs = jnp.concatenate([jnp.where(lane < HEAD_DIM, qb, zero), jnp.where(lane >= HEAD_DIM, qb, zero)], axis=0)
        return lax.dot_general(k, qs, (((1,), (1,)), ((), ())), preferred_element_type=F32)

    def finish(s, i, st):
        cols = slice(None) if shared_kv else slice(i * LANES, (i + 1) * LANES)
        v = v_ref[pl.ds(window(s), nk), cols]
        st = jnp.minimum(st, cap(s))
        m = jnp.max(st, axis=0, keepdims=True)
        p = jnp.exp2(st - m)
        l = jnp.sum(p, axis=0, keepdims=True)
        ot = lax.dot_general(v, p.astype(BF16), (((0,), (0,)), ((), ())), preferred_element_type=F32) / l
        rows, sl = slice(s * t, (s + 1) * t), slice(i * LANES, (i + 1) * LANES)
        o_ref[rows, sl] = _merge_rows(ot[:, 0:t], ot[:, t:2 * t]).T.astype(o_ref.dtype)
        if with_lse:
            lse = m + jnp.log2(l)
            lse_ref[rows, sl] = _merge_rows(jnp.broadcast_to(lse[:, 0:t], (LANES, t)),
                                            jnp.broadcast_to(lse[:, t:2 * t], (LANES, t))).T

    items = [(s, i) for s in range(nsub) for i in range(nblk)]
    pending = [scores(*it) for it in items[:BAND_LOOKAHEAD]]
    for j, it in enumerate(items):
        if j + BAND_LOOKAHEAD < len(items):
            pending.append(scores(*items[j + BAND_LOOKAHEAD]))
        finish(*it, pending[j])


def _banded(src, *, nseq, seqlen, q_off, k_off, v_off, kv_width, max_dist, with_lse):
    t = BAND_T
    rows = min(BAND_ROWS, seqlen)
    nq = seqlen // rows
    nprev = -(-max_dist // t)
    shared = kv_width == LANES
    assert (nprev + 1) * t <= seqlen and seqlen % rows == 0 and rows % t == 0
    qo = pl.BlockSpec((rows, A_WIDTH), lambda n, q: (n * nq + q, q_off // A_WIDTH))
    out = pl.BlockSpec((rows, A_WIDTH), lambda n, q: (n * nq + q, 0))
    in_specs = [qo,
                pl.BlockSpec((seqlen, kv_width), lambda n, q: (n, k_off // kv_width)),
                pl.BlockSpec((seqlen, kv_width), lambda n, q: (n, v_off // kv_width))]
    out_specs = [out]
    out_shape = [jax.ShapeDtypeStruct((nseq * seqlen, A_WIDTH), BF16)]
    if with_lse:
        out_specs.append(out)
        out_shape.append(jax.ShapeDtypeStruct((nseq * seqlen, A_WIDTH), F32))
    return pl.pallas_call(
        functools.partial(_banded_kernel, nprev=nprev, max_dist=max_dist, shared_kv=shared, with_lse=with_lse),
        grid=(nseq, nq),
        in_specs=in_specs,
        out_specs=out_specs,
        out_shape=out_shape,
        compiler_params=_cparams(2),
        name="banded_attention",
    )(src, src, src)


MLA_TM = 512


def _rms(x, g):
    return x * lax.rsqrt(jnp.mean(x * x, axis=-1, keepdims=True) + EPS) * g


def _mla_prep_kernel(cq_ref, ckv_ref, kr_ref, c_ref, s1_ref, s2_ref, gq_ref, gkv_ref, wq_ref, wk_ref, wv_ref,
                     q_out, k_out, vt_out):
    half = QK_ROPE // 2
    tk = FLASH_TK
    cm, s1, s2 = c_ref[...], s1_ref[...], s2_ref[...]

    def rope(xc):
        return xc * cm + pltpu.roll(xc, half, 1) * s1 + pltpu.roll(xc, LANES - half, 1) * s2

    qn = _rms(cq_ref[...].astype(F32), gq_ref[...]).astype(BF16)
    q = jnp.dot(qn, wq_ref[...], preferred_element_type=F32)
    kvn = _rms(ckv_ref[...].astype(F32), gkv_ref[...]).astype(BF16)
    kk = jnp.dot(kvn, wk_ref[...], preferred_element_type=F32)
    kr = rope(kr_ref[...].astype(F32))
    for h in range(B_HEADS):
        sl = slice(h * LANES, (h + 1) * LANES)
        q_out[:, sl] = (rope(q[:, sl]) * MLA_SCALE).astype(q_out.dtype)
        k_out[:, sl] = (kk[:, sl] + kr).astype(k_out.dtype)
    v = jnp.dot(kvn, wv_ref[...], preferred_element_type=F32)
    for j in range(v.shape[0] // tk):
        for hp in range(B_HEADS // 2):
            pair_t = v[j * tk:(j + 1) * tk, hp * LANES:(hp + 1) * LANES].T.astype(vt_out.dtype)
            for h, vt in enumerate(_head_values_t(pair_t)):
                vt_out[j, hp, h] = vt


def _mla_prep(u, tabs, g_q, g_kv, wq, wk, wv, layer):
    t = u.shape[0]
    tm = min(MLA_TM, t)
    row = lambda i: (i, 0)
    wide = B_HEADS * LANES
    assert U_BCQ % Q_LORA == 0 and tm % FLASH_TK == 0
    return pl.pallas_call(
        _mla_prep_kernel,
        grid=(t // tm,),
        in_specs=[pl.BlockSpec((tm, Q_LORA), lambda i: (i, U_BCQ // Q_LORA)),
                  pl.BlockSpec((tm, LANES), lambda i: (i, U_BCKV // LANES)),
                  pl.BlockSpec((tm, LANES), lambda i: (i, U_BKR // LANES)),
                  pl.BlockSpec((tm, LANES), row), pl.BlockSpec((tm, LANES), row), pl.BlockSpec((tm, LANES), row),
                  pl.BlockSpec((None, 1, Q_LORA), lambda i: (layer, 0, 0)),
                  pl.BlockSpec((None, 1, KV_LORA), lambda i: (layer, 0, 0)),
                  pl.BlockSpec((None, Q_LORA, wide), lambda i: (layer, 0, 0)),
                  pl.BlockSpec((None, KV_LORA, wide), lambda i: (layer, 0, 0)),
                  pl.BlockSpec((None, KV_LORA, B_WIDTH), lambda i: (layer, 0, 0))],
        out_specs=[pl.BlockSpec((tm, wide), row), pl.BlockSpec((tm, wide), row),
                   pl.BlockSpec((tm // FLASH_TK, B_HEADS // 2, 2, VT_ROWS, FLASH_TK), lambda i: (i, 0, 0, 0, 0))],
        out_shape=[jax.ShapeDtypeStruct((t, wide), BF16), jax.ShapeDtypeStruct((t, wide), BF16),
                   jax.ShapeDtypeStruct((t // FLASH_TK, B_HEADS // 2, 2, VT_ROWS, FLASH_TK), BF16)],
        compiler_params=_cparams(1),
        name="mla_prep",
    )(u, u, u, *tabs, g_q, g_kv, wq, wk, wv)


MLA_TQ = 1024


def _mla_flash_kernel(q_ref, k_ref, vt_ref, o_ref, qt_ref, st_ref, m_ref, acc_ref):
    tq, tk, cw = MLA_TQ, FLASH_TK, FLASH_CHUNK
    nhalf = tq // cw
    qi = pl.program_id(2)
    for h in range(2):
        qt_ref[h] = _transpose_bf16(q_ref[:, h * LANES:(h + 1) * LANES])
    _flash_reset(m_ref, acc_ref)

    def put(kt, c, slot):
        h, half = divmod(c, nhalf)
        start = pl.multiple_of(kt * tk, tk)
        k = k_ref[pl.ds(start, tk), h * LANES:(h + 1) * LANES]
        st_ref[slot, c] = jnp.dot(k, qt_ref[h, :, half * cw:(half + 1) * cw], preferred_element_type=F32)

    def use(kt, c, slot, masked=False):
        mask = None
        if masked:
            qpos = qi * tq + (c % nhalf) * cw + _lane_iota((1, cw))
            mask = (kt * tk + _sub_iota((tk, 1))) <= qpos
        _flash_update(st_ref[slot, c], vt_ref[kt, c // nhalf], m_ref, acc_ref, c, mask)

    assert tk == cw and nhalf % 2 == 0
    n_full = nhalf * qi
    _flash_pipeline(n_full // 2, 2 * nhalf, put, use)
    live = lambda d: [c for c in range(2 * nhalf) if c % nhalf >= d]
    for d in range(nhalf):
        if d + 1 < nhalf:
            for c in live(d + 1):
                put(n_full + d + 1, c, (d + 1) % 2)
        for c in live(d):
            use(n_full + d, c, d % 2, masked=(c % nhalf == d))
    ots = []
    for h in range(2):
        ots.append(jnp.concatenate([_flash_result(acc_ref, h * nhalf + half) for half in range(nhalf)],
                                   axis=1))
    o_ref[...] = jnp.concatenate(ots, axis=0).T.astype(o_ref.dtype)


def _mla_flash(q, k, vt, batch, seq):
    assert FLASH_TK == FLASH_CHUNK
    q3 = q.reshape(batch, seq, -1)
    k3 = k.reshape(batch, seq, -1)
    vt6 = vt.reshape(batch, seq // FLASH_TK, B_HEADS // 2, 2, VT_ROWS, FLASH_TK)
    nchain = 2 * (MLA_TQ // FLASH_CHUNK)
    o = pl.pallas_call(
        _mla_flash_kernel,
        grid=(batch, B_HEADS // 2, seq // MLA_TQ),
        in_specs=[pl.BlockSpec((None, MLA_TQ, 2 * LANES), lambda b, h, i: (b, i, h)),
                  pl.BlockSpec((None, seq, 2 * LANES), lambda b, h, i: (b, 0, h)),
                  pl.BlockSpec((None, seq // FLASH_TK, None, 2, VT_ROWS, FLASH_TK),
                               lambda b, h, i: (b, 0, h, 0, 0, 0))],
        out_specs=pl.BlockSpec((None, MLA_TQ, LANES), lambda b, h, i: (b, i, h)),
        out_shape=jax.ShapeDtypeStruct((batch, seq, B_WIDTH), BF16),
        scratch_shapes=[pltpu.VMEM((2, LANES, MLA_TQ), BF16),
                        pltpu.VMEM((2, nchain, FLASH_TK, FLASH_CHUNK), F32),
                        pltpu.VMEM((nchain, 1, FLASH_CHUNK), F32),
                        pltpu.VMEM((nchain, VT_ROWS, FLASH_CHUNK), F32)],
        compiler_params=_cparams(3),
        name="mla_flash",
    )(q3, k3, vt6)
    return o.reshape(batch * seq, B_WIDTH)


OUT_TM = 256


def _out_kernel(x_ref, p_ref, ocmp_ref, oslc_ref, owin_ref, gate_ref, e_ref, az_ref, ob_ref, bz_ref,
                oc0_ref, oc1_ref, oc2_ref, l0_ref, l1_ref, l2_ref, cz_ref, mg0_ref, mg1_ref, mg2_ref,
                wa_ref, wb_ref, wc_ref, wout_ref, wplg_ref, wple_ref, gpost_ref, o_ref, *tok_refs):
    tok = iter(tok_refs)

    def sig(z):
        return 0.5 * jnp.tanh(0.5 * z) + 0.5

    def f(r):
        if len(r.shape) == 2:
            return r[...].astype(F32)
        dil, n, width = r.shape
        buf = next(tok)
        for res in range(dil):
            blk = r[res].astype(F32)
            for c in range(width // LANES):
                buf[c, pl.ds(res, n, stride=dil), :] = blk[:, c * LANES:(c + 1) * LANES]
        return jnp.concatenate([buf[c] for c in range(width // LANES)], axis=1)

    def silu(z):
        return z * sig(z)

    def mm(a, w_ref):
        return jnp.dot(a.astype(BF16), w_ref[...], preferred_element_type=F32)

    g = sig(gate_ref[...].astype(F32))
    gs = jnp.dot(g.astype(BF16), e_ref[...], preferred_element_type=F32)
    o_a = (gs[:, 0:A_WIDTH] * f(ocmp_ref) + gs[:, A_WIDTH:2 * A_WIDTH] * f(oslc_ref)
           + gs[:, 2 * A_WIDTH:3 * A_WIDTH] * f(owin_ref))
    y_a = mm(o_a * silu(f(az_ref)), wa_ref)
    y_b = mm(f(ob_ref) * silu(f(bz_ref)), wb_ref)
    l0, l1, l2 = f(l0_ref), f(l1_ref), f(l2_ref)
    mx = jnp.maximum(jnp.maximum(l0, l1), l2)
    e0, e1, e2 = jnp.exp2(l0 - mx), jnp.exp2(l1 - mx), jnp.exp2(l2 - mx)
    o_c = (e0 * f(oc0_ref) + e1 * f(oc1_ref) + e2 * f(oc2_ref)) / (e0 + e1 + e2)
    y_c = mm(o_c * silu(f(cz_ref)), wc_ref)
    mix = sig(f(mg0_ref)) * y_a + sig(f(mg1_ref)) * y_b + sig(f(mg2_ref)) * y_c
    y = mm(mix, wout_ref)
    x1 = x_ref[...] + _rms(y, gpost_ref[...])
    o_ref[...] = x1 + sig(mm(x1, wplg_ref)) * mm(p_ref[...], wple_ref)


def _out_layer(x2, p, layer, o_cmp, o_slc, o_win, u, expand, o_b, o_c, lse_c, w, batch, dils):
    t = x2.shape[0]
    seq = t // batch
    tm = min(OUT_TM, seq)
    nper = seq // tm
    row = lambda i: (i, 0)
    ucol = lambda off, width: pl.BlockSpec((tm, width), lambda i: (i, off // width))
    half = pl.BlockSpec((tm, A_WIDTH), row)
    wspec = lambda r, c: pl.BlockSpec((None, r, c), lambda i: (layer, 0, 0))

    def group_spec(dil):
        if dil == 1:
            return half
        return pl.BlockSpec((None, dil, tm // dil, C_WIDTH), lambda i: (i // nper, 0, i % nper, 0))

    def group_view(a, dil):
        return a if dil == 1 else a.reshape(batch, dil, seq // dil, C_WIDTH)

    gspecs = [group_spec(d) for d in dils]
    o_c = [group_view(a, d) for a, d in zip(o_c, dils)]
    lse_c = [group_view(a, d) for a, d in zip(lse_c, dils)]
    n_tok = 2 * sum(1 for d in dils if d > 1)
    in_specs = [pl.BlockSpec((tm, D_MODEL), row),
                pl.BlockSpec((None, tm, PLE_DIM), lambda i: (layer, i, 0)),
                half, half, half,
                ucol(U_GATE, LANES), pl.BlockSpec((LANES, 3 * A_WIDTH), lambda i: (0, 0)),
                ucol(U_AZ, A_WIDTH), half, ucol(U_BZ, B_WIDTH),
                *gspecs, *gspecs, ucol(U_CZ, C_WIDTH),
                ucol(U_MG, D_MODEL), ucol(U_MG + D_MODEL, D_MODEL), ucol(U_MG + 2 * D_MODEL, D_MODEL),
                wspec(A_WIDTH, D_MODEL), wspec(B_WIDTH, D_MODEL), wspec(C_WIDTH, D_MODEL),
                wspec(D_MODEL, D_MODEL), wspec(D_MODEL, D_MODEL), wspec(PLE_DIM, D_MODEL),
                wspec(1, D_MODEL)]
    return pl.pallas_call(
        _out_kernel,
        grid=(t // tm,),
        in_specs=in_specs,
        out_specs=pl.BlockSpec((tm, D_MODEL), row),
        out_shape=jax.ShapeDtypeStruct((t, D_MODEL), F32),
        scratch_shapes=[pltpu.VMEM((C_WIDTH // LANES, tm, LANES), F32)] * n_tok,
        compiler_params=_cparams(1),
        name="out_layer",
    )(x2, p, o_cmp, o_slc, o_win, u, expand, u, o_b, u, *o_c, *lse_c, u, u, u, u,
      w["w_a"], w["w_b"], w["w_c"], w["w_out"], w["w_plg"], w["w_ple"], w["g_post"])


def _overlap_t(seq):
    nc = seq // CMP_STRIDE
    ns = seq // SLC_LEN
    cs = np.arange(nc)[:, None] * CMP_STRIDE
    js = np.arange(ns)[None, :] * SLC_LEN
    ov = np.clip(np.minimum(cs + CMP_LEN, js + SLC_LEN) - np.maximum(cs, js), 0, None).astype(np.float32) / CMP_LEN
    return jnp.asarray(ov.T)


def _block_onehot(seq):
    blk = np.arange(seq)[:, None] // SLC_LEN
    return jnp.asarray((np.arange(LANES)[None, :] % HEAD_DIM == blk).astype(np.float32), dtype=BF16)


def _gate_expand():
    e = np.zeros((LANES, 3 * A_WIDTH), np.float32)
    for br in range(3):
        for i in range(A_HEADS // 2):
            for g in range(A_KV_GROUPS):
                h = g * (A_HEADS // 2) + i
                c0 = br * A_WIDTH + i * LANES + g * HEAD_DIM
                e[h * 3 + br, c0:c0 + HEAD_DIM] = 1.0
    return jnp.asarray(e, dtype=BF16)


def kernel(x, p, positions, g_pre, g_post, w_in, nsa_cmp_pe, nsa_cmp_w1, nsa_cmp_w2, w_a, mla_g_q, mla_w_uq,
           mla_g_kv, mla_w_ukv, w_b, w_c, w_out, w_ple, w_plg):
    batch, seq, _ = x.shape
    depth = w_in.shape[0]
    t = batch * seq
    nch = seq // CMP_STRIDE

    pos = positions.astype(F32)
    part_tabs = _rope_tables(pos.reshape(-1), ROT_DIM, HEAD_DIM, 0)
    mla_tabs = _rope_tables(pos.reshape(-1), QK_ROPE, LANES, QK_NOPE)
    dils = [dil for _, dil in C_PAIRS]
    assert dils[0] == 1
    perm_tabs = {dil: _rope_tables(_residue_major_positions(pos, dil), ROT_DIM, HEAD_DIM, 0) for dil in dils[1:]}
    w_main, w_groups = _regroup_w_in(w_in)
    g_pre3 = g_pre.reshape(depth, 1, D_MODEL)
    pe2, w1d, w2d = _compress_weights(nsa_cmp_pe, nsa_cmp_w1, nsa_cmp_w2)
    wq = mla_w_uq.reshape(depth, Q_LORA, B_HEADS, QK_NOPE + QK_ROPE)
    wq = jnp.pad(wq, ((0, 0), (0, 0), (0, 0), (0, LANES - QK_NOPE - QK_ROPE))).reshape(depth, Q_LORA, -1).astype(BF16)
    wkv = mla_w_ukv.reshape(depth, KV_LORA, B_HEADS, QK_NOPE + V_DIM)
    wk = jnp.pad(wkv[..., :QK_NOPE], ((0, 0), (0, 0), (0, 0), (0, LANES - QK_NOPE))).reshape(depth, KV_LORA, -1)
    wk = wk.astype(BF16)
    wv = wkv[..., QK_NOPE:].reshape(depth, KV_LORA, B_WIDTH).astype(BF16)
    gq3 = mla_g_q.reshape(depth, 1, Q_LORA)
    gkv3 = mla_g_kv.reshape(depth, 1, KV_LORA)
    w_a_p = w_a.reshape(depth, 2, 4, HEAD_DIM, D_MODEL).swapaxes(1, 2).reshape(depth, A_WIDTH, D_MODEL)
    wts = {"w_a": w_a_p.astype(BF16), "w_b": w_b.astype(BF16), "w_c": w_c.astype(BF16),
           "w_out": w_out.astype(BF16), "w_plg": w_plg.astype(BF16), "w_ple": w_ple.astype(BF16),
           "g_post": g_post.reshape(depth, 1, D_MODEL)}
    p3 = p.reshape(depth, t, PLE_DIM)
    ovt = _overlap_t(seq)
    onehot = _block_onehot(seq)
    expand = _gate_expand()

    x2 = x.reshape(t, D_MODEL)
    for layer in range(depth):
        h, h_res = _prenorm(x2, g_pre3, layer, batch, dils[1:])
        u = _inproj(h, w_main, layer, part_tabs, ROPE_COLS)
        u3 = u.reshape(batch, seq, DP)
        kc, vct = _compress(u3, pe2, w1d, w2d, layer)
        o_cmp, biast = _cmp_select(u3, kc, vct, ovt)
        o_slc = _slc_attention(u3, biast, onehot)
        (o_win,) = _banded(u, nseq=batch, seqlen=seq, q_off=U_AQ, k_off=U_KWIN, v_off=U_VWIN,
                           kv_width=LANES, max_dist=WIN - 1, with_lse=False)
        q_b, k_b, vt_b = _mla_prep(u, mla_tabs, gq3, gkv3, wq, wk, wv, layer)
        o_b = _mla_flash(q_b, k_b, vt_b, batch, seq)
        o_c, lse_c = [], []
        for gi, (window, dil) in enumerate(C_PAIRS):
            if dil == 1:
                o_g, lse_g = _banded(u, nseq=batch, seqlen=seq, q_off=U_CQ0, k_off=U_CK0, v_off=U_CV0,
                                     kv_width=C_WIDTH, max_dist=window, with_lse=True)
            else:
                u_g = _inproj(h_res[gi - 1], w_groups[gi - 1], layer, perm_tabs[dil], DG_ROPE_COLS)
                o_g, lse_g = _banded(u_g, nseq=batch * dil, seqlen=seq // dil, q_off=0, k_off=C_WIDTH,
                                     v_off=2 * C_WIDTH, kv_width=C_WIDTH, max_dist=window // dil, with_lse=True)
            o_c.append(o_g)
            lse_c.append(lse_g)
        x2 = _out_layer(x2, p3, layer, o_cmp.reshape(t, A_WIDTH), o_slc.reshape(t, A_WIDTH), o_win, u, expand,
                        o_b, o_c, lse_c, wts, batch, dils)
    return x2.reshape(batch, seq, D_MODEL)
```

```python
import functools

import numpy as np
import jax
import jax.numpy as jnp
from jax import lax
from jax.experimental import pallas as pl
from jax.experimental.pallas import tpu as pltpu

F32 = jnp.float32
BF16 = jnp.bfloat16

D_MODEL = 1024
PLE_DIM = 256
ROPE_THETA = 500000.0
HEAD_DIM = 64
ROT_DIM = HEAD_DIM // 4
EPS = 1e-6
NEG = -1e30
BIG = 1e30
LANES = 128

A_HEADS = 8
A_KV_GROUPS = 2
A_WIDTH = A_HEADS * HEAD_DIM
CMP_LEN = 32
CMP_STRIDE = 16
CMP_HID = 256
SLC_LEN = 64
SLC_TOPK = 16
WIN = 512

B_HEADS = 8
Q_LORA = 384
KV_LORA = 128
QK_NOPE = 64
QK_ROPE = 32
V_DIM = 64
B_WIDTH = B_HEADS * V_DIM

C_PAIRS = ((128, 1), (512, 4), (2048, 16))
C_GROUPS = len(C_PAIRS)
C_HEADS = 8
C_WIDTH = C_HEADS * HEAD_DIM

IN_SIZES = (A_WIDTH, 6 * A_KV_GROUPS * HEAD_DIM, 3 * A_HEADS, A_WIDTH, Q_LORA, KV_LORA, QK_ROPE,
            B_WIDTH, C_GROUPS * 3 * C_WIDTH, C_WIDTH, 3 * D_MODEL)
IN_OFF = tuple(int(v) for v in np.cumsum((0,) + IN_SIZES))
(OFF_AQ, OFF_AKV, OFF_AG, OFF_AZ, OFF_BCQ, OFF_BCKV, OFF_BKR, OFF_BZ, OFF_CQKV, OFF_CZ, OFF_MG) = IN_OFF[:-1]

U_AQ = 0
U_CQ0 = 512
U_CK0 = 1024
U_KCMP = 1536
U_KSLC = 1664
U_KWIN = 1792
ROPE_COLS = 1920
U_VCMP = 1920
U_AZ = 2048
U_BZ = 2560
U_CV0 = 3072
U_CZ = 3584
U_MG = 4096
U_VSLC = 7168
U_VWIN = 7296
U_GATE = 7424
U_BCKV = 7552
U_BCQ = 7680
U_BKR = 8064
DP = 8192
DG_COLS = 3 * C_WIDTH
DG_ROPE_COLS = 2 * C_WIDTH

LOG2E = float(np.log2(np.e))
Q_SCALE = float(HEAD_DIM ** -0.5) * LOG2E
MLA_SCALE = float((QK_NOPE + QK_ROPE) ** -0.5) * LOG2E
VMEM_LIMIT = 48 * 1024 * 1024


def _cparams(n_axes):
    return pltpu.CompilerParams(dimension_semantics=("arbitrary",) * n_axes, vmem_limit_bytes=VMEM_LIMIT)


def _lane_iota(shape):
    return lax.broadcasted_iota(jnp.int32, shape, len(shape) - 1)


def _sub_iota(shape):
    return lax.broadcasted_iota(jnp.int32, shape, len(shape) - 2)


def _pair_cols(w):
    lead = w.shape[:-1]
    return w.reshape(lead + (2, 4, HEAD_DIM)).swapaxes(-3, -2).reshape(lead + (A_WIDTH,))


def _regroup_w_in(w_in):
    def col(off, n):
        return w_in[..., off:off + n]
    z = lambda n: jnp.zeros(w_in.shape[:-1] + (n,), w_in.dtype)
    akv = lambda which: col(OFF_AKV + which * 128, 128)
    cq = lambda gi, t: col(OFF_CQKV + gi * 3 * C_WIDTH + t * C_WIDTH, C_WIDTH)
    pieces = [_pair_cols(col(OFF_AQ, A_WIDTH)) * Q_SCALE, cq(0, 0) * Q_SCALE, cq(0, 1)]
    pieces += [akv(0), akv(2), akv(4)]
    pieces += [akv(1)]
    pieces += [_pair_cols(col(OFF_AZ, A_WIDTH)), col(OFF_BZ, B_WIDTH), cq(0, 2)]
    pieces += [col(OFF_CZ, C_WIDTH), col(OFF_MG, 3 * D_MODEL)]
    pieces += [akv(3), akv(5)]
    pieces += [col(OFF_AG, 3 * A_HEADS), z(128 - 3 * A_HEADS)]
    pieces += [col(OFF_BCKV, KV_LORA), col(OFF_BCQ, Q_LORA)]
    kr = col(OFF_BKR, QK_ROPE)
    pieces += [_mla_lanes(kr[..., :QK_ROPE // 2], kr[..., QK_ROPE // 2:], z(QK_NOPE))]
    main = jnp.concatenate(pieces, axis=-1).astype(BF16)
    assert main.shape[-1] == DP
    groups = [jnp.concatenate([cq(gi, 0) * Q_SCALE, cq(gi, 1), cq(gi, 2)], axis=-1).astype(BF16)
              for gi in range(1, C_GROUPS)]
    return main, groups


def _rope_tables(pos, dim, period, offset):
    half = dim // 2
    ch = np.arange(LANES) % period - offset
    in_span = (ch >= 0) & (ch < dim)
    freq = np.where(in_span, ch % half, 0).astype(np.float32)
    inv = jnp.where(jnp.asarray(in_span), ROPE_THETA ** (-2.0 * jnp.asarray(freq) / dim), 0.0)
    ang = pos[:, None] * inv[None, :]
    cos, sin = jnp.cos(ang), jnp.sin(ang)
    first = jnp.asarray(in_span & (ch < half))[None, :]
    second = jnp.asarray(in_span & (ch >= half))[None, :]
    return cos, jnp.where(second, sin, 0.0), jnp.where(first, -sin, 0.0)


def _mla_lanes(rope_first, rope_second, nope):
    pad = jnp.zeros(nope.shape[:-1] + (LANES - QK_NOPE - QK_ROPE,), nope.dtype)
    cut = LANES // 2 - QK_ROPE // 2
    return jnp.concatenate([rope_first, nope[..., :cut], rope_second, nope[..., cut:], pad], axis=-1)


def _mla_rope_tables(pos):
    half = QK_ROPE // 2
    lane = np.arange(LANES)
    first, second = lane < half, (lane >= LANES // 2) & (lane < LANES // 2 + half)
    freq = np.where(first, lane, np.where(second, lane - LANES // 2, 0)).astype(np.float32)
    inv = jnp.where(jnp.asarray(first | second), ROPE_THETA ** (-2.0 * jnp.asarray(freq) / QK_ROPE), 0.0)
    ang = pos[:, None] * inv[None, :]
    sin = jnp.sin(ang)
    return jnp.cos(ang), jnp.where(jnp.asarray(first)[None, :], -sin, sin)


def _residue_major_positions(positions, dil):
    b, s = positions.shape
    return positions.reshape(b, s // dil, dil).swapaxes(1, 2).reshape(-1)


NORM_TM = 512
IN_TM = 1024
IN_TN = 2048
IN_CHUNK = 512


def _norm_kernel(x_ref, g_ref, o_ref, *rest, dils):
    o_dil, y_ref = rest[:len(dils)], rest[len(dils)]
    x = x_ref[...]
    ms = jnp.mean(x * x, axis=-1, keepdims=True)
    y = x * lax.rsqrt(ms + EPS) * g_ref[...]
    o_ref[...] = y.astype(o_ref.dtype)
    tm = x.shape[0]
    nlane = y_ref.shape[0]
    for c in range(nlane):
        y_ref[c] = y[:, c * LANES:(c + 1) * LANES]
    for o_d, dil in zip(o_dil, dils):
        for r in range(dil):
            for c in range(nlane):
                o_d[r, :, c * LANES:(c + 1) * LANES] = (
                    y_ref[c, pl.ds(r, tm // dil, stride=dil), :].astype(o_d.dtype))


def _prenorm(x2, g_pre, layer, batch, dils):
    t = x2.shape[0]
    seq = t // batch
    tm = min(NORM_TM, seq)
    nper = seq // tm
    out_specs = [pl.BlockSpec((tm, D_MODEL), lambda b, i: (b * nper + i, 0))]
    out_shape = [jax.ShapeDtypeStruct((t, D_MODEL), BF16)]
    for dil in dils:
        out_specs.append(pl.BlockSpec((None, dil, tm // dil, D_MODEL), lambda b, i: (b, 0, i, 0)))
        out_shape.append(jax.ShapeDtypeStruct((batch, dil, seq // dil, D_MODEL), BF16))
    outs = pl.pallas_call(
        functools.partial(_norm_kernel, dils=tuple(dils)),
        grid=(batch, nper),
        in_specs=[pl.BlockSpec((tm, D_MODEL), lambda b, i: (b * nper + i, 0)),
                  pl.BlockSpec((None, 1, D_MODEL), lambda b, i: (layer, 0, 0))],
        out_specs=out_specs,
        out_shape=out_shape,
        scratch_shapes=[pltpu.VMEM((D_MODEL // LANES, tm, LANES), F32)],
        compiler_params=_cparams(2),
        name="prenorm",
    )(x2, g_pre)
    return outs[0], [o.reshape(t, D_MODEL) for o in outs[1:]]


def _inproj_kernel(h_ref, w_ref, c_ref, s1_ref, s2_ref, o_ref, *, rope_cols):
    j = pl.program_id(0)
    tn = w_ref.shape[1]
    nsub = tn // LANES
    full_tiles, rem = divmod(rope_cols // LANES, nsub)
    chunk = IN_CHUNK if tn % IN_CHUNK == 0 else tn
    per = chunk // LANES

    def emit(n_rope):
        for g in range(tn // chunk):
            acc = jnp.dot(h_ref[...], w_ref[:, g * chunk:(g + 1) * chunk], preferred_element_type=F32)
            for k in range(per):
                c = g * per + k
                xc = acc[:, k * LANES:(k + 1) * LANES]
                if c < n_rope:
                    xc = (xc * c_ref[...] + pltpu.roll(xc, ROT_DIM // 2, 1) * s1_ref[...]
                          + pltpu.roll(xc, LANES - ROT_DIM // 2, 1) * s2_ref[...])
                o_ref[:, c * LANES:(c + 1) * LANES] = xc.astype(o_ref.dtype)

    pl.when(j < full_tiles)(lambda: emit(nsub))
    pl.when(j == full_tiles)(lambda: emit(rem))
    pl.when(j > full_tiles)(lambda: emit(0))


def _inproj(h, w, layer, tabs, rope_cols):
    t = h.shape[0]
    ncols = w.shape[-1]
    tn = min(IN_TN, ncols)
    ntile = ncols // tn
    tm = min(IN_TM, t)
    rope_tiles = -(-rope_cols // tn)
    row = lambda j, i: (i, 0)
    tab = lambda j, i: (jnp.where(j < rope_tiles, i, 0), 0)
    return pl.pallas_call(
        functools.partial(_inproj_kernel, rope_cols=rope_cols),
        grid=(ntile, t // tm),
        in_specs=[pl.BlockSpec((tm, D_MODEL), row),
                  pl.BlockSpec((None, D_MODEL, tn), lambda j, i: (layer, 0, j)),
                  pl.BlockSpec((tm, LANES), tab), pl.BlockSpec((tm, LANES), tab), pl.BlockSpec((tm, LANES), tab)],
        out_specs=pl.BlockSpec((tm, tn), lambda j, i: (i, j)),
        out_shape=jax.ShapeDtypeStruct((t, ntile * tn), BF16),
        compiler_params=_cparams(2),
        name="inproj",
    )(h, w, *tabs)


def _compress_kernel(k_ref, v_ref, pe_ref, w1_ref, w2_ref, kc_ref, vct_ref, x_ref):
    nch = x_ref.shape[0] // CMP_STRIDE
    for which, src in enumerate((k_ref, v_ref)):
        x_ref[...] = src[...].astype(F32)
        rows = [x_ref[pl.ds(l, nch, stride=CMP_STRIDE), :] for l in range(CMP_STRIDE)]
        halves = []
        for part in range(CMP_LEN // CMP_STRIDE):
            xs = [(rows[l] + pe_ref[which, part * CMP_STRIDE + l:part * CMP_STRIDE + l + 1, :]).astype(BF16)
                  for l in range(CMP_STRIDE)]
            halves.append(jnp.dot(jnp.concatenate(xs, axis=1), w1_ref[which, part], preferred_element_type=F32))
        hid = halves[0] + pltpu.roll(halves[1], nch - 1, 0)
        act = hid * jax.nn.sigmoid(hid)
        tok = jnp.dot(act.astype(BF16), w2_ref[which], preferred_element_type=F32)
        if which == 0:
            kc_ref[...] = tok.astype(kc_ref.dtype)
        else:
            vct_ref[...] = tok.T.astype(vct_ref.dtype)


def _compress_weights(pe, w1, w2):
    depth = w1.shape[0]
    g = A_KV_GROUPS
    assert CMP_LEN == 2 * CMP_STRIDE and g == 2
    w1r = w1.reshape(depth, 2, CMP_LEN, HEAD_DIM, CMP_HID)
    z1 = jnp.zeros_like(w1r)
    w1d = jnp.stack([jnp.concatenate([w1r, z1], axis=-1), jnp.concatenate([z1, w1r], axis=-1)], axis=3)
    w1d = w1d.reshape(depth, 2, 2, CMP_STRIDE * g * HEAD_DIM, g * CMP_HID).astype(BF16)
    z2 = jnp.zeros_like(w2)
    w2d = jnp.concatenate([jnp.concatenate([w2, z2], axis=-1), jnp.concatenate([z2, w2], axis=-1)], axis=2)
    pe2 = jnp.concatenate([pe, pe], axis=-1)
    return pe2, w1d, w2d.astype(BF16)


def _compress(u3, pe2, w1d, w2d, layer):
    b, s, _ = u3.shape
    nch = s // CMP_STRIDE
    slab = lambda off: pl.BlockSpec((None, s, LANES), lambda i: (i, 0, off // LANES))
    return pl.pallas_call(
        _compress_kernel,
        grid=(b,),
        in_specs=[slab(U_KCMP), slab(U_VCMP),
                  pl.BlockSpec((None,) + pe2.shape[1:], lambda i: (layer, 0, 0, 0)),
                  pl.BlockSpec((None,) + w1d.shape[1:], lambda i: (layer, 0, 0, 0, 0)),
                  pl.BlockSpec((None,) + w2d.shape[1:], lambda i: (layer, 0, 0, 0))],
        out_specs=[pl.BlockSpec((None, nch, LANES), lambda i: (i, 0, 0)),
                   pl.BlockSpec((None, LANES, nch), lambda i: (i, 0, 0))],
        out_shape=[jax.ShapeDtypeStruct((b, nch, LANES), BF16), jax.ShapeDtypeStruct((b, LANES, nch), BF16)],
        scratch_shapes=[pltpu.VMEM((s, LANES), F32)],
        compiler_params=_cparams(1),
        name="nsa_compress",
    )(u3, u3, pe2, w1d, w2d)


CS_TQ = 256


def _cmp_select_kernel(q_ref, kc_ref, vct_ref, ovt_ref, o_ref, bias_ref, sc_ref):
    tq = CS_TQ
    qi = pl.program_id(1)
    nc = kc_ref.shape[0]
    ns = ovt_ref.shape[0]
    tpos = qi * tq + _lane_iota((1, tq))
    kc = kc_ref[...]
    lane = _lane_iota((1, LANES))
    kst = jnp.concatenate([jnp.where(lane < HEAD_DIM, kc, jnp.zeros_like(kc)),
                           jnp.where(lane >= HEAD_DIM, kc, jnp.zeros_like(kc))], axis=0)
    c_end = _sub_iota((nc, 1)) * CMP_STRIDE + (CMP_LEN - 1)
    cv = c_end <= tpos
    vct = vct_ref[...]
    psum = [jnp.zeros((nc, tq), F32), jnp.zeros((nc, tq), F32)]
    sub = _sub_iota((LANES, 1))
    for i in range(A_HEADS // 2):
        qb = q_ref[:, i * LANES:(i + 1) * LANES]
        st = lax.dot_general(kst, qb, (((1,), (1,)), ((), ())), preferred_element_type=F32)
        ot = []
        for g in range(A_KV_GROUPS):
            s = jnp.where(cv, st[g * nc:(g + 1) * nc], NEG)
            m = jnp.max(s, axis=0, keepdims=True)
            e = jnp.where(cv, jnp.exp2(s - m), 0.0)
            den = jnp.sum(e, axis=0, keepdims=True)
            p = e / jnp.where(den > 0.0, den, 1.0)
            psum[g] = psum[g] + p
            ot.append(jnp.dot(vct, p.astype(BF16), preferred_element_type=F32))
        o_pair = jnp.where(sub < HEAD_DIM, ot[0], ot[1])
        o_ref[:, i * LANES:(i + 1) * LANES] = o_pair.T.astype(o_ref.dtype)
    jdx = _sub_iota((ns, 1))
    cur = tpos // SLC_LEN
    valid = jdx <= cur
    forced = (jdx == 0) | (jdx == cur) | (jdx == cur - 1)
    for g in range(A_KV_GROUPS):
        imp = jnp.dot(ovt_ref[...], psum[g], preferred_element_type=F32, precision=lax.Precision.HIGHEST)
        score = jnp.where(forced, BIG, jnp.where(valid, imp, NEG))
        sc_ref[...] = score
        slab = 8
        parts = [score[v * slab:(v + 1) * slab] for v in range(ns // slab)]
        cnts = [jnp.zeros((slab, tq), F32) for _ in parts]
        for jp in range(ns):
            row = sc_ref[jp:jp + 1, :]
            for v, sv in enumerate(parts):
                if v * slab > jp:
                    ahead = row >= sv
                elif (v + 1) * slab - 1 < jp:
                    ahead = row > sv
                else:
                    tie = jnp.where(jdx[v * slab:(v + 1) * slab] > jp, 1.0, 0.0)
                    cnts[v] = cnts[v] + jnp.where(row == sv, tie, 0.0)
                    ahead = row > sv
                cnts[v] = cnts[v] + jnp.where(ahead, 1.0, 0.0)
        cnt = jnp.concatenate(cnts, axis=0)
        bias = jnp.where(cnt < float(min(SLC_TOPK, ns)), 0.0, NEG)
        bias_ref[g * ns:(g + 1) * ns, :] = bias.astype(bias_ref.dtype)


def _cmp_select(u3, kc, vct, ovt):
    b, s, _ = u3.shape
    nc = kc.shape[1]
    ns = s // SLC_LEN
    assert A_KV_GROUPS * ns == LANES
    return pl.pallas_call(
        _cmp_select_kernel,
        grid=(b, s // CS_TQ),
        in_specs=[pl.BlockSpec((None, CS_TQ, A_WIDTH), lambda i, q: (i, q, U_AQ // A_WIDTH)),
                  pl.BlockSpec((None, nc, LANES), lambda i, q: (i, 0, 0)),
                  pl.BlockSpec((None, LANES, nc), lambda i, q: (i, 0, 0)),
                  pl.BlockSpec((ns, nc), lambda i, q: (0, 0))],
        out_specs=[pl.BlockSpec((None, CS_TQ, A_WIDTH), lambda i, q: (i, q, 0)),
                   pl.BlockSpec((None, LANES, CS_TQ), lambda i, q: (i, 0, q))],
        out_shape=[jax.ShapeDtypeStruct((b, s, A_WIDTH), BF16), jax.ShapeDtypeStruct((b, LANES, s), BF16)],
        scratch_shapes=[pltpu.VMEM((ns, CS_TQ), F32)],
        compiler_params=_cparams(2),
        name="nsa_cmp_select",
    )(u3, kc, vct, ovt)


FLASH_TK = 256
FLASH_CHUNK = 256
FLASH_UNROLL = 4
ONES_ROWS = 16
VT_ROWS = HEAD_DIM + ONES_ROWS


def _head_values_t(vt_pair):
    ones = jnp.ones((ONES_ROWS, vt_pair.shape[1]), vt_pair.dtype)
    return [jnp.concatenate([vt_pair[h * HEAD_DIM:(h + 1) * HEAD_DIM], ones], axis=0) for h in range(2)]


def _flash_reset(m_ref, acc_ref):
    m_ref[...] = jnp.full(m_ref.shape, NEG, F32)
    acc_ref[...] = jnp.zeros(acc_ref.shape, F32)


def _flash_update(st, vt, m_ref, acc_ref, idx, mask):
    if mask is not None:
        st = jnp.where(mask, st, NEG)
    m_old = m_ref[idx]
    m_new = jnp.maximum(m_old, jnp.max(st, axis=0, keepdims=True))
    alpha = jnp.exp2(m_old - m_new)
    p = jnp.exp2((st - m_new).astype(BF16))
    acc_ref[idx] = alpha * acc_ref[idx] + jnp.dot(vt, p, preferred_element_type=F32)
    m_ref[idx] = m_new


def _flash_result(acc_ref, idx):
    acc = acc_ref[idx]
    return acc[0:HEAD_DIM] / acc[HEAD_DIM:HEAD_DIM + 1]


def _flash_pipeline(n_pairs, n_chain, put, use):
    for c in range(n_chain):
        put(0, c, 0)

    def run(t0, count):
        for t in range(count):
            for c in range(n_chain):
                put(t0 + t + 1, c, (t + 1) % 2)
                use(t0 + t, c, t % 2)

    per = FLASH_UNROLL // 2
    n_long = n_pairs // per

    def long_body(j, carry):
        run(FLASH_UNROLL * j, FLASH_UNROLL)
        return carry

    def pair_body(j, carry):
        run(FLASH_UNROLL * n_long + 2 * j, 2)
        return carry

    lax.fori_loop(0, n_long, long_body, 0)
    lax.fori_loop(0, n_pairs - per * n_long, pair_body, 0)


def _transpose_bf16(x):
    return x.astype(F32).T.astype(BF16)


def _merge_rows(lo, hi):
    return jnp.where(_sub_iota((LANES, 1)) < HEAD_DIM, lo, hi)


SLC_TQ = 256


def _slc_kernel(q_ref, biast_ref, k_ref, v_ref, oh_ref, o_ref, vt_ref, qa_ref, st_ref, m_ref, acc_ref):
    tq, tk = SLC_TQ, FLASH_TK
    qi = pl.program_id(1)
    nblk = A_HEADS // 2
    sub = _sub_iota((LANES, 1))

    @pl.when(qi == 0)
    def _():
        for kt in range(vt_ref.shape[0]):
            for g, vt in enumerate(_head_values_t(_transpose_bf16(v_ref[kt * tk:(kt + 1) * tk, :]))):
                vt_ref[kt, g] = vt

    nchain = 2 * nblk
    biast = biast_ref[...]
    zero = jnp.zeros_like(biast)
    mine = (sub < HEAD_DIM, sub >= HEAD_DIM)
    for i in range(nblk):
        qt = _transpose_bf16(q_ref[:, i * LANES:(i + 1) * LANES])
        for h in range(2):
            qa_ref[2 * i + h] = jnp.concatenate([jnp.where(mine[h], qt, zero), jnp.where(mine[h], biast, zero)],
                                                axis=0)
    _flash_reset(m_ref, acc_ref)

    def put(kt, c, slot):
        start = pl.multiple_of(kt * tk, tk)
        k = jnp.concatenate([k_ref[pl.ds(start, tk), :], oh_ref[pl.ds(start, tk), :]], axis=1)
        st_ref[slot, c] = jnp.dot(k, qa_ref[c], preferred_element_type=F32)

    def use(kt, c, slot, mask=None):
        _flash_update(st_ref[slot, c], vt_ref[kt, c % 2], m_ref, acc_ref, c, mask)

    n_full = qi
    n_pairs = n_full // 2
    _flash_pipeline(n_pairs, nchain, put, use)
    causal = (n_full * tk + _sub_iota((tk, 1))) <= (qi * tq + _lane_iota((1, tq)))

    @pl.when(n_full == 2 * n_pairs)
    def _():
        for c in range(nchain):
            use(n_full, c, 0, causal)

    @pl.when(n_full != 2 * n_pairs)
    def _():
        for c in range(nchain):
            put(n_full, c, 1)
            use(n_full - 1, c, 0)
        for c in range(nchain):
            use(n_full, c, 1, causal)

    for i in range(nblk):
        ot = jnp.concatenate([_flash_result(acc_ref, 2 * i), _flash_result(acc_ref, 2 * i + 1)], axis=0)
        o_ref[:, i * LANES:(i + 1) * LANES] = ot.T.astype(o_ref.dtype)


def _slc_attention(u3, biast, onehot):
    b, s, _ = u3.shape
    nblk = A_HEADS
    assert SLC_TQ == FLASH_CHUNK == FLASH_TK
    return pl.pallas_call(
        _slc_kernel,
        grid=(b, s // SLC_TQ),
        in_specs=[pl.BlockSpec((None, SLC_TQ, A_WIDTH), lambda i, q: (i, q, U_AQ // A_WIDTH)),
                  pl.BlockSpec((None, LANES, SLC_TQ), lambda i, q: (i, 0, q)),
                  pl.BlockSpec((None, s, LANES), lambda i, q: (i, 0, U_KSLC // LANES)),
                  pl.BlockSpec((None, s, LANES), lambda i, q: (i, 0, U_VSLC // LANES)),
                  pl.BlockSpec((s, LANES), lambda i, q: (0, 0))],
        out_specs=pl.BlockSpec((None, SLC_TQ, A_WIDTH), lambda i, q: (i, q, 0)),
        out_shape=jax.ShapeDtypeStruct((b, s, A_WIDTH), BF16),
        scratch_shapes=[pltpu.VMEM((s // FLASH_TK, 2, VT_ROWS, FLASH_TK), BF16),
                        pltpu.VMEM((nblk, 2 * LANES, FLASH_CHUNK), BF16),
                        pltpu.VMEM((2, nblk, FLASH_TK, FLASH_CHUNK), F32),
                        pltpu.VMEM((nblk, 1, FLASH_CHUNK), F32),
                        pltpu.VMEM((nblk, VT_ROWS, FLASH_CHUNK), F32)],
        compiler_params=_cparams(2),
        name="nsa_selected",
    )(u3, biast, u3, u3, onehot)


BAND_T = 128
BAND_ROWS = 1024
BAND_LOOKAHEAD = 2


def _banded_kernel(*refs, nprev, max_dist, shared_kv, with_lse):
    t = BAND_T
    q_ref, k_ref, v_ref, o_ref = refs[:4]
    lse_ref = refs[4] if with_lse else None
    qi = pl.program_id(1)
    nsub = q_ref.shape[0] // t
    nblk = q_ref.shape[1] // LANES
    nk = (nprev + 1) * t
    col = _lane_iota((1, 2 * t))
    base = jnp.where(col >= t, col - t, col) - _sub_iota((nk, 1))
    sub = _sub_iota((LANES, 1))

    def window(s):
        return pl.multiple_of(jnp.maximum(qi * nsub + s - nprev, 0) * t, t)

    def band_cap(dist):
        rel = base + dist
        return jnp.where((rel >= 0) & (rel <= max_dist), BIG, NEG)

    assert nsub >= nprev
    regular = band_cap(nprev * t)
    clamped = [band_cap(s * t) for s in range(nprev)]

    def cap(s):
        return jnp.where(qi == 0, clamped[s], regular) if s < nprev else regular

    def scores(s, i):
        start = window(s)
        cols = slice(None) if shared_kv else slice(i * LANES, (i + 1) * LANES)
        k = k_ref[pl.ds(start, nk), cols]
        qb = q_ref[s * t:(s + 1) * t, i * LANES:(i + 1) * LANES]
        lane = _lane_iota((1, LANES))
        zero = jnp.zeros_like(qb)
        qs = jnp.concatenate([jnp.where(lane < HEAD_DIM, qb, zero), jnp.where(lane >= HEAD_DIM, qb, zero)], axis=0)
        return lax.dot_general(k, qs, (((1,), (1,)), ((), ())), preferred_element_type=F32)

    def finish(s, i, st):
        cols = slice(None) if shared_kv else slice(i * LANES, (i + 1) * LANES)
        v = v_ref[pl.ds(window(s), nk), cols]
        st = jnp.minimum(st, cap(s))
        m = jnp.max(st, axis=0, keepdims=True)
        p = jnp.exp2(st - m)
        l = jnp.sum(p, axis=0, keepdims=True)
        ot = lax.dot_general(v, p.astype(BF16), (((0,), (0,)), ((), ())), preferred_element_type=F32) / l
        rows, sl = slice(s * t, (s + 1) * t), slice(i * LANES, (i + 1) * LANES)
        o_ref[rows, sl] = _merge_rows(ot[:, 0:t], ot[:, t:2 * t]).T.astype(o_ref.dtype)
        if with_lse:
            lse = m + jnp.log2(l)
            lse_ref[rows, sl] = _merge_rows(jnp.broadcast_to(lse[:, 0:t], (LANES, t)),
                                            jnp.broadcast_to(lse[:, t:2 * t], (LANES, t))).T

    items = [(s, i) for s in range(nsub) for i in range(nblk)]
    pending = [scores(*it) for it in items[:BAND_LOOKAHEAD]]
    for j, it in enumerate(items):
        if j + BAND_LOOKAHEAD < len(items):
            pending.append(scores(*items[j + BAND_LOOKAHEAD]))
        finish(*it, pending[j])


def _banded(src, *, nseq, seqlen, q_off, k_off, v_off, kv_width, max_dist, with_lse):
    t = BAND_T
    rows = min(BAND_ROWS, seqlen)
    nq = seqlen // rows
    nprev = -(-max_dist // t)
    shared = kv_width == LANES
    assert (nprev + 1) * t <= seqlen and seqlen % rows == 0 and rows % t == 0
    qo = pl.BlockSpec((rows, A_WIDTH), lambda n, q: (n * nq + q, q_off // A_WIDTH))
    out = pl.BlockSpec((rows, A_WIDTH), lambda n, q: (n * nq + q, 0))
    in_specs = [qo,
                pl.BlockSpec((seqlen, kv_width), lambda n, q: (n, k_off // kv_width)),
                pl.BlockSpec((seqlen, kv_width), lambda n, q: (n, v_off // kv_width))]
    out_specs = [out]
    out_shape = [jax.ShapeDtypeStruct((nseq * seqlen, A_WIDTH), BF16)]
    if with_lse:
        out_specs.append(out)
        out_shape.append(jax.ShapeDtypeStruct((nseq * seqlen, A_WIDTH), F32))
    return pl.pallas_call(
        functools.partial(_banded_kernel, nprev=nprev, max_dist=max_dist, shared_kv=shared, with_lse=with_lse),
        grid=(nseq, nq),
        in_specs=in_specs,
        out_specs=out_specs,
        out_shape=out_shape,
        compiler_params=_cparams(2),
        name="banded_attention",
    )(src, src, src)


MLA_TM = 512


def _rms(x, g):
    return x * lax.rsqrt(jnp.mean(x * x, axis=-1, keepdims=True) + EPS) * g


def _mla_prep_kernel(cq_ref, ckv_ref, kr_ref, c_ref, s_ref, gq_ref, gkv_ref, wq_ref, wk_ref, wv_ref,
                     q_out, k_out, vt_out):
    tk = FLASH_TK
    cm, sm = c_ref[...], s_ref[...]

    def rope(xc):
        return xc * cm + pltpu.roll(xc, LANES // 2, 1) * sm

    qn = _rms(cq_ref[...].astype(F32), gq_ref[...]).astype(BF16)
    q = jnp.dot(qn, wq_ref[...], preferred_element_type=F32)
    kvn = _rms(ckv_ref[...].astype(F32), gkv_ref[...]).astype(BF16)
    kk = jnp.dot(kvn, wk_ref[...], preferred_element_type=F32)
    kr = rope(kr_ref[...].astype(F32))
    for h in range(B_HEADS):
        sl = slice(h * LANES, (h + 1) * LANES)
        q_out[:, sl] = (rope(q[:, sl]) * MLA_SCALE).astype(q_out.dtype)
        k_out[:, sl] = (kk[:, sl] + kr).astype(k_out.dtype)
    v = jnp.dot(kvn, wv_ref[...], preferred_element_type=F32)
    for j in range(v.shape[0] // tk):
        for hp in range(B_HEADS // 2):
            pair_t = v[j * tk:(j + 1) * tk, hp * LANES:(hp + 1) * LANES].T.astype(vt_out.dtype)
            for h, vt in enumerate(_head_values_t(pair_t)):
                vt_out[j, hp, h] = vt


def _mla_prep(u, tabs, g_q, g_kv, wq, wk, wv, layer):
    t = u.shape[0]
    tm = min(MLA_TM, t)
    row = lambda i: (i, 0)
    wide = B_HEADS * LANES
    assert U_BCQ % Q_LORA == 0 and tm % FLASH_TK == 0
    return pl.pallas_call(
        _mla_prep_kernel,
        grid=(t // tm,),
        in_specs=[pl.BlockSpec((tm, Q_LORA), lambda i: (i, U_BCQ // Q_LORA)),
                  pl.BlockSpec((tm, LANES), lambda i: (i, U_BCKV // LANES)),
                  pl.BlockSpec((tm, LANES), lambda i: (i, U_BKR // LANES)),
                  pl.BlockSpec((tm, LANES), row), pl.BlockSpec((tm, LANES), row),
                  pl.BlockSpec((None, 1, Q_LORA), lambda i: (layer, 0, 0)),
                  pl.BlockSpec((None, 1, KV_LORA), lambda i: (layer, 0, 0)),
                  pl.BlockSpec((None, Q_LORA, wide), lambda i: (layer, 0, 0)),
                  pl.BlockSpec((None, KV_LORA, wide), lambda i: (layer, 0, 0)),
                  pl.BlockSpec((None, KV_LORA, B_WIDTH), lambda i: (layer, 0, 0))],
        out_specs=[pl.BlockSpec((tm, wide), row), pl.BlockSpec((tm, wide), row),
                   pl.BlockSpec((tm // FLASH_TK, B_HEADS // 2, 2, VT_ROWS, FLASH_TK), lambda i: (i, 0, 0, 0, 0))],
        out_shape=[jax.ShapeDtypeStruct((t, wide), BF16), jax.ShapeDtypeStruct((t, wide), BF16),
                   jax.ShapeDtypeStruct((t // FLASH_TK, B_HEADS // 2, 2, VT_ROWS, FLASH_TK), BF16)],
        compiler_params=_cparams(1),
        name="mla_prep",
    )(u, u, u, *tabs, g_q, g_kv, wq, wk, wv)


MLA_TQ = 1024


def _mla_flash_kernel(q_ref, k_ref, vt_ref, o_ref, qt_ref, st_ref, m_ref, acc_ref):
    tq, tk, cw = MLA_TQ, FLASH_TK, FLASH_CHUNK
    nhalf = tq // cw
    qi = pl.program_id(2)
    for h in range(2):
        qt_ref[h] = _transpose_bf16(q_ref[:, h * LANES:(h + 1) * LANES])
    _flash_reset(m_ref, acc_ref)

    def put(kt, c, slot):
        h, half = divmod(c, nhalf)
        start = pl.multiple_of(kt * tk, tk)
        k = k_ref[pl.ds(start, tk), h * LANES:(h + 1) * LANES]
        st_ref[slot, c] = jnp.dot(k, qt_ref[h, :, half * cw:(half + 1) * cw], preferred_element_type=F32)

    def use(kt, c, slot, masked=False):
        mask = None
        if masked:
            qpos = qi * tq + (c % nhalf) * cw + _lane_iota((1, cw))
            mask = (kt * tk + _sub_iota((tk, 1))) <= qpos
        _flash_update(st_ref[slot, c], vt_ref[kt, c // nhalf], m_ref, acc_ref, c, mask)

    assert tk == cw and nhalf % 2 == 0
    n_full = nhalf * qi
    _flash_pipeline(n_full // 2, 2 * nhalf, put, use)
    live = lambda d: [c for c in range(2 * nhalf) if c % nhalf >= d]
    for d in range(nhalf):
        if d + 1 < nhalf:
            for c in live(d + 1):
                put(n_full + d + 1, c, (d + 1) % 2)
        for c in live(d):
            use(n_full + d, c, d % 2, masked=(c % nhalf == d))
    ots = []
    for h in range(2):
        ots.append(jnp.concatenate([_flash_result(acc_ref, h * nhalf + half) for half in range(nhalf)],
                                   axis=1))
    o_ref[...] = jnp.concatenate(ots, axis=0).T.astype(o_ref.dtype)


def _mla_flash(q, k, vt, batch, seq):
    assert FLASH_TK == FLASH_CHUNK
    q3 = q.reshape(batch, seq, -1)
    k3 = k.reshape(batch, seq, -1)
    vt6 = vt.reshape(batch, seq // FLASH_TK, B_HEADS // 2, 2, VT_ROWS, FLASH_TK)
    nchain = 2 * (MLA_TQ // FLASH_CHUNK)
    o = pl.pallas_call(
        _mla_flash_kernel,
        grid=(batch, B_HEADS // 2, seq // MLA_TQ),
        in_specs=[pl.BlockSpec((None, MLA_TQ, 2 * LANES), lambda b, h, i: (b, i, h)),
                  pl.BlockSpec((None, seq, 2 * LANES), lambda b, h, i: (b, 0, h)),
                  pl.BlockSpec((None, seq // FLASH_TK, None, 2, VT_ROWS, FLASH_TK),
                               lambda b, h, i: (b, 0, h, 0, 0, 0))],
        out_specs=pl.BlockSpec((None, MLA_TQ, LANES), lambda b, h, i: (b, i, h)),
        out_shape=jax.ShapeDtypeStruct((batch, seq, B_WIDTH), BF16),
        scratch_shapes=[pltpu.VMEM((2, LANES, MLA_TQ), BF16),
                        pltpu.VMEM((2, nchain, FLASH_TK, FLASH_CHUNK), F32),
                        pltpu.VMEM((nchain, 1, FLASH_CHUNK), F32),
                        pltpu.VMEM((nchain, VT_ROWS, FLASH_CHUNK), F32)],
        compiler_params=_cparams(3),
        name="mla_flash",
    )(q3, k3, vt6)
    return o.reshape(batch * seq, B_WIDTH)


OUT_TM = 256


def _out_kernel(x_ref, p_ref, ocmp_ref, oslc_ref, owin_ref, gate_ref, e_ref, az_ref, ob_ref, bz_ref,
                oc0_ref, oc1_ref, oc2_ref, l0_ref, l1_ref, l2_ref, cz_ref, mg0_ref, mg1_ref, mg2_ref,
                wa_ref, wb_ref, wc_ref, wout_ref, wplg_ref, wple_ref, gpost_ref, o_ref, *tok_refs):
    tok = iter(tok_refs)

    def sig(z):
        return 0.5 * jnp.tanh(0.5 * z) + 0.5

    def f(r):
        if len(r.shape) == 2:
            return r[...].astype(F32)
        dil, n, width = r.shape
        buf = next(tok)
        for res in range(dil):
            blk = r[res].astype(F32)
            for c in range(width // LANES):
                buf[c, pl.ds(res, n, stride=dil), :] = blk[:, c * LANES:(c + 1) * LANES]
        return jnp.concatenate([buf[c] for c in range(width // LANES)], axis=1)

    def silu(z):
        return z * sig(z)

    def mm(a, w_ref):
        return jnp.dot(a.astype(BF16), w_ref[...], preferred_element_type=F32)

    g = sig(gate_ref[...].astype(F32))
    gs = jnp.dot(g.astype(BF16), e_ref[...], preferred_element_type=F32)
    o_a = (gs[:, 0:A_WIDTH] * f(ocmp_ref) + gs[:, A_WIDTH:2 * A_WIDTH] * f(oslc_ref)
           + gs[:, 2 * A_WIDTH:3 * A_WIDTH] * f(owin_ref))
    y_a = mm(o_a * silu(f(az_ref)), wa_ref)
    y_b = mm(f(ob_ref) * silu(f(bz_ref)), wb_ref)
    l0, l1, l2 = f(l0_ref), f(l1_ref), f(l2_ref)
    mx = jnp.maximum(jnp.maximum(l0, l1), l2)
    e0, e1, e2 = jnp.exp2(l0 - mx), jnp.exp2(l1 - mx), jnp.exp2(l2 - mx)
    o_c = (e0 * f(oc0_ref) + e1 * f(oc1_ref) + e2 * f(oc2_ref)) / (e0 + e1 + e2)
    y_c = mm(o_c * silu(f(cz_ref)), wc_ref)
    mix = sig(f(mg0_ref)) * y_a + sig(f(mg1_ref)) * y_b + sig(f(mg2_ref)) * y_c
    y = mm(mix, wout_ref)
    x1 = x_ref[...] + _rms(y, gpost_ref[...])
    o_ref[...] = x1 + sig(mm(x1, wplg_ref)) * mm(p_ref[...], wple_ref)


def _out_layer(x2, p, layer, o_cmp, o_slc, o_win, u, expand, o_b, o_c, lse_c, w, batch, dils):
    t = x2.shape[0]
    seq = t // batch
    tm = min(OUT_TM, seq)
    nper = seq // tm
    row = lambda i: (i, 0)
    ucol = lambda off, width: pl.BlockSpec((tm, width), lambda i: (i, off // width))
    half = pl.BlockSpec((tm, A_WIDTH), row)
    wspec = lambda r, c: pl.BlockSpec((None, r, c), lambda i: (layer, 0, 0))

    def group_spec(dil):
        if dil == 1:
            return half
        return pl.BlockSpec((None, dil, tm // dil, C_WIDTH), lambda i: (i // nper, 0, i % nper, 0))

    def group_view(a, dil):
        return a if dil == 1 else a.reshape(batch, dil, seq // dil, C_WIDTH)

    gspecs = [group_spec(d) for d in dils]
    o_c = [group_view(a, d) for a, d in zip(o_c, dils)]
    lse_c = [group_view(a, d) for a, d in zip(lse_c, dils)]
    n_tok = 2 * sum(1 for d in dils if d > 1)
    in_specs = [pl.BlockSpec((tm, D_MODEL), row),
                pl.BlockSpec((None, tm, PLE_DIM), lambda i: (layer, i, 0)),
                half, half, half,
                ucol(U_GATE, LANES), pl.BlockSpec((LANES, 3 * A_WIDTH), lambda i: (0, 0)),
                ucol(U_AZ, A_WIDTH), half, ucol(U_BZ, B_WIDTH),
                *gspecs, *gspecs, ucol(U_CZ, C_WIDTH),
                ucol(U_MG, D_MODEL), ucol(U_MG + D_MODEL, D_MODEL), ucol(U_MG + 2 * D_MODEL, D_MODEL),
                wspec(A_WIDTH, D_MODEL), wspec(B_WIDTH, D_MODEL), wspec(C_WIDTH, D_MODEL),
                wspec(D_MODEL, D_MODEL), wspec(D_MODEL, D_MODEL), wspec(PLE_DIM, D_MODEL),
                wspec(1, D_MODEL)]
    return pl.pallas_call(
        _out_kernel,
        grid=(t // tm,),
        in_specs=in_specs,
        out_specs=pl.BlockSpec((tm, D_MODEL), row),
        out_shape=jax.ShapeDtypeStruct((t, D_MODEL), F32),
        scratch_shapes=[pltpu.VMEM((C_WIDTH // LANES, tm, LANES), F32)] * n_tok,
        compiler_params=_cparams(1),
        name="out_layer",
    )(x2, p, o_cmp, o_slc, o_win, u, expand, u, o_b, u, *o_c, *lse_c, u, u, u, u,
      w["w_a"], w["w_b"], w["w_c"], w["w_out"], w["w_plg"], w["w_ple"], w["g_post"])


def _overlap_t(seq):
    nc = seq // CMP_STRIDE
    ns = seq // SLC_LEN
    cs = np.arange(nc)[:, None] * CMP_STRIDE
    js = np.arange(ns)[None, :] * SLC_LEN
    ov = np.clip(np.minimum(cs + CMP_LEN, js + SLC_LEN) - np.maximum(cs, js), 0, None).astype(np.float32) / CMP_LEN
    return jnp.asarray(ov.T)


def _block_onehot(seq):
    blk = np.arange(seq)[:, None] // SLC_LEN
    return jnp.asarray((np.arange(LANES)[None, :] % HEAD_DIM == blk).astype(np.float32), dtype=BF16)


def _gate_expand():
    e = np.zeros((LANES, 3 * A_WIDTH), np.float32)
    for br in range(3):
        for i in range(A_HEADS // 2):
            for g in range(A_KV_GROUPS):
                h = g * (A_HEADS // 2) + i
                c0 = br * A_WIDTH + i * LANES + g * HEAD_DIM
                e[h * 3 + br, c0:c0 + HEAD_DIM] = 1.0
    return jnp.asarray(e, dtype=BF16)


def kernel(x, p, positions, g_pre, g_post, w_in, nsa_cmp_pe, nsa_cmp_w1, nsa_cmp_w2, w_a, mla_g_q, mla_w_uq,
           mla_g_kv, mla_w_ukv, w_b, w_c, w_out, w_ple, w_plg):
    batch, seq, _ = x.shape
    depth = w_in.shape[0]
    t = batch * seq
    nch = seq // CMP_STRIDE

    pos = positions.astype(F32)
    part_tabs = _rope_tables(pos.reshape(-1), ROT_DIM, HEAD_DIM, 0)
    mla_tabs = _mla_rope_tables(pos.reshape(-1))
    dils = [dil for _, dil in C_PAIRS]
    assert dils[0] == 1
    perm_tabs = {dil: _rope_tables(_residue_major_positions(pos, dil), ROT_DIM, HEAD_DIM, 0) for dil in dils[1:]}
    w_main, w_groups = _regroup_w_in(w_in)
    g_pre3 = g_pre.reshape(depth, 1, D_MODEL)
    pe2, w1d, w2d = _compress_weights(nsa_cmp_pe, nsa_cmp_w1, nsa_cmp_w2)
    wq = mla_w_uq.reshape(depth, Q_LORA, B_HEADS, QK_NOPE + QK_ROPE)
    wq = _mla_lanes(wq[..., QK_NOPE:QK_NOPE + QK_ROPE // 2], wq[..., QK_NOPE + QK_ROPE // 2:], wq[..., :QK_NOPE])
    wq = wq.reshape(depth, Q_LORA, -1).astype(BF16)
    wkv = mla_w_ukv.reshape(depth, KV_LORA, B_HEADS, QK_NOPE + V_DIM)
    no_rope = jnp.zeros(wkv.shape[:-1] + (QK_ROPE // 2,), wkv.dtype)
    wk = _mla_lanes(no_rope, no_rope, wkv[..., :QK_NOPE]).reshape(depth, KV_LORA, -1).astype(BF16)
    wv = wkv[..., QK_NOPE:].reshape(depth, KV_LORA, B_WIDTH).astype(BF16)
    gq3 = mla_g_q.reshape(depth, 1, Q_LORA)
    gkv3 = mla_g_kv.reshape(depth, 1, KV_LORA)
    w_a_p = w_a.reshape(depth, 2, 4, HEAD_DIM, D_MODEL).swapaxes(1, 2).reshape(depth, A_WIDTH, D_MODEL)
    wts = {"w_a": w_a_p.astype(BF16), "w_b": w_b.astype(BF16), "w_c": w_c.astype(BF16),
           "w_out": w_out.astype(BF16), "w_plg": w_plg.astype(BF16), "w_ple": w_ple.astype(BF16),
           "g_post": g_post.reshape(depth, 1, D_MODEL)}
    p3 = p.reshape(depth, t, PLE_DIM)
    ovt = _overlap_t(seq)
    onehot = _block_onehot(seq)
    expand = _gate_expand()

    x2 = x.reshape(t, D_MODEL)
    for layer in range(depth):
        h, h_res = _prenorm(x2, g_pre3, layer, batch, dils[1:])
        u = _inproj(h, w_main, layer, part_tabs, ROPE_COLS)
        u3 = u.reshape(batch, seq, DP)
        kc, vct = _compress(u3, pe2, w1d, w2d, layer)
        o_cmp, biast = _cmp_select(u3, kc, vct, ovt)
        o_slc = _slc_attention(u3, biast, onehot)
        (o_win,) = _banded(u, nseq=batch, seqlen=seq, q_off=U_AQ, k_off=U_KWIN, v_off=U_VWIN,
                           kv_width=LANES, max_dist=WIN - 1, with_lse=False)
        q_b, k_b, vt_b = _mla_prep(u, mla_tabs, gq3, gkv3, wq, wk, wv, layer)
        o_b = _mla_flash(q_b, k_b, vt_b, batch, seq)
        o_c, lse_c = [], []
        for gi, (window, dil) in enumerate(C_PAIRS):
            if dil == 1:
                o_g, lse_g = _banded(u, nseq=batch, seqlen=seq, q_off=U_CQ0, k_off=U_CK0, v_off=U_CV0,
                                     kv_width=C_WIDTH, max_dist=window, with_lse=True)
            else:
                u_g = _inproj(h_res[gi - 1], w_groups[gi - 1], layer, perm_tabs[dil], DG_ROPE_COLS)
                o_g, lse_g = _banded(u_g, nseq=batch * dil, seqlen=seq // dil, q_off=0, k_off=C_WIDTH,
                                     v_off=2 * C_WIDTH, kv_width=C_WIDTH, max_dist=window // dil, with_lse=True)
            o_c.append(o_g)
            lse_c.append(lse_g)
        x2 = _out_layer(x2, p3, layer, o_cmp.reshape(t, A_WIDTH), o_slc.reshape(t, A_WIDTH), o_win, u, expand,
                        o_b, o_c, lse_c, wts, batch, dils)
    return x2.reshape(batch, seq, D_MODEL)
```

```python
import functools

import numpy as np
import jax
import jax.numpy as jnp
from jax import lax
from jax.experimental import pallas as pl
from jax.experimental.pallas import tpu as pltpu

F32 = jnp.float32
BF16 = jnp.bfloat16

D_MODEL = 1024
PLE_DIM = 256
ROPE_THETA = 500000.0
HEAD_DIM = 64
ROT_DIM = HEAD_DIM // 4
EPS = 1e-6
NEG = -1e30
BIG = 1e30
LANES = 128

A_HEADS = 8
A_KV_GROUPS = 2
A_WIDTH = A_HEADS * HEAD_DIM
CMP_LEN = 32
CMP_STRIDE = 16
CMP_HID = 256
SLC_LEN = 64
SLC_TOPK = 16
WIN = 512

B_HEADS = 8
Q_LORA = 384
KV_LORA = 128
QK_NOPE = 64
QK_ROPE = 32
V_DIM = 64
B_WIDTH = B_HEADS * V_DIM

C_PAIRS = ((128, 1), (512, 4), (2048, 16))
C_GROUPS = len(C_PAIRS)
C_HEADS = 8
C_WIDTH = C_HEADS * HEAD_DIM

IN_SIZES = (A_WIDTH, 6 * A_KV_GROUPS * HEAD_DIM, 3 * A_HEADS, A_WIDTH, Q_LORA, KV_LORA, QK_ROPE,
            B_WIDTH, C_GROUPS * 3 * C_WIDTH, C_WIDTH, 3 * D_MODEL)
IN_OFF = tuple(int(v) for v in np.cumsum((0,) + IN_SIZES))
(OFF_AQ, OFF_AKV, OFF_AG, OFF_AZ, OFF_BCQ, OFF_BCKV, OFF_BKR, OFF_BZ, OFF_CQKV, OFF_CZ, OFF_MG) = IN_OFF[:-1]

U_AQ = 0
U_CQ0 = 512
U_CK0 = 1024
U_KCMP = 1536
U_KSLC = 1664
U_KWIN = 1792
ROPE_COLS = 1920
U_VCMP = 1920
U_AZ = 2048
U_BZ = 2560
U_CV0 = 3072
U_CZ = 3584
U_MG = 4096
U_VSLC = 7168
U_VWIN = 7296
U_GATE = 7424
U_BCKV = 7552
U_BCQ = 7680
U_BKR = 8064
DP = 8192
DG_COLS = 3 * C_WIDTH
DG_ROPE_COLS = 2 * C_WIDTH

LOG2E = float(np.log2(np.e))
Q_SCALE = float(HEAD_DIM ** -0.5) * LOG2E
MLA_SCALE = float((QK_NOPE + QK_ROPE) ** -0.5) * LOG2E
VMEM_LIMIT = 48 * 1024 * 1024


def _cparams(n_axes):
    return pltpu.CompilerParams(dimension_semantics=("arbitrary",) * n_axes, vmem_limit_bytes=VMEM_LIMIT)


def _lane_iota(shape):
    return lax.broadcasted_iota(jnp.int32, shape, len(shape) - 1)


def _sub_iota(shape):
    return lax.broadcasted_iota(jnp.int32, shape, len(shape) - 2)


def _pair_cols(w):
    lead = w.shape[:-1]
    return w.reshape(lead + (2, 4, HEAD_DIM)).swapaxes(-3, -2).reshape(lead + (A_WIDTH,))


def _regroup_w_in(w_in):
    def col(off, n):
        return w_in[..., off:off + n]
    z = lambda n: jnp.zeros(w_in.shape[:-1] + (n,), w_in.dtype)
    akv = lambda which: col(OFF_AKV + which * 128, 128)
    cq = lambda gi, t: col(OFF_CQKV + gi * 3 * C_WIDTH + t * C_WIDTH, C_WIDTH)
    pieces = [_pair_cols(col(OFF_AQ, A_WIDTH)) * Q_SCALE, cq(0, 0) * Q_SCALE, cq(0, 1)]
    pieces += [akv(0), akv(2), akv(4)]
    pieces += [akv(1)]
    pieces += [_pair_cols(col(OFF_AZ, A_WIDTH)), col(OFF_BZ, B_WIDTH), cq(0, 2)]
    pieces += [col(OFF_CZ, C_WIDTH), col(OFF_MG, 3 * D_MODEL)]
    pieces += [akv(3), akv(5)]
    pieces += [col(OFF_AG, 3 * A_HEADS), z(128 - 3 * A_HEADS)]
    pieces += [col(OFF_BCKV, KV_LORA), col(OFF_BCQ, Q_LORA)]
    kr = col(OFF_BKR, QK_ROPE)
    pieces += [_mla_lanes(kr[..., :QK_ROPE // 2], kr[..., QK_ROPE // 2:], z(QK_NOPE))]
    main = jnp.concatenate(pieces, axis=-1).astype(BF16)
    assert main.shape[-1] == DP
    groups = [jnp.concatenate([cq(gi, 0) * Q_SCALE, cq(gi, 1), cq(gi, 2)], axis=-1).astype(BF16)
              for gi in range(1, C_GROUPS)]
    return main, groups


def _rope_tables(pos, dim, period, offset):
    half = dim // 2
    ch = np.arange(LANES) % period - offset
    in_span = (ch >= 0) & (ch < dim)
    cos, sin = _trig_lanes(pos, dim, np.where(in_span, ch % half, -1))
    first = jnp.asarray(in_span & (ch < half))[None, :]
    second = jnp.asarray(in_span & (ch >= half))[None, :]
    return cos, jnp.where(second, sin, 0.0), jnp.where(first, -sin, 0.0)


def _trig_lanes(pos, dim, freq_of_lane):
    half = dim // 2
    inv = ROPE_THETA ** (-jnp.arange(0, dim, 2, dtype=F32) / dim)
    ang = pos[:, None] * inv[None, :]
    spread = jnp.asarray((freq_of_lane[None, :] == np.arange(half)[:, None]).astype(np.float32))
    expand = lambda a: jnp.dot(a, spread, precision=lax.Precision.HIGHEST)
    return expand(jnp.cos(ang)) + jnp.asarray((freq_of_lane < 0).astype(np.float32))[None, :], expand(jnp.sin(ang))


def _mla_lanes(rope_first, rope_second, nope):
    pad = jnp.zeros(nope.shape[:-1] + (LANES - QK_NOPE - QK_ROPE,), nope.dtype)
    cut = LANES // 2 - QK_ROPE // 2
    return jnp.concatenate([rope_first, nope[..., :cut], rope_second, nope[..., cut:], pad], axis=-1)


def _mla_rope_tables(pos):
    half = QK_ROPE // 2
    lane = np.arange(LANES)
    first, second = lane < half, (lane >= LANES // 2) & (lane < LANES // 2 + half)
    cos, sin = _trig_lanes(pos, QK_ROPE, np.where(first, lane, np.where(second, lane - LANES // 2, -1)))
    return cos, jnp.where(jnp.asarray(first)[None, :], -sin, sin)


def _residue_major_positions(positions, dil):
    b, s = positions.shape
    return positions.reshape(b, s // dil, dil).swapaxes(1, 2).reshape(-1)


NORM_TM = 512
IN_TM = 1024
IN_TN = 2048
IN_CHUNK = 512


def _norm_kernel(x_ref, g_ref, o_ref, *rest, dils):
    o_dil, y_ref = rest[:len(dils)], rest[len(dils)]
    x = x_ref[...]
    ms = jnp.mean(x * x, axis=-1, keepdims=True)
    y = x * lax.rsqrt(ms + EPS) * g_ref[...]
    o_ref[...] = y.astype(o_ref.dtype)
    tm = x.shape[0]
    nlane = y_ref.shape[0]
    for c in range(nlane):
        y_ref[c] = y[:, c * LANES:(c + 1) * LANES]
    for o_d, dil in zip(o_dil, dils):
        for r in range(dil):
            for c in range(nlane):
                o_d[r, :, c * LANES:(c + 1) * LANES] = (
                    y_ref[c, pl.ds(r, tm // dil, stride=dil), :].astype(o_d.dtype))


def _prenorm(x2, g_pre, layer, batch, dils):
    t = x2.shape[0]
    seq = t // batch
    tm = min(NORM_TM, seq)
    nper = seq // tm
    out_specs = [pl.BlockSpec((tm, D_MODEL), lambda b, i: (b * nper + i, 0))]
    out_shape = [jax.ShapeDtypeStruct((t, D_MODEL), BF16)]
    for dil in dils:
        out_specs.append(pl.BlockSpec((None, dil, tm // dil, D_MODEL), lambda b, i: (b, 0, i, 0)))
        out_shape.append(jax.ShapeDtypeStruct((batch, dil, seq // dil, D_MODEL), BF16))
    outs = pl.pallas_call(
        functools.partial(_norm_kernel, dils=tuple(dils)),
        grid=(batch, nper),
        in_specs=[pl.BlockSpec((tm, D_MODEL), lambda b, i: (b * nper + i, 0)),
                  pl.BlockSpec((None, 1, D_MODEL), lambda b, i: (layer, 0, 0))],
        out_specs=out_specs,
        out_shape=out_shape,
        scratch_shapes=[pltpu.VMEM((D_MODEL // LANES, tm, LANES), F32)],
        compiler_params=_cparams(2),
        name="prenorm",
    )(x2, g_pre)
    return outs[0], [o.reshape(t, D_MODEL) for o in outs[1:]]


def _inproj_kernel(h_ref, w_ref, c_ref, s1_ref, s2_ref, o_ref, *, rope_cols):
    j = pl.program_id(0)
    tn = w_ref.shape[1]
    nsub = tn // LANES
    full_tiles, rem = divmod(rope_cols // LANES, nsub)
    chunk = IN_CHUNK if tn % IN_CHUNK == 0 else tn
    per = chunk // LANES

    def emit(n_rope):
        for g in range(tn // chunk):
            acc = jnp.dot(h_ref[...], w_ref[:, g * chunk:(g + 1) * chunk], preferred_element_type=F32)
            for k in range(per):
                c = g * per + k
                xc = acc[:, k * LANES:(k + 1) * LANES]
                if c < n_rope:
                    xc = (xc * c_ref[...] + pltpu.roll(xc, ROT_DIM // 2, 1) * s1_ref[...]
                          + pltpu.roll(xc, LANES - ROT_DIM // 2, 1) * s2_ref[...])
                o_ref[:, c * LANES:(c + 1) * LANES] = xc.astype(o_ref.dtype)

    pl.when(j < full_tiles)(lambda: emit(nsub))
    pl.when(j == full_tiles)(lambda: emit(rem))
    pl.when(j > full_tiles)(lambda: emit(0))


def _inproj(h, w, layer, tabs, rope_cols):
    t = h.shape[0]
    ncols = w.shape[-1]
    tn = min(IN_TN, ncols)
    ntile = ncols // tn
    tm = min(IN_TM, t)
    rope_tiles = -(-rope_cols // tn)
    row = lambda j, i: (i, 0)
    tab = lambda j, i: (jnp.where(j < rope_tiles, i, 0), 0)
    return pl.pallas_call(
        functools.partial(_inproj_kernel, rope_cols=rope_cols),
        grid=(ntile, t // tm),
        in_specs=[pl.BlockSpec((tm, D_MODEL), row),
                  pl.BlockSpec((None, D_MODEL, tn), lambda j, i: (layer, 0, j)),
                  pl.BlockSpec((tm, LANES), tab), pl.BlockSpec((tm, LANES), tab), pl.BlockSpec((tm, LANES), tab)],
        out_specs=pl.BlockSpec((tm, tn), lambda j, i: (i, j)),
        out_shape=jax.ShapeDtypeStruct((t, ntile * tn), BF16),
        compiler_params=_cparams(2),
        name="inproj",
    )(h, w, *tabs)


def _compress_kernel(k_ref, v_ref, pe_ref, w1_ref, w2_ref, kc_ref, vct_ref, x_ref):
    nch = x_ref.shape[0] // CMP_STRIDE
    for which, src in enumerate((k_ref, v_ref)):
        x_ref[...] = src[...].astype(F32)
        rows = [x_ref[pl.ds(l, nch, stride=CMP_STRIDE), :] for l in range(CMP_STRIDE)]
        halves = []
        for part in range(CMP_LEN // CMP_STRIDE):
            xs = [(rows[l] + pe_ref[which, part * CMP_STRIDE + l:part * CMP_STRIDE + l + 1, :]).astype(BF16)
                  for l in range(CMP_STRIDE)]
            halves.append(jnp.dot(jnp.concatenate(xs, axis=1), w1_ref[which, part], preferred_element_type=F32))
        hid = halves[0] + pltpu.roll(halves[1], nch - 1, 0)
        act = hid * jax.nn.sigmoid(hid)
        tok = jnp.dot(act.astype(BF16), w2_ref[which], preferred_element_type=F32)
        if which == 0:
            kc_ref[...] = tok.astype(kc_ref.dtype)
        else:
            vct_ref[...] = tok.T.astype(vct_ref.dtype)


def _compress_weights(pe, w1, w2):
    depth = w1.shape[0]
    g = A_KV_GROUPS
    assert CMP_LEN == 2 * CMP_STRIDE and g == 2
    w1r = w1.reshape(depth, 2, CMP_LEN, HEAD_DIM, CMP_HID)
    z1 = jnp.zeros_like(w1r)
    w1d = jnp.stack([jnp.concatenate([w1r, z1], axis=-1), jnp.concatenate([z1, w1r], axis=-1)], axis=3)
    w1d = w1d.reshape(depth, 2, 2, CMP_STRIDE * g * HEAD_DIM, g * CMP_HID).astype(BF16)
    z2 = jnp.zeros_like(w2)
    w2d = jnp.concatenate([jnp.concatenate([w2, z2], axis=-1), jnp.concatenate([z2, w2], axis=-1)], axis=2)
    pe2 = jnp.concatenate([pe, pe], axis=-1)
    return pe2, w1d, w2d.astype(BF16)


def _compress(u3, pe2, w1d, w2d, layer):
    b, s, _ = u3.shape
    nch = s // CMP_STRIDE
    slab = lambda off: pl.BlockSpec((None, s, LANES), lambda i: (i, 0, off // LANES))
    return pl.pallas_call(
        _compress_kernel,
        grid=(b,),
        in_specs=[slab(U_KCMP), slab(U_VCMP),
                  pl.BlockSpec((None,) + pe2.shape[1:], lambda i: (layer, 0, 0, 0)),
                  pl.BlockSpec((None,) + w1d.shape[1:], lambda i: (layer, 0, 0, 0, 0)),
                  pl.BlockSpec((None,) + w2d.shape[1:], lambda i: (layer, 0, 0, 0))],
        out_specs=[pl.BlockSpec((None, nch, LANES), lambda i: (i, 0, 0)),
                   pl.BlockSpec((None, LANES, nch), lambda i: (i, 0, 0))],
        out_shape=[jax.ShapeDtypeStruct((b, nch, LANES), BF16), jax.ShapeDtypeStruct((b, LANES, nch), BF16)],
        scratch_shapes=[pltpu.VMEM((s, LANES), F32)],
        compiler_params=_cparams(1),
        name="nsa_compress",
    )(u3, u3, pe2, w1d, w2d)


CS_TQ = 256


def _cmp_select_kernel(q_ref, kc_ref, vct_ref, ovt_ref, o_ref, bias_ref, sc_ref, cnt_ref):
    tq = CS_TQ
    qi = pl.program_id(1)
    nc = kc_ref.shape[0]
    ns = ovt_ref.shape[0]
    tpos = qi * tq + _lane_iota((1, tq))
    kc = kc_ref[...]
    lane = _lane_iota((1, LANES))
    kst = jnp.concatenate([jnp.where(lane < HEAD_DIM, kc, jnp.zeros_like(kc)),
                           jnp.where(lane >= HEAD_DIM, kc, jnp.zeros_like(kc))], axis=0)
    c_end = _sub_iota((nc, 1)) * CMP_STRIDE + (CMP_LEN - 1)
    cv = c_end <= tpos
    vct = vct_ref[...]
    psum = [jnp.zeros((nc, tq), F32), jnp.zeros((nc, tq), F32)]
    sub = _sub_iota((LANES, 1))
    for i in range(A_HEADS // 2):
        qb = q_ref[:, i * LANES:(i + 1) * LANES]
        st = lax.dot_general(kst, qb, (((1,), (1,)), ((), ())), preferred_element_type=F32)
        ot = []
        for g in range(A_KV_GROUPS):
            s = jnp.where(cv, st[g * nc:(g + 1) * nc], NEG)
            m = jnp.max(s, axis=0, keepdims=True)
            e = jnp.where(cv, jnp.exp2(s - m), 0.0)
            den = jnp.sum(e, axis=0, keepdims=True)
            p = e / jnp.where(den > 0.0, den, 1.0)
            psum[g] = psum[g] + p
            ot.append(jnp.dot(vct, p.astype(BF16), preferred_element_type=F32))
        o_pair = jnp.where(sub < HEAD_DIM, ot[0], ot[1])
        o_ref[:, i * LANES:(i + 1) * LANES] = o_pair.T.astype(o_ref.dtype)
    jdx = _sub_iota((ns, 1))
    cur = tpos // SLC_LEN
    valid = jdx <= cur
    forced = (jdx == 0) | (jdx == cur) | (jdx == cur - 1)
    for g in range(A_KV_GROUPS):
        imp = jnp.dot(ovt_ref[...], psum[g], preferred_element_type=F32, precision=lax.Precision.HIGHEST)
        score = jnp.where(forced, BIG, jnp.where(valid, imp, NEG))
        sc_ref[...] = score
        slab = 8
        cnt_ref[...] = jnp.zeros((ns, tq), F32)
        last_block = (qi * tq + tq - 1) // SLC_LEN

        for grp in range(ns // slab):
            @pl.when(grp * slab <= last_block)
            def _(grp=grp):
                for v in range(ns // slab):
                    sv = sc_ref[v * slab:(v + 1) * slab, :]
                    acc = cnt_ref[v * slab:(v + 1) * slab, :]
                    for jp in range(grp * slab, (grp + 1) * slab):
                        row = sc_ref[jp:jp + 1, :]
                        if v * slab > jp:
                            ahead = row >= sv
                        elif (v + 1) * slab - 1 < jp:
                            ahead = row > sv
                        else:
                            tie = jnp.where(jdx[v * slab:(v + 1) * slab] > jp, 1.0, 0.0)
                            acc = acc + jnp.where(row == sv, tie, 0.0)
                            ahead = row > sv
                        acc = acc + jnp.where(ahead, 1.0, 0.0)
                    cnt_ref[v * slab:(v + 1) * slab, :] = acc
        bias = jnp.where(cnt_ref[...] < float(min(SLC_TOPK, ns)), 0.0, NEG)
        bias_ref[g * ns:(g + 1) * ns, :] = bias.astype(bias_ref.dtype)


def _cmp_select(u3, kc, vct, ovt):
    b, s, _ = u3.shape
    nc = kc.shape[1]
    ns = s // SLC_LEN
    assert A_KV_GROUPS * ns == LANES
    return pl.pallas_call(
        _cmp_select_kernel,
        grid=(b, s // CS_TQ),
        in_specs=[pl.BlockSpec((None, CS_TQ, A_WIDTH), lambda i, q: (i, q, U_AQ // A_WIDTH)),
                  pl.BlockSpec((None, nc, LANES), lambda i, q: (i, 0, 0)),
                  pl.BlockSpec((None, LANES, nc), lambda i, q: (i, 0, 0)),
                  pl.BlockSpec((ns, nc), lambda i, q: (0, 0))],
        out_specs=[pl.BlockSpec((None, CS_TQ, A_WIDTH), lambda i, q: (i, q, 0)),
                   pl.BlockSpec((None, LANES, CS_TQ), lambda i, q: (i, 0, q))],
        out_shape=[jax.ShapeDtypeStruct((b, s, A_WIDTH), BF16), jax.ShapeDtypeStruct((b, LANES, s), BF16)],
        scratch_shapes=[pltpu.VMEM((ns, CS_TQ), F32), pltpu.VMEM((ns, CS_TQ), F32)],
        compiler_params=_cparams(2),
        name="nsa_cmp_select",
    )(u3, kc, vct, ovt)


FLASH_TK = 256
FLASH_CHUNK = 256
FLASH_UNROLL = 4
ONES_ROWS = 16
VT_ROWS = HEAD_DIM + ONES_ROWS


def _head_values_t(vt_pair):
    ones = jnp.ones((ONES_ROWS, vt_pair.shape[1]), vt_pair.dtype)
    return [jnp.concatenate([vt_pair[h * HEAD_DIM:(h + 1) * HEAD_DIM], ones], axis=0) for h in range(2)]


def _flash_reset(m_ref, acc_ref):
    m_ref[...] = jnp.full(m_ref.shape, NEG, F32)
    acc_ref[...] = jnp.zeros(acc_ref.shape, F32)


def _flash_update(st, vt, m_ref, acc_ref, idx, mask):
    if mask is not None:
        st = jnp.where(mask, st, NEG)
    m_old = m_ref[idx]
    m_new = jnp.maximum(m_old, jnp.max(st, axis=0, keepdims=True))
    alpha = jnp.exp2(m_old - m_new)
    p = jnp.exp2((st - m_new).astype(BF16))
    acc_ref[idx] = alpha * acc_ref[idx] + jnp.dot(vt, p, preferred_element_type=F32)
    m_ref[idx] = m_new


def _flash_result(acc_ref, idx):
    acc = acc_ref[idx]
    return acc[0:HEAD_DIM] / acc[HEAD_DIM:HEAD_DIM + 1]


def _flash_pipeline(n_pairs, n_chain, put, use):
    for c in range(n_chain):
        put(0, c, 0)

    def run(t0, count):
        for t in range(count):
            for c in range(n_chain):
                put(t0 + t + 1, c, (t + 1) % 2)
                use(t0 + t, c, t % 2)

    per = FLASH_UNROLL // 2
    n_long = n_pairs // per

    def long_body(j, carry):
        run(FLASH_UNROLL * j, FLASH_UNROLL)
        return carry

    def pair_body(j, carry):
        run(FLASH_UNROLL * n_long + 2 * j, 2)
        return carry

    lax.fori_loop(0, n_long, long_body, 0)
    lax.fori_loop(0, n_pairs - per * n_long, pair_body, 0)


def _transpose_bf16(x):
    return x.astype(F32).T.astype(BF16)


def _merge_rows(lo, hi):
    return jnp.where(_sub_iota((LANES, 1)) < HEAD_DIM, lo, hi)


SLC_TQ = 256


def _slc_kernel(q_ref, biast_ref, k_ref, v_ref, oh_ref, o_ref, vt_ref, qa_ref, st_ref, m_ref, acc_ref):
    tq, tk = SLC_TQ, FLASH_TK
    qi = pl.program_id(1)
    nblk = A_HEADS // 2
    sub = _sub_iota((LANES, 1))

    @pl.when(qi == 0)
    def _():
        for kt in range(vt_ref.shape[0]):
            for g, vt in enumerate(_head_values_t(_transpose_bf16(v_ref[kt * tk:(kt + 1) * tk, :]))):
                vt_ref[kt, g] = vt

    nchain = 2 * nblk
    biast = biast_ref[...]
    zero = jnp.zeros_like(biast)
    mine = (sub < HEAD_DIM, sub >= HEAD_DIM)
    for i in range(nblk):
        qt = _transpose_bf16(q_ref[:, i * LANES:(i + 1) * LANES])
        for h in range(2):
            qa_ref[2 * i + h] = jnp.concatenate([jnp.where(mine[h], qt, zero), jnp.where(mine[h], biast, zero)],
                                                axis=0)
    _flash_reset(m_ref, acc_ref)

    def put(kt, c, slot):
        start = pl.multiple_of(kt * tk, tk)
        k = jnp.concatenate([k_ref[pl.ds(start, tk), :], oh_ref[pl.ds(start, tk), :]], axis=1)
        st_ref[slot, c] = jnp.dot(k, qa_ref[c], preferred_element_type=F32)

    def use(kt, c, slot, mask=None):
        _flash_update(st_ref[slot, c], vt_ref[kt, c % 2], m_ref, acc_ref, c, mask)

    n_full = qi
    n_pairs = n_full // 2
    _flash_pipeline(n_pairs, nchain, put, use)
    causal = (n_full * tk + _sub_iota((tk, 1))) <= (qi * tq + _lane_iota((1, tq)))

    @pl.when(n_full == 2 * n_pairs)
    def _():
        for c in range(nchain):
            use(n_full, c, 0, causal)

    @pl.when(n_full != 2 * n_pairs)
    def _():
        for c in range(nchain):
            put(n_full, c, 1)
            use(n_full - 1, c, 0)
        for c in range(nchain):
            use(n_full, c, 1, causal)

    for i in range(nblk):
        ot = jnp.concatenate([_flash_result(acc_ref, 2 * i), _flash_result(acc_ref, 2 * i + 1)], axis=0)
        o_ref[:, i * LANES:(i + 1) * LANES] = ot.T.astype(o_ref.dtype)


def _slc_attention(u3, biast, onehot):
    b, s, _ = u3.shape
    nblk = A_HEADS
    assert SLC_TQ == FLASH_CHUNK == FLASH_TK
    return pl.pallas_call(
        _slc_kernel,
        grid=(b, s // SLC_TQ),
        in_specs=[pl.BlockSpec((None, SLC_TQ, A_WIDTH), lambda i, q: (i, q, U_AQ // A_WIDTH)),
                  pl.BlockSpec((None, LANES, SLC_TQ), lambda i, q: (i, 0, q)),
                  pl.BlockSpec((None, s, LANES), lambda i, q: (i, 0, U_KSLC // LANES)),
                  pl.BlockSpec((None, s, LANES), lambda i, q: (i, 0, U_VSLC // LANES)),
                  pl.BlockSpec((s, LANES), lambda i, q: (0, 0))],
        out_specs=pl.BlockSpec((None, SLC_TQ, A_WIDTH), lambda i, q: (i, q, 0)),
        out_shape=jax.ShapeDtypeStruct((b, s, A_WIDTH), BF16),
        scratch_shapes=[pltpu.VMEM((s // FLASH_TK, 2, VT_ROWS, FLASH_TK), BF16),
                        pltpu.VMEM((nblk, 2 * LANES, FLASH_CHUNK), BF16),
                        pltpu.VMEM((2, nblk, FLASH_TK, FLASH_CHUNK), F32),
                        pltpu.VMEM((nblk, 1, FLASH_CHUNK), F32),
                        pltpu.VMEM((nblk, VT_ROWS, FLASH_CHUNK), F32)],
        compiler_params=_cparams(2),
        name="nsa_selected",
    )(u3, biast, u3, u3, onehot)


BAND_T = 128
BAND_ROWS = 1024
BAND_LOOKAHEAD = 2


def _banded_kernel(*refs, nprev, max_dist, shared_kv, with_lse):
    t = BAND_T
    q_ref, k_ref, v_ref, o_ref = refs[:4]
    lse_ref = refs[4] if with_lse else None
    qi = pl.program_id(1)
    nsub = q_ref.shape[0] // t
    nblk = q_ref.shape[1] // LANES
    nk = (nprev + 1) * t
    col = _lane_iota((1, 2 * t))
    base = jnp.where(col >= t, col - t, col) - _sub_iota((nk, 1))
    sub = _sub_iota((LANES, 1))

    def window(s):
        return pl.multiple_of(jnp.maximum(qi * nsub + s - nprev, 0) * t, t)

    def band_cap(dist):
        rel = base + dist
        return jnp.where((rel >= 0) & (rel <= max_dist), BIG, NEG)

    assert nsub >= nprev
    regular = band_cap(nprev * t)
    clamped = [band_cap(s * t) for s in range(nprev)]

    def cap(s):
        return jnp.where(qi == 0, clamped[s], regular) if s < nprev else regular

    def scores(s, i):
        start = window(s)
        cols = slice(None) if shared_kv else slice(i * LANES, (i + 1) * LANES)
        k = k_ref[pl.ds(start, nk), cols]
        qb = q_ref[s * t:(s + 1) * t, i * LANES:(i + 1) * LANES]
        lane = _lane_iota((1, LANES))
        zero = jnp.zeros_like(qb)
        qs = jnp.concatenate([jnp.where(lane < HEAD_DIM, qb, zero), jnp.where(lane >= HEAD_DIM, qb, zero)], axis=0)
        return lax.dot_general(k, qs, (((1,), (1,)), ((), ())), preferred_element_type=F32)

    def finish(s, i, st):
        cols = slice(None) if shared_kv else slice(i * LANES, (i + 1) * LANES)
        v = v_ref[pl.ds(window(s), nk), cols]
        st = jnp.minimum(st, cap(s))
        m = jnp.max(st, axis=0, keepdims=True)
        p = jnp.exp2(st - m)
        l = jnp.sum(p, axis=0, keepdims=True)
        ot = lax.dot_general(v, p.astype(BF16), (((0,), (0,)), ((), ())), preferred_element_type=F32) / l
        rows, sl = slice(s * t, (s + 1) * t), slice(i * LANES, (i + 1) * LANES)
        o_ref[rows, sl] = _merge_rows(ot[:, 0:t], ot[:, t:2 * t]).T.astype(o_ref.dtype)
        if with_lse:
            lse = m + jnp.log2(l)
            lse_ref[rows, sl] = _merge_rows(jnp.broadcast_to(lse[:, 0:t], (LANES, t)),
                                            jnp.broadcast_to(lse[:, t:2 * t], (LANES, t))).T

    items = [(s, i) for s in range(nsub) for i in range(nblk)]
    pending = [scores(*it) for it in items[:BAND_LOOKAHEAD]]
    for j, it in enumerate(items):
        if j + BAND_LOOKAHEAD < len(items):
            pending.append(scores(*items[j + BAND_LOOKAHEAD]))
        finish(*it, pending[j])


def _banded(src, *, nseq, seqlen, q_off, k_off, v_off, kv_width, max_dist, with_lse):
    t = BAND_T
    rows = min(BAND_ROWS, seqlen)
    nq = seqlen // rows
    nprev = -(-max_dist // t)
    shared = kv_width == LANES
    assert (nprev + 1) * t <= seqlen and seqlen % rows == 0 and rows % t == 0
    qo = pl.BlockSpec((rows, A_WIDTH), lambda n, q: (n * nq + q, q_off // A_WIDTH))
    out = pl.BlockSpec((rows, A_WIDTH), lambda n, q: (n * nq + q, 0))
    in_specs = [qo,
                pl.BlockSpec((seqlen, kv_width), lambda n, q: (n, k_off // kv_width)),
                pl.BlockSpec((seqlen, kv_width), lambda n, q: (n, v_off // kv_width))]
    out_specs = [out]
    out_shape = [jax.ShapeDtypeStruct((nseq * seqlen, A_WIDTH), BF16)]
    if with_lse:
        out_specs.append(out)
        out_shape.append(jax.ShapeDtypeStruct((nseq * seqlen, A_WIDTH), F32))
    return pl.pallas_call(
        functools.partial(_banded_kernel, nprev=nprev, max_dist=max_dist, shared_kv=shared, with_lse=with_lse),
        grid=(nseq, nq),
        in_specs=in_specs,
        out_specs=out_specs,
        out_shape=out_shape,
        compiler_params=_cparams(2),
        name="banded_attention",
    )(src, src, src)


MLA_TM = 512


def _rms(x, g):
    return x * lax.rsqrt(jnp.mean(x * x, axis=-1, keepdims=True) + EPS) * g


def _mla_prep_kernel(cq_ref, ckv_ref, kr_ref, c_ref, s_ref, gq_ref, gkv_ref, wq_ref, wk_ref, wv_ref,
                     q_out, k_out, vt_out):
    tk = FLASH_TK
    cm, sm = c_ref[...], s_ref[...]

    def rope(xc):
        return xc * cm + pltpu.roll(xc, LANES // 2, 1) * sm

    qn = _rms(cq_ref[...].astype(F32), gq_ref[...]).astype(BF16)
    q = jnp.dot(qn, wq_ref[...], preferred_element_type=F32)
    kvn = _rms(ckv_ref[...].astype(F32), gkv_ref[...]).astype(BF16)
    kk = jnp.dot(kvn, wk_ref[...], preferred_element_type=F32)
    kr = rope(kr_ref[...].astype(F32))
    for h in range(B_HEADS):
        sl = slice(h * LANES, (h + 1) * LANES)
        q_out[:, sl] = (rope(q[:, sl]) * MLA_SCALE).astype(q_out.dtype)
        k_out[:, sl] = (kk[:, sl] + kr).astype(k_out.dtype)
    v = jnp.dot(kvn, wv_ref[...], preferred_element_type=F32)
    for j in range(v.shape[0] // tk):
        for hp in range(B_HEADS // 2):
            pair_t = v[j * tk:(j + 1) * tk, hp * LANES:(hp + 1) * LANES].T.astype(vt_out.dtype)
            for h, vt in enumerate(_head_values_t(pair_t)):
                vt_out[j, hp, h] = vt


def _mla_prep(u, tabs, g_q, g_kv, wq, wk, wv, layer):
    t = u.shape[0]
    tm = min(MLA_TM, t)
    row = lambda i: (i, 0)
    wide = B_HEADS * LANES
    assert U_BCQ % Q_LORA == 0 and tm % FLASH_TK == 0
    return pl.pallas_call(
        _mla_prep_kernel,
        grid=(t // tm,),
        in_specs=[pl.BlockSpec((tm, Q_LORA), lambda i: (i, U_BCQ // Q_LORA)),
                  pl.BlockSpec((tm, LANES), lambda i: (i, U_BCKV // LANES)),
                  pl.BlockSpec((tm, LANES), lambda i: (i, U_BKR // LANES)),
                  pl.BlockSpec((tm, LANES), row), pl.BlockSpec((tm, LANES), row),
                  pl.BlockSpec((None, 1, Q_LORA), lambda i: (layer, 0, 0)),
                  pl.BlockSpec((None, 1, KV_LORA), lambda i: (layer, 0, 0)),
                  pl.BlockSpec((None, Q_LORA, wide), lambda i: (layer, 0, 0)),
                  pl.BlockSpec((None, KV_LORA, wide), lambda i: (layer, 0, 0)),
                  pl.BlockSpec((None, KV_LORA, B_WIDTH), lambda i: (layer, 0, 0))],
        out_specs=[pl.BlockSpec((tm, wide), row), pl.BlockSpec((tm, wide), row),
                   pl.BlockSpec((tm // FLASH_TK, B_HEADS // 2, 2, VT_ROWS, FLASH_TK), lambda i: (i, 0, 0, 0, 0))],
        out_shape=[jax.ShapeDtypeStruct((t, wide), BF16), jax.ShapeDtypeStruct((t, wide), BF16),
                   jax.ShapeDtypeStruct((t // FLASH_TK, B_HEADS // 2, 2, VT_ROWS, FLASH_TK), BF16)],
        compiler_params=_cparams(1),
        name="mla_prep",
    )(u, u, u, *tabs, g_q, g_kv, wq, wk, wv)


MLA_TQ = 1024


def _mla_flash_kernel(q_ref, k_ref, vt_ref, o_ref, qt_ref, st_ref, m_ref, acc_ref):
    tq, tk, cw = MLA_TQ, FLASH_TK, FLASH_CHUNK
    nhalf = tq // cw
    qi = pl.program_id(2)
    for h in range(2):
        qt_ref[h] = _transpose_bf16(q_ref[:, h * LANES:(h + 1) * LANES])
    _flash_reset(m_ref, acc_ref)

    def put(kt, c, slot):
        h, half = divmod(c, nhalf)
        start = pl.multiple_of(kt * tk, tk)
        k = k_ref[pl.ds(start, tk), h * LANES:(h + 1) * LANES]
        st_ref[slot, c] = jnp.dot(k, qt_ref[h, :, half * cw:(half + 1) * cw], preferred_element_type=F32)

    def use(kt, c, slot, masked=False):
        mask = None
        if masked:
            qpos = qi * tq + (c % nhalf) * cw + _lane_iota((1, cw))
            mask = (kt * tk + _sub_iota((tk, 1))) <= qpos
        _flash_update(st_ref[slot, c], vt_ref[kt, c // nhalf], m_ref, acc_ref, c, mask)

    assert tk == cw and nhalf % 2 == 0
    n_full = nhalf * qi
    _flash_pipeline(n_full // 2, 2 * nhalf, put, use)
    live = lambda d: [c for c in range(2 * nhalf) if c % nhalf >= d]
    for d in range(nhalf):
        if d + 1 < nhalf:
            for c in live(d + 1):
                put(n_full + d + 1, c, (d + 1) % 2)
        for c in live(d):
            use(n_full + d, c, d % 2, masked=(c % nhalf == d))
    ots = []
    for h in range(2):
        ots.append(jnp.concatenate([_flash_result(acc_ref, h * nhalf + half) for half in range(nhalf)],
                                   axis=1))
    o_ref[...] = jnp.concatenate(ots, axis=0).T.astype(o_ref.dtype)


def _mla_flash(q, k, vt, batch, seq):
    assert FLASH_TK == FLASH_CHUNK
    q3 = q.reshape(batch, seq, -1)
    k3 = k.reshape(batch, seq, -1)
    vt6 = vt.reshape(batch, seq // FLASH_TK, B_HEADS // 2, 2, VT_ROWS, FLASH_TK)
    nchain = 2 * (MLA_TQ // FLASH_CHUNK)
    o = pl.pallas_call(
        _mla_flash_kernel,
        grid=(batch, B_HEADS // 2, seq // MLA_TQ),
        in_specs=[pl.BlockSpec((None, MLA_TQ, 2 * LANES), lambda b, h, i: (b, i, h)),
                  pl.BlockSpec((None, seq, 2 * LANES), lambda b, h, i: (b, 0, h)),
                  pl.BlockSpec((None, seq // FLASH_TK, None, 2, VT_ROWS, FLASH_TK),
                               lambda b, h, i: (b, 0, h, 0, 0, 0))],
        out_specs=pl.BlockSpec((None, MLA_TQ, LANES), lambda b, h, i: (b, i, h)),
        out_shape=jax.ShapeDtypeStruct((batch, seq, B_WIDTH), BF16),
        scratch_shapes=[pltpu.VMEM((2, LANES, MLA_TQ), BF16),
                        pltpu.VMEM((2, nchain, FLASH_TK, FLASH_CHUNK), F32),
                        pltpu.VMEM((nchain, 1, FLASH_CHUNK), F32),
                        pltpu.VMEM((nchain, VT_ROWS, FLASH_CHUNK), F32)],
        compiler_params=_cparams(3),
        name="mla_flash",
    )(q3, k3, vt6)
    return o.reshape(batch * seq, B_WIDTH)


OUT_TM = 256


def _out_kernel(x_ref, p_ref, ocmp_ref, oslc_ref, owin_ref, gate_ref, e_ref, az_ref, ob_ref, bz_ref,
                oc0_ref, oc1_ref, oc2_ref, l0_ref, l1_ref, l2_ref, cz_ref, mg0_ref, mg1_ref, mg2_ref,
                wa_ref, wb_ref, wc_ref, wout_ref, wplg_ref, wple_ref, gpost_ref, o_ref, *tok_refs):
    tok = iter(tok_refs)

    def sig(z):
        return 0.5 * jnp.tanh(0.5 * z) + 0.5

    def f(r):
        if len(r.shape) == 2:
            return r[...].astype(F32)
        dil, n, width = r.shape
        buf = next(tok)
        for res in range(dil):
            blk = r[res].astype(F32)
            for c in range(width // LANES):
                buf[c, pl.ds(res, n, stride=dil), :] = blk[:, c * LANES:(c + 1) * LANES]
        return jnp.concatenate([buf[c] for c in range(width // LANES)], axis=1)

    def silu(z):
        return z * sig(z)

    def mm(a, w_ref):
        return jnp.dot(a.astype(BF16), w_ref[...], preferred_element_type=F32)

    g = sig(gate_ref[...].astype(F32))
    gs = jnp.dot(g.astype(BF16), e_ref[...], preferred_element_type=F32)
    o_a = (gs[:, 0:A_WIDTH] * f(ocmp_ref) + gs[:, A_WIDTH:2 * A_WIDTH] * f(oslc_ref)
           + gs[:, 2 * A_WIDTH:3 * A_WIDTH] * f(owin_ref))
    y_a = mm(o_a * silu(f(az_ref)), wa_ref)
    y_b = mm(f(ob_ref) * silu(f(bz_ref)), wb_ref)
    l0, l1, l2 = f(l0_ref), f(l1_ref), f(l2_ref)
    mx = jnp.maximum(jnp.maximum(l0, l1), l2)
    e0, e1, e2 = jnp.exp2(l0 - mx), jnp.exp2(l1 - mx), jnp.exp2(l2 - mx)
    o_c = (e0 * f(oc0_ref) + e1 * f(oc1_ref) + e2 * f(oc2_ref)) / (e0 + e1 + e2)
    y_c = mm(o_c * silu(f(cz_ref)), wc_ref)
    mix = sig(f(mg0_ref)) * y_a + sig(f(mg1_ref)) * y_b + sig(f(mg2_ref)) * y_c
    y = mm(mix, wout_ref)
    x1 = x_ref[...] + _rms(y, gpost_ref[...])
    o_ref[...] = x1 + sig(mm(x1, wplg_ref)) * mm(p_ref[...], wple_ref)


def _out_layer(x2, p, layer, o_cmp, o_slc, o_win, u, expand, o_b, o_c, lse_c, w, batch, dils):
    t = x2.shape[0]
    seq = t // batch
    tm = min(OUT_TM, seq)
    nper = seq // tm
    row = lambda i: (i, 0)
    ucol = lambda off, width: pl.BlockSpec((tm, width), lambda i: (i, off // width))
    half = pl.BlockSpec((tm, A_WIDTH), row)
    wspec = lambda r, c: pl.BlockSpec((None, r, c), lambda i: (layer, 0, 0))

    def group_spec(dil):
        if dil == 1:
            return half
        return pl.BlockSpec((None, dil, tm // dil, C_WIDTH), lambda i: (i // nper, 0, i % nper, 0))

    def group_view(a, dil):
        return a if dil == 1 else a.reshape(batch, dil, seq // dil, C_WIDTH)

    gspecs = [group_spec(d) for d in dils]
    o_c = [group_view(a, d) for a, d in zip(o_c, dils)]
    lse_c = [group_view(a, d) for a, d in zip(lse_c, dils)]
    n_tok = 2 * sum(1 for d in dils if d > 1)
    in_specs = [pl.BlockSpec((tm, D_MODEL), row),
                pl.BlockSpec((None, tm, PLE_DIM), lambda i: (layer, i, 0)),
                half, half, half,
                ucol(U_GATE, LANES), pl.BlockSpec((LANES, 3 * A_WIDTH), lambda i: (0, 0)),
                ucol(U_AZ, A_WIDTH), half, ucol(U_BZ, B_WIDTH),
                *gspecs, *gspecs, ucol(U_CZ, C_WIDTH),
                ucol(U_MG, D_MODEL), ucol(U_MG + D_MODEL, D_MODEL), ucol(U_MG + 2 * D_MODEL, D_MODEL),
                wspec(A_WIDTH, D_MODEL), wspec(B_WIDTH, D_MODEL), wspec(C_WIDTH, D_MODEL),
                wspec(D_MODEL, D_MODEL), wspec(D_MODEL, D_MODEL), wspec(PLE_DIM, D_MODEL),
                wspec(1, D_MODEL)]
    return pl.pallas_call(
        _out_kernel,
        grid=(t // tm,),
        in_specs=in_specs,
        out_specs=pl.BlockSpec((tm, D_MODEL), row),
        out_shape=jax.ShapeDtypeStruct((t, D_MODEL), F32),
        scratch_shapes=[pltpu.VMEM((C_WIDTH // LANES, tm, LANES), F32)] * n_tok,
        compiler_params=_cparams(1),
        name="out_layer",
    )(x2, p, o_cmp, o_slc, o_win, u, expand, u, o_b, u, *o_c, *lse_c, u, u, u, u,
      w["w_a"], w["w_b"], w["w_c"], w["w_out"], w["w_plg"], w["w_ple"], w["g_post"])


def _overlap_t(seq):
    nc = seq // CMP_STRIDE
    ns = seq // SLC_LEN
    cs = np.arange(nc)[:, None] * CMP_STRIDE
    js = np.arange(ns)[None, :] * SLC_LEN
    ov = np.clip(np.minimum(cs + CMP_LEN, js + SLC_LEN) - np.maximum(cs, js), 0, None).astype(np.float32) / CMP_LEN
    return jnp.asarray(ov.T)


def _block_onehot(seq):
    blk = np.arange(seq)[:, None] // SLC_LEN
    return jnp.asarray((np.arange(LANES)[None, :] % HEAD_DIM == blk).astype(np.float32), dtype=BF16)


def _gate_expand():
    e = np.zeros((LANES, 3 * A_WIDTH), np.float32)
    for br in range(3):
        for i in range(A_HEADS // 2):
            for g in range(A_KV_GROUPS):
                h = g * (A_HEADS // 2) + i
                c0 = br * A_WIDTH + i * LANES + g * HEAD_DIM
                e[h * 3 + br, c0:c0 + HEAD_DIM] = 1.0
    return jnp.asarray(e, dtype=BF16)


def kernel(x, p, positions, g_pre, g_post, w_in, nsa_cmp_pe, nsa_cmp_w1, nsa_cmp_w2, w_a, mla_g_q, mla_w_uq,
           mla_g_kv, mla_w_ukv, w_b, w_c, w_out, w_ple, w_plg):
    batch, seq, _ = x.shape
    depth = w_in.shape[0]
    t = batch * seq
    nch = seq // CMP_STRIDE

    pos = positions.astype(F32)
    part_tabs = _rope_tables(pos.reshape(-1), ROT_DIM, HEAD_DIM, 0)
    mla_tabs = _mla_rope_tables(pos.reshape(-1))
    dils = [dil for _, dil in C_PAIRS]
    assert dils[0] == 1
    perm_tabs = {dil: _rope_tables(_residue_major_positions(pos, dil), ROT_DIM, HEAD_DIM, 0) for dil in dils[1:]}
    w_main, w_groups = _regroup_w_in(w_in)
    g_pre3 = g_pre.reshape(depth, 1, D_MODEL)
    pe2, w1d, w2d = _compress_weights(nsa_cmp_pe, nsa_cmp_w1, nsa_cmp_w2)
    wq = mla_w_uq.reshape(depth, Q_LORA, B_HEADS, QK_NOPE + QK_ROPE)
    wq = _mla_lanes(wq[..., QK_NOPE:QK_NOPE + QK_ROPE // 2], wq[..., QK_NOPE + QK_ROPE // 2:], wq[..., :QK_NOPE])
    wq = wq.reshape(depth, Q_LORA, -1).astype(BF16)
    wkv = mla_w_ukv.reshape(depth, KV_LORA, B_HEADS, QK_NOPE + V_DIM)
    no_rope = jnp.zeros(wkv.shape[:-1] + (QK_ROPE // 2,), wkv.dtype)
    wk = _mla_lanes(no_rope, no_rope, wkv[..., :QK_NOPE]).reshape(depth, KV_LORA, -1).astype(BF16)
    wv = wkv[..., QK_NOPE:].reshape(depth, KV_LORA, B_WIDTH).astype(BF16)
    gq3 = mla_g_q.reshape(depth, 1, Q_LORA)
    gkv3 = mla_g_kv.reshape(depth, 1, KV_LORA)
    w_a_p = w_a.reshape(depth, 2, 4, HEAD_DIM, D_MODEL).swapaxes(1, 2).reshape(depth, A_WIDTH, D_MODEL)
    wts = {"w_a": w_a_p.astype(BF16), "w_b": w_b.astype(BF16), "w_c": w_c.astype(BF16),
           "w_out": w_out.astype(BF16), "w_plg": w_plg.astype(BF16), "w_ple": w_ple.astype(BF16),
           "g_post": g_post.reshape(depth, 1, D_MODEL)}
    p3 = p.reshape(depth, t, PLE_DIM)
    ovt = _overlap_t(seq)
    onehot = _block_onehot(seq)
    expand = _gate_expand()

    x2 = x.reshape(t, D_MODEL)
    for layer in range(depth):
        h, h_res = _prenorm(x2, g_pre3, layer, batch, dils[1:])
        u = _inproj(h, w_main, layer, part_tabs, ROPE_COLS)
        u3 = u.reshape(batch, seq, DP)
        kc, vct = _compress(u3, pe2, w1d, w2d, layer)
        o_cmp, biast = _cmp_select(u3, kc, vct, ovt)
        o_slc = _slc_attention(u3, biast, onehot)
        (o_win,) = _banded(u, nseq=batch, seqlen=seq, q_off=U_AQ, k_off=U_KWIN, v_off=U_VWIN,
                           kv_width=LANES, max_dist=WIN - 1, with_lse=False)
        q_b, k_b, vt_b = _mla_prep(u, mla_tabs, gq3, gkv3, wq, wk, wv, layer)
        o_b = _mla_flash(q_b, k_b, vt_b, batch, seq)
        o_c, lse_c = [], []
        for gi, (window, dil) in enumerate(C_PAIRS):
            if dil == 1:
                o_g, lse_g = _banded(u, nseq=batch, seqlen=seq, q_off=U_CQ0, k_off=U_CK0, v_off=U_CV0,
                                     kv_width=C_WIDTH, max_dist=window, with_lse=True)
            else:
                u_g = _inproj(h_res[gi - 1], w_groups[gi - 1], layer, perm_tabs[dil], DG_ROPE_COLS)
                o_g, lse_g = _banded(u_g, nseq=batch * dil, seqlen=seq // dil, q_off=0, k_off=C_WIDTH,
                                     v_off=2 * C_WIDTH, kv_width=C_WIDTH, max_dist=window // dil, with_lse=True)
            o_c.append(o_g)
            lse_c.append(lse_g)
        x2 = _out_layer(x2, p3, layer, o_cmp.reshape(t, A_WIDTH), o_slc.reshape(t, A_WIDTH), o_win, u, expand,
                        o_b, o_c, lse_c, wts, batch, dils)
    return x2.reshape(batch, seq, D_MODEL)
```

```python
import functools

import numpy as np
import jax
import jax.numpy as jnp
from jax import lax
from jax.experimental import pallas as pl
from jax.experimental.pallas import tpu as pltpu

F32 = jnp.float32
BF16 = jnp.bfloat16

D_MODEL = 1024
PLE_DIM = 256
ROPE_THETA = 500000.0
HEAD_DIM = 64
ROT_DIM = HEAD_DIM // 4
EPS = 1e-6
NEG = -1e30
BIG = 1e30
LANES = 128

A_HEADS = 8
A_KV_GROUPS = 2
A_WIDTH = A_HEADS * HEAD_DIM
CMP_LEN = 32
CMP_STRIDE = 16
CMP_HID = 256
SLC_LEN = 64
SLC_TOPK = 16
WIN = 512

B_HEADS = 8
Q_LORA = 384
KV_LORA = 128
QK_NOPE = 64
QK_ROPE = 32
V_DIM = 64
B_WIDTH = B_HEADS * V_DIM

C_PAIRS = ((128, 1), (512, 4), (2048, 16))
C_GROUPS = len(C_PAIRS)
C_HEADS = 8
C_WIDTH = C_HEADS * HEAD_DIM

IN_SIZES = (A_WIDTH, 6 * A_KV_GROUPS * HEAD_DIM, 3 * A_HEADS, A_WIDTH, Q_LORA, KV_LORA, QK_ROPE,
            B_WIDTH, C_GROUPS * 3 * C_WIDTH, C_WIDTH, 3 * D_MODEL)
IN_OFF = tuple(int(v) for v in np.cumsum((0,) + IN_SIZES))
(OFF_AQ, OFF_AKV, OFF_AG, OFF_AZ, OFF_BCQ, OFF_BCKV, OFF_BKR, OFF_BZ, OFF_CQKV, OFF_CZ, OFF_MG) = IN_OFF[:-1]

U_AQ = 0
U_CQ0 = 512
U_CK0 = 1024
U_KCMP = 1536
U_KSLC = 1664
U_KWIN = 1792
ROPE_COLS = 1920
U_VCMP = 1920
U_AZ = 2048
U_BZ = 2560
U_CV0 = 3072
U_CZ = 3584
U_MG = 4096
U_VSLC = 7168
U_VWIN = 7296
U_GATE = 7424
U_BCKV = 7552
U_BCQ = 7680
U_BKR = 8064
DP = 8192
DG_COLS = 3 * C_WIDTH
DG_ROPE_COLS = 2 * C_WIDTH

LOG2E = float(np.log2(np.e))
Q_SCALE = float(HEAD_DIM ** -0.5) * LOG2E
MLA_SCALE = float((QK_NOPE + QK_ROPE) ** -0.5) * LOG2E
VMEM_LIMIT = 48 * 1024 * 1024


def _cparams(n_axes):
    return pltpu.CompilerParams(dimension_semantics=("arbitrary",) * n_axes, vmem_limit_bytes=VMEM_LIMIT)


def _lane_iota(shape):
    return lax.broadcasted_iota(jnp.int32, shape, len(shape) - 1)


def _sub_iota(shape):
    return lax.broadcasted_iota(jnp.int32, shape, len(shape) - 2)


def _pair_cols(w):
    lead = w.shape[:-1]
    return w.reshape(lead + (2, 4, HEAD_DIM)).swapaxes(-3, -2).reshape(lead + (A_WIDTH,))


def _regroup_w_in(w_in):
    def col(off, n):
        return w_in[..., off:off + n]
    z = lambda n: jnp.zeros(w_in.shape[:-1] + (n,), w_in.dtype)
    akv = lambda which: col(OFF_AKV + which * 128, 128)
    cq = lambda gi, t: col(OFF_CQKV + gi * 3 * C_WIDTH + t * C_WIDTH, C_WIDTH)
    pieces = [_pair_cols(col(OFF_AQ, A_WIDTH)) * Q_SCALE, cq(0, 0) * Q_SCALE, cq(0, 1)]
    pieces += [akv(0), akv(2), akv(4)]
    pieces += [akv(1)]
    pieces += [_pair_cols(col(OFF_AZ, A_WIDTH)), col(OFF_BZ, B_WIDTH), cq(0, 2)]
    pieces += [col(OFF_CZ, C_WIDTH), col(OFF_MG, 3 * D_MODEL)]
    pieces += [akv(3), akv(5)]
    pieces += [col(OFF_AG, 3 * A_HEADS), z(128 - 3 * A_HEADS)]
    pieces += [col(OFF_BCKV, KV_LORA), col(OFF_BCQ, Q_LORA)]
    kr = col(OFF_BKR, QK_ROPE)
    pieces += [_mla_lanes(kr[..., :QK_ROPE // 2], kr[..., QK_ROPE // 2:], z(QK_NOPE))]
    main = jnp.concatenate(pieces, axis=-1).astype(BF16)
    assert main.shape[-1] == DP
    groups = [jnp.concatenate([cq(gi, 0) * Q_SCALE, cq(gi, 1), cq(gi, 2)], axis=-1).astype(BF16)
              for gi in range(1, C_GROUPS)]
    return main, groups


def _rope_tables(pos, dim, period, offset):
    half = dim // 2
    ch = np.arange(LANES) % period - offset
    in_span = (ch >= 0) & (ch < dim)
    cos, sin = _trig_lanes(pos, dim, np.where(in_span, ch % half, -1))
    first = jnp.asarray(in_span & (ch < half))[None, :]
    second = jnp.asarray(in_span & (ch >= half))[None, :]
    return cos, jnp.where(second, sin, 0.0), jnp.where(first, -sin, 0.0)


def _trig_lanes(pos, dim, freq_of_lane):
    half = dim // 2
    inv = ROPE_THETA ** (-jnp.arange(0, dim, 2, dtype=F32) / dim)
    ang = pos[:, None] * inv[None, :]
    spread = jnp.asarray((freq_of_lane[None, :] == np.arange(half)[:, None]).astype(np.float32))
    expand = lambda a: jnp.dot(a, spread, precision=lax.Precision.HIGHEST)
    return expand(jnp.cos(ang)) + jnp.asarray((freq_of_lane < 0).astype(np.float32))[None, :], expand(jnp.sin(ang))


def _mla_lanes(rope_first, rope_second, nope):
    pad = jnp.zeros(nope.shape[:-1] + (LANES - QK_NOPE - QK_ROPE,), nope.dtype)
    cut = LANES // 2 - QK_ROPE // 2
    return jnp.concatenate([rope_first, nope[..., :cut], rope_second, nope[..., cut:], pad], axis=-1)


def _mla_rope_tables(pos):
    half = QK_ROPE // 2
    lane = np.arange(LANES)
    first, second = lane < half, (lane >= LANES // 2) & (lane < LANES // 2 + half)
    cos, sin = _trig_lanes(pos, QK_ROPE, np.where(first, lane, np.where(second, lane - LANES // 2, -1)))
    return cos, jnp.where(jnp.asarray(first)[None, :], -sin, sin)


def _residue_major_positions(positions, dil):
    b, s = positions.shape
    return positions.reshape(b, s // dil, dil).swapaxes(1, 2).reshape(-1)


NORM_TM = 512
IN_TM = 1024
IN_TN = 2048
IN_CHUNK = 512


def _norm_kernel(x_ref, g_ref, o_ref, *rest, dils):
    o_dil, y_ref = rest[:len(dils)], rest[len(dils)]
    x = x_ref[...]
    ms = jnp.mean(x * x, axis=-1, keepdims=True)
    y = x * lax.rsqrt(ms + EPS) * g_ref[...]
    o_ref[...] = y.astype(o_ref.dtype)
    tm = x.shape[0]
    nlane = y_ref.shape[0]
    for c in range(nlane):
        y_ref[c] = y[:, c * LANES:(c + 1) * LANES]
    for o_d, dil in zip(o_dil, dils):
        for r in range(dil):
            for c in range(nlane):
                o_d[r, :, c * LANES:(c + 1) * LANES] = (
                    y_ref[c, pl.ds(r, tm // dil, stride=dil), :].astype(o_d.dtype))


def _prenorm(x2, g_pre, layer, batch, dils):
    t = x2.shape[0]
    seq = t // batch
    tm = min(NORM_TM, seq)
    nper = seq // tm
    out_specs = [pl.BlockSpec((tm, D_MODEL), lambda b, i: (b * nper + i, 0))]
    out_shape = [jax.ShapeDtypeStruct((t, D_MODEL), BF16)]
    for dil in dils:
        out_specs.append(pl.BlockSpec((None, dil, tm // dil, D_MODEL), lambda b, i: (b, 0, i, 0)))
        out_shape.append(jax.ShapeDtypeStruct((batch, dil, seq // dil, D_MODEL), BF16))
    outs = pl.pallas_call(
        functools.partial(_norm_kernel, dils=tuple(dils)),
        grid=(batch, nper),
        in_specs=[pl.BlockSpec((tm, D_MODEL), lambda b, i: (b * nper + i, 0)),
                  pl.BlockSpec((None, 1, D_MODEL), lambda b, i: (layer, 0, 0))],
        out_specs=out_specs,
        out_shape=out_shape,
        scratch_shapes=[pltpu.VMEM((D_MODEL // LANES, tm, LANES), F32)],
        compiler_params=_cparams(2),
        name="prenorm",
    )(x2, g_pre)
    return outs[0], [o.reshape(t, D_MODEL) for o in outs[1:]]


def _inproj_kernel(h_ref, w_ref, c_ref, s1_ref, s2_ref, o_ref, *, rope_cols):
    j = pl.program_id(0)
    tn = w_ref.shape[1]
    nsub = tn // LANES
    full_tiles, rem = divmod(rope_cols // LANES, nsub)
    chunk = IN_CHUNK if tn % IN_CHUNK == 0 else tn
    per = chunk // LANES

    def emit(n_rope):
        for g in range(tn // chunk):
            acc = jnp.dot(h_ref[...], w_ref[:, g * chunk:(g + 1) * chunk], preferred_element_type=F32)
            for k in range(per):
                c = g * per + k
                xc = acc[:, k * LANES:(k + 1) * LANES]
                if c < n_rope:
                    xc = (xc * c_ref[...] + pltpu.roll(xc, ROT_DIM // 2, 1) * s1_ref[...]
                          + pltpu.roll(xc, LANES - ROT_DIM // 2, 1) * s2_ref[...])
                o_ref[:, c * LANES:(c + 1) * LANES] = xc.astype(o_ref.dtype)

    pl.when(j < full_tiles)(lambda: emit(nsub))
    pl.when(j == full_tiles)(lambda: emit(rem))
    pl.when(j > full_tiles)(lambda: emit(0))


def _inproj(h, w, layer, tabs, rope_cols):
    t = h.shape[0]
    ncols = w.shape[-1]
    tn = min(IN_TN, ncols)
    ntile = ncols // tn
    tm = min(IN_TM, t)
    rope_tiles = -(-rope_cols // tn)
    row = lambda j, i: (i, 0)
    tab = lambda j, i: (jnp.where(j < rope_tiles, i, 0), 0)
    return pl.pallas_call(
        functools.partial(_inproj_kernel, rope_cols=rope_cols),
        grid=(ntile, t // tm),
        in_specs=[pl.BlockSpec((tm, D_MODEL), row),
                  pl.BlockSpec((None, D_MODEL, tn), lambda j, i: (layer, 0, j)),
                  pl.BlockSpec((tm, LANES), tab), pl.BlockSpec((tm, LANES), tab), pl.BlockSpec((tm, LANES), tab)],
        out_specs=pl.BlockSpec((tm, tn), lambda j, i: (i, j)),
        out_shape=jax.ShapeDtypeStruct((t, ntile * tn), BF16),
        compiler_params=_cparams(2),
        name="inproj",
    )(h, w, *tabs)


def _compress_kernel(k_ref, v_ref, pe_ref, w1_ref, w2_ref, kc_ref, vct_ref, x_ref):
    nch = x_ref.shape[0] // CMP_STRIDE
    for which, src in enumerate((k_ref, v_ref)):
        x_ref[...] = src[...].astype(F32)
        rows = [x_ref[pl.ds(l, nch, stride=CMP_STRIDE), :] for l in range(CMP_STRIDE)]
        halves = []
        for part in range(CMP_LEN // CMP_STRIDE):
            xs = [(rows[l] + pe_ref[which, part * CMP_STRIDE + l:part * CMP_STRIDE + l + 1, :]).astype(BF16)
                  for l in range(CMP_STRIDE)]
            halves.append(jnp.dot(jnp.concatenate(xs, axis=1), w1_ref[which, part], preferred_element_type=F32))
        hid = halves[0] + pltpu.roll(halves[1], nch - 1, 0)
        act = hid * jax.nn.sigmoid(hid)
        tok = jnp.dot(act.astype(BF16), w2_ref[which], preferred_element_type=F32)
        if which == 0:
            kc_ref[...] = tok.astype(kc_ref.dtype)
        else:
            vct_ref[...] = tok.T.astype(vct_ref.dtype)


def _compress_weights(pe, w1, w2):
    depth = w1.shape[0]
    g = A_KV_GROUPS
    assert CMP_LEN == 2 * CMP_STRIDE and g == 2
    w1r = w1.reshape(depth, 2, CMP_LEN, HEAD_DIM, CMP_HID)
    z1 = jnp.zeros_like(w1r)
    w1d = jnp.stack([jnp.concatenate([w1r, z1], axis=-1), jnp.concatenate([z1, w1r], axis=-1)], axis=3)
    w1d = w1d.reshape(depth, 2, 2, CMP_STRIDE * g * HEAD_DIM, g * CMP_HID).astype(BF16)
    z2 = jnp.zeros_like(w2)
    w2d = jnp.concatenate([jnp.concatenate([w2, z2], axis=-1), jnp.concatenate([z2, w2], axis=-1)], axis=2)
    pe2 = jnp.concatenate([pe, pe], axis=-1)
    return pe2, w1d, w2d.astype(BF16)


def _compress(u3, pe2, w1d, w2d, layer):
    b, s, _ = u3.shape
    nch = s // CMP_STRIDE
    slab = lambda off: pl.BlockSpec((None, s, LANES), lambda i: (i, 0, off // LANES))
    return pl.pallas_call(
        _compress_kernel,
        grid=(b,),
        in_specs=[slab(U_KCMP), slab(U_VCMP),
                  pl.BlockSpec((None,) + pe2.shape[1:], lambda i: (layer, 0, 0, 0)),
                  pl.BlockSpec((None,) + w1d.shape[1:], lambda i: (layer, 0, 0, 0, 0)),
                  pl.BlockSpec((None,) + w2d.shape[1:], lambda i: (layer, 0, 0, 0))],
        out_specs=[pl.BlockSpec((None, nch, LANES), lambda i: (i, 0, 0)),
                   pl.BlockSpec((None, LANES, nch), lambda i: (i, 0, 0))],
        out_shape=[jax.ShapeDtypeStruct((b, nch, LANES), BF16), jax.ShapeDtypeStruct((b, LANES, nch), BF16)],
        scratch_shapes=[pltpu.VMEM((s, LANES), F32)],
        compiler_params=_cparams(1),
        name="nsa_compress",
    )(u3, u3, pe2, w1d, w2d)


CS_TQ = 128


def _cmp_select_kernel(q_ref, kc_ref, vct_ref, ovt_ref, o_ref, bias_ref, sc_ref):
    tq = CS_TQ
    qi = pl.program_id(1)
    nc = kc_ref.shape[0]
    ns = ovt_ref.shape[0]
    tpos = qi * tq + _lane_iota((1, tq))
    kc = kc_ref[...]
    lane = _lane_iota((1, LANES))
    kst = jnp.concatenate([jnp.where(lane < HEAD_DIM, kc, jnp.zeros_like(kc)),
                           jnp.where(lane >= HEAD_DIM, kc, jnp.zeros_like(kc))], axis=0)
    c_end = _sub_iota((nc, 1)) * CMP_STRIDE + (CMP_LEN - 1)
    cv = c_end <= tpos
    vct = vct_ref[...]
    psum = [jnp.zeros((nc, tq), F32), jnp.zeros((nc, tq), F32)]
    sub = _sub_iota((LANES, 1))
    for i in range(A_HEADS // 2):
        qb = q_ref[:, i * LANES:(i + 1) * LANES]
        st = lax.dot_general(kst, qb, (((1,), (1,)), ((), ())), preferred_element_type=F32)
        ot = []
        for g in range(A_KV_GROUPS):
            s = jnp.where(cv, st[g * nc:(g + 1) * nc], NEG)
            m = jnp.max(s, axis=0, keepdims=True)
            e = jnp.where(cv, jnp.exp2(s - m), 0.0)
            den = jnp.sum(e, axis=0, keepdims=True)
            p = e / jnp.where(den > 0.0, den, 1.0)
            psum[g] = psum[g] + p
            ot.append(jnp.dot(vct, p.astype(BF16), preferred_element_type=F32))
        o_pair = jnp.where(sub < HEAD_DIM, ot[0], ot[1])
        o_ref[:, i * LANES:(i + 1) * LANES] = o_pair.T.astype(o_ref.dtype)
    jdx = _sub_iota((ns, 1))
    cur = tpos // SLC_LEN
    valid = jdx <= cur
    forced = (jdx == 0) | (jdx == cur) | (jdx == cur - 1)
    for g in range(A_KV_GROUPS):
        imp = jnp.dot(ovt_ref[...], psum[g], preferred_element_type=F32, precision=lax.Precision.HIGHEST)
        score = jnp.where(forced, BIG, jnp.where(valid, imp, NEG))
        sc_ref[...] = score
        slab = 8
        parts = [score[v * slab:(v + 1) * slab] for v in range(ns // slab)]
        cnts = [jnp.zeros((slab, tq), F32) for _ in parts]
        for jp in range(ns):
            row = sc_ref[jp:jp + 1, :]
            for v, sv in enumerate(parts):
                if v * slab > jp:
                    ahead = row >= sv
                elif (v + 1) * slab - 1 < jp:
                    ahead = row > sv
                else:
                    tie = jnp.where(jdx[v * slab:(v + 1) * slab] > jp, 1.0, 0.0)
                    cnts[v] = cnts[v] + jnp.where(row == sv, tie, 0.0)
                    ahead = row > sv
                cnts[v] = cnts[v] + jnp.where(ahead, 1.0, 0.0)
        cnt = jnp.concatenate(cnts, axis=0)
        bias = jnp.where(cnt < float(min(SLC_TOPK, ns)), 0.0, NEG)
        bias_ref[g * ns:(g + 1) * ns, :] = bias.astype(bias_ref.dtype)


def _cmp_select(u3, kc, vct, ovt):
    b, s, _ = u3.shape
    nc = kc.shape[1]
    ns = s // SLC_LEN
    assert A_KV_GROUPS * ns == LANES
    return pl.pallas_call(
        _cmp_select_kernel,
        grid=(b, s // CS_TQ),
        in_specs=[pl.BlockSpec((None, CS_TQ, A_WIDTH), lambda i, q: (i, q, U_AQ // A_WIDTH)),
                  pl.BlockSpec((None, nc, LANES), lambda i, q: (i, 0, 0)),
                  pl.BlockSpec((None, LANES, nc), lambda i, q: (i, 0, 0)),
                  pl.BlockSpec((ns, nc), lambda i, q: (0, 0))],
        out_specs=[pl.BlockSpec((None, CS_TQ, A_WIDTH), lambda i, q: (i, q, 0)),
                   pl.BlockSpec((None, LANES, CS_TQ), lambda i, q: (i, 0, q))],
        out_shape=[jax.ShapeDtypeStruct((b, s, A_WIDTH), BF16), jax.ShapeDtypeStruct((b, LANES, s), BF16)],
        scratch_shapes=[pltpu.VMEM((ns, CS_TQ), F32)],
        compiler_params=_cparams(2),
        name="nsa_cmp_select",
    )(u3, kc, vct, ovt)


FLASH_TK = 256
FLASH_CHUNK = 256
FLASH_UNROLL = 4
ONES_ROWS = 16
VT_ROWS = HEAD_DIM + ONES_ROWS


def _head_values_t(vt_pair):
    ones = jnp.ones((ONES_ROWS, vt_pair.shape[1]), vt_pair.dtype)
    return [jnp.concatenate([vt_pair[h * HEAD_DIM:(h + 1) * HEAD_DIM], ones], axis=0) for h in range(2)]


def _flash_reset(m_ref, acc_ref):
    m_ref[...] = jnp.full(m_ref.shape, NEG, F32)
    acc_ref[...] = jnp.zeros(acc_ref.shape, F32)


def _flash_update(st, vt, m_ref, acc_ref, idx, mask):
    if mask is not None:
        st = jnp.where(mask, st, NEG)
    m_old = m_ref[idx]
    m_new = jnp.maximum(m_old, jnp.max(st, axis=0, keepdims=True))
    alpha = jnp.exp2(m_old - m_new)
    p = jnp.exp2((st - m_new).astype(BF16))
    acc_ref[idx] = alpha * acc_ref[idx] + jnp.dot(vt, p, preferred_element_type=F32)
    m_ref[idx] = m_new


def _flash_result(acc_ref, idx):
    acc = acc_ref[idx]
    return acc[0:HEAD_DIM] / acc[HEAD_DIM:HEAD_DIM + 1]


def _flash_pipeline(n_pairs, n_chain, put, use):
    for c in range(n_chain):
        put(0, c, 0)

    def run(t0, count):
        for t in range(count):
            for c in range(n_chain):
                put(t0 + t + 1, c, (t + 1) % 2)
                use(t0 + t, c, t % 2)

    per = FLASH_UNROLL // 2
    n_long = n_pairs // per

    def long_body(j, carry):
        run(FLASH_UNROLL * j, FLASH_UNROLL)
        return carry

    def pair_body(j, carry):
        run(FLASH_UNROLL * n_long + 2 * j, 2)
        return carry

    lax.fori_loop(0, n_long, long_body, 0)
    lax.fori_loop(0, n_pairs - per * n_long, pair_body, 0)


def _transpose_bf16(x):
    return x.astype(F32).T.astype(BF16)


def _merge_rows(lo, hi):
    return jnp.where(_sub_iota((LANES, 1)) < HEAD_DIM, lo, hi)


SLC_TQ = 256


def _slc_kernel(q_ref, biast_ref, k_ref, v_ref, oh_ref, o_ref, vt_ref, qa_ref, st_ref, m_ref, acc_ref):
    tq, tk = SLC_TQ, FLASH_TK
    qi = pl.program_id(1)
    nblk = A_HEADS // 2
    sub = _sub_iota((LANES, 1))

    @pl.when(qi == 0)
    def _():
        for kt in range(vt_ref.shape[0]):
            for g, vt in enumerate(_head_values_t(_transpose_bf16(v_ref[kt * tk:(kt + 1) * tk, :]))):
                vt_ref[kt, g] = vt

    nchain = 2 * nblk
    biast = biast_ref[...]
    zero = jnp.zeros_like(biast)
    mine = (sub < HEAD_DIM, sub >= HEAD_DIM)
    for i in range(nblk):
        qt = _transpose_bf16(q_ref[:, i * LANES:(i + 1) * LANES])
        for h in range(2):
            qa_ref[2 * i + h] = jnp.concatenate([jnp.where(mine[h], qt, zero), jnp.where(mine[h], biast, zero)],
                                                axis=0)
    _flash_reset(m_ref, acc_ref)

    def put(kt, c, slot):
        start = pl.multiple_of(kt * tk, tk)
        k = jnp.concatenate([k_ref[pl.ds(start, tk), :], oh_ref[pl.ds(start, tk), :]], axis=1)
        st_ref[slot, c] = jnp.dot(k, qa_ref[c], preferred_element_type=F32)

    def use(kt, c, slot, mask=None):
        _flash_update(st_ref[slot, c], vt_ref[kt, c % 2], m_ref, acc_ref, c, mask)

    n_full = qi
    n_pairs = n_full // 2
    _flash_pipeline(n_pairs, nchain, put, use)
    causal = (n_full * tk + _sub_iota((tk, 1))) <= (qi * tq + _lane_iota((1, tq)))

    @pl.when(n_full == 2 * n_pairs)
    def _():
        for c in range(nchain):
            use(n_full, c, 0, causal)

    @pl.when(n_full != 2 * n_pairs)
    def _():
        for c in range(nchain):
            put(n_full, c, 1)
            use(n_full - 1, c, 0)
        for c in range(nchain):
            use(n_full, c, 1, causal)

    for i in range(nblk):
        ot = jnp.concatenate([_flash_result(acc_ref, 2 * i), _flash_result(acc_ref, 2 * i + 1)], axis=0)
        o_ref[:, i * LANES:(i + 1) * LANES] = ot.T.astype(o_ref.dtype)


def _slc_attention(u3, biast, onehot):
    b, s, _ = u3.shape
    nblk = A_HEADS
    assert SLC_TQ == FLASH_CHUNK == FLASH_TK
    return pl.pallas_call(
        _slc_kernel,
        grid=(b, s // SLC_TQ),
        in_specs=[pl.BlockSpec((None, SLC_TQ, A_WIDTH), lambda i, q: (i, q, U_AQ // A_WIDTH)),
                  pl.BlockSpec((None, LANES, SLC_TQ), lambda i, q: (i, 0, q)),
                  pl.BlockSpec((None, s, LANES), lambda i, q: (i, 0, U_KSLC // LANES)),
                  pl.BlockSpec((None, s, LANES), lambda i, q: (i, 0, U_VSLC // LANES)),
                  pl.BlockSpec((s, LANES), lambda i, q: (0, 0))],
        out_specs=pl.BlockSpec((None, SLC_TQ, A_WIDTH), lambda i, q: (i, q, 0)),
        out_shape=jax.ShapeDtypeStruct((b, s, A_WIDTH), BF16),
        scratch_shapes=[pltpu.VMEM((s // FLASH_TK, 2, VT_ROWS, FLASH_TK), BF16),
                        pltpu.VMEM((nblk, 2 * LANES, FLASH_CHUNK), BF16),
                        pltpu.VMEM((2, nblk, FLASH_TK, FLASH_CHUNK), F32),
                        pltpu.VMEM((nblk, 1, FLASH_CHUNK), F32),
                        pltpu.VMEM((nblk, VT_ROWS, FLASH_CHUNK), F32)],
        compiler_params=_cparams(2),
        name="nsa_selected",
    )(u3, biast, u3, u3, onehot)


BAND_T = 128
BAND_ROWS = 1024
BAND_LOOKAHEAD = 2


def _banded_kernel(*refs, nprev, max_dist, shared_kv, with_lse):
    t = BAND_T
    q_ref, k_ref, v_ref, o_ref = refs[:4]
    lse_ref = refs[4] if with_lse else None
    qi = pl.program_id(1)
    nsub = q_ref.shape[0] // t
    nblk = q_ref.shape[1] // LANES
    nk = (nprev + 1) * t
    col = _lane_iota((1, 2 * t))
    base = jnp.where(col >= t, col - t, col) - _sub_iota((nk, 1))
    sub = _sub_iota((LANES, 1))

    def window(s):
        return pl.multiple_of(jnp.maximum(qi * nsub + s - nprev, 0) * t, t)

    def band_cap(dist):
        rel = base + dist
        return jnp.where((rel >= 0) & (rel <= max_dist), BIG, NEG)

    assert nsub >= nprev
    regular = band_cap(nprev * t)
    clamped = [band_cap(s * t) for s in range(nprev)]

    def cap(s):
        return jnp.where(qi == 0, clamped[s], regular) if s < nprev else regular

    def scores(s, i):
        start = window(s)
        cols = slice(None) if shared_kv else slice(i * LANES, (i + 1) * LANES)
        k = k_ref[pl.ds(start, nk), cols]
        qb = q_ref[s * t:(s + 1) * t, i * LANES:(i + 1) * LANES]
        lane = _lane_iota((1, LANES))
        zero = jnp.zeros_like(qb)
        qs = jnp.concatenate([jnp.where(lane < HEAD_DIM, qb, zero), jnp.where(lane >= HEAD_DIM, qb, zero)], axis=0)
        return lax.dot_general(k, qs, (((1,), (1,)), ((), ())), preferred_element_type=F32)

    def finish(s, i, st):
        cols = slice(None) if shared_kv else slice(i * LANES, (i + 1) * LANES)
        v = v_ref[pl.ds(window(s), nk), cols]
        st = jnp.minimum(st, cap(s))
        m = jnp.max(st, axis=0, keepdims=True)
        p = jnp.exp2(st - m)
        l = jnp.sum(p, axis=0, keepdims=True)
        ot = lax.dot_general(v, p.astype(BF16), (((0,), (0,)), ((), ())), preferred_element_type=F32) / l
        rows, sl = slice(s * t, (s + 1) * t), slice(i * LANES, (i + 1) * LANES)
        o_ref[rows, sl] = _merge_rows(ot[:, 0:t], ot[:, t:2 * t]).T.astype(o_ref.dtype)
        if with_lse:
            lse = m + jnp.log2(l)
            lse_ref[rows, sl] = _merge_rows(jnp.broadcast_to(lse[:, 0:t], (LANES, t)),
                                            jnp.broadcast_to(lse[:, t:2 * t], (LANES, t))).T

    items = [(s, i) for s in range(nsub) for i in range(nblk)]
    pending = [scores(*it) for it in items[:BAND_LOOKAHEAD]]
    for j, it in enumerate(items):
        if j + BAND_LOOKAHEAD < len(items):
            pending.append(scores(*items[j + BAND_LOOKAHEAD]))
        finish(*it, pending[j])


def _banded(src, *, nseq, seqlen, q_off, k_off, v_off, kv_width, max_dist, with_lse):
    t = BAND_T
    rows = min(BAND_ROWS, seqlen)
    nq = seqlen // rows
    nprev = -(-max_dist // t)
    shared = kv_width == LANES
    assert (nprev + 1) * t <= seqlen and seqlen % rows == 0 and rows % t == 0
    qo = pl.BlockSpec((rows, A_WIDTH), lambda n, q: (n * nq + q, q_off // A_WIDTH))
    out = pl.BlockSpec((rows, A_WIDTH), lambda n, q: (n * nq + q, 0))
    in_specs = [qo,
                pl.BlockSpec((seqlen, kv_width), lambda n, q: (n, k_off // kv_width)),
                pl.BlockSpec((seqlen, kv_width), lambda n, q: (n, v_off // kv_width))]
    out_specs = [out]
    out_shape = [jax.ShapeDtypeStruct((nseq * seqlen, A_WIDTH), BF16)]
    if with_lse:
        out_specs.append(out)
        out_shape.append(jax.ShapeDtypeStruct((nseq * seqlen, A_WIDTH), F32))
    return pl.pallas_call(
        functools.partial(_banded_kernel, nprev=nprev, max_dist=max_dist, shared_kv=shared, with_lse=with_lse),
        grid=(nseq, nq),
        in_specs=in_specs,
        out_specs=out_specs,
        out_shape=out_shape,
        compiler_params=_cparams(2),
        name="banded_attention",
    )(src, src, src)


MLA_TM = 512


def _rms(x, g):
    return x * lax.rsqrt(jnp.mean(x * x, axis=-1, keepdims=True) + EPS) * g


def _mla_prep_kernel(cq_ref, ckv_ref, kr_ref, c_ref, s_ref, gq_ref, gkv_ref, wq_ref, wk_ref, wv_ref,
                     q_out, k_out, vt_out):
    tk = FLASH_TK
    cm, sm = c_ref[...], s_ref[...]

    def rope(xc):
        return xc * cm + pltpu.roll(xc, LANES // 2, 1) * sm

    qn = _rms(cq_ref[...].astype(F32), gq_ref[...]).astype(BF16)
    q = jnp.dot(qn, wq_ref[...], preferred_element_type=F32)
    kvn = _rms(ckv_ref[...].astype(F32), gkv_ref[...]).astype(BF16)
    kk = jnp.dot(kvn, wk_ref[...], preferred_element_type=F32)
    kr = rope(kr_ref[...].astype(F32))
    for h in range(B_HEADS):
        sl = slice(h * LANES, (h + 1) * LANES)
        q_out[:, sl] = (rope(q[:, sl]) * MLA_SCALE).astype(q_out.dtype)
        k_out[:, sl] = (kk[:, sl] + kr).astype(k_out.dtype)
    v = jnp.dot(kvn, wv_ref[...], preferred_element_type=F32)
    for j in range(v.shape[0] // tk):
        for hp in range(B_HEADS // 2):
            pair_t = v[j * tk:(j + 1) * tk, hp * LANES:(hp + 1) * LANES].T.astype(vt_out.dtype)
            for h, vt in enumerate(_head_values_t(pair_t)):
                vt_out[j, hp, h] = vt


def _mla_prep(u, tabs, g_q, g_kv, wq, wk, wv, layer):
    t = u.shape[0]
    tm = min(MLA_TM, t)
    row = lambda i: (i, 0)
    wide = B_HEADS * LANES
    assert U_BCQ % Q_LORA == 0 and tm % FLASH_TK == 0
    return pl.pallas_call(
        _mla_prep_kernel,
        grid=(t // tm,),
        in_specs=[pl.BlockSpec((tm, Q_LORA), lambda i: (i, U_BCQ // Q_LORA)),
                  pl.BlockSpec((tm, LANES), lambda i: (i, U_BCKV // LANES)),
                  pl.BlockSpec((tm, LANES), lambda i: (i, U_BKR // LANES)),
                  pl.BlockSpec((tm, LANES), row), pl.BlockSpec((tm, LANES), row),
                  pl.BlockSpec((None, 1, Q_LORA), lambda i: (layer, 0, 0)),
                  pl.BlockSpec((None, 1, KV_LORA), lambda i: (layer, 0, 0)),
                  pl.BlockSpec((None, Q_LORA, wide), lambda i: (layer, 0, 0)),
                  pl.BlockSpec((None, KV_LORA, wide), lambda i: (layer, 0, 0)),
                  pl.BlockSpec((None, KV_LORA, B_WIDTH), lambda i: (layer, 0, 0))],
        out_specs=[pl.BlockSpec((tm, wide), row), pl.BlockSpec((tm, wide), row),
                   pl.BlockSpec((tm // FLASH_TK, B_HEADS // 2, 2, VT_ROWS, FLASH_TK), lambda i: (i, 0, 0, 0, 0))],
        out_shape=[jax.ShapeDtypeStruct((t, wide), BF16), jax.ShapeDtypeStruct((t, wide), BF16),
                   jax.ShapeDtypeStruct((t // FLASH_TK, B_HEADS // 2, 2, VT_ROWS, FLASH_TK), BF16)],
        compiler_params=_cparams(1),
        name="mla_prep",
    )(u, u, u, *tabs, g_q, g_kv, wq, wk, wv)


MLA_TQ = 1024


def _mla_flash_kernel(q_ref, k_ref, vt_ref, o_ref, qt_ref, st_ref, m_ref, acc_ref):
    tq, tk, cw = MLA_TQ, FLASH_TK, FLASH_CHUNK
    nhalf = tq // cw
    qi = pl.program_id(2)
    for h in range(2):
        qt_ref[h] = _transpose_bf16(q_ref[:, h * LANES:(h + 1) * LANES])
    _flash_reset(m_ref, acc_ref)

    def put(kt, c, slot):
        h, half = divmod(c, nhalf)
        start = pl.multiple_of(kt * tk, tk)
        k = k_ref[pl.ds(start, tk), h * LANES:(h + 1) * LANES]
        st_ref[slot, c] = jnp.dot(k, qt_ref[h, :, half * cw:(half + 1) * cw], preferred_element_type=F32)

    def use(kt, c, slot, masked=False):
        mask = None
        if masked:
            qpos = qi * tq + (c % nhalf) * cw + _lane_iota((1, cw))
            mask = (kt * tk + _sub_iota((tk, 1))) <= qpos
        _flash_update(st_ref[slot, c], vt_ref[kt, c // nhalf], m_ref, acc_ref, c, mask)

    assert tk == cw and nhalf % 2 == 0
    n_full = nhalf * qi
    _flash_pipeline(n_full // 2, 2 * nhalf, put, use)
    live = lambda d: [c for c in range(2 * nhalf) if c % nhalf >= d]
    for d in range(nhalf):
        if d + 1 < nhalf:
            for c in live(d + 1):
                put(n_full + d + 1, c, (d + 1) % 2)
        for c in live(d):
            use(n_full + d, c, d % 2, masked=(c % nhalf == d))
    ots = []
    for h in range(2):
        ots.append(jnp.concatenate([_flash_result(acc_ref, h * nhalf + half) for half in range(nhalf)],
                                   axis=1))
    o_ref[...] = jnp.concatenate(ots, axis=0).T.astype(o_ref.dtype)


def _mla_flash(q, k, vt, batch, seq):
    assert FLASH_TK == FLASH_CHUNK
    q3 = q.reshape(batch, seq, -1)
    k3 = k.reshape(batch, seq, -1)
    vt6 = vt.reshape(batch, seq // FLASH_TK, B_HEADS // 2, 2, VT_ROWS, FLASH_TK)
    nchain = 2 * (MLA_TQ // FLASH_CHUNK)
    o = pl.pallas_call(
        _mla_flash_kernel,
        grid=(batch, B_HEADS // 2, seq // MLA_TQ),
        in_specs=[pl.BlockSpec((None, MLA_TQ, 2 * LANES), lambda b, h, i: (b, i, h)),
                  pl.BlockSpec((None, seq, 2 * LANES), lambda b, h, i: (b, 0, h)),
                  pl.BlockSpec((None, seq // FLASH_TK, None, 2, VT_ROWS, FLASH_TK),
                               lambda b, h, i: (b, 0, h, 0, 0, 0))],
        out_specs=pl.BlockSpec((None, MLA_TQ, LANES), lambda b, h, i: (b, i, h)),
        out_shape=jax.ShapeDtypeStruct((batch, seq, B_WIDTH), BF16),
        scratch_shapes=[pltpu.VMEM((2, LANES, MLA_TQ), BF16),
                        pltpu.VMEM((2, nchain, FLASH_TK, FLASH_CHUNK), F32),
                        pltpu.VMEM((nchain, 1, FLASH_CHUNK), F32),
                        pltpu.VMEM((nchain, VT_ROWS, FLASH_CHUNK), F32)],
        compiler_params=_cparams(3),
        name="mla_flash",
    )(q3, k3, vt6)
    return o.reshape(batch * seq, B_WIDTH)


OUT_TM = 256


def _out_kernel(x_ref, p_ref, ocmp_ref, oslc_ref, owin_ref, gate_ref, e_ref, az_ref, ob_ref, bz_ref,
                oc0_ref, oc1_ref, oc2_ref, l0_ref, l1_ref, l2_ref, cz_ref, mg0_ref, mg1_ref, mg2_ref,
                wa_ref, wb_ref, wc_ref, wout_ref, wplg_ref, wple_ref, gpost_ref, o_ref, *tok_refs):
    tok = iter(tok_refs)

    def sig(z):
        return 0.5 * jnp.tanh(0.5 * z) + 0.5

    def f(r):
        if len(r.shape) == 2:
            return r[...].astype(F32)
        dil, n, width = r.shape
        buf = next(tok)
        for res in range(dil):
            blk = r[res].astype(F32)
            for c in range(width // LANES):
                buf[c, pl.ds(res, n, stride=dil), :] = blk[:, c * LANES:(c + 1) * LANES]
        return jnp.concatenate([buf[c] for c in range(width // LANES)], axis=1)

    def silu(z):
        return z * sig(z)

    def mm(a, w_ref):
        return jnp.dot(a.astype(BF16), w_ref[...], preferred_element_type=F32)

    g = sig(gate_ref[...].astype(F32))
    gs = jnp.dot(g.astype(BF16), e_ref[...], preferred_element_type=F32)
    o_a = (gs[:, 0:A_WIDTH] * f(ocmp_ref) + gs[:, A_WIDTH:2 * A_WIDTH] * f(oslc_ref)
           + gs[:, 2 * A_WIDTH:3 * A_WIDTH] * f(owin_ref))
    y_a = mm(o_a * silu(f(az_ref)), wa_ref)
    y_b = mm(f(ob_ref) * silu(f(bz_ref)), wb_ref)
    l0, l1, l2 = f(l0_ref), f(l1_ref), f(l2_ref)
    mx = jnp.maximum(jnp.maximum(l0, l1), l2)
    e0, e1, e2 = jnp.exp2(l0 - mx), jnp.exp2(l1 - mx), jnp.exp2(l2 - mx)
    o_c = (e0 * f(oc0_ref) + e1 * f(oc1_ref) + e2 * f(oc2_ref)) / (e0 + e1 + e2)
    y_c = mm(o_c * silu(f(cz_ref)), wc_ref)
    mix = sig(f(mg0_ref)) * y_a + sig(f(mg1_ref)) * y_b + sig(f(mg2_ref)) * y_c
    y = mm(mix, wout_ref)
    x1 = x_ref[...] + _rms(y, gpost_ref[...])
    o_ref[...] = x1 + sig(mm(x1, wplg_ref)) * mm(p_ref[...], wple_ref)


def _out_layer(x2, p, layer, o_cmp, o_slc, o_win, u, expand, o_b, o_c, lse_c, w, batch, dils):
    t = x2.shape[0]
    seq = t // batch
    tm = min(OUT_TM, seq)
    nper = seq // tm
    row = lambda i: (i, 0)
    ucol = lambda off, width: pl.BlockSpec((tm, width), lambda i: (i, off // width))
    half = pl.BlockSpec((tm, A_WIDTH), row)
    wspec = lambda r, c: pl.BlockSpec((None, r, c), lambda i: (layer, 0, 0))

    def group_spec(dil):
        if dil == 1:
            return half
        return pl.BlockSpec((None, dil, tm // dil, C_WIDTH), lambda i: (i // nper, 0, i % nper, 0))

    def group_view(a, dil):
        return a if dil == 1 else a.reshape(batch, dil, seq // dil, C_WIDTH)

    gspecs = [group_spec(d) for d in dils]
    o_c = [group_view(a, d) for a, d in zip(o_c, dils)]
    lse_c = [group_view(a, d) for a, d in zip(lse_c, dils)]
    n_tok = 2 * sum(1 for d in dils if d > 1)
    in_specs = [pl.BlockSpec((tm, D_MODEL), row),
                pl.BlockSpec((None, tm, PLE_DIM), lambda i: (layer, i, 0)),
                half, half, half,
                ucol(U_GATE, LANES), pl.BlockSpec((LANES, 3 * A_WIDTH), lambda i: (0, 0)),
                ucol(U_AZ, A_WIDTH), half, ucol(U_BZ, B_WIDTH),
                *gspecs, *gspecs, ucol(U_CZ, C_WIDTH),
                ucol(U_MG, D_MODEL), ucol(U_MG + D_MODEL, D_MODEL), ucol(U_MG + 2 * D_MODEL, D_MODEL),
                wspec(A_WIDTH, D_MODEL), wspec(B_WIDTH, D_MODEL), wspec(C_WIDTH, D_MODEL),
                wspec(D_MODEL, D_MODEL), wspec(D_MODEL, D_MODEL), wspec(PLE_DIM, D_MODEL),
                wspec(1, D_MODEL)]
    return pl.pallas_call(
        _out_kernel,
        grid=(t // tm,),
        in_specs=in_specs,
        out_specs=pl.BlockSpec((tm, D_MODEL), row),
        out_shape=jax.ShapeDtypeStruct((t, D_MODEL), F32),
        scratch_shapes=[pltpu.VMEM((C_WIDTH // LANES, tm, LANES), F32)] * n_tok,
        compiler_params=_cparams(1),
        name="out_layer",
    )(x2, p, o_cmp, o_slc, o_win, u, expand, u, o_b, u, *o_c, *lse_c, u, u, u, u,
      w["w_a"], w["w_b"], w["w_c"], w["w_out"], w["w_plg"], w["w_ple"], w["g_post"])


def _overlap_t(seq):
    nc = seq // CMP_STRIDE
    ns = seq // SLC_LEN
    cs = np.arange(nc)[:, None] * CMP_STRIDE
    js = np.arange(ns)[None, :] * SLC_LEN
    ov = np.clip(np.minimum(cs + CMP_LEN, js + SLC_LEN) - np.maximum(cs, js), 0, None).astype(np.float32) / CMP_LEN
    return jnp.asarray(ov.T)


def _block_onehot(seq):
    blk = np.arange(seq)[:, None] // SLC_LEN
    return jnp.asarray((np.arange(LANES)[None, :] % HEAD_DIM == blk).astype(np.float32), dtype=BF16)


def _gate_expand():
    e = np.zeros((LANES, 3 * A_WIDTH), np.float32)
    for br in range(3):
        for i in range(A_HEADS // 2):
            for g in range(A_KV_GROUPS):
                h = g * (A_HEADS // 2) + i
                c0 = br * A_WIDTH + i * LANES + g * HEAD_DIM
                e[h * 3 + br, c0:c0 + HEAD_DIM] = 1.0
    return jnp.asarray(e, dtype=BF16)


def kernel(x, p, positions, g_pre, g_post, w_in, nsa_cmp_pe, nsa_cmp_w1, nsa_cmp_w2, w_a, mla_g_q, mla_w_uq,
           mla_g_kv, mla_w_ukv, w_b, w_c, w_out, w_ple, w_plg):
    batch, seq, _ = x.shape
    depth = w_in.shape[0]
    t = batch * seq
    nch = seq // CMP_STRIDE

    pos = positions.astype(F32)
    part_tabs = _rope_tables(pos.reshape(-1), ROT_DIM, HEAD_DIM, 0)
    mla_tabs = _mla_rope_tables(pos.reshape(-1))
    dils = [dil for _, dil in C_PAIRS]
    assert dils[0] == 1
    perm_tabs = {dil: _rope_tables(_residue_major_positions(pos, dil), ROT_DIM, HEAD_DIM, 0) for dil in dils[1:]}
    w_main, w_groups = _regroup_w_in(w_in)
    g_pre3 = g_pre.reshape(depth, 1, D_MODEL)
    pe2, w1d, w2d = _compress_weights(nsa_cmp_pe, nsa_cmp_w1, nsa_cmp_w2)
    wq = mla_w_uq.reshape(depth, Q_LORA, B_HEADS, QK_NOPE + QK_ROPE)
    wq = _mla_lanes(wq[..., QK_NOPE:QK_NOPE + QK_ROPE // 2], wq[..., QK_NOPE + QK_ROPE // 2:], wq[..., :QK_NOPE])
    wq = wq.reshape(depth, Q_LORA, -1).astype(BF16)
    wkv = mla_w_ukv.reshape(depth, KV_LORA, B_HEADS, QK_NOPE + V_DIM)
    no_rope = jnp.zeros(wkv.shape[:-1] + (QK_ROPE // 2,), wkv.dtype)
    wk = _mla_lanes(no_rope, no_rope, wkv[..., :QK_NOPE]).reshape(depth, KV_LORA, -1).astype(BF16)
    wv = wkv[..., QK_NOPE:].reshape(depth, KV_LORA, B_WIDTH).astype(BF16)
    gq3 = mla_g_q.reshape(depth, 1, Q_LORA)
    gkv3 = mla_g_kv.reshape(depth, 1, KV_LORA)
    w_a_p = w_a.reshape(depth, 2, 4, HEAD_DIM, D_MODEL).swapaxes(1, 2).reshape(depth, A_WIDTH, D_MODEL)
    wts = {"w_a": w_a_p.astype(BF16), "w_b": w_b.astype(BF16), "w_c": w_c.astype(BF16),
           "w_out": w_out.astype(BF16), "w_plg": w_plg.astype(BF16), "w_ple": w_ple.astype(BF16),
           "g_post": g_post.reshape(depth, 1, D_MODEL)}
    p3 = p.reshape(depth, t, PLE_DIM)
    ovt = _overlap_t(seq)
    onehot = _block_onehot(seq)
    expand = _gate_expand()

    x2 = x.reshape(t, D_MODEL)
    for layer in range(depth):
        h, h_res = _prenorm(x2, g_pre3, layer, batch, dils[1:])
        u = _inproj(h, w_main, layer, part_tabs, ROPE_COLS)
        u3 = u.reshape(batch, seq, DP)
        kc, vct = _compress(u3, pe2, w1d, w2d, layer)
        o_cmp, biast = _cmp_select(u3, kc, vct, ovt)
        o_slc = _slc_attention(u3, biast, onehot)
        (o_win,) = _banded(u, nseq=batch, seqlen=seq, q_off=U_AQ, k_off=U_KWIN, v_off=U_VWIN,
                           kv_width=LANES, max_dist=WIN - 1, with_lse=False)
        q_b, k_b, vt_b = _mla_prep(u, mla_tabs, gq3, gkv3, wq, wk, wv, layer)
        o_b = _mla_flash(q_b, k_b, vt_b, batch, seq)
        o_c, lse_c = [], []
        for gi, (window, dil) in enumerate(C_PAIRS):
            if dil == 1:
                o_g, lse_g = _banded(u, nseq=batch, seqlen=seq, q_off=U_CQ0, k_off=U_CK0, v_off=U_CV0,
                                     kv_width=C_WIDTH, max_dist=window, with_lse=True)
            else:
                u_g = _inproj(h_res[gi - 1], w_groups[gi - 1], layer, perm_tabs[dil], DG_ROPE_COLS)
                o_g, lse_g = _banded(u_g, nseq=batch * dil, seqlen=seq // dil, q_off=0, k_off=C_WIDTH,
                                     v_off=2 * C_WIDTH, kv_width=C_WIDTH, max_dist=window // dil, with_lse=True)
            o_c.append(o_g)
            lse_c.append(lse_g)
        x2 = _out_layer(x2, p3, layer, o_cmp.reshape(t, A_WIDTH), o_slc.reshape(t, A_WIDTH), o_win, u, expand,
                        o_b, o_c, lse_c, wts, batch, dils)
    return x2.reshape(batch, seq, D_MODEL)
```

```python
import functools

import numpy as np
import jax
import jax.numpy as jnp
from jax import lax
from jax.experimental import pallas as pl
from jax.experimental.pallas import tpu as pltpu

F32 = jnp.float32
BF16 = jnp.bfloat16

D_MODEL = 1024
PLE_DIM = 256
ROPE_THETA = 500000.0
HEAD_DIM = 64
ROT_DIM = HEAD_DIM // 4
EPS = 1e-6
NEG = -1e30
BIG = 1e30
LANES = 128

A_HEADS = 8
A_KV_GROUPS = 2
A_WIDTH = A_HEADS * HEAD_DIM
CMP_LEN = 32
CMP_STRIDE = 16
CMP_HID = 256
SLC_LEN = 64
SLC_TOPK = 16
WIN = 512

B_HEADS = 8
Q_LORA = 384
KV_LORA = 128
QK_NOPE = 64
QK_ROPE = 32
V_DIM = 64
B_WIDTH = B_HEADS * V_DIM

C_PAIRS = ((128, 1), (512, 4), (2048, 16))
C_GROUPS = len(C_PAIRS)
C_HEADS = 8
C_WIDTH = C_HEADS * HEAD_DIM

IN_SIZES = (A_WIDTH, 6 * A_KV_GROUPS * HEAD_DIM, 3 * A_HEADS, A_WIDTH, Q_LORA, KV_LORA, QK_ROPE,
            B_WIDTH, C_GROUPS * 3 * C_WIDTH, C_WIDTH, 3 * D_MODEL)
IN_OFF = tuple(int(v) for v in np.cumsum((0,) + IN_SIZES))
(OFF_AQ, OFF_AKV, OFF_AG, OFF_AZ, OFF_BCQ, OFF_BCKV, OFF_BKR, OFF_BZ, OFF_CQKV, OFF_CZ, OFF_MG) = IN_OFF[:-1]

U_AQ = 0
U_CQ0 = 512
U_CK0 = 1024
U_KCMP = 1536
U_KSLC = 1664
U_KWIN = 1792
ROPE_COLS = 1920
U_VCMP = 1920
U_AZ = 2048
U_BZ = 2560
U_CV0 = 3072
U_CZ = 3584
U_MG = 4096
U_VSLC = 7168
U_VWIN = 7296
U_GATE = 7424
U_BCKV = 7552
U_BCQ = 7680
U_BKR = 8064
DP = 8192
DG_COLS = 3 * C_WIDTH
DG_ROPE_COLS = 2 * C_WIDTH

LOG2E = float(np.log2(np.e))
Q_SCALE = float(HEAD_DIM ** -0.5) * LOG2E
MLA_SCALE = float((QK_NOPE + QK_ROPE) ** -0.5) * LOG2E
VMEM_LIMIT = 48 * 1024 * 1024


def _cparams(n_axes):
    return pltpu.CompilerParams(dimension_semantics=("arbitrary",) * n_axes, vmem_limit_bytes=VMEM_LIMIT)


def _lane_iota(shape):
    return lax.broadcasted_iota(jnp.int32, shape, len(shape) - 1)


def _sub_iota(shape):
    return lax.broadcasted_iota(jnp.int32, shape, len(shape) - 2)


def _pair_cols(w):
    lead = w.shape[:-1]
    return w.reshape(lead + (2, 4, HEAD_DIM)).swapaxes(-3, -2).reshape(lead + (A_WIDTH,))


def _regroup_w_in(w_in):
    def col(off, n):
        return w_in[..., off:off + n]
    z = lambda n: jnp.zeros(w_in.shape[:-1] + (n,), w_in.dtype)
    akv = lambda which: col(OFF_AKV + which * 128, 128)
    cq = lambda gi, t: col(OFF_CQKV + gi * 3 * C_WIDTH + t * C_WIDTH, C_WIDTH)
    pieces = [_pair_cols(col(OFF_AQ, A_WIDTH)) * Q_SCALE, cq(0, 0) * Q_SCALE, cq(0, 1)]
    pieces += [akv(0), akv(2), akv(4)]
    pieces += [akv(1)]
    pieces += [_pair_cols(col(OFF_AZ, A_WIDTH)), col(OFF_BZ, B_WIDTH), cq(0, 2)]
    pieces += [col(OFF_CZ, C_WIDTH), col(OFF_MG, 3 * D_MODEL)]
    pieces += [akv(3), akv(5)]
    pieces += [col(OFF_AG, 3 * A_HEADS), z(128 - 3 * A_HEADS)]
    pieces += [col(OFF_BCKV, KV_LORA), col(OFF_BCQ, Q_LORA)]
    kr = col(OFF_BKR, QK_ROPE)
    pieces += [_mla_lanes(kr[..., :QK_ROPE // 2], kr[..., QK_ROPE // 2:], z(QK_NOPE))]
    main = jnp.concatenate(pieces, axis=-1).astype(BF16)
    assert main.shape[-1] == DP
    groups = [jnp.concatenate([cq(gi, 0) * Q_SCALE, cq(gi, 1), cq(gi, 2)], axis=-1).astype(BF16)
              for gi in range(1, C_GROUPS)]
    return main, groups


def _rope_tables(pos, dim, period, offset):
    half = dim // 2
    ch = np.arange(LANES) % period - offset
    in_span = (ch >= 0) & (ch < dim)
    cos, sin = _trig_lanes(pos, dim, np.where(in_span, ch % half, -1))
    first = jnp.asarray(in_span & (ch < half))[None, :]
    second = jnp.asarray(in_span & (ch >= half))[None, :]
    return cos, jnp.where(second, sin, 0.0), jnp.where(first, -sin, 0.0)


def _trig_lanes(pos, dim, freq_of_lane):
    half = dim // 2
    inv = ROPE_THETA ** (-jnp.arange(0, dim, 2, dtype=F32) / dim)
    ang = pos[:, None] * inv[None, :]
    spread = jnp.asarray((freq_of_lane[None, :] == np.arange(half)[:, None]).astype(np.float32))
    expand = lambda a: jnp.dot(a, spread, precision=lax.Precision.HIGHEST)
    return expand(jnp.cos(ang)) + jnp.asarray((freq_of_lane < 0).astype(np.float32))[None, :], expand(jnp.sin(ang))


def _mla_lanes(rope_first, rope_second, nope):
    pad = jnp.zeros(nope.shape[:-1] + (LANES - QK_NOPE - QK_ROPE,), nope.dtype)
    cut = LANES // 2 - QK_ROPE // 2
    return jnp.concatenate([rope_first, nope[..., :cut], rope_second, nope[..., cut:], pad], axis=-1)


def _mla_rope_tables(pos):
    half = QK_ROPE // 2
    lane = np.arange(LANES)
    first, second = lane < half, (lane >= LANES // 2) & (lane < LANES // 2 + half)
    cos, sin = _trig_lanes(pos, QK_ROPE, np.where(first, lane, np.where(second, lane - LANES // 2, -1)))
    return cos, jnp.where(jnp.asarray(first)[None, :], -sin, sin)


def _residue_major_positions(positions, dil):
    b, s = positions.shape
    return positions.reshape(b, s // dil, dil).swapaxes(1, 2).reshape(-1)


NORM_TM = 512
IN_TM = 1024
IN_TN = 2048
IN_CHUNK = 512


def _norm_kernel(x_ref, g_ref, o_ref, *rest, dils):
    o_dil, y_ref = rest[:len(dils)], rest[len(dils)]
    x = x_ref[...]
    ms = jnp.mean(x * x, axis=-1, keepdims=True)
    y = x * lax.rsqrt(ms + EPS) * g_ref[...]
    o_ref[...] = y.astype(o_ref.dtype)
    tm = x.shape[0]
    nlane = y_ref.shape[0]
    for c in range(nlane):
        y_ref[c] = y[:, c * LANES:(c + 1) * LANES]
    for o_d, dil in zip(o_dil, dils):
        for r in range(dil):
            for c in range(nlane):
                o_d[r, :, c * LANES:(c + 1) * LANES] = (
                    y_ref[c, pl.ds(r, tm // dil, stride=dil), :].astype(o_d.dtype))


def _prenorm(x2, g_pre, layer, batch, dils):
    t = x2.shape[0]
    seq = t // batch
    tm = min(NORM_TM, seq)
    nper = seq // tm
    out_specs = [pl.BlockSpec((tm, D_MODEL), lambda b, i: (b * nper + i, 0))]
    out_shape = [jax.ShapeDtypeStruct((t, D_MODEL), BF16)]
    for dil in dils:
        out_specs.append(pl.BlockSpec((None, dil, tm // dil, D_MODEL), lambda b, i: (b, 0, i, 0)))
        out_shape.append(jax.ShapeDtypeStruct((batch, dil, seq // dil, D_MODEL), BF16))
    outs = pl.pallas_call(
        functools.partial(_norm_kernel, dils=tuple(dils)),
        grid=(batch, nper),
        in_specs=[pl.BlockSpec((tm, D_MODEL), lambda b, i: (b * nper + i, 0)),
                  pl.BlockSpec((None, 1, D_MODEL), lambda b, i: (layer, 0, 0))],
        out_specs=out_specs,
        out_shape=out_shape,
        scratch_shapes=[pltpu.VMEM((D_MODEL // LANES, tm, LANES), F32)],
        compiler_params=_cparams(2),
        name="prenorm",
    )(x2, g_pre)
    return outs[0], [o.reshape(t, D_MODEL) for o in outs[1:]]


def _inproj_kernel(h_ref, w_ref, c_ref, s1_ref, s2_ref, o_ref, *, rope_cols):
    j = pl.program_id(0)
    tn = w_ref.shape[1]
    nsub = tn // LANES
    full_tiles, rem = divmod(rope_cols // LANES, nsub)
    chunk = IN_CHUNK if tn % IN_CHUNK == 0 else tn
    per = chunk // LANES

    def emit(n_rope):
        for g in range(tn // chunk):
            acc = jnp.dot(h_ref[...], w_ref[:, g * chunk:(g + 1) * chunk], preferred_element_type=F32)
            for k in range(per):
                c = g * per + k
                xc = acc[:, k * LANES:(k + 1) * LANES]
                if c < n_rope:
                    xc = (xc * c_ref[...] + pltpu.roll(xc, ROT_DIM // 2, 1) * s1_ref[...]
                          + pltpu.roll(xc, LANES - ROT_DIM // 2, 1) * s2_ref[...])
                o_ref[:, c * LANES:(c + 1) * LANES] = xc.astype(o_ref.dtype)

    pl.when(j < full_tiles)(lambda: emit(nsub))
    pl.when(j == full_tiles)(lambda: emit(rem))
    pl.when(j > full_tiles)(lambda: emit(0))


def _inproj(h, w, layer, tabs, rope_cols):
    t = h.shape[0]
    ncols = w.shape[-1]
    tn = min(IN_TN, ncols)
    ntile = ncols // tn
    tm = min(IN_TM, t)
    rope_tiles = -(-rope_cols // tn)
    row = lambda j, i: (i, 0)
    tab = lambda j, i: (jnp.where(j < rope_tiles, i, 0), 0)
    return pl.pallas_call(
        functools.partial(_inproj_kernel, rope_cols=rope_cols),
        grid=(ntile, t // tm),
        in_specs=[pl.BlockSpec((tm, D_MODEL), row),
                  pl.BlockSpec((None, D_MODEL, tn), lambda j, i: (layer, 0, j)),
                  pl.BlockSpec((tm, LANES), tab), pl.BlockSpec((tm, LANES), tab), pl.BlockSpec((tm, LANES), tab)],
        out_specs=pl.BlockSpec((tm, tn), lambda j, i: (i, j)),
        out_shape=jax.ShapeDtypeStruct((t, ntile * tn), BF16),
        compiler_params=_cparams(2),
        name="inproj",
    )(h, w, *tabs)


def _compress_kernel(k_ref, v_ref, pe_ref, w1_ref, w2_ref, kc_ref, vct_ref, x_ref):
    nch = x_ref.shape[0] // CMP_STRIDE
    for which, src in enumerate((k_ref, v_ref)):
        x_ref[...] = src[...].astype(F32)
        rows = [x_ref[pl.ds(l, nch, stride=CMP_STRIDE), :] for l in range(CMP_STRIDE)]
        halves = []
        for part in range(CMP_LEN // CMP_STRIDE):
            xs = [(rows[l] + pe_ref[which, part * CMP_STRIDE + l:part * CMP_STRIDE + l + 1, :]).astype(BF16)
                  for l in range(CMP_STRIDE)]
            halves.append(jnp.dot(jnp.concatenate(xs, axis=1), w1_ref[which, part], preferred_element_type=F32))
        hid = halves[0] + pltpu.roll(halves[1], nch - 1, 0)
        act = hid * jax.nn.sigmoid(hid)
        tok = jnp.dot(act.astype(BF16), w2_ref[which], preferred_element_type=F32)
        if which == 0:
            kc_ref[...] = tok.astype(kc_ref.dtype)
        else:
            vct_ref[...] = tok.T.astype(vct_ref.dtype)


def _compress_weights(pe, w1, w2):
    depth = w1.shape[0]
    g = A_KV_GROUPS
    assert CMP_LEN == 2 * CMP_STRIDE and g == 2
    w1r = w1.reshape(depth, 2, CMP_LEN, HEAD_DIM, CMP_HID)
    z1 = jnp.zeros_like(w1r)
    w1d = jnp.stack([jnp.concatenate([w1r, z1], axis=-1), jnp.concatenate([z1, w1r], axis=-1)], axis=3)
    w1d = w1d.reshape(depth, 2, 2, CMP_STRIDE * g * HEAD_DIM, g * CMP_HID).astype(BF16)
    z2 = jnp.zeros_like(w2)
    w2d = jnp.concatenate([jnp.concatenate([w2, z2], axis=-1), jnp.concatenate([z2, w2], axis=-1)], axis=2)
    pe2 = jnp.concatenate([pe, pe], axis=-1)
    return pe2, w1d, w2d.astype(BF16)


def _compress(u3, pe2, w1d, w2d, layer):
    b, s, _ = u3.shape
    nch = s // CMP_STRIDE
    slab = lambda off: pl.BlockSpec((None, s, LANES), lambda i: (i, 0, off // LANES))
    return pl.pallas_call(
        _compress_kernel,
        grid=(b,),
        in_specs=[slab(U_KCMP), slab(U_VCMP),
                  pl.BlockSpec((None,) + pe2.shape[1:], lambda i: (layer, 0, 0, 0)),
                  pl.BlockSpec((None,) + w1d.shape[1:], lambda i: (layer, 0, 0, 0, 0)),
                  pl.BlockSpec((None,) + w2d.shape[1:], lambda i: (layer, 0, 0, 0))],
        out_specs=[pl.BlockSpec((None, nch, LANES), lambda i: (i, 0, 0)),
                   pl.BlockSpec((None, LANES, nch), lambda i: (i, 0, 0))],
        out_shape=[jax.ShapeDtypeStruct((b, nch, LANES), BF16), jax.ShapeDtypeStruct((b, LANES, nch), BF16)],
        scratch_shapes=[pltpu.VMEM((s, LANES), F32)],
        compiler_params=_cparams(1),
        name="nsa_compress",
    )(u3, u3, pe2, w1d, w2d)


CS_TQ = 128


def _cmp_select_kernel(q_ref, kc_ref, vct_ref, ovt_ref, o_ref, bias_ref, sc_ref):
    tq = CS_TQ
    qi = pl.program_id(1)
    nc = kc_ref.shape[0]
    ns = ovt_ref.shape[0]
    tpos = qi * tq + _lane_iota((1, tq))
    kc = kc_ref[...]
    lane = _lane_iota((1, LANES))
    kst = jnp.concatenate([jnp.where(lane < HEAD_DIM, kc, jnp.zeros_like(kc)),
                           jnp.where(lane >= HEAD_DIM, kc, jnp.zeros_like(kc))], axis=0)
    c_end = _sub_iota((nc, 1)) * CMP_STRIDE + (CMP_LEN - 1)
    cv = c_end <= tpos
    vct = vct_ref[...]
    psum = [jnp.zeros((nc, tq), F32), jnp.zeros((nc, tq), F32)]
    sub = _sub_iota((LANES, 1))
    for i in range(A_HEADS // 2):
        qb = q_ref[:, i * LANES:(i + 1) * LANES]
        st = lax.dot_general(kst, qb, (((1,), (1,)), ((), ())), preferred_element_type=F32)
        ot = []
        for g in range(A_KV_GROUPS):
            s = jnp.where(cv, st[g * nc:(g + 1) * nc], NEG)
            m = jnp.max(s, axis=0, keepdims=True)
            e = jnp.where(cv, jnp.exp2(s - m), 0.0)
            den = jnp.sum(e, axis=0, keepdims=True)
            p = e / jnp.where(den > 0.0, den, 1.0)
            psum[g] = psum[g] + p
            ot.append(jnp.dot(vct, p.astype(BF16), preferred_element_type=F32))
        o_pair = jnp.where(sub < HEAD_DIM, ot[0], ot[1])
        o_ref[:, i * LANES:(i + 1) * LANES] = o_pair.T.astype(o_ref.dtype)
    jdx = _sub_iota((ns, 1))
    cur = tpos // SLC_LEN
    valid = jdx <= cur
    forced = (jdx == 0) | (jdx == cur) | (jdx == cur - 1)
    for g in range(A_KV_GROUPS):
        imp = jnp.dot(ovt_ref[...], psum[g], preferred_element_type=F32, precision=lax.Precision.HIGHEST)
        score = jnp.where(forced, BIG, jnp.where(valid, imp, NEG))
        sc_ref[...] = score
        slab = 8
        parts = [score[v * slab:(v + 1) * slab] for v in range(ns // slab)]
        cnts = [jnp.zeros((slab, tq), F32) for _ in parts]
        for jp in range(ns):
            row = sc_ref[jp:jp + 1, :]
            for v, sv in enumerate(parts):
                if v * slab > jp:
                    ahead = row >= sv
                elif (v + 1) * slab - 1 < jp:
                    ahead = row > sv
                else:
                    tie = jnp.where(jdx[v * slab:(v + 1) * slab] > jp, 1.0, 0.0)
                    cnts[v] = cnts[v] + jnp.where(row == sv, tie, 0.0)
                    ahead = row > sv
                cnts[v] = cnts[v] + jnp.where(ahead, 1.0, 0.0)
        cnt = jnp.concatenate(cnts, axis=0)
        bias = jnp.where(cnt < float(min(SLC_TOPK, ns)), 0.0, NEG)
        bias_ref[g * ns:(g + 1) * ns, :] = bias.astype(bias_ref.dtype)


def _cmp_select(u3, kc, vct, ovt):
    b, s, _ = u3.shape
    nc = kc.shape[1]
    ns = s // SLC_LEN
    assert A_KV_GROUPS * ns == LANES
    return pl.pallas_call(
        _cmp_select_kernel,
        grid=(b, s // CS_TQ),
        in_specs=[pl.BlockSpec((None, CS_TQ, A_WIDTH), lambda i, q: (i, q, U_AQ // A_WIDTH)),
                  pl.BlockSpec((None, nc, LANES), lambda i, q: (i, 0, 0)),
                  pl.BlockSpec((None, LANES, nc), lambda i, q: (i, 0, 0)),
                  pl.BlockSpec((ns, nc), lambda i, q: (0, 0))],
        out_specs=[pl.BlockSpec((None, CS_TQ, A_WIDTH), lambda i, q: (i, q, 0)),
                   pl.BlockSpec((None, LANES, CS_TQ), lambda i, q: (i, 0, q))],
        out_shape=[jax.ShapeDtypeStruct((b, s, A_WIDTH), BF16), jax.ShapeDtypeStruct((b, LANES, s), BF16)],
        scratch_shapes=[pltpu.VMEM((ns, CS_TQ), F32)],
        compiler_params=_cparams(2),
        name="nsa_cmp_select",
    )(u3, kc, vct, ovt)


FLASH_TK = 256
FLASH_CHUNK = 256
FLASH_UNROLL = 4
ONES_ROWS = 16
VT_ROWS = HEAD_DIM + ONES_ROWS


def _head_values_t(vt_pair):
    ones = jnp.ones((ONES_ROWS, vt_pair.shape[1]), vt_pair.dtype)
    return [jnp.concatenate([vt_pair[h * HEAD_DIM:(h + 1) * HEAD_DIM], ones], axis=0) for h in range(2)]


def _flash_reset(m_ref, acc_ref):
    m_ref[...] = jnp.full(m_ref.shape, NEG, F32)
    acc_ref[...] = jnp.zeros(acc_ref.shape, F32)


def _flash_update(st, vt, m_ref, acc_ref, idx, mask):
    if mask is not None:
        st = jnp.where(mask, st, NEG)
    m_old = m_ref[idx]
    m_new = jnp.maximum(m_old, jnp.max(st, axis=0, keepdims=True))
    alpha = jnp.exp2(m_old - m_new)
    p = jnp.exp2((st - m_new).astype(BF16))
    acc_ref[idx] = alpha * acc_ref[idx] + jnp.dot(vt, p, preferred_element_type=F32)
    m_ref[idx] = m_new


def _flash_result(acc_ref, idx):
    acc = acc_ref[idx]
    return acc[0:HEAD_DIM] / acc[HEAD_DIM:HEAD_DIM + 1]


def _flash_pipeline(n_pairs, n_chain, put, use):
    for c in range(n_chain):
        put(0, c, 0)

    def run(t0, count):
        for t in range(count):
            for c in range(n_chain):
                put(t0 + t + 1, c, (t + 1) % 2)
                use(t0 + t, c, t % 2)

    per = FLASH_UNROLL // 2
    n_long = n_pairs // per

    def long_body(j, carry):
        run(FLASH_UNROLL * j, FLASH_UNROLL)
        return carry

    def pair_body(j, carry):
        run(FLASH_UNROLL * n_long + 2 * j, 2)
        return carry

    lax.fori_loop(0, n_long, long_body, 0)
    lax.fori_loop(0, n_pairs - per * n_long, pair_body, 0)


def _transpose_bf16(x):
    return x.astype(F32).T.astype(BF16)


def _merge_rows(lo, hi):
    return jnp.where(_sub_iota((LANES, 1)) < HEAD_DIM, lo, hi)


SLC_TQ = 256


def _slc_kernel(q_ref, biast_ref, k_ref, v_ref, oh_ref, o_ref, vt_ref, qa_ref, st_ref, m_ref, acc_ref):
    tq, tk = SLC_TQ, FLASH_TK
    qi = pl.program_id(1)
    nblk = A_HEADS // 2
    sub = _sub_iota((LANES, 1))

    @pl.when(qi == 0)
    def _():
        for kt in range(vt_ref.shape[0]):
            for g, vt in enumerate(_head_values_t(_transpose_bf16(v_ref[kt * tk:(kt + 1) * tk, :]))):
                vt_ref[kt, g] = vt

    nchain = 2 * nblk
    biast = biast_ref[...]
    zero = jnp.zeros_like(biast)
    mine = (sub < HEAD_DIM, sub >= HEAD_DIM)
    for i in range(nblk):
        qt = _transpose_bf16(q_ref[:, i * LANES:(i + 1) * LANES])
        for h in range(2):
            qa_ref[2 * i + h] = jnp.concatenate([jnp.where(mine[h], qt, zero), jnp.where(mine[h], biast, zero)],
                                                axis=0)
    _flash_reset(m_ref, acc_ref)

    def put(kt, c, slot):
        start = pl.multiple_of(kt * tk, tk)
        k = jnp.concatenate([k_ref[pl.ds(start, tk), :], oh_ref[pl.ds(start, tk), :]], axis=1)
        st_ref[slot, c] = jnp.dot(k, qa_ref[c], preferred_element_type=F32)

    def use(kt, c, slot, mask=None):
        _flash_update(st_ref[slot, c], vt_ref[kt, c % 2], m_ref, acc_ref, c, mask)

    n_full = qi
    n_pairs = n_full // 2
    _flash_pipeline(n_pairs, nchain, put, use)
    causal = (n_full * tk + _sub_iota((tk, 1))) <= (qi * tq + _lane_iota((1, tq)))

    @pl.when(n_full == 2 * n_pairs)
    def _():
        for c in range(nchain):
            use(n_full, c, 0, causal)

    @pl.when(n_full != 2 * n_pairs)
    def _():
        for c in range(nchain):
            put(n_full, c, 1)
            use(n_full - 1, c, 0)
        for c in range(nchain):
            use(n_full, c, 1, causal)

    for i in range(nblk):
        ot = jnp.concatenate([_flash_result(acc_ref, 2 * i), _flash_result(acc_ref, 2 * i + 1)], axis=0)
        o_ref[:, i * LANES:(i + 1) * LANES] = ot.T.astype(o_ref.dtype)


def _slc_attention(u3, biast, onehot):
    b, s, _ = u3.shape
    nblk = A_HEADS
    assert SLC_TQ == FLASH_CHUNK == FLASH_TK
    return pl.pallas_call(
        _slc_kernel,
        grid=(b, s // SLC_TQ),
        in_specs=[pl.BlockSpec((None, SLC_TQ, A_WIDTH), lambda i, q: (i, q, U_AQ // A_WIDTH)),
                  pl.BlockSpec((None, LANES, SLC_TQ), lambda i, q: (i, 0, q)),
                  pl.BlockSpec((None, s, LANES), lambda i, q: (i, 0, U_KSLC // LANES)),
                  pl.BlockSpec((None, s, LANES), lambda i, q: (i, 0, U_VSLC // LANES)),
                  pl.BlockSpec((s, LANES), lambda i, q: (0, 0))],
        out_specs=pl.BlockSpec((None, SLC_TQ, A_WIDTH), lambda i, q: (i, q, 0)),
        out_shape=jax.ShapeDtypeStruct((b, s, A_WIDTH), BF16),
        scratch_shapes=[pltpu.VMEM((s // FLASH_TK, 2, VT_ROWS, FLASH_TK), BF16),
                        pltpu.VMEM((nblk, 2 * LANES, FLASH_CHUNK), BF16),
                        pltpu.VMEM((2, nblk, FLASH_TK, FLASH_CHUNK), F32),
                        pltpu.VMEM((nblk, 1, FLASH_CHUNK), F32),
                        pltpu.VMEM((nblk, VT_ROWS, FLASH_CHUNK), F32)],
        compiler_params=_cparams(2),
        name="nsa_selected",
    )(u3, biast, u3, u3, onehot)


BAND_T = 128
BAND_ROWS = 1024
BAND_LOOKAHEAD = 2


def _banded_kernel(*refs, nprev, max_dist, shared_kv, with_lse):
    t = BAND_T
    q_ref, k_ref, v_ref, o_ref = refs[:4]
    lse_ref = refs[4] if with_lse else None
    qi = pl.program_id(1)
    nsub = q_ref.shape[0] // t
    nblk = q_ref.shape[1] // LANES
    nk = (nprev + 1) * t
    col = _lane_iota((1, 2 * t))
    base = jnp.where(col >= t, col - t, col) - _sub_iota((nk, 1))
    sub = _sub_iota((LANES, 1))

    def window(s):
        return pl.multiple_of(jnp.maximum(qi * nsub + s - nprev, 0) * t, t)

    def band_cap(dist):
        rel = base + dist
        return jnp.where((rel >= 0) & (rel <= max_dist), BIG, NEG)

    assert nsub >= nprev
    regular = band_cap(nprev * t)
    clamped = [band_cap(s * t) for s in range(nprev)]

    def cap(s):
        return jnp.where(qi == 0, clamped[s], regular) if s < nprev else regular

    def scores(s, i):
        start = window(s)
        cols = slice(None) if shared_kv else slice(i * LANES, (i + 1) * LANES)
        k = k_ref[pl.ds(start, nk), cols]
        qb = q_ref[s * t:(s + 1) * t, i * LANES:(i + 1) * LANES]
        lane = _lane_iota((1, LANES))
        zero = jnp.zeros_like(qb)
        qs = jnp.concatenate([jnp.where(lane < HEAD_DIM, qb, zero), jnp.where(lane >= HEAD_DIM, qb, zero)], axis=0)
        return lax.dot_general(k, qs, (((1,), (1,)), ((), ())), preferred_element_type=F32)

    def finish(s, i, st):
        cols = slice(None) if shared_kv else slice(i * LANES, (i + 1) * LANES)
        v = v_ref[pl.ds(window(s), nk), cols]
        st = jnp.minimum(st, cap(s))
        m = jnp.max(st, axis=0, keepdims=True)
        p = jnp.exp2(st - m)
        l = jnp.sum(p, axis=0, keepdims=True)
        ot = lax.dot_general(v, p.astype(BF16), (((0,), (0,)), ((), ())), preferred_element_type=F32) / l
        rows, sl = slice(s * t, (s + 1) * t), slice(i * LANES, (i + 1) * LANES)
        o_ref[rows, sl] = _merge_rows(ot[:, 0:t], ot[:, t:2 * t]).T.astype(o_ref.dtype)
        if with_lse:
            lse = m + jnp.log2(l)
            lse_ref[rows, sl] = _merge_rows(jnp.broadcast_to(lse[:, 0:t], (LANES, t)),
                                            jnp.broadcast_to(lse[:, t:2 * t], (LANES, t))).T

    items = [(s, i) for s in range(nsub) for i in range(nblk)]
    pending = [scores(*it) for it in items[:BAND_LOOKAHEAD]]
    for j, it in enumerate(items):
        if j + BAND_LOOKAHEAD < len(items):
            pending.append(scores(*items[j + BAND_LOOKAHEAD]))
        finish(*it, pending[j])


def _banded(src, *, nseq, seqlen, q_off, k_off, v_off, kv_width, max_dist, with_lse):
    t = BAND_T
    rows = min(BAND_ROWS, seqlen)
    nq = seqlen // rows
    nprev = -(-max_dist // t)
    shared = kv_width == LANES
    assert (nprev + 1) * t <= seqlen and seqlen % rows == 0 and rows % t == 0
    qo = pl.BlockSpec((rows, A_WIDTH), lambda n, q: (n * nq + q, q_off // A_WIDTH))
    out = pl.BlockSpec((rows, A_WIDTH), lambda n, q: (n * nq + q, 0))
    in_specs = [qo,
                pl.BlockSpec((seqlen, kv_width), lambda n, q: (n, k_off // kv_width)),
                pl.BlockSpec((seqlen, kv_width), lambda n, q: (n, v_off // kv_width))]
    out_specs = [out]
    out_shape = [jax.ShapeDtypeStruct((nseq * seqlen, A_WIDTH), BF16)]
    if with_lse:
        out_specs.append(out)
        out_shape.append(jax.ShapeDtypeStruct((nseq * seqlen, A_WIDTH), F32))
    return pl.pallas_call(
        functools.partial(_banded_kernel, nprev=nprev, max_dist=max_dist, shared_kv=shared, with_lse=with_lse),
        grid=(nseq, nq),
        in_specs=in_specs,
        out_specs=out_specs,
        out_shape=out_shape,
        compiler_params=_cparams(2),
        name="banded_attention",
    )(src, src, src)


MLA_TM = 512


def _rms(x, g):
    return x * lax.rsqrt(jnp.mean(x * x, axis=-1, keepdims=True) + EPS) * g


def _mla_prep_kernel(cq_ref, ckv_ref, kr_ref, c_ref, s_ref, gq_ref, gkv_ref, wq_ref, wk_ref, wv_ref,
                     q_out, k_out, vt_out):
    tk = FLASH_TK
    cm, sm = c_ref[...], s_ref[...]

    def rope(xc):
        return xc * cm + pltpu.roll(xc, LANES // 2, 1) * sm

    qn = _rms(cq_ref[...].astype(F32), gq_ref[...]).astype(BF16)
    q = jnp.dot(qn, wq_ref[...], preferred_element_type=F32)
    kvn = _rms(ckv_ref[...].astype(F32), gkv_ref[...]).astype(BF16)
    kk = jnp.dot(kvn, wk_ref[...], preferred_element_type=F32)
    kr = rope(kr_ref[...].astype(F32))
    for h in range(B_HEADS):
        sl = slice(h * LANES, (h + 1) * LANES)
        q_out[:, sl] = (rope(q[:, sl]) * MLA_SCALE).astype(q_out.dtype)
        k_out[:, sl] = (kk[:, sl] + kr).astype(k_out.dtype)
    v = jnp.dot(kvn, wv_ref[...], preferred_element_type=F32)
    for j in range(v.shape[0] // tk):
        for hp in range(B_HEADS // 2):
            pair_t = v[j * tk:(j + 1) * tk, hp * LANES:(hp + 1) * LANES].T.astype(vt_out.dtype)
            for h, vt in enumerate(_head_values_t(pair_t)):
                vt_out[j, hp, h] = vt


def _mla_prep(u, tabs, g_q, g_kv, wq, wk, wv, layer):
    t = u.shape[0]
    tm = min(MLA_TM, t)
    row = lambda i: (i, 0)
    wide = B_HEADS * LANES
    assert U_BCQ % Q_LORA == 0 and tm % FLASH_TK == 0
    return pl.pallas_call(
        _mla_prep_kernel,
        grid=(t // tm,),
        in_specs=[pl.BlockSpec((tm, Q_LORA), lambda i: (i, U_BCQ // Q_LORA)),
                  pl.BlockSpec((tm, LANES), lambda i: (i, U_BCKV // LANES)),
                  pl.BlockSpec((tm, LANES), lambda i: (i, U_BKR // LANES)),
                  pl.BlockSpec((tm, LANES), row), pl.BlockSpec((tm, LANES), row),
                  pl.BlockSpec((None, 1, Q_LORA), lambda i: (layer, 0, 0)),
                  pl.BlockSpec((None, 1, KV_LORA), lambda i: (layer, 0, 0)),
                  pl.BlockSpec((None, Q_LORA, wide), lambda i: (layer, 0, 0)),
                  pl.BlockSpec((None, KV_LORA, wide), lambda i: (layer, 0, 0)),
                  pl.BlockSpec((None, KV_LORA, B_WIDTH), lambda i: (layer, 0, 0))],
        out_specs=[pl.BlockSpec((tm, wide), row), pl.BlockSpec((tm, wide), row),
                   pl.BlockSpec((tm // FLASH_TK, B_HEADS // 2, 2, VT_ROWS, FLASH_TK), lambda i: (i, 0, 0, 0, 0))],
        out_shape=[jax.ShapeDtypeStruct((t, wide), BF16), jax.ShapeDtypeStruct((t, wide), BF16),
                   jax.ShapeDtypeStruct((t // FLASH_TK, B_HEADS // 2, 2, VT_ROWS, FLASH_TK), BF16)],
        compiler_params=_cparams(1),
        name="mla_prep",
    )(u, u, u, *tabs, g_q, g_kv, wq, wk, wv)


MLA_TQ = 1024


def _mla_flash_kernel(q_ref, k_ref, vt_ref, o_ref, qt_ref, st_ref, m_ref, acc_ref):
    tq, tk, cw = MLA_TQ, FLASH_TK, FLASH_CHUNK
    nhalf = tq // cw
    qi = pl.program_id(2)
    for h in range(2):
        qt_ref[h] = _transpose_bf16(q_ref[:, h * LANES:(h + 1) * LANES])
    _flash_reset(m_ref, acc_ref)

    def put(kt, c, slot):
        h, half = divmod(c, nhalf)
        start = pl.multiple_of(kt * tk, tk)
        k = k_ref[pl.ds(start, tk), h * LANES:(h + 1) * LANES]
        st_ref[slot, c] = jnp.dot(k, qt_ref[h, :, half * cw:(half + 1) * cw], preferred_element_type=F32)

    def use(kt, c, slot, masked=False):
        mask = None
        if masked:
            qpos = qi * tq + (c % nhalf) * cw + _lane_iota((1, cw))
            mask = (kt * tk + _sub_iota((tk, 1))) <= qpos
        _flash_update(st_ref[slot, c], vt_ref[kt, c // nhalf], m_ref, acc_ref, c, mask)

    assert tk == cw and nhalf % 2 == 0
    n_full = nhalf * qi
    _flash_pipeline(n_full // 2, 2 * nhalf, put, use)
    live = lambda d: [c for c in range(2 * nhalf) if c % nhalf >= d]
    for d in range(nhalf):
        if d + 1 < nhalf:
            for c in live(d + 1):
                put(n_full + d + 1, c, (d + 1) % 2)
        for c in live(d):
            use(n_full + d, c, d % 2, masked=(c % nhalf == d))
    ots = []
    for h in range(2):
        ots.append(jnp.concatenate([_flash_result(acc_ref, h * nhalf + half) for half in range(nhalf)],
                                   axis=1))
    o_ref[...] = jnp.concatenate(ots, axis=0).T.astype(o_ref.dtype)


def _mla_flash(q, k, vt, batch, seq):
    assert FLASH_TK == FLASH_CHUNK
    q3 = q.reshape(batch, seq, -1)
    k3 = k.reshape(batch, seq, -1)
    vt6 = vt.reshape(batch, seq // FLASH_TK, B_HEADS // 2, 2, VT_ROWS, FLASH_TK)
    nchain = 2 * (MLA_TQ // FLASH_CHUNK)
    o = pl.pallas_call(
        _mla_flash_kernel,
        grid=(batch, B_HEADS // 2, seq // MLA_TQ),
        in_specs=[pl.BlockSpec((None, MLA_TQ, 2 * LANES), lambda b, h, i: (b, i, h)),
                  pl.BlockSpec((None, seq, 2 * LANES), lambda b, h, i: (b, 0, h)),
                  pl.BlockSpec((None, seq // FLASH_TK, None, 2, VT_ROWS, FLASH_TK),
                               lambda b, h, i: (b, 0, h, 0, 0, 0))],
        out_specs=pl.BlockSpec((None, MLA_TQ, LANES), lambda b, h, i: (b, i, h)),
        out_shape=jax.ShapeDtypeStruct((batch, seq, B_WIDTH), BF16),
        scratch_shapes=[pltpu.VMEM((2, LANES, MLA_TQ), BF16),
                        pltpu.VMEM((2, nchain, FLASH_TK, FLASH_CHUNK), F32),
                        pltpu.VMEM((nchain, 1, FLASH_CHUNK), F32),
                        pltpu.VMEM((nchain, VT_ROWS, FLASH_CHUNK), F32)],
        compiler_params=_cparams(3),
        name="mla_flash",
    )(q3, k3, vt6)
    return o.reshape(batch * seq, B_WIDTH)


OUT_TM = 512


def _out_kernel(x_ref, p_ref, ocmp_ref, oslc_ref, owin_ref, gate_ref, e_ref, az_ref, ob_ref, bz_ref,
                oc0_ref, oc1_ref, oc2_ref, l0_ref, l1_ref, l2_ref, cz_ref, mg0_ref, mg1_ref, mg2_ref,
                wa_ref, wb_ref, wc_ref, wout_ref, wplg_ref, wple_ref, gpost_ref, o_ref, *tok_refs):
    tok = iter(tok_refs)

    def sig(z):
        return 0.5 * jnp.tanh(0.5 * z) + 0.5

    def f(r):
        if len(r.shape) == 2:
            return r[...].astype(F32)
        dil, n, width = r.shape
        buf = next(tok)
        for res in range(dil):
            blk = r[res].astype(F32)
            for c in range(width // LANES):
                buf[c, pl.ds(res, n, stride=dil), :] = blk[:, c * LANES:(c + 1) * LANES]
        return jnp.concatenate([buf[c] for c in range(width // LANES)], axis=1)

    def silu(z):
        return z * sig(z)

    def mm(a, w_ref):
        return jnp.dot(a.astype(BF16), w_ref[...], preferred_element_type=F32)

    g = sig(gate_ref[...].astype(F32))
    gs = jnp.dot(g.astype(BF16), e_ref[...], preferred_element_type=F32)
    o_a = (gs[:, 0:A_WIDTH] * f(ocmp_ref) + gs[:, A_WIDTH:2 * A_WIDTH] * f(oslc_ref)
           + gs[:, 2 * A_WIDTH:3 * A_WIDTH] * f(owin_ref))
    y_a = mm(o_a * silu(f(az_ref)), wa_ref)
    y_b = mm(f(ob_ref) * silu(f(bz_ref)), wb_ref)
    l0, l1, l2 = f(l0_ref), f(l1_ref), f(l2_ref)
    mx = jnp.maximum(jnp.maximum(l0, l1), l2)
    e0, e1, e2 = jnp.exp2(l0 - mx), jnp.exp2(l1 - mx), jnp.exp2(l2 - mx)
    o_c = (e0 * f(oc0_ref) + e1 * f(oc1_ref) + e2 * f(oc2_ref)) / (e0 + e1 + e2)
    y_c = mm(o_c * silu(f(cz_ref)), wc_ref)
    mix = sig(f(mg0_ref)) * y_a + sig(f(mg1_ref)) * y_b + sig(f(mg2_ref)) * y_c
    y = mm(mix, wout_ref)
    x1 = x_ref[...] + _rms(y, gpost_ref[...])
    o_ref[...] = x1 + sig(mm(x1, wplg_ref)) * mm(p_ref[...], wple_ref)


def _out_layer(x2, p, layer, o_cmp, o_slc, o_win, u, expand, o_b, o_c, lse_c, w, batch, dils):
    t = x2.shape[0]
    seq = t // batch
    tm = min(OUT_TM, seq)
    nper = seq // tm
    row = lambda i: (i, 0)
    ucol = lambda off, width: pl.BlockSpec((tm, width), lambda i: (i, off // width))
    half = pl.BlockSpec((tm, A_WIDTH), row)
    wspec = lambda r, c: pl.BlockSpec((None, r, c), lambda i: (layer, 0, 0), pipeline_mode=pl.Buffered(1))

    def group_spec(dil):
        if dil == 1:
            return half
        return pl.BlockSpec((None, dil, tm // dil, C_WIDTH), lambda i: (i // nper, 0, i % nper, 0))

    def group_view(a, dil):
        return a if dil == 1 else a.reshape(batch, dil, seq // dil, C_WIDTH)

    gspecs = [group_spec(d) for d in dils]
    o_c = [group_view(a, d) for a, d in zip(o_c, dils)]
    lse_c = [group_view(a, d) for a, d in zip(lse_c, dils)]
    n_tok = 2 * sum(1 for d in dils if d > 1)
    in_specs = [pl.BlockSpec((tm, D_MODEL), row),
                pl.BlockSpec((None, tm, PLE_DIM), lambda i: (layer, i, 0)),
                half, half, half,
                ucol(U_GATE, LANES), pl.BlockSpec((LANES, 3 * A_WIDTH), lambda i: (0, 0)),
                ucol(U_AZ, A_WIDTH), half, ucol(U_BZ, B_WIDTH),
                *gspecs, *gspecs, ucol(U_CZ, C_WIDTH),
                ucol(U_MG, D_MODEL), ucol(U_MG + D_MODEL, D_MODEL), ucol(U_MG + 2 * D_MODEL, D_MODEL),
                wspec(A_WIDTH, D_MODEL), wspec(B_WIDTH, D_MODEL), wspec(C_WIDTH, D_MODEL),
                wspec(D_MODEL, D_MODEL), wspec(D_MODEL, D_MODEL), wspec(PLE_DIM, D_MODEL),
                wspec(1, D_MODEL)]
    return pl.pallas_call(
        _out_kernel,
        grid=(t // tm,),
        in_specs=in_specs,
        out_specs=pl.BlockSpec((tm, D_MODEL), row),
        out_shape=jax.ShapeDtypeStruct((t, D_MODEL), F32),
        scratch_shapes=[pltpu.VMEM((C_WIDTH // LANES, tm, LANES), F32)] * n_tok,
        compiler_params=_cparams(1),
        name="out_layer",
    )(x2, p, o_cmp, o_slc, o_win, u, expand, u, o_b, u, *o_c, *lse_c, u, u, u, u,
      w["w_a"], w["w_b"], w["w_c"], w["w_out"], w["w_plg"], w["w_ple"], w["g_post"])


def _overlap_t(seq):
    nc = seq // CMP_STRIDE
    ns = seq // SLC_LEN
    cs = np.arange(nc)[:, None] * CMP_STRIDE
    js = np.arange(ns)[None, :] * SLC_LEN
    ov = np.clip(np.minimum(cs + CMP_LEN, js + SLC_LEN) - np.maximum(cs, js), 0, None).astype(np.float32) / CMP_LEN
    return jnp.asarray(ov.T)


def _block_onehot(seq):
    blk = np.arange(seq)[:, None] // SLC_LEN
    return jnp.asarray((np.arange(LANES)[None, :] % HEAD_DIM == blk).astype(np.float32), dtype=BF16)


def _gate_expand():
    e = np.zeros((LANES, 3 * A_WIDTH), np.float32)
    for br in range(3):
        for i in range(A_HEADS // 2):
            for g in range(A_KV_GROUPS):
                h = g * (A_HEADS // 2) + i
                c0 = br * A_WIDTH + i * LANES + g * HEAD_DIM
                e[h * 3 + br, c0:c0 + HEAD_DIM] = 1.0
    return jnp.asarray(e, dtype=BF16)


def kernel(x, p, positions, g_pre, g_post, w_in, nsa_cmp_pe, nsa_cmp_w1, nsa_cmp_w2, w_a, mla_g_q, mla_w_uq,
           mla_g_kv, mla_w_ukv, w_b, w_c, w_out, w_ple, w_plg):
    batch, seq, _ = x.shape
    depth = w_in.shape[0]
    t = batch * seq
    nch = seq // CMP_STRIDE

    pos = positions.astype(F32)
    part_tabs = _rope_tables(pos.reshape(-1), ROT_DIM, HEAD_DIM, 0)
    mla_tabs = _mla_rope_tables(pos.reshape(-1))
    dils = [dil for _, dil in C_PAIRS]
    assert dils[0] == 1
    perm_tabs = {dil: _rope_tables(_residue_major_positions(pos, dil), ROT_DIM, HEAD_DIM, 0) for dil in dils[1:]}
    w_main, w_groups = _regroup_w_in(w_in)
    g_pre3 = g_pre.reshape(depth, 1, D_MODEL)
    pe2, w1d, w2d = _compress_weights(nsa_cmp_pe, nsa_cmp_w1, nsa_cmp_w2)
    wq = mla_w_uq.reshape(depth, Q_LORA, B_HEADS, QK_NOPE + QK_ROPE)
    wq = _mla_lanes(wq[..., QK_NOPE:QK_NOPE + QK_ROPE // 2], wq[..., QK_NOPE + QK_ROPE // 2:], wq[..., :QK_NOPE])
    wq = wq.reshape(depth, Q_LORA, -1).astype(BF16)
    wkv = mla_w_ukv.reshape(depth, KV_LORA, B_HEADS, QK_NOPE + V_DIM)
    no_rope = jnp.zeros(wkv.shape[:-1] + (QK_ROPE // 2,), wkv.dtype)
    wk = _mla_lanes(no_rope, no_rope, wkv[..., :QK_NOPE]).reshape(depth, KV_LORA, -1).astype(BF16)
    wv = wkv[..., QK_NOPE:].reshape(depth, KV_LORA, B_WIDTH).astype(BF16)
    gq3 = mla_g_q.reshape(depth, 1, Q_LORA)
    gkv3 = mla_g_kv.reshape(depth, 1, KV_LORA)
    w_a_p = w_a.reshape(depth, 2, 4, HEAD_DIM, D_MODEL).swapaxes(1, 2).reshape(depth, A_WIDTH, D_MODEL)
    wts = {"w_a": w_a_p.astype(BF16), "w_b": w_b.astype(BF16), "w_c": w_c.astype(BF16),
           "w_out": w_out.astype(BF16), "w_plg": w_plg.astype(BF16), "w_ple": w_ple.astype(BF16),
           "g_post": g_post.reshape(depth, 1, D_MODEL)}
    p3 = p.reshape(depth, t, PLE_DIM)
    ovt = _overlap_t(seq)
    onehot = _block_onehot(seq)
    expand = _gate_expand()

    x2 = x.reshape(t, D_MODEL)
    for layer in range(depth):
        h, h_res = _prenorm(x2, g_pre3, layer, batch, dils[1:])
        u = _inproj(h, w_main, layer, part_tabs, ROPE_COLS)
        u3 = u.reshape(batch, seq, DP)
        kc, vct = _compress(u3, pe2, w1d, w2d, layer)
        o_cmp, biast = _cmp_select(u3, kc, vct, ovt)
        o_slc = _slc_attention(u3, biast, onehot)
        (o_win,) = _banded(u, nseq=batch, seqlen=seq, q_off=U_AQ, k_off=U_KWIN, v_off=U_VWIN,
                           kv_width=LANES, max_dist=WIN - 1, with_lse=False)
        q_b, k_b, vt_b = _mla_prep(u, mla_tabs, gq3, gkv3, wq, wk, wv, layer)
        o_b = _mla_flash(q_b, k_b, vt_b, batch, seq)
        o_c, lse_c = [], []
        for gi, (window, dil) in enumerate(C_PAIRS):
            if dil == 1:
                o_g, lse_g = _banded(u, nseq=batch, seqlen=seq, q_off=U_CQ0, k_off=U_CK0, v_off=U_CV0,
                                     kv_width=C_WIDTH, max_dist=window, with_lse=True)
            else:
                u_g = _inproj(h_res[gi - 1], w_groups[gi - 1], layer, perm_tabs[dil], DG_ROPE_COLS)
                o_g, lse_g = _banded(u_g, nseq=batch * dil, seqlen=seq // dil, q_off=0, k_off=C_WIDTH,
                                     v_off=2 * C_WIDTH, kv_width=C_WIDTH, max_dist=window // dil, with_lse=True)
            o_c.append(o_g)
            lse_c.append(lse_g)
        x2 = _out_layer(x2, p3, layer, o_cmp.reshape(t, A_WIDTH), o_slc.reshape(t, A_WIDTH), o_win, u, expand,
                        o_b, o_c, lse_c, wts, batch, dils)
    return x2.reshape(batch, seq, D_MODEL)
```
